```python
import math
import jax, jax.numpy as jnp
from jax import lax
import numpy as np

D_MODEL = 2048
BATCH = 4
SEQ = 2048
DEPTH = 4

GRID_W = 64
CTX_LEN = 256
F32 = jnp.float32
NORM_EPS = 1e-6

HY_W = 1024
HY_ORDER = 2
HY_BANDS = 16
HY_POS_DIM = 2 * HY_BANDS + 1
HY_FILTER_HIDDEN = 64
HY_DECAY_MIN = -math.log(1e-2) / 1.5
HY_DECAY_MAX = -math.log(1e-2) / 0.3

FN_W = 1024
FN_GROUPS = 8
FN_GROUP_W = FN_W // FN_GROUPS

HG_W = 1024
HG_HEADS = 8
HG_DK = HG_W // HG_HEADS
HG_DV = HG_W // HG_HEADS
HG_CHUNK = 64

HY_OFF = 0
FN_OFF = HY_OFF + 3 * HY_W
HG_Q = FN_OFF + FN_W
HG_G = HG_Q + HG_W
HG_FF = HG_G + HG_W
HG_FB = HG_FF + HG_W
HG_I = HG_FB + HG_W
GATE_OFF = HG_I + HG_W
IN_COLS = GATE_OFF + 3 * D_MODEL

FFN_DENSE = 5632
N_EXPERTS = 8
TOP_K = 2
FFN_EXPERT = 4096
N_DENSE_LAYERS = (DEPTH + 1) // 2
N_MOE_LAYERS = DEPTH // 2

kernel_name = "hybrid_hyena_fnet_hgrn2_moe_dit"


def rms_norm(x, w):
    xf = x.astype(F32)
    y = xf * lax.rsqrt(jnp.mean(xf * xf, axis=-1, keepdims=True) + NORM_EPS)
    return (y * w.astype(F32)).astype(x.dtype)


def modulate(h, shift, scale):
    return h * (1 + scale) + shift


def short_conv3(z, w, b):
    zp = jnp.pad(z, ((0, 0), (1, 1), (0, 0)))
    return zp[:, :-2] * w[0] + zp[:, 1:-1] * w[1] + zp[:, 2:] * w[2] + b


def hyena_filters(L, w1, b1, w2, b2, w3, freq, decay):
    pos = jnp.arange(L, dtype=F32)
    t = pos / max(L - 1, 1)
    bands = jnp.linspace(1e-4, HY_BANDS - 1, HY_BANDS, dtype=F32)
    ang = 2 * math.pi * pos[:, None] * bands[None, :] / L
    feats = jnp.concatenate([t[:, None], jnp.cos(ang), -jnp.sin(ang)], axis=-1)
    h = jnp.sin(freq[0] * (feats @ w1 + b1))
    h = jnp.sin(freq[1] * (h @ w2 + b2))
    h = (h @ w3).reshape(L, HY_ORDER, 2, HY_W)
    h = h * jnp.exp(-t[:, None, None, None] * decay)
    k = jnp.concatenate([h[:, :, 0], jnp.zeros((1, HY_ORDER, HY_W), h.dtype), h[:0:-1, :, 1]], axis=0)
    k = k * lax.rsqrt(jnp.sum(k * k, axis=0, keepdims=True) + NORM_EPS)
    return jnp.fft.rfft(k, axis=0)


def fft_long_conv(u, kf):
    L = u.shape[1]
    uf = jnp.fft.rfft(u.astype(F32), n=2 * L, axis=1)
    return jnp.fft.irfft(uf * kf[None], n=2 * L, axis=1)[:, :L]


def hyena_branch(z, conv_w, conv_b, kf, skip):
    s = short_conv3(z.astype(F32), conv_w, conv_b)
    v, g1, g2 = jnp.split(s, 3, axis=-1)
    u = g1 * (fft_long_conv(v, kf[:, 0]) + v * skip[0])
    return g2 * (fft_long_conv(u, kf[:, 1]) + u * skip[1])


def fourier_branch(z):
    B, L, _ = z.shape
    g = z.astype(F32).reshape(B, L, FN_GROUPS, FN_GROUP_W)
    y = jnp.fft.fft2(g, axes=(1, 3), norm="ortho").real
    return y.reshape(B, L, FN_W)


def hgrn_forget(zf, lb):
    B, L, _ = zf.shape
    zf = zf.astype(F32).reshape(B, L, HG_HEADS, HG_DK)
    lb = lb.reshape(HG_HEADS, HG_DK)
    f = lb + (1 - lb) * jax.nn.sigmoid(zf)
    return jnp.log(f), 1 - f


def gla_chunkwise(q, k, v, log_f, s0):
    B, L, H, _ = q.shape
    n = L // HG_CHUNK

    def chunks(a):
        return a.reshape(B, n, HG_CHUNK, H, a.shape[-1]).transpose(1, 0, 3, 2, 4)

    lower = jnp.tril(jnp.ones((HG_CHUNK, HG_CHUNK), bool))[:, :, None]

    def step(S, inp):
        qc, kc, vc, lc = inp
        b = jnp.cumsum(lc, axis=2)
        o = jnp.einsum('bhtk,bhkv->bhtv', qc * jnp.exp(b), S)
        decay = jnp.exp(jnp.where(lower, b[:, :, :, None, :] - b[:, :, None, :, :], -jnp.inf))
        a = jnp.einsum('bhtk,bhsk,bhtsk->bhts', qc, kc, decay)
        o = o + jnp.einsum('bhts,bhsv->bhtv', a, vc)
        b_end = b[:, :, -1:, :]
        S = jnp.exp(b_end[:, :, 0, :, None]) * S + jnp.einsum('bhsk,bhsv->bhkv', kc * jnp.exp(b_end - b), vc)
        return S, o

    S, o = lax.scan(step, s0, (chunks(q), chunks(k), chunks(v), chunks(log_f)))
    return o.transpose(1, 0, 3, 2, 4).reshape(B, L, H, -1), S


def gla_final_state(k, v, log_f):
    b = jnp.cumsum(log_f, axis=1)
    return jnp.einsum('blhk,blhv->bhkv', k * jnp.exp(b[:, -1:] - b), v)


def hgrn2_branch(z, lb_f, lb_b, norm_w, s0_f, s0_b):
    B, L, _ = z.shape
    q = z[..., HG_Q:HG_Q + HG_W].astype(F32).reshape(B, L, HG_HEADS, HG_DK)
    i = z[..., HG_I:HG_I + HG_W].astype(F32).reshape(B, L, HG_HEADS, HG_DV)
    g = z[..., HG_G:HG_G + HG_W].astype(F32)
    logf_f, k_f = hgrn_forget(z[..., HG_FF:HG_FF + HG_W], lb_f)
    logf_b, k_b = hgrn_forget(z[..., HG_FB:HG_FB + HG_W], lb_b)
    o_f, s_f = gla_chunkwise(q, k_f, i, logf_f, s0_f)
    o_b, s_b = gla_chunkwise(q[:, ::-1], k_b[:, ::-1], i[:, ::-1], logf_b[:, ::-1], s0_b)
    o = rms_norm(o_f + o_b[:, ::-1], norm_w).reshape(B, L, HG_W) * jax.nn.silu(g)
    return o, s_f, s_b


def hgrn2_context_states(zc, lb_f, lb_b):
    B, L, _ = zc.shape
    logf_f, k_f = hgrn_forget(zc[..., :HG_W], lb_f)
    logf_b, k_b = hgrn_forget(zc[..., HG_W:2 * HG_W], lb_b)
    i = zc[..., 2 * HG_W:].astype(F32).reshape(B, L, HG_HEADS, HG_DV)
    s_f = gla_final_state(k_f, i, logf_f)
    s_b = gla_final_state(k_b[:, ::-1], i[:, ::-1], logf_b[:, ::-1])
    return s_f, s_b


def token_mixer(h, w_in_l, conv_w, conv_b, kf, skip, lb_f, lb_b, hg_nw, wb_hy, wb_fn, wb_hg, w_o, s0_f, s0_b):
    dt = h.dtype
    z = h @ w_in_l
    y_hy = hyena_branch(z[..., HY_OFF:HY_OFF + 3 * HY_W], conv_w, conv_b, kf, skip)
    y_fn = fourier_branch(z[..., FN_OFF:FN_OFF + FN_W])
    y_hg, s_f, s_b = hgrn2_branch(z, lb_f, lb_b, hg_nw, s0_f, s0_b)
    g_hy, g_fn, g_hg = jnp.split(jax.nn.sigmoid(z[..., GATE_OFF:].astype(F32)), 3, axis=-1)
    merged = (g_hy * (y_hy.astype(dt) @ wb_hy) + g_fn * (y_fn.astype(dt) @ wb_fn)
              + g_hg * (y_hg.astype(dt) @ wb_hg))
    return merged.astype(dt) @ w_o, s_f, s_b


def swiglu(h, w1, w3, w2):
    return (jax.nn.silu(h @ w1) * (h @ w3)) @ w2


def moe_swiglu(h, router, w1, w3, w2):
    logits = (h @ router).astype(F32)
    top_v, top_i = lax.top_k(logits, TOP_K)
    probs = jax.nn.softmax(top_v, axis=-1)
    comb = jnp.sum(jax.nn.one_hot(top_i, N_EXPERTS, dtype=F32) * probs[..., None], axis=-2)
    out = jnp.zeros(h.shape, F32)
    for e in range(N_EXPERTS):
        out = out + comb[..., e:e + 1] * swiglu(h, w1[e], w3[e], w2[e])
    return out.astype(h.dtype)


def setup_inputs(seed: int = 0) -> dict:
    key = jax.random.key(seed)
    keys = iter(jax.random.split(key, 48))

    def nrm(shape, scale):
        return jax.random.normal(next(keys), shape, F32) * scale

    def gain(shape):
        return 1.0 + nrm(shape, 0.02)

    D = D_MODEL
    return {
        "x": nrm((BATCH, SEQ, D), 1.0),
        "c": nrm((BATCH, D), 1.0),
        "ctx": nrm((BATCH, CTX_LEN, D), 1.0),
        "c_ctx": nrm((D,), 1.0),
        "w_mod": nrm((DEPTH, D, 6 * D), 0.5 * D ** -0.5),
        "b_mod": nrm((DEPTH, 6 * D), 0.02),
        "norm1_w": gain((DEPTH, D)),
        "norm2_w": gain((DEPTH, D)),
        "w_in": nrm((DEPTH, D, IN_COLS), D ** -0.5),
        "hy_conv_w": nrm((DEPTH, 3, 3 * HY_W), 3 ** -0.5),
        "hy_conv_b": nrm((DEPTH, 3 * HY_W), 0.02),
        "hy_f_w1": nrm((DEPTH, HY_POS_DIM, HY_FILTER_HIDDEN), HY_POS_DIM ** -0.5),
        "hy_f_b1": nrm((DEPTH, HY_FILTER_HIDDEN), 0.1),
        "hy_f_w2": nrm((DEPTH, HY_FILTER_HIDDEN, HY_FILTER_HIDDEN), HY_FILTER_HIDDEN ** -0.5),
        "hy_f_b2": nrm((DEPTH, HY_FILTER_HIDDEN), 0.1),
        "hy_f_w3": nrm((DEPTH, HY_FILTER_HIDDEN, HY_ORDER * 2 * HY_W), HY_FILTER_HIDDEN ** -0.5),
        "hy_f_freq": 1.0 + nrm((DEPTH, 2, HY_FILTER_HIDDEN), 0.1),
        "hy_decay": jax.random.uniform(next(keys), (DEPTH, HY_ORDER, 2, HY_W), F32, HY_DECAY_MIN, HY_DECAY_MAX),
        "hy_skip": nrm((DEPTH, HY_ORDER, HY_W), 0.5),
        "hg_lower": 1.0 + nrm((2, DEPTH, HG_W), 0.1),
        "hg_norm_w": gain((DEPTH, HG_DV)),
        "w_br_hy": nrm((DEPTH, HY_W, D), HY_W ** -0.5),
        "w_br_fn": nrm((DEPTH, FN_W, D), FN_W ** -0.5),
        "w_br_hg": nrm((DEPTH, HG_W, D), HG_W ** -0.5),
        "w_out": nrm((DEPTH, D, D), D ** -0.5),
        "ffn_w1": nrm((N_DENSE_LAYERS, D, FFN_DENSE), D ** -0.5),
        "ffn_w3": nrm((N_DENSE_LAYERS, D, FFN_DENSE), D ** -0.5),
        "ffn_w2": nrm((N_DENSE_LAYERS, FFN_DENSE, D), FFN_DENSE ** -0.5),
        "moe_router": nrm((N_MOE_LAYERS, D, N_EXPERTS), D ** -0.5),
        "moe_w1": nrm((N_MOE_LAYERS, N_EXPERTS, D, FFN_EXPERT), D ** -0.5),
        "moe_w3": nrm((N_MOE_LAYERS, N_EXPERTS, D, FFN_EXPERT), D ** -0.5),
        "moe_w2": nrm((N_MOE_LAYERS, N_EXPERTS, FFN_EXPERT, D), FFN_EXPERT ** -0.5),
        "final_norm_w": gain((D,)),
    }


def reference(x, c, ctx, c_ctx, w_mod, b_mod, norm1_w, norm2_w, w_in, hy_conv_w, hy_conv_b,
              hy_f_w1, hy_f_b1, hy_f_w2, hy_f_b2, hy_f_w3, hy_f_freq, hy_decay, hy_skip,
              hg_lower, hg_norm_w, w_br_hy, w_br_fn, w_br_hg, w_out,
              ffn_w1, ffn_w3, ffn_w2, moe_router, moe_w1, moe_w3, moe_w2, final_norm_w):
    B, L_lat, _ = x.shape
    L_ctx = ctx.shape[1]
    lb_all = jnp.cumsum(jax.nn.softmax(hg_lower.astype(F32), axis=1), axis=1)
    lb_all = lb_all - lb_all[:, :1]
    zero_state = jnp.zeros((B, HG_HEADS, HG_DK, HG_DV), F32)
    silu_c = jax.nn.silu(c)
    silu_cc = jax.nn.silu(c_ctx)

    for l in range(DEPTH):
        last = l == DEPTH - 1
        mod = (silu_c @ w_mod[l] + b_mod[l])[:, None, :]
        mod_c = silu_cc @ w_mod[l] + b_mod[l]
        sh1, sc1, gt1, sh2, sc2, gt2 = jnp.split(mod, 6, axis=-1)
        csh1, csc1, cgt1, csh2, csc2, cgt2 = jnp.split(mod_c, 6, axis=-1)
        filt = (hy_f_w1[l], hy_f_b1[l], hy_f_w2[l], hy_f_b2[l], hy_f_w3[l], hy_f_freq[l], hy_decay[l])
        mix_w = (w_in[l], hy_conv_w[l], hy_conv_b[l])
        mix_rest = (hy_skip[l], lb_all[0, l], lb_all[1, l], hg_norm_w[l], w_br_hy[l], w_br_fn[l], w_br_hg[l], w_out[l])

        h_c = modulate(rms_norm(ctx, norm1_w[l]), csh1, csc1)
        h_x = modulate(rms_norm(x, norm1_w[l]), sh1, sc1)
        if last:
            s_f, s_b = hgrn2_context_states(h_c @ w_in[l][:, HG_FF:HG_I + HG_W], lb_all[0, l], lb_all[1, l])
        else:
            kf_ctx = hyena_filters(L_ctx, *filt)
            m_c, s_f, s_b = token_mixer(h_c, *mix_w, kf_ctx, *mix_rest, zero_state, zero_state)
            ctx = ctx + cgt1 * m_c
        kf_lat = hyena_filters(L_lat, *filt)
        m_x, _, _ = token_mixer(h_x, *mix_w, kf_lat, *mix_rest, s_f, s_b)
        x = x + gt1 * m_x

        h_x = modulate(rms_norm(x, norm2_w[l]), sh2, sc2)
        if l % 2 == 0:
            p = (ffn_w1[l // 2], ffn_w3[l // 2], ffn_w2[l // 2])
            x = x + gt2 * swiglu(h_x, *p)
            if not last:
                ctx = ctx + cgt2 * swiglu(modulate(rms_norm(ctx, norm2_w[l]), csh2, csc2), *p)
        else:
            p = (moe_router[l // 2], moe_w1[l // 2], moe_w3[l // 2], moe_w2[l // 2])
            x = x + gt2 * moe_swiglu(h_x, *p)
            if not last:
                ctx = ctx + cgt2 * moe_swiglu(modulate(rms_norm(ctx, norm2_w[l]), csh2, csc2), *p)

    return rms_norm(x, final_norm_w)
```

```python
import functools
import math

import numpy as np
import jax
import jax.numpy as jnp
from jax import lax
from jax.experimental import pallas as pl
from jax.experimental.pallas import tpu as pltpu

F32 = jnp.float32
BF16 = jnp.bfloat16
NORM_EPS = 1e-6

D_MODEL = 2048
DEPTH = 4
HY_W = 1024
HY_BANDS = 16
HY_POS_DIM = 2 * HY_BANDS + 1
HY_FILTER_HIDDEN = 64
FN_W = 1024
FN_GROUP_W = 128
HG_W = 1024
HG_HEADS = 8
HG_DK = 128
HG_CHUNK = 64
HG_LEVELS = 6
N_EXPERTS = 8
LANE = 128

HG_FF = 3 * HY_W + FN_W + 2 * HG_W
HG_I = HG_FF + 2 * HG_W
GATE_OFF = HG_I + HG_W
ZM_GATE = 0
ZM_HY = ZM_GATE + 3 * D_MODEL
ZM_FN = ZM_HY + 3 * HY_W
ZM_Q = ZM_FN + FN_W
ZM_G = ZM_Q + HG_W
ZM_I = ZM_G + HG_W
ZM_COLS = ZM_I + HG_W

VMEM_LIMIT = 56 * 1024 * 1024


def _cparams(n_axes, vmem=VMEM_LIMIT):
    return pltpu.CompilerParams(dimension_semantics=("arbitrary",) * n_axes, vmem_limit_bytes=vmem)


def _resident(shape, index_map):
    return pl.BlockSpec(shape, index_map, pipeline_mode=pl.Buffered(1))


def _silu(v):
    return v * jax.nn.sigmoid(v)


def _bdot(a, b):
    return jnp.dot(a, b, preferred_element_type=F32)


def _mod_kernel(c_ref, w_ref, b_ref, o_ref):
    c = _silu(c_ref[...])
    o_ref[0] = jnp.dot(c, w_ref[0], preferred_element_type=F32,
                       precision=lax.Precision.HIGHEST) + b_ref[0]


def _modulation(cc, w_mod, b_mod):
    depth, d, n6 = w_mod.shape
    tn = 2048
    return pl.pallas_call(
        _mod_kernel,
        grid=(depth, n6 // tn),
        in_specs=[
            pl.BlockSpec((8, d), lambda l, n: (0, 0)),
            pl.BlockSpec((1, d, tn), lambda l, n: (l, 0, n)),
            pl.BlockSpec((1, 1, tn), lambda l, n: (l, 0, n)),
        ],
        out_specs=pl.BlockSpec((1, 8, tn), lambda l, n: (l, 0, n)),
        out_shape=jax.ShapeDtypeStruct((depth, 8, n6), F32),
        compiler_params=_cparams(2),
        name="modulation",
    )(cc, w_mod, b_mod.reshape(depth, 1, n6))


def _norm_mod(x, nw, shift, scale):
    ms = jnp.mean(x * x, axis=-1, keepdims=True)
    return (x * lax.rsqrt(ms + NORM_EPS) * nw) * (1 + scale) + shift


def _group_of(row_block, tm, seq, batch):
    return jnp.minimum((row_block * tm) // seq, batch)


def _inproj_kernel(x_ref, nw_ref, mod_ref, w_ref, o_ref, h_scr):
    @pl.when(pl.program_id(1) == 0)
    def _():
        h = _norm_mod(x_ref[...], nw_ref[...], mod_ref[0, 0:1, :], mod_ref[0, 1:2, :])
        h_scr[...] = h.astype(BF16)

    o_ref[...] = _bdot(h_scr[...], w_ref[...]).astype(o_ref.dtype)


def _in_proj(x, nw, mods, w, out_dtype, *, rows, seq, batch, tm=1024, tn=1024):
    d = x.shape[1]
    n = w.shape[1]
    tm = min(tm, rows)
    return pl.pallas_call(
        _inproj_kernel,
        grid=(rows // tm, n // tn),
        in_specs=[
            pl.BlockSpec((tm, d), lambda m, j: (m, 0)),
            pl.BlockSpec((1, d), lambda m, j: (0, 0)),
            pl.BlockSpec((1, 6, d), lambda m, j: (_group_of(m, tm, seq, batch), 0, 0)),
            pl.BlockSpec((d, tn), lambda m, j: (0, j)),
        ],
        out_specs=pl.BlockSpec((tm, tn), lambda m, j: (m, j)),
        out_shape=jax.ShapeDtypeStruct((x.shape[0], n), out_dtype),
        scratch_shapes=[pltpu.VMEM((tm, d), BF16)],
        compiler_params=_cparams(2),
        name="in_proj",
    )(x, nw.reshape(1, d), mods, w)


def _merge_kernel(yhy_ref, yfn_ref, of_ref, ob_ref, g_ref, gate_ref, x_ref, hnw_ref, mod_ref,
                  wbhy_ref, wbfn_ref, wbhg_ref, wo_ref, o_ref):
    d = x_ref.shape[1]
    ohg = of_ref[...] + ob_ref[...]
    nw = hnw_ref[...]
    heads = []
    for h in range(ohg.shape[1] // HG_DK):
        oh = ohg[:, h * HG_DK:(h + 1) * HG_DK]
        ms = jnp.mean(oh * oh, axis=-1, keepdims=True)
        heads.append(oh * lax.rsqrt(ms + NORM_EPS) * nw)
    y_hg = jnp.concatenate(heads, axis=-1) * _silu(g_ref[...].astype(F32))
    gate = lambda i: jax.nn.sigmoid(gate_ref[:, i * d:(i + 1) * d].astype(F32))
    merged = gate(0) * _bdot(yhy_ref[...], wbhy_ref[...])
    merged += gate(1) * _bdot(yfn_ref[...], wbfn_ref[...])
    merged += gate(2) * _bdot(y_hg.astype(BF16), wbhg_ref[...])
    out = _bdot(merged.astype(BF16), wo_ref[...])
    o_ref[...] = x_ref[...] + mod_ref[0, 2:3, :] * out


def _merge_out(y_hy, y_fn, o_f, o_b, zm, x, hg_nw, mods, wb_hy, wb_fn, wb_hg, w_o,
               *, rows, seq, batch, tm=256):
    d = x.shape[1]
    w = y_hy.shape[1]
    tm = min(tm, rows)
    row = lambda m: (m, 0)
    const = lambda m: (0, 0)
    return pl.pallas_call(
        _merge_kernel,
        grid=(rows // tm,),
        in_specs=[
            pl.BlockSpec((tm, w), row),
            pl.BlockSpec((tm, w), row),
            pl.BlockSpec((tm, w), row),
            pl.BlockSpec((tm, w), row),
            pl.BlockSpec((tm, w), lambda m: (m, ZM_G // w)),
            pl.BlockSpec((tm, 3 * d), lambda m: (m, ZM_GATE // (3 * d))),
            pl.BlockSpec((tm, d), row),
            pl.BlockSpec((1, HG_DK), const),
            pl.BlockSpec((1, 6, d), lambda m: (_group_of(m, tm, seq, batch), 0, 0)),
            _resident((w, d), const),
            _resident((w, d), const),
            _resident((w, d), const),
            _resident((d, d), const),
        ],
        out_specs=pl.BlockSpec((tm, d), row),
        out_shape=jax.ShapeDtypeStruct((rows, d), F32),
        compiler_params=_cparams(1),
        name="merge_out",
    )(y_hy, y_fn, o_f, o_b, zm, zm, x, hg_nw.reshape(1, HG_DK), mods, wb_hy, wb_fn, wb_hg, w_o)


def _ffn_kernel(x_ref, nw_ref, mod_ref, w1_ref, w3_ref, w2_ref, o_ref, h_scr, acc_scr):
    f = pl.program_id(1)

    @pl.when(f == 0)
    def _():
        h = _norm_mod(x_ref[...], nw_ref[...], mod_ref[0, 3:4, :], mod_ref[0, 4:5, :])
        h_scr[...] = h.astype(BF16)
        acc_scr[...] = jnp.zeros_like(acc_scr)

    h = h_scr[...]
    act = _silu(_bdot(h, w1_ref[...])) * _bdot(h, w3_ref[...])
    acc_scr[...] += _bdot(act.astype(BF16), w2_ref[...])

    @pl.when(f == pl.num_programs(1) - 1)
    def _():
        o_ref[...] = x_ref[...] + mod_ref[0, 5:6, :] * acc_scr[...]


def _ffn_dense(x, nw, mods, w1, w3, w2, *, rows, seq, batch, tm=512, tf=512):
    d = x.shape[1]
    ff = w1.shape[1]
    tm = min(tm, rows)
    return pl.pallas_call(
        _ffn_kernel,
        grid=(rows // tm, ff // tf),
        in_specs=[
            pl.BlockSpec((tm, d), lambda m, f: (m, 0)),
            pl.BlockSpec((1, d), lambda m, f: (0, 0)),
            pl.BlockSpec((1, 6, d), lambda m, f: (_group_of(m, tm, seq, batch), 0, 0)),
            pl.BlockSpec((d, tf), lambda m, f: (0, f)),
            pl.BlockSpec((d, tf), lambda m, f: (0, f)),
            pl.BlockSpec((tf, d), lambda m, f: (f, 0)),
        ],
        out_specs=pl.BlockSpec((tm, d), lambda m, f: (m, 0)),
        out_shape=jax.ShapeDtypeStruct((rows, d), F32),
        scratch_shapes=[pltpu.VMEM((tm, d), BF16), pltpu.VMEM((tm, d), F32)],
        compiler_params=_cparams(2),
        name="ffn_dense",
    )(x, nw.reshape(1, d), mods, w1, w3, w2)


def _top2_comb(logits, n_experts):
    lane = lax.broadcasted_iota(jnp.int32, logits.shape, 1).astype(F32)
    neg = jnp.float32(-jnp.inf)
    lg = jnp.where(lane < n_experts, logits, neg)
    m1 = jnp.max(lg, axis=-1, keepdims=True)
    i1 = jnp.min(jnp.where(lg == m1, lane, float(LANE)), axis=-1, keepdims=True)
    lg2 = jnp.where(lane == i1, neg, lg)
    m2 = jnp.max(lg2, axis=-1, keepdims=True)
    i2 = jnp.min(jnp.where(lg2 == m2, lane, float(LANE)), axis=-1, keepdims=True)
    e2 = jnp.exp(m2 - m1)
    p1 = 1.0 / (1.0 + e2)
    p2 = e2 / (1.0 + e2)
    return jnp.where(lane == i1, p1, 0.0) + jnp.where(lane == i2, p2, 0.0)


def _moe_kernel(x_ref, nw_ref, mod_ref, r_ref, w1_ref, w3_ref, w2_ref, o_ref,
                h_scr, comb_scr, acc_scr):
    e = pl.program_id(1)
    f = pl.program_id(2)

    @pl.when((e == 0) & (f == 0))
    def _():
        h = _norm_mod(x_ref[...], nw_ref[...], mod_ref[0, 3:4, :], mod_ref[0, 4:5, :])
        logits = jnp.dot(h, r_ref[...], preferred_element_type=F32, precision=lax.Precision.HIGHEST)
        comb_scr[...] = _top2_comb(logits, N_EXPERTS)
        h_scr[...] = h.astype(BF16)
        acc_scr[...] = jnp.zeros_like(acc_scr)

    h = h_scr[...]
    lane = lax.broadcasted_iota(jnp.int32, comb_scr.shape, 1)
    ce = jnp.sum(jnp.where(lane == e, comb_scr[...], 0.0), axis=-1, keepdims=True)
    act = _silu(_bdot(h, w1_ref[0])) * _bdot(h, w3_ref[0]) * ce
    acc_scr[...] += _bdot(act.astype(BF16), w2_ref[0])

    @pl.when((e == pl.num_programs(1) - 1) & (f == pl.num_programs(2) - 1))
    def _():
        o_ref[...] = x_ref[...] + mod_ref[0, 5:6, :] * acc_scr[...]


def _moe_dense(x, nw, mods, router, w1, w3, w2, *, rows, seq, batch, tm=512, tf=512):
    d = x.shape[1]
    ne, _, ff = w1.shape
    tm = min(tm, rows)
    rpad = jnp.zeros((d, LANE), F32).at[:, :ne].set(router)
    return pl.pallas_call(
        _moe_kernel,
        grid=(rows // tm, ne, ff // tf),
        in_specs=[
            pl.BlockSpec((tm, d), lambda m, e, f: (m, 0)),
            pl.BlockSpec((1, d), lambda m, e, f: (0, 0)),
            pl.BlockSpec((1, 6, d), lambda m, e, f: (_group_of(m, tm, seq, batch), 0, 0)),
            pl.BlockSpec((d, LANE), lambda m, e, f: (0, 0)),
            pl.BlockSpec((1, d, tf), lambda m, e, f: (e, 0, f)),
            pl.BlockSpec((1, d, tf), lambda m, e, f: (e, 0, f)),
            pl.BlockSpec((1, tf, d), lambda m, e, f: (e, f, 0)),
        ],
        out_specs=pl.BlockSpec((tm, d), lambda m, e, f: (m, 0)),
        out_shape=jax.ShapeDtypeStruct((rows, d), F32),
        scratch_shapes=[pltpu.VMEM((tm, d), BF16), pltpu.VMEM((tm, LANE), F32),
                        pltpu.VMEM((tm, d), F32)],
        compiler_params=_cparams(3),
        name="moe_dense",
    )(x, nw.reshape(1, d), mods, rpad, w1, w3, w2)


def _final_norm_kernel(x_ref, w_ref, o_ref):
    x = x_ref[...]
    ms = jnp.mean(x * x, axis=-1, keepdims=True)
    o_ref[...] = x * lax.rsqrt(ms + NORM_EPS) * w_ref[...]


def _final_norm(x, w, *, rows, tm=512):
    d = x.shape[1]
    return pl.pallas_call(
        _final_norm_kernel,
        grid=(rows // tm,),
        in_specs=[pl.BlockSpec((tm, d), lambda m: (m, 0)), pl.BlockSpec((1, d), lambda m: (0, 0))],
        out_specs=pl.BlockSpec((tm, d), lambda m: (m, 0)),
        out_shape=jax.ShapeDtypeStruct((rows, d), F32),
        compiler_params=_cparams(1),
        name="final_norm",
    )(x, w.reshape(1, d))


def _dft_angles(n, period):
    k = lax.broadcasted_iota(jnp.int32, (n, n), 0)
    s = lax.broadcasted_iota(jnp.int32, (n, n), 1)
    return ((k * s) % period).astype(F32) * (2.0 * math.pi / period)


def _hyena_dft(seq_len):
    n = 2 * seq_len
    ang = _dft_angles(seq_len, n)
    row = lax.broadcasted_iota(jnp.int32, ang.shape, 0)
    col = lax.broadcasted_iota(jnp.int32, ang.shape, 1)
    cos = jnp.cos(ang)
    sin = jnp.sin(ang)
    fc = cos
    fs = jnp.where(row == 0, jnp.where(col % 2 == 0, 1.0, -1.0), sin)
    ic = jnp.where(col == 0, 1.0 / n, 2.0 / n * cos)
    isn = jnp.where(col == 0, jnp.where(row % 2 == 0, 1.0, -1.0) / n, 2.0 / n * sin)
    return fc.astype(BF16), fs.astype(BF16), ic.astype(BF16), isn.astype(BF16)


def _hyena_feats(seq_len):
    pos = np.arange(seq_len, dtype=np.float64)
    t = pos / max(seq_len - 1, 1)
    bands = np.linspace(1e-4, HY_BANDS - 1, HY_BANDS).astype(np.float32).astype(np.float64)
    ang = 2 * math.pi * pos[:, None] * bands[None, :] / seq_len
    feats = np.concatenate([t[:, None], np.cos(ang), -np.sin(ang)], axis=-1)
    out = np.zeros((seq_len, LANE), np.float32)
    out[:, :HY_POS_DIM] = feats
    return jnp.asarray(out)


def _split_dot(w, a):
    hi = a.astype(BF16)
    lo = (a - hi.astype(F32)).astype(BF16)
    return _bdot(w, hi) + _bdot(w, lo)


def _filter_kernel(feat_ref, w1_ref, b1_ref, w2_ref, b2_ref, fr_ref, w3f_ref, w3b_ref,
                   dcf_ref, dcb_ref, fc_ref, fs_ref, o_ref):
    hp = lax.Precision.HIGHEST
    feats = feat_ref[...]
    h = jnp.sin(fr_ref[0, 0:1, :] * (jnp.dot(feats, w1_ref[0], preferred_element_type=F32,
                                             precision=hp) + b1_ref[0]))
    h = jnp.sin(fr_ref[0, 1:2, :] * (jnp.dot(h, w2_ref[0], preferred_element_type=F32,
                                             precision=hp) + b2_ref[0]))
    t = feats[:, 0:1]
    hf = jnp.dot(h, w3f_ref[0], preferred_element_type=F32, precision=hp) * jnp.exp(-t * dcf_ref[0])
    hb = jnp.dot(h, w3b_ref[0], preferred_element_type=F32, precision=hp) * jnp.exp(-t * dcb_ref[0])
    row = lax.broadcasted_iota(jnp.int32, hf.shape, 0)
    hb = jnp.where(row == 0, 0.0, hb)
    ss = jnp.sum(hf * hf + hb * hb, axis=0, keepdims=True)
    scale = lax.rsqrt(ss + NORM_EPS)
    even = hf * scale + hb * scale
    odd = hf * scale - hb * scale
    kc = _split_dot(fc_ref[...], even)
    ks = _split_dot(fs_ref[...], odd)
    sign = jnp.where(row % 2 == 0, 1.0, -1.0)
    nyq = jnp.sum(even * sign, axis=0, keepdims=True)
    o_ref[0, 0] = kc
    o_ref[0, 1] = jnp.where(row == 0, nyq, ks)


def _hyena_filters(seq_len, w1p, b1, w2, b2, w3, freq, decay, fc, fs, *, tn=256):
    depth = w1p.shape[0]
    hid = HY_FILTER_HIDDEN
    nb = HY_W // tn
    feats = _hyena_feats(seq_len)
    col = lambda d: (lambda l, o, c: (l, 0, (2 * o + d) * nb + c))
    lyr = lambda l, o, c: (l, 0, 0)
    return pl.pallas_call(
        _filter_kernel,
        grid=(depth, 2, nb),
        in_specs=[
            pl.BlockSpec((seq_len, LANE), lambda l, o, c: (0, 0)),
            pl.BlockSpec((1, LANE, hid), lyr),
            pl.BlockSpec((1, 1, hid), lyr),
            pl.BlockSpec((1, hid, hid), lyr),
            pl.BlockSpec((1, 1, hid), lyr),
            pl.BlockSpec((1, 2, hid), lyr),
            pl.BlockSpec((1, hid, tn), col(0)),
            pl.BlockSpec((1, hid, tn), col(1)),
            pl.BlockSpec((1, 1, tn), col(0)),
            pl.BlockSpec((1, 1, tn), col(1)),
            _resident((seq_len, seq_len), lambda l, o, c: (0, 0)),
            _resident((seq_len, seq_len), lambda l, o, c: (0, 0)),
        ],
        out_specs=pl.BlockSpec((1, 2, seq_len, tn), lambda l, o, c: (l, 0, 0, o * nb + c)),
        out_shape=jax.ShapeDtypeStruct((depth, 2, seq_len, 2 * HY_W), F32),
        compiler_params=_cparams(3),
        name="hyena_filters",
    )(feats, w1p, b1.reshape(depth, 1, hid), w2, b2.reshape(depth, 1, hid), freq,
      w3, w3, decay.reshape(depth, 1, 4 * HY_W), decay.reshape(depth, 1, 4 * HY_W), fc, fs)


def _conv3(z, w_ref, b_ref):
    n = z.shape[0]
    row = lax.broadcasted_iota(jnp.int32, z.shape, 0)
    prev = jnp.where(row == 0, 0.0, pltpu.roll(z, 1, 0))
    nxt = jnp.where(row == n - 1, 0.0, pltpu.roll(z, n - 1, 0))
    return prev * w_ref[0:1, :] + z * w_ref[1:2, :] + nxt * w_ref[2:3, :] + b_ref[...]


def _longconv_kernel(a_ref, g_ref, aw_ref, ab_ref, gw_ref, gb_ref, kf_ref, skip_ref,
                     fc_ref, fs_ref, ic_ref, is_ref, o_ref, acc_scr, *, conv_a, fblk):
    a = a_ref[...].astype(F32)
    if conv_a:
        a = _conv3(a, aw_ref, ab_ref)
    a16 = a.astype(BF16)
    n = a.shape[0]
    for j in range(n // fblk):
        rows = slice(j * fblk, (j + 1) * fblk)
        ac = _bdot(fc_ref[rows, :], a16)
        asn = _bdot(fs_ref[rows, :], a16)
        kc = kf_ref[0, 0, rows, :]
        ks = kf_ref[0, 1, rows, :]
        cross = asn * ks
        yc = ac * kc - cross
        ys = ac * ks + asn * kc
        if j == 0:
            row = lax.broadcasted_iota(jnp.int32, ac.shape, 0)
            yc = jnp.where(row == 0, ac * kc, yc)
            ys = jnp.where(row == 0, cross, ys)
        part = _bdot(ic_ref[:, rows], yc.astype(BF16)) + _bdot(is_ref[:, rows], ys.astype(BF16))
        if j == 0:
            acc_scr[...] = part
        else:
            acc_scr[...] += part
    gate = _conv3(g_ref[...].astype(F32), gw_ref, gb_ref)
    o_ref[...] = (gate * (acc_scr[...] + a * skip_ref[0])).astype(o_ref.dtype)


def _hyena_longconv(a_src, a_part, zm, gate_part, conv_w, conv_b, kf, layer, order, skip,
                    dft, *, seq_len, n_seq, row0, out_rows, prev_out=None, tn=256):
    fc, fs, ic, isn = dft
    nb = HY_W // tn
    rb0 = row0 // seq_len
    fblk = min(512, seq_len)
    const = lambda c, b: (0, 0)
    conv_a = a_part is not None
    aw_blk = a_part * nb if conv_a else 0
    a_blk = ZM_HY // tn + aw_blk if conv_a else 0
    gw_blk = gate_part * nb
    g_blk = ZM_HY // tn + gw_blk
    kernel = functools.partial(_longconv_kernel, conv_a=conv_a, fblk=fblk)
    in_specs = [
        pl.BlockSpec((seq_len, tn), lambda c, b: (rb0 + b, a_blk + c)),
        pl.BlockSpec((seq_len, tn), lambda c, b: (rb0 + b, g_blk + c)),
        pl.BlockSpec((3, tn), lambda c, b: (0, aw_blk + c)),
        pl.BlockSpec((1, tn), lambda c, b: (0, aw_blk + c)),
        pl.BlockSpec((3, tn), lambda c, b: (0, gw_blk + c)),
        pl.BlockSpec((1, tn), lambda c, b: (0, gw_blk + c)),
        pl.BlockSpec((1, 2, seq_len, tn), lambda c, b: (layer, 0, 0, order * nb + c)),
        pl.BlockSpec((1, 1, tn), lambda c, b: (order, 0, c)),
        _resident((seq_len, seq_len), const),
        _resident((seq_len, seq_len), const),
        _resident((seq_len, seq_len), const),
        _resident((seq_len, seq_len), const),
    ]
    args = [a_src, zm, conv_w, conv_b.reshape(1, -1), conv_w, conv_b.reshape(1, -1), kf,
            skip.reshape(2, 1, HY_W), fc, fs, ic, isn]
    aliases = {}
    if prev_out is not None:
        in_specs.append(pl.BlockSpec(memory_space=pl.ANY))
        args.append(prev_out)
        aliases = {len(args) - 1: 0}
        kernel = functools.partial(_drop_last_input, kernel, 12)
    return pl.pallas_call(
        kernel,
        grid=(nb, n_seq),
        in_specs=in_specs,
        out_specs=pl.BlockSpec((seq_len, tn), lambda c, b: (rb0 + b, c)),
        out_shape=jax.ShapeDtypeStruct((out_rows, HY_W), BF16),
        scratch_shapes=[pltpu.VMEM((seq_len, tn), F32)],
        input_output_aliases=aliases,
        compiler_params=_cparams(2),
        name="hyena_longconv",
    )(*args)


def _drop_last_input(kernel, n_in, *refs):
    return kernel(*refs[:n_in], *refs[n_in + 1:])


def _fnet_dft(seq_len):
    ang_l = _dft_angles(seq_len, seq_len)
    ang_w = _dft_angles(FN_GROUP_W, FN_GROUP_W)
    scale = 1.0 / math.sqrt(seq_len * FN_GROUP_W)
    return ((jnp.cos(ang_l) * scale).astype(BF16), (jnp.sin(ang_l) * scale).astype(BF16),
            jnp.cos(ang_w).astype(BF16), jnp.sin(ang_w).astype(BF16))


def _fnet_kernel(z_ref, cl_ref, sl_ref, cw_ref, sw_ref, o_ref):
    z = z_ref[...]
    gc, gs = [], []
    for j in range(z.shape[1] // FN_GROUP_W):
        zj = z[:, j * FN_GROUP_W:(j + 1) * FN_GROUP_W]
        gc.append(_bdot(zj, cw_ref[...]).astype(BF16))
        gs.append(_bdot(zj, sw_ref[...]).astype(BF16))
    gc = jnp.concatenate(gc, axis=-1)
    gs = jnp.concatenate(gs, axis=-1)
    o_ref[...] = (_bdot(cl_ref[...], gc) - _bdot(sl_ref[...], gs)).astype(o_ref.dtype)


def _fnet(zm, dft, *, seq_len, n_seq, row0, out_rows, prev_out=None, tn=256):
    cl, sl, cw, sw = dft
    nb = FN_W // tn
    rb0 = row0 // seq_len
    c0 = ZM_FN // tn
    const = lambda c, b: (0, 0)
    in_specs = [
        pl.BlockSpec((seq_len, tn), lambda c, b: (rb0 + b, c0 + c)),
        _resident((seq_len, seq_len), const),
        _resident((seq_len, seq_len), const),
        pl.BlockSpec((FN_GROUP_W, FN_GROUP_W), const),
        pl.BlockSpec((FN_GROUP_W, FN_GROUP_W), const),
    ]
    args = [zm, cl, sl, cw, sw]
    kernel = _fnet_kernel
    aliases = {}
    if prev_out is not None:
        in_specs.append(pl.BlockSpec(memory_space=pl.ANY))
        args.append(prev_out)
        aliases = {len(args) - 1: 0}
        kernel = functools.partial(_drop_last_input, _fnet_kernel, 5)
    return pl.pallas_call(
        kernel,
        grid=(nb, n_seq),
        in_specs=in_specs,
        out_specs=pl.BlockSpec((seq_len, tn), lambda c, b: (rb0 + b, c)),
        out_shape=jax.ShapeDtypeStruct((out_rows, FN_W), BF16),
        input_output_aliases=aliases,
        compiler_params=_cparams(2),
        name="fnet",
    )(*args)


def _hgrn_tables(reverse):
    c = HG_CHUNK
    tri = np.zeros((c, c), np.float32)
    for t in range(c):
        if reverse:
            tri[t, t:] = 1.0
        else:
            tri[t, :t + 1] = 1.0
    mats = [tri]
    masks = []
    for lvl in range(HG_LEVELS):
        half = (c // 2) >> lvl
        sel = np.zeros((c, c), np.float32)
        mask = np.zeros((c, c), np.float32)
        for t in range(c):
            start = (t // (2 * half)) * 2 * half
            mid = start + half
            sel[t, mid if reverse else mid - 1] = 1.0
            for s in range(start, start + 2 * half):
                if reverse and t < mid <= s:
                    mask[t, s] = 1.0
                if (not reverse) and s < mid <= t:
                    mask[t, s] = 1.0
        mats.append(tri - sel @ tri)
        masks.append(mask)
    return np.concatenate(mats, axis=0), np.stack(masks, axis=0)


def _hgrn_direction(q_ref, v_ref, zf_ref, lb_ref, tab_ref, mask_ref, st_scr, o_ref, end_row):
    c = HG_CHUNK
    lb = lb_ref[...]
    f = lb + (1.0 - lb) * jax.nn.sigmoid(zf_ref[...])
    lf = jnp.log(f)
    kk = 1.0 - f
    p1 = lf.astype(BF16)
    r1 = lf - p1.astype(F32)
    p2 = r1.astype(BF16)
    p3 = (r1 - p2.astype(F32)).astype(BF16)
    tab = tab_ref[...]
    dec = _bdot(tab, p1) + _bdot(tab, p2) + _bdot(tab, p3)
    q = q_ref[...].astype(F32)
    v = v_ref[...]
    for h in range(HG_HEADS):
        cols = slice(h * HG_DK, (h + 1) * HG_DK)
        qh = q[:, cols]
        kh = kk[:, cols]
        vh = v[:, cols]
        b = dec[0:c, cols]
        st = st_scr[h]
        qd = (qh * jnp.exp(b)).astype(BF16)
        o = lax.dot_general(qd, st.astype(BF16), (((1,), (1,)), ((), ())),
                            preferred_element_type=F32)
        o += jnp.sum(qh * kh, axis=-1, keepdims=True) * vh.astype(F32)
        att = jnp.zeros((c, c), F32)
        for lvl in range(HG_LEVELS):
            d = dec[(lvl + 1) * c:(lvl + 2) * c, cols]
            ql = (qh * jnp.exp(jnp.minimum(d, 0.0))).astype(BF16)
            kl = (kh * jnp.exp(jnp.minimum(-d, 0.0))).astype(BF16)
            a = lax.dot_general(ql, kl, (((1,), (1,)), ((), ())), preferred_element_type=F32)
            att += jnp.where(mask_ref[lvl] > 0.5, a, 0.0)
        o += _bdot(att.astype(BF16), vh)
        o_ref[:, cols] = o
        b_end = b[end_row:end_row + 1, :]
        kd = (kh * jnp.exp(b_end - b)).astype(BF16)
        upd = lax.dot_general(vh, kd, (((0,), (0,)), ((), ())), preferred_element_type=F32)
        st_scr[h] = jnp.exp(b_end) * st + upd


def _hgrn_kernel(qf_ref, vf_ref, zff_ref, qb_ref, vb_ref, zfb_ref, lbf_ref, lbb_ref,
                 tabf_ref, maskf_ref, tabb_ref, maskb_ref, s0f_ref, s0b_ref,
                 of_ref, ob_ref, sf_ref, sb_ref, stf_scr, stb_scr):
    ci = pl.program_id(1)

    @pl.when(ci == 0)
    def _():
        stf_scr[...] = s0f_ref[0]
        stb_scr[...] = s0b_ref[0]

    _hgrn_direction(qf_ref, vf_ref, zff_ref, lbf_ref, tabf_ref, maskf_ref, stf_scr, of_ref,
                    HG_CHUNK - 1)
    _hgrn_direction(qb_ref, vb_ref, zfb_ref, lbb_ref, tabb_ref, maskb_ref, stb_scr, ob_ref, 0)

    @pl.when(ci == pl.num_programs(1) - 1)
    def _():
        sf_ref[0] = stf_scr[...]
        sb_ref[0] = stb_scr[...]


def _hgrn(zm, zf, lb_f, lb_b, s0_f, s0_b, *, seq_len, n_seq, row0, out_rows,
          prev_of=None, prev_ob=None):
    c = HG_CHUNK
    nc = seq_len // c
    rb0 = row0 // c
    w = HG_W
    tab_f, mask_f = _hgrn_tables(False)
    tab_b, mask_b = _hgrn_tables(True)
    fwd = lambda col: (lambda b, i: (rb0 + b * nc + i, col))
    bwd = lambda col: (lambda b, i: (rb0 + b * nc + nc - 1 - i, col))
    const2 = lambda b, i: (0, 0)
    const3 = lambda b, i: (0, 0, 0)
    state = lambda b, i: (b, 0, 0, 0)
    nt = (1 + HG_LEVELS) * c
    in_specs = [
        pl.BlockSpec((c, w), fwd(ZM_Q // w)),
        pl.BlockSpec((c, w), fwd(ZM_I // w)),
        pl.BlockSpec((c, w), fwd(0)),
        pl.BlockSpec((c, w), bwd(ZM_Q // w)),
        pl.BlockSpec((c, w), bwd(ZM_I // w)),
        pl.BlockSpec((c, w), bwd(1)),
        pl.BlockSpec((1, w), const2),
        pl.BlockSpec((1, w), const2),
        pl.BlockSpec((nt, c), const2),
        pl.BlockSpec((HG_LEVELS, c, c), const3),
        pl.BlockSpec((nt, c), const2),
        pl.BlockSpec((HG_LEVELS, c, c), const3),
        pl.BlockSpec((1, HG_HEADS, HG_DK, HG_DK), state),
        pl.BlockSpec((1, HG_HEADS, HG_DK, HG_DK), state),
    ]
    args = [zm, zm, zf, zm, zm, zf, lb_f.reshape(1, w), lb_b.reshape(1, w),
            jnp.asarray(tab_f, BF16), jnp.asarray(mask_f), jnp.asarray(tab_b, BF16),
            jnp.asarray(mask_b), s0_f, s0_b]
    kernel = _hgrn_kernel
    aliases = {}
    if prev_of is not None:
        in_specs += [pl.BlockSpec(memory_space=pl.ANY), pl.BlockSpec(memory_space=pl.ANY)]
        args += [prev_of, prev_ob]
        aliases = {14: 0, 15: 1}
        kernel = functools.partial(_hgrn_alias_kernel, 14)
    st_shape = jax.ShapeDtypeStruct((n_seq, HG_HEADS, HG_DK, HG_DK), F32)
    return pl.pallas_call(
        kernel,
        grid=(n_seq, nc),
        in_specs=in_specs,
        out_specs=[
            pl.BlockSpec((c, w), fwd(0)),
            pl.BlockSpec((c, w), bwd(0)),
            pl.BlockSpec((1, HG_HEADS, HG_DK, HG_DK), state),
            pl.BlockSpec((1, HG_HEADS, HG_DK, HG_DK), state),
        ],
        out_shape=[jax.ShapeDtypeStruct((out_rows, w), F32),
                   jax.ShapeDtypeStruct((out_rows, w), F32), st_shape, st_shape],
        scratch_shapes=[pltpu.VMEM((HG_HEADS, HG_DK, HG_DK), F32),
                        pltpu.VMEM((HG_HEADS, HG_DK, HG_DK), F32)],
        input_output_aliases=aliases,
        compiler_params=_cparams(2),
        name="hgrn2",
    )(*args)


def _hgrn_alias_kernel(n_in, *refs):
    return _hgrn_kernel(*refs[:n_in], *refs[n_in + 2:])


def kernel(x, c, ctx, c_ctx, w_mod, b_mod, norm1_w, norm2_w, w_in, hy_conv_w, hy_conv_b,
           hy_f_w1, hy_f_b1, hy_f_w2, hy_f_b2, hy_f_w3, hy_f_freq, hy_decay, hy_skip,
           hg_lower, hg_norm_w, w_br_hy, w_br_fn, w_br_hg, w_out,
           ffn_w1, ffn_w3, ffn_w2, moe_router, moe_w1, moe_w3, moe_w2, final_norm_w):
    batch, seq, d = x.shape
    l_ctx = ctx.shape[1]
    depth = w_in.shape[0]
    n_lat = batch * seq
    n_all = n_lat + batch * l_ctx

    lb_all = jnp.cumsum(jax.nn.softmax(hg_lower.astype(F32), axis=1), axis=1)
    lb_all = lb_all - lb_all[:, :1]

    cc = jnp.zeros((8, d), F32).at[:batch].set(c).at[batch].set(c_ctx)
    mods = _modulation(cc, w_mod, b_mod).reshape(depth, 8, 6, d)

    w1p = jnp.zeros((depth, LANE, HY_FILTER_HIDDEN), F32).at[:, :HY_POS_DIM].set(hy_f_w1)
    dft_lat = _hyena_dft(seq)
    dft_ctx = _hyena_dft(l_ctx)
    kf_lat = _hyena_filters(seq, w1p, hy_f_b1, hy_f_w2, hy_f_b2, hy_f_w3, hy_f_freq, hy_decay,
                            dft_lat[0], dft_lat[1])
    kf_ctx = _hyena_filters(l_ctx, w1p, hy_f_b1, hy_f_w2, hy_f_b2, hy_f_w3, hy_f_freq, hy_decay,
                            dft_ctx[0], dft_ctx[1])
    fn_lat = _fnet_dft(seq)
    fn_ctx = _fnet_dft(l_ctx)

    rows_x = jnp.concatenate([x.reshape(n_lat, d), ctx.reshape(batch * l_ctx, d)], axis=0)
    zero_state = jnp.zeros((batch, HG_HEADS, HG_DK, HG_DK), F32)
    tile_cap = math.gcd(seq, batch * l_ctx)
    tok = dict(seq=seq, batch=batch)
    tile = lambda t: dict(tm=min(t, tile_cap))
    lat = dict(seq_len=seq, n_seq=batch, row0=0, out_rows=n_all)
    cx = dict(seq_len=l_ctx, n_seq=batch, row0=n_lat, out_rows=n_all)

    for l in range(depth):
        last = l == depth - 1
        rows = n_lat if last else n_all
        m = mods[l]
        w_main = jnp.concatenate([w_in[l][:, GATE_OFF:], w_in[l][:, :HG_FF],
                                  w_in[l][:, HG_I:GATE_OFF]], axis=1).astype(BF16)
        w_forget = w_in[l][:, HG_FF:HG_I].astype(BF16)

        zm = _in_proj(rows_x, norm1_w[l], m, w_main, BF16, rows=n_all, **tok, **tile(1024))
        zf = _in_proj(rows_x, norm1_w[l], m, w_forget, F32, rows=n_all, **tok, **tile(1024))

        of, ob, s_f, s_b = _hgrn(zm, zf, lb_all[0, l], lb_all[1, l], zero_state, zero_state, **cx)
        of, ob, _, _ = _hgrn(zm, zf, lb_all[0, l], lb_all[1, l], s_f, s_b, prev_of=of,
                             prev_ob=ob, **lat)

        hy_args = (hy_conv_w[l], hy_conv_b[l])
        u = _hyena_longconv(zm, 0, zm, 1, *hy_args, kf_lat, l, 0, hy_skip[l], dft_lat, **lat)
        y_hy = _hyena_longconv(u, None, zm, 2, *hy_args, kf_lat, l, 1, hy_skip[l], dft_lat, **lat)
        y_fn = _fnet(zm, fn_lat, **lat)
        if not last:
            u = _hyena_longconv(zm, 0, zm, 1, *hy_args, kf_ctx, l, 0, hy_skip[l], dft_ctx,
                                prev_out=u, **cx)
            y_hy = _hyena_longconv(u, None, zm, 2, *hy_args, kf_ctx, l, 1, hy_skip[l], dft_ctx,
                                   prev_out=y_hy, **cx)
            y_fn = _fnet(zm, fn_ctx, prev_out=y_fn, **cx)

        rows_x = _merge_out(y_hy, y_fn, of, ob, zm, rows_x, hg_norm_w[l], m,
                            w_br_hy[l].astype(BF16), w_br_fn[l].astype(BF16),
                            w_br_hg[l].astype(BF16), w_out[l].astype(BF16), rows=rows,
                            **tok, **tile(256))

        if l % 2 == 0:
            i = l // 2
            rows_x = _ffn_dense(rows_x, norm2_w[l], m, ffn_w1[i].astype(BF16),
                                ffn_w3[i].astype(BF16), ffn_w2[i].astype(BF16), rows=rows,
                                **tok, **tile(512))
        else:
            i = l // 2
            rows_x = _moe_dense(rows_x, norm2_w[l], m, moe_router[i], moe_w1[i].astype(BF16),
                                moe_w3[i].astype(BF16), moe_w2[i].astype(BF16), rows=rows,
                                **tok, **tile(512))

    out = _final_norm(rows_x, final_norm_w, rows=n_lat, **tile(512))
    return out.reshape(batch, seq, d)
```

```python
import functools
import math

import numpy as np
import jax
import jax.numpy as jnp
from jax import lax
from jax.experimental import pallas as pl
from jax.experimental.pallas import tpu as pltpu

F32 = jnp.float32
BF16 = jnp.bfloat16
NORM_EPS = 1e-6

D_MODEL = 2048
DEPTH = 4
HY_W = 1024
HY_BANDS = 16
HY_POS_DIM = 2 * HY_BANDS + 1
HY_FILTER_HIDDEN = 64
FN_W = 1024
FN_GROUP_W = 128
HG_W = 1024
HG_HEADS = 8
HG_DK = 128
HG_CHUNK = 64
HG_LEVELS = 6
N_EXPERTS = 8
LANE = 128

HG_FF = 3 * HY_W + FN_W + 2 * HG_W
HG_I = HG_FF + 2 * HG_W
GATE_OFF = HG_I + HG_W
ZM_GATE = 0
ZM_HY = ZM_GATE + 3 * D_MODEL
ZM_FN = ZM_HY + 3 * HY_W
ZM_Q = ZM_FN + FN_W
ZM_G = ZM_Q + HG_W
ZM_I = ZM_G + HG_W
ZM_COLS = ZM_I + HG_W

VMEM_LIMIT = 56 * 1024 * 1024


def _cparams(n_axes, vmem=VMEM_LIMIT):
    return pltpu.CompilerParams(dimension_semantics=("arbitrary",) * n_axes, vmem_limit_bytes=vmem)


def _resident(shape, index_map):
    return pl.BlockSpec(shape, index_map, pipeline_mode=pl.Buffered(1))


def _silu(v):
    return v * jax.nn.sigmoid(v)


def _bdot(a, b):
    return jnp.dot(a, b, preferred_element_type=F32)


def _mod_kernel(c_ref, w_ref, b_ref, o_ref):
    c = _silu(c_ref[...])
    o_ref[0] = jnp.dot(c, w_ref[0], preferred_element_type=F32,
                       precision=lax.Precision.HIGHEST) + b_ref[0]


def _modulation(cc, w_mod, b_mod):
    depth, d, n6 = w_mod.shape
    tn = 2048
    return pl.pallas_call(
        _mod_kernel,
        grid=(depth, n6 // tn),
        in_specs=[
            pl.BlockSpec((8, d), lambda l, n: (0, 0)),
            pl.BlockSpec((1, d, tn), lambda l, n: (l, 0, n)),
            pl.BlockSpec((1, 1, tn), lambda l, n: (l, 0, n)),
        ],
        out_specs=pl.BlockSpec((1, 8, tn), lambda l, n: (l, 0, n)),
        out_shape=jax.ShapeDtypeStruct((depth, 8, n6), F32),
        compiler_params=_cparams(2),
        name="modulation",
    )(cc, w_mod, b_mod.reshape(depth, 1, n6))


def _norm_mod(x, nw, shift, scale):
    ms = jnp.mean(x * x, axis=-1, keepdims=True)
    return (x * lax.rsqrt(ms + NORM_EPS) * nw) * (1 + scale) + shift


def _group_of(row_block, tm, seq, batch):
    return jnp.minimum((row_block * tm) // seq, batch)


def _inproj_kernel(x_ref, nw_ref, mod_ref, w_ref, o_ref, h_scr):
    @pl.when(pl.program_id(1) == 0)
    def _():
        h = _norm_mod(x_ref[...], nw_ref[...], mod_ref[0, 0:1, :], mod_ref[0, 1:2, :])
        h_scr[...] = h.astype(BF16)

    o_ref[...] = _bdot(h_scr[...], w_ref[...]).astype(o_ref.dtype)


def _in_proj(x, nw, mods, w, out_dtype, *, rows, seq, batch, tm=1024, tn=1024):
    d = x.shape[1]
    n = w.shape[1]
    tm = min(tm, rows)
    return pl.pallas_call(
        _inproj_kernel,
        grid=(rows // tm, n // tn),
        in_specs=[
            pl.BlockSpec((tm, d), lambda m, j: (m, 0)),
            pl.BlockSpec((1, d), lambda m, j: (0, 0)),
            pl.BlockSpec((1, 6, d), lambda m, j: (_group_of(m, tm, seq, batch), 0, 0)),
            pl.BlockSpec((d, tn), lambda m, j: (0, j)),
        ],
        out_specs=pl.BlockSpec((tm, tn), lambda m, j: (m, j)),
        out_shape=jax.ShapeDtypeStruct((x.shape[0], n), out_dtype),
        scratch_shapes=[pltpu.VMEM((tm, d), BF16)],
        compiler_params=_cparams(2),
        name="in_proj",
    )(x, nw.reshape(1, d), mods, w)


def _merge_kernel(yhy_ref, yfn_ref, of_ref, ob_ref, g_ref, gate_ref, x_ref, hnw_ref, mod_ref,
                  wbhy_ref, wbfn_ref, wbhg_ref, wo_ref, o_ref):
    d = x_ref.shape[1]
    ohg = of_ref[...] + ob_ref[...]
    nw = hnw_ref[...]
    heads = []
    for h in range(ohg.shape[1] // HG_DK):
        oh = ohg[:, h * HG_DK:(h + 1) * HG_DK]
        ms = jnp.mean(oh * oh, axis=-1, keepdims=True)
        heads.append(oh * lax.rsqrt(ms + NORM_EPS) * nw)
    y_hg = jnp.concatenate(heads, axis=-1) * _silu(g_ref[...].astype(F32))
    gate = lambda i: jax.nn.sigmoid(gate_ref[:, i * d:(i + 1) * d].astype(F32))
    merged = gate(0) * _bdot(yhy_ref[...], wbhy_ref[...])
    merged += gate(1) * _bdot(yfn_ref[...], wbfn_ref[...])
    merged += gate(2) * _bdot(y_hg.astype(BF16), wbhg_ref[...])
    out = _bdot(merged.astype(BF16), wo_ref[...])
    o_ref[...] = x_ref[...] + mod_ref[0, 2:3, :] * out


def _merge_out(y_hy, y_fn, o_f, o_b, zm, x, hg_nw, mods, wb_hy, wb_fn, wb_hg, w_o,
               *, rows, seq, batch, tm=256):
    d = x.shape[1]
    w = y_hy.shape[1]
    tm = min(tm, rows)
    row = lambda m: (m, 0)
    const = lambda m: (0, 0)
    return pl.pallas_call(
        _merge_kernel,
        grid=(rows // tm,),
        in_specs=[
            pl.BlockSpec((tm, w), row),
            pl.BlockSpec((tm, w), row),
            pl.BlockSpec((tm, w), row),
            pl.BlockSpec((tm, w), row),
            pl.BlockSpec((tm, w), lambda m: (m, ZM_G // w)),
            pl.BlockSpec((tm, 3 * d), lambda m: (m, ZM_GATE // (3 * d))),
            pl.BlockSpec((tm, d), row),
            pl.BlockSpec((1, HG_DK), const),
            pl.BlockSpec((1, 6, d), lambda m: (_group_of(m, tm, seq, batch), 0, 0)),
            _resident((w, d), const),
            _resident((w, d), const),
            _resident((w, d), const),
            _resident((d, d), const),
        ],
        out_specs=pl.BlockSpec((tm, d), row),
        out_shape=jax.ShapeDtypeStruct((rows, d), F32),
        compiler_params=_cparams(1),
        name="merge_out",
    )(y_hy, y_fn, o_f, o_b, zm, zm, x, hg_nw.reshape(1, HG_DK), mods, wb_hy, wb_fn, wb_hg, w_o)


def _ffn_kernel(x_ref, nw_ref, mod_ref, w1_ref, w3_ref, w2_ref, o_ref, h_scr, acc_scr):
    f = pl.program_id(1)

    @pl.when(f == 0)
    def _():
        h = _norm_mod(x_ref[...], nw_ref[...], mod_ref[0, 3:4, :], mod_ref[0, 4:5, :])
        h_scr[...] = h.astype(BF16)
        acc_scr[...] = jnp.zeros_like(acc_scr)

    h = h_scr[...]
    act = _silu(_bdot(h, w1_ref[...])) * _bdot(h, w3_ref[...])
    acc_scr[...] += _bdot(act.astype(BF16), w2_ref[...])

    @pl.when(f == pl.num_programs(1) - 1)
    def _():
        o_ref[...] = x_ref[...] + mod_ref[0, 5:6, :] * acc_scr[...]


def _ffn_dense(x, nw, mods, w1, w3, w2, *, rows, seq, batch, tm=512, tf=512):
    d = x.shape[1]
    ff = w1.shape[1]
    tm = min(tm, rows)
    return pl.pallas_call(
        _ffn_kernel,
        grid=(rows // tm, ff // tf),
        in_specs=[
            pl.BlockSpec((tm, d), lambda m, f: (m, 0)),
            pl.BlockSpec((1, d), lambda m, f: (0, 0)),
            pl.BlockSpec((1, 6, d), lambda m, f: (_group_of(m, tm, seq, batch), 0, 0)),
            pl.BlockSpec((d, tf), lambda m, f: (0, f)),
            pl.BlockSpec((d, tf), lambda m, f: (0, f)),
            pl.BlockSpec((tf, d), lambda m, f: (f, 0)),
        ],
        out_specs=pl.BlockSpec((tm, d), lambda m, f: (m, 0)),
        out_shape=jax.ShapeDtypeStruct((rows, d), F32),
        scratch_shapes=[pltpu.VMEM((tm, d), BF16), pltpu.VMEM((tm, d), F32)],
        compiler_params=_cparams(2),
        name="ffn_dense",
    )(x, nw.reshape(1, d), mods, w1, w3, w2)


def _top2(logits, n_experts):
    lane = lax.broadcasted_iota(jnp.int32, logits.shape, 1).astype(F32)
    neg = jnp.float32(-jnp.inf)
    lg = jnp.where(lane < n_experts, logits, neg)
    m1 = jnp.max(lg, axis=-1, keepdims=True)
    i1 = jnp.min(jnp.where(lg == m1, lane, float(LANE)), axis=-1, keepdims=True)
    lg2 = jnp.where(lane == i1, neg, lg)
    m2 = jnp.max(lg2, axis=-1, keepdims=True)
    i2 = jnp.min(jnp.where(lg2 == m2, lane, float(LANE)), axis=-1, keepdims=True)
    e2 = jnp.exp(m2 - m1)
    return i1, i2, 1.0 / (1.0 + e2), e2 / (1.0 + e2)


def _route_kernel(x_ref, nw_ref, mod_ref, r_ref, h_ref, route_ref):
    h = _norm_mod(x_ref[...], nw_ref[...], mod_ref[0, 3:4, :], mod_ref[0, 4:5, :])
    logits = jnp.dot(h, r_ref[...], preferred_element_type=F32, precision=lax.Precision.HIGHEST)
    i1, i2, p1, p2 = _top2(logits, N_EXPERTS)
    lane = lax.broadcasted_iota(jnp.int32, logits.shape, 1)
    route = jnp.where(lane == 0, i1, jnp.where(lane == 1, i2,
                      jnp.where(lane == 2, p1, jnp.where(lane == 3, p2, 0.0))))
    h_ref[...] = h
    route_ref[...] = route


def _moe_route(x, nw, mods, router, *, rows, seq, batch, tm=512):
    d = x.shape[1]
    rpad = jnp.zeros((d, LANE), F32).at[:, :router.shape[1]].set(router)
    return pl.pallas_call(
        _route_kernel,
        grid=(rows // tm,),
        in_specs=[
            pl.BlockSpec((tm, d), lambda m: (m, 0)),
            pl.BlockSpec((1, d), lambda m: (0, 0)),
            pl.BlockSpec((1, 6, d), lambda m: (_group_of(m, tm, seq, batch), 0, 0)),
            pl.BlockSpec((d, LANE), lambda m: (0, 0)),
        ],
        out_specs=[pl.BlockSpec((tm, d), lambda m: (m, 0)),
                   pl.BlockSpec((tm, LANE), lambda m: (m, 0))],
        out_shape=[jax.ShapeDtypeStruct((rows, d), F32), jax.ShapeDtypeStruct((rows, LANE), F32)],
        compiler_params=_cparams(1),
        name="moe_route",
    )(x, nw.reshape(1, d), mods, rpad)


def _moe_plan(route, n_experts, tile):
    rows = route.shape[0]
    e = route[:, :2].astype(jnp.int32).reshape(-1)
    onehot = (e[:, None] == jnp.arange(n_experts, dtype=jnp.int32)[None, :]).astype(jnp.int32)
    before = jnp.cumsum(onehot, axis=0) - onehot
    rank = jnp.sum(before * onehot, axis=1)
    counts = jnp.sum(onehot, axis=0)
    padded = ((counts + tile - 1) // tile) * tile
    ends = jnp.cumsum(padded)
    starts = ends - padded
    pos = jnp.sum(starts[None, :] * onehot, axis=1) + rank
    n_tiles = (2 * rows) // tile + n_experts
    src = jnp.zeros((n_tiles * tile,), jnp.int32).at[pos].set(
        jnp.arange(2 * rows, dtype=jnp.int32) // 2)
    tile_start = jnp.arange(n_tiles, dtype=jnp.int32) * tile
    tile_expert = jnp.sum((tile_start[:, None] >= ends[None, :]).astype(jnp.int32), axis=1)
    tile_expert = jnp.minimum(tile_expert, n_experts - 1)
    n_used = (ends[-1] // tile).reshape(1)
    return pos, src, tile_expert, n_used


def _row_copy(src_hbm, src_row, dst, dst_row, sem):
    return pltpu.make_async_copy(src_hbm.at[pl.ds(src_row, 1)], dst.at[pl.ds(dst_row, 1)], sem)


def _gather_rows(idx_ref, base, stride, src_hbm, dst, sem, *, wait):
    def body(r, carry):
        copy = _row_copy(src_hbm, idx_ref[base + stride * r], dst, r, sem)
        copy.wait() if wait else copy.start()
        return carry

    lax.fori_loop(0, dst.shape[0], body, 0, unroll=8)


def _gather_kernel(src_ref, h_hbm, o_ref, sem):
    base = pl.program_id(0) * o_ref.shape[0]
    _gather_rows(src_ref, base, 1, h_hbm, o_ref, sem, wait=False)
    _gather_rows(src_ref, base, 1, h_hbm, o_ref, sem, wait=True)


def _moe_gather(h, src, *, tg=256):
    d = h.shape[1]
    n = src.shape[0]
    return pl.pallas_call(
        _gather_kernel,
        grid_spec=pltpu.PrefetchScalarGridSpec(
            num_scalar_prefetch=1,
            grid=(n // tg,),
            in_specs=[pl.BlockSpec(memory_space=pl.ANY)],
            out_specs=pl.BlockSpec((tg, d), lambda i, src: (i, 0)),
            scratch_shapes=[pltpu.SemaphoreType.DMA(())],
        ),
        out_shape=jax.ShapeDtypeStruct((n, d), F32),
        compiler_params=_cparams(1),
        name="moe_gather",
    )(src, h)


def _moe_ffn_kernel(te_ref, nu_ref, hs_ref, w1_ref, w3_ref, w2_ref, o_ref, h_scr, acc_scr):
    f = pl.program_id(1)
    last = pl.num_programs(1) - 1
    used = pl.program_id(0) < nu_ref[0]

    @pl.when(used & (f == 0))
    def _():
        h_scr[...] = hs_ref[...].astype(BF16)
        acc_scr[...] = jnp.zeros_like(acc_scr)

    @pl.when(used)
    def _():
        h = h_scr[...]
        act = _silu(_bdot(h, w1_ref[0])) * _bdot(h, w3_ref[0])
        acc_scr[...] += _bdot(act.astype(BF16), w2_ref[0])

    @pl.when(used & (f == last))
    def _():
        o_ref[...] = acc_scr[...]

    @pl.when(jnp.logical_not(used) & (f == last))
    def _():
        o_ref[...] = jnp.zeros_like(o_ref)


def _moe_ffn(hs, tile_expert, n_used, w1, w3, w2, *, tile, tf=512):
    n, d = hs.shape
    ff = w1.shape[2]
    nf = ff // tf
    fsel = lambda t, f, nu: jnp.where(t < nu[0], f, nf - 1)
    return pl.pallas_call(
        _moe_ffn_kernel,
        grid_spec=pltpu.PrefetchScalarGridSpec(
            num_scalar_prefetch=2,
            grid=(n // tile, nf),
            in_specs=[
                pl.BlockSpec((tile, d), lambda t, f, te, nu: (t, 0)),
                pl.BlockSpec((1, d, tf), lambda t, f, te, nu: (te[t], 0, fsel(t, f, nu))),
                pl.BlockSpec((1, d, tf), lambda t, f, te, nu: (te[t], 0, fsel(t, f, nu))),
                pl.BlockSpec((1, tf, d), lambda t, f, te, nu: (te[t], fsel(t, f, nu), 0)),
            ],
            out_specs=pl.BlockSpec((tile, d), lambda t, f, te, nu: (t, 0)),
            scratch_shapes=[pltpu.VMEM((tile, d), BF16), pltpu.VMEM((tile, d), F32)],
        ),
        out_shape=jax.ShapeDtypeStruct((n, d), F32),
        compiler_params=_cparams(2),
        name="moe_ffn",
    )(tile_expert, n_used, hs, w1, w3, w2)


def _combine_kernel(pos_ref, ys_hbm, x_ref, route_ref, mod_ref, o_ref, y1_scr, y2_scr, sem):
    base = 2 * pl.program_id(0) * x_ref.shape[0]
    _gather_rows(pos_ref, base, 2, ys_hbm, y1_scr, sem, wait=False)
    _gather_rows(pos_ref, base + 1, 2, ys_hbm, y2_scr, sem, wait=False)
    _gather_rows(pos_ref, base, 2, ys_hbm, y1_scr, sem, wait=True)
    _gather_rows(pos_ref, base + 1, 2, ys_hbm, y2_scr, sem, wait=True)
    mix = route_ref[:, 2:3] * y1_scr[...] + route_ref[:, 3:4] * y2_scr[...]
    o_ref[...] = x_ref[...] + mod_ref[0, 5:6, :] * mix


def _moe_combine(x, mods, route, ys, pos, *, rows, seq, batch, tm=256):
    d = x.shape[1]
    return pl.pallas_call(
        _combine_kernel,
        grid_spec=pltpu.PrefetchScalarGridSpec(
            num_scalar_prefetch=1,
            grid=(rows // tm,),
            in_specs=[
                pl.BlockSpec(memory_space=pl.ANY),
                pl.BlockSpec((tm, d), lambda m, pos: (m, 0)),
                pl.BlockSpec((tm, LANE), lambda m, pos: (m, 0)),
                pl.BlockSpec((1, 6, d), lambda m, pos: (_group_of(m, tm, seq, batch), 0, 0)),
            ],
            out_specs=pl.BlockSpec((tm, d), lambda m, pos: (m, 0)),
            scratch_shapes=[pltpu.VMEM((tm, d), F32), pltpu.VMEM((tm, d), F32),
                            pltpu.SemaphoreType.DMA(())],
        ),
        out_shape=jax.ShapeDtypeStruct((rows, d), F32),
        compiler_params=_cparams(1),
        name="moe_combine",
    )(pos, ys, x, route, mods)


def _moe_top2(x, nw, mods, router, w1, w3, w2, *, rows, seq, batch, tm, tile=512):
    tok = dict(rows=rows, seq=seq, batch=batch)
    h, route = _moe_route(x, nw, mods, router, tm=tm, **tok)
    pos, src, tile_expert, n_used = _moe_plan(route, w1.shape[0], tile)
    hs = _moe_gather(h, src)
    ys = _moe_ffn(hs, tile_expert, n_used, w1, w3, w2, tile=tile)
    return _moe_combine(x, mods, route, ys, pos, tm=min(tm, 256), **tok)


def _final_norm_kernel(x_ref, w_ref, o_ref):
    x = x_ref[...]
    ms = jnp.mean(x * x, axis=-1, keepdims=True)
    o_ref[...] = x * lax.rsqrt(ms + NORM_EPS) * w_ref[...]


def _final_norm(x, w, *, rows, tm=512):
    d = x.shape[1]
    return pl.pallas_call(
        _final_norm_kernel,
        grid=(rows // tm,),
        in_specs=[pl.BlockSpec((tm, d), lambda m: (m, 0)), pl.BlockSpec((1, d), lambda m: (0, 0))],
        out_specs=pl.BlockSpec((tm, d), lambda m: (m, 0)),
        out_shape=jax.ShapeDtypeStruct((rows, d), F32),
        compiler_params=_cparams(1),
        name="final_norm",
    )(x, w.reshape(1, d))


def _dft_angles(n, period):
    k = lax.broadcasted_iota(jnp.int32, (n, n), 0)
    s = lax.broadcasted_iota(jnp.int32, (n, n), 1)
    return ((k * s) % period).astype(F32) * (2.0 * math.pi / period)


def _hyena_dft(seq_len):
    n = 2 * seq_len
    ang = _dft_angles(seq_len, n)
    row = lax.broadcasted_iota(jnp.int32, ang.shape, 0)
    col = lax.broadcasted_iota(jnp.int32, ang.shape, 1)
    cos = jnp.cos(ang)
    sin = jnp.sin(ang)
    fc = cos
    fs = jnp.where(row == 0, jnp.where(col % 2 == 0, 1.0, -1.0), sin)
    ic = jnp.where(col == 0, 1.0 / n, 2.0 / n * cos)
    isn = jnp.where(col == 0, jnp.where(row % 2 == 0, 1.0, -1.0) / n, 2.0 / n * sin)
    return fc.astype(BF16), fs.astype(BF16), ic.astype(BF16), isn.astype(BF16)


def _hyena_feats(seq_len):
    pos = np.arange(seq_len, dtype=np.float64)
    t = pos / max(seq_len - 1, 1)
    bands = np.linspace(1e-4, HY_BANDS - 1, HY_BANDS).astype(np.float32).astype(np.float64)
    ang = 2 * math.pi * pos[:, None] * bands[None, :] / seq_len
    feats = np.concatenate([t[:, None], np.cos(ang), -np.sin(ang)], axis=-1)
    out = np.zeros((seq_len, LANE), np.float32)
    out[:, :HY_POS_DIM] = feats
    return jnp.asarray(out)


def _split_dot(w, a):
    hi = a.astype(BF16)
    lo = (a - hi.astype(F32)).astype(BF16)
    return _bdot(w, hi) + _bdot(w, lo)


def _filter_kernel(feat_ref, w1_ref, b1_ref, w2_ref, b2_ref, fr_ref, w3f_ref, w3b_ref,
                   dcf_ref, dcb_ref, fc_ref, fs_ref, o_ref):
    hp = lax.Precision.HIGHEST
    feats = feat_ref[...]
    h = jnp.sin(fr_ref[0, 0:1, :] * (jnp.dot(feats, w1_ref[0], preferred_element_type=F32,
                                             precision=hp) + b1_ref[0]))
    h = jnp.sin(fr_ref[0, 1:2, :] * (jnp.dot(h, w2_ref[0], preferred_element_type=F32,
                                             precision=hp) + b2_ref[0]))
    t = feats[:, 0:1]
    hf = jnp.dot(h, w3f_ref[0], preferred_element_type=F32, precision=hp) * jnp.exp(-t * dcf_ref[0])
    hb = jnp.dot(h, w3b_ref[0], preferred_element_type=F32, precision=hp) * jnp.exp(-t * dcb_ref[0])
    row = lax.broadcasted_iota(jnp.int32, hf.shape, 0)
    hb = jnp.where(row == 0, 0.0, hb)
    ss = jnp.sum(hf * hf + hb * hb, axis=0, keepdims=True)
    scale = lax.rsqrt(ss + NORM_EPS)
    even = hf * scale + hb * scale
    odd = hf * scale - hb * scale
    kc = _split_dot(fc_ref[...], even)
    ks = _split_dot(fs_ref[...], odd)
    sign = jnp.where(row % 2 == 0, 1.0, -1.0)
    nyq = jnp.sum(even * sign, axis=0, keepdims=True)
    o_ref[0, 0] = kc
    o_ref[0, 1] = jnp.where(row == 0, nyq, ks)


def _hyena_filters(seq_len, w1p, b1, w2, b2, w3, freq, decay, fc, fs, *, tn=256):
    depth = w1p.shape[0]
    hid = HY_FILTER_HIDDEN
    nb = HY_W // tn
    feats = _hyena_feats(seq_len)
    col = lambda d: (lambda l, o, c: (l, 0, (2 * o + d) * nb + c))
    lyr = lambda l, o, c: (l, 0, 0)
    return pl.pallas_call(
        _filter_kernel,
        grid=(depth, 2, nb),
        in_specs=[
            pl.BlockSpec((seq_len, LANE), lambda l, o, c: (0, 0)),
            pl.BlockSpec((1, LANE, hid), lyr),
            pl.BlockSpec((1, 1, hid), lyr),
            pl.BlockSpec((1, hid, hid), lyr),
            pl.BlockSpec((1, 1, hid), lyr),
            pl.BlockSpec((1, 2, hid), lyr),
            pl.BlockSpec((1, hid, tn), col(0)),
            pl.BlockSpec((1, hid, tn), col(1)),
            pl.BlockSpec((1, 1, tn), col(0)),
            pl.BlockSpec((1, 1, tn), col(1)),
            _resident((seq_len, seq_len), lambda l, o, c: (0, 0)),
            _resident((seq_len, seq_len), lambda l, o, c: (0, 0)),
        ],
        out_specs=pl.BlockSpec((1, 2, seq_len, tn), lambda l, o, c: (l, 0, 0, o * nb + c)),
        out_shape=jax.ShapeDtypeStruct((depth, 2, seq_len, 2 * HY_W), F32),
        compiler_params=_cparams(3),
        name="hyena_filters",
    )(feats, w1p, b1.reshape(depth, 1, hid), w2, b2.reshape(depth, 1, hid), freq,
      w3, w3, decay.reshape(depth, 1, 4 * HY_W), decay.reshape(depth, 1, 4 * HY_W), fc, fs)


def _conv3(z, w_ref, b_ref):
    n = z.shape[0]
    row = lax.broadcasted_iota(jnp.int32, z.shape, 0)
    prev = jnp.where(row == 0, 0.0, pltpu.roll(z, 1, 0))
    nxt = jnp.where(row == n - 1, 0.0, pltpu.roll(z, n - 1, 0))
    return prev * w_ref[0:1, :] + z * w_ref[1:2, :] + nxt * w_ref[2:3, :] + b_ref[...]


def _longconv_kernel(a_ref, g_ref, aw_ref, ab_ref, gw_ref, gb_ref, kf_ref, skip_ref,
                     fc_ref, fs_ref, ic_ref, is_ref, o_ref, acc_scr, *, conv_a, fblk):
    a = a_ref[...].astype(F32)
    if conv_a:
        a = _conv3(a, aw_ref, ab_ref)
    a16 = a.astype(BF16)
    n = a.shape[0]
    for j in range(n // fblk):
        rows = slice(j * fblk, (j + 1) * fblk)
        ac = _bdot(fc_ref[rows, :], a16)
        asn = _bdot(fs_ref[rows, :], a16)
        kc = kf_ref[0, 0, rows, :]
        ks = kf_ref[0, 1, rows, :]
        cross = asn * ks
        yc = ac * kc - cross
        ys = ac * ks + asn * kc
        if j == 0:
            row = lax.broadcasted_iota(jnp.int32, ac.shape, 0)
            yc = jnp.where(row == 0, ac * kc, yc)
            ys = jnp.where(row == 0, cross, ys)
        part = _bdot(ic_ref[:, rows], yc.astype(BF16)) + _bdot(is_ref[:, rows], ys.astype(BF16))
        if j == 0:
            acc_scr[...] = part
        else:
            acc_scr[...] += part
    gate = _conv3(g_ref[...].astype(F32), gw_ref, gb_ref)
    o_ref[...] = (gate * (acc_scr[...] + a * skip_ref[0])).astype(o_ref.dtype)


def _hyena_longconv(a_src, a_part, zm, gate_part, conv_w, conv_b, kf, layer, order, skip,
                    dft, *, seq_len, n_seq, row0, out_rows, prev_out=None, tn=256):
    fc, fs, ic, isn = dft
    nb = HY_W // tn
    rb0 = row0 // seq_len
    fblk = min(512, seq_len)
    const = lambda c, b: (0, 0)
    conv_a = a_part is not None
    aw_blk = a_part * nb if conv_a else 0
    a_blk = ZM_HY // tn + aw_blk if conv_a else 0
    gw_blk = gate_part * nb
    g_blk = ZM_HY // tn + gw_blk
    kernel = functools.partial(_longconv_kernel, conv_a=conv_a, fblk=fblk)
    in_specs = [
        pl.BlockSpec((seq_len, tn), lambda c, b: (rb0 + b, a_blk + c)),
        pl.BlockSpec((seq_len, tn), lambda c, b: (rb0 + b, g_blk + c)),
        pl.BlockSpec((3, tn), lambda c, b: (0, aw_blk + c)),
        pl.BlockSpec((1, tn), lambda c, b: (0, aw_blk + c)),
        pl.BlockSpec((3, tn), lambda c, b: (0, gw_blk + c)),
        pl.BlockSpec((1, tn), lambda c, b: (0, gw_blk + c)),
        pl.BlockSpec((1, 2, seq_len, tn), lambda c, b: (layer, 0, 0, order * nb + c)),
        pl.BlockSpec((1, 1, tn), lambda c, b: (order, 0, c)),
        _resident((seq_len, seq_len), const),
        _resident((seq_len, seq_len), const),
        _resident((seq_len, seq_len), const),
        _resident((seq_len, seq_len), const),
    ]
    args = [a_src, zm, conv_w, conv_b.reshape(1, -1), conv_w, conv_b.reshape(1, -1), kf,
            skip.reshape(2, 1, HY_W), fc, fs, ic, isn]
    aliases = {}
    if prev_out is not None:
        in_specs.append(pl.BlockSpec(memory_space=pl.ANY))
        args.append(prev_out)
        aliases = {len(args) - 1: 0}
        kernel = functools.partial(_drop_last_input, kernel, 12)
    return pl.pallas_call(
        kernel,
        grid=(nb, n_seq),
        in_specs=in_specs,
        out_specs=pl.BlockSpec((seq_len, tn), lambda c, b: (rb0 + b, c)),
        out_shape=jax.ShapeDtypeStruct((out_rows, HY_W), BF16),
        scratch_shapes=[pltpu.VMEM((seq_len, tn), F32)],
        input_output_aliases=aliases,
        compiler_params=_cparams(2),
        name="hyena_longconv",
    )(*args)


def _drop_last_input(kernel, n_in, *refs):
    return kernel(*refs[:n_in], *refs[n_in + 1:])


def _fnet_dft(seq_len):
    ang_l = _dft_angles(seq_len, seq_len)
    ang_w = _dft_angles(FN_GROUP_W, FN_GROUP_W)
    scale = 1.0 / math.sqrt(seq_len * FN_GROUP_W)
    return ((jnp.cos(ang_l) * scale).astype(BF16), (jnp.sin(ang_l) * scale).astype(BF16),
            jnp.cos(ang_w).astype(BF16), jnp.sin(ang_w).astype(BF16))


def _fnet_kernel(z_ref, cl_ref, sl_ref, cw_ref, sw_ref, o_ref):
    z = z_ref[...]
    gc, gs = [], []
    for j in range(z.shape[1] // FN_GROUP_W):
        zj = z[:, j * FN_GROUP_W:(j + 1) * FN_GROUP_W]
        gc.append(_bdot(zj, cw_ref[...]).astype(BF16))
        gs.append(_bdot(zj, sw_ref[...]).astype(BF16))
    gc = jnp.concatenate(gc, axis=-1)
    gs = jnp.concatenate(gs, axis=-1)
    o_ref[...] = (_bdot(cl_ref[...], gc) - _bdot(sl_ref[...], gs)).astype(o_ref.dtype)


def _fnet(zm, dft, *, seq_len, n_seq, row0, out_rows, prev_out=None, tn=256):
    cl, sl, cw, sw = dft
    nb = FN_W // tn
    rb0 = row0 // seq_len
    c0 = ZM_FN // tn
    const = lambda c, b: (0, 0)
    in_specs = [
        pl.BlockSpec((seq_len, tn), lambda c, b: (rb0 + b, c0 + c)),
        _resident((seq_len, seq_len), const),
        _resident((seq_len, seq_len), const),
        pl.BlockSpec((FN_GROUP_W, FN_GROUP_W), const),
        pl.BlockSpec((FN_GROUP_W, FN_GROUP_W), const),
    ]
    args = [zm, cl, sl, cw, sw]
    kernel = _fnet_kernel
    aliases = {}
    if prev_out is not None:
        in_specs.append(pl.BlockSpec(memory_space=pl.ANY))
        args.append(prev_out)
        aliases = {len(args) - 1: 0}
        kernel = functools.partial(_drop_last_input, _fnet_kernel, 5)
    return pl.pallas_call(
        kernel,
        grid=(nb, n_seq),
        in_specs=in_specs,
        out_specs=pl.BlockSpec((seq_len, tn), lambda c, b: (rb0 + b, c)),
        out_shape=jax.ShapeDtypeStruct((out_rows, FN_W), BF16),
        input_output_aliases=aliases,
        compiler_params=_cparams(2),
        name="fnet",
    )(*args)


def _hgrn_tables(reverse):
    c = HG_CHUNK
    tri = np.zeros((c, c), np.float32)
    for t in range(c):
        if reverse:
            tri[t, t:] = 1.0
        else:
            tri[t, :t + 1] = 1.0
    mats = [tri]
    masks = []
    for lvl in range(HG_LEVELS):
        half = (c // 2) >> lvl
        sel = np.zeros((c, c), np.float32)
        mask = np.zeros((c, c), np.float32)
        for t in range(c):
            start = (t // (2 * half)) * 2 * half
            mid = start + half
            sel[t, mid if reverse else mid - 1] = 1.0
            for s in range(start, start + 2 * half):
                if reverse and t < mid <= s:
                    mask[t, s] = 1.0
                if (not reverse) and s < mid <= t:
                    mask[t, s] = 1.0
        mats.append(tri - sel @ tri)
        masks.append(mask)
    return np.concatenate(mats, axis=0), np.stack(masks, axis=0)


def _hgrn_direction(q_ref, v_ref, zf_ref, lb_ref, tab_ref, mask_ref, st_scr, o_ref, end_row):
    c = HG_CHUNK
    lb = lb_ref[...]
    f = lb + (1.0 - lb) * jax.nn.sigmoid(zf_ref[...])
    lf = jnp.log(f)
    kk = 1.0 - f
    p1 = lf.astype(BF16)
    r1 = lf - p1.astype(F32)
    p2 = r1.astype(BF16)
    p3 = (r1 - p2.astype(F32)).astype(BF16)
    tab = tab_ref[...]
    dec = _bdot(tab, p1) + _bdot(tab, p2) + _bdot(tab, p3)
    q = q_ref[...].astype(F32)
    v = v_ref[...]
    for h in range(HG_HEADS):
        cols = slice(h * HG_DK, (h + 1) * HG_DK)
        qh = q[:, cols]
        kh = kk[:, cols]
        vh = v[:, cols]
        b = dec[0:c, cols]
        st = st_scr[h]
        qd = (qh * jnp.exp(b)).astype(BF16)
        o = lax.dot_general(qd, st.astype(BF16), (((1,), (1,)), ((), ())),
                            preferred_element_type=F32)
        o += jnp.sum(qh * kh, axis=-1, keepdims=True) * vh.astype(F32)
        att = jnp.zeros((c, c), F32)
        for lvl in range(HG_LEVELS):
            d = dec[(lvl + 1) * c:(lvl + 2) * c, cols]
            ql = (qh * jnp.exp(jnp.minimum(d, 0.0))).astype(BF16)
            kl = (kh * jnp.exp(jnp.minimum(-d, 0.0))).astype(BF16)
            a = lax.dot_general(ql, kl, (((1,), (1,)), ((), ())), preferred_element_type=F32)
            att += jnp.where(mask_ref[lvl] > 0.5, a, 0.0)
        o += _bdot(att.astype(BF16), vh)
        o_ref[:, cols] = o
        b_end = b[end_row:end_row + 1, :]
        kd = (kh * jnp.exp(b_end - b)).astype(BF16)
        upd = lax.dot_general(vh, kd, (((0,), (0,)), ((), ())), preferred_element_type=F32)
        st_scr[h] = jnp.exp(b_end) * st + upd


def _hgrn_kernel(qf_ref, vf_ref, zff_ref, qb_ref, vb_ref, zfb_ref, lbf_ref, lbb_ref,
                 tabf_ref, maskf_ref, tabb_ref, maskb_ref, s0f_ref, s0b_ref,
                 of_ref, ob_ref, sf_ref, sb_ref, stf_scr, stb_scr):
    ci = pl.program_id(1)

    @pl.when(ci == 0)
    def _():
        stf_scr[...] = s0f_ref[0]
        stb_scr[...] = s0b_ref[0]

    _hgrn_direction(qf_ref, vf_ref, zff_ref, lbf_ref, tabf_ref, maskf_ref, stf_scr, of_ref,
                    HG_CHUNK - 1)
    _hgrn_direction(qb_ref, vb_ref, zfb_ref, lbb_ref, tabb_ref, maskb_ref, stb_scr, ob_ref, 0)

    @pl.when(ci == pl.num_programs(1) - 1)
    def _():
        sf_ref[0] = stf_scr[...]
        sb_ref[0] = stb_scr[...]


def _hgrn(zm, zf, lb_f, lb_b, s0_f, s0_b, *, seq_len, n_seq, row0, out_rows,
          prev_of=None, prev_ob=None):
    c = HG_CHUNK
    nc = seq_len // c
    rb0 = row0 // c
    w = HG_W
    tab_f, mask_f = _hgrn_tables(False)
    tab_b, mask_b = _hgrn_tables(True)
    fwd = lambda col: (lambda b, i: (rb0 + b * nc + i, col))
    bwd = lambda col: (lambda b, i: (rb0 + b * nc + nc - 1 - i, col))
    const2 = lambda b, i: (0, 0)
    const3 = lambda b, i: (0, 0, 0)
    state = lambda b, i: (b, 0, 0, 0)
    nt = (1 + HG_LEVELS) * c
    in_specs = [
        pl.BlockSpec((c, w), fwd(ZM_Q // w)),
        pl.BlockSpec((c, w), fwd(ZM_I // w)),
        pl.BlockSpec((c, w), fwd(0)),
        pl.BlockSpec((c, w), bwd(ZM_Q // w)),
        pl.BlockSpec((c, w), bwd(ZM_I // w)),
        pl.BlockSpec((c, w), bwd(1)),
        pl.BlockSpec((1, w), const2),
        pl.BlockSpec((1, w), const2),
        pl.BlockSpec((nt, c), const2),
        pl.BlockSpec((HG_LEVELS, c, c), const3),
        pl.BlockSpec((nt, c), const2),
        pl.BlockSpec((HG_LEVELS, c, c), const3),
        pl.BlockSpec((1, HG_HEADS, HG_DK, HG_DK), state),
        pl.BlockSpec((1, HG_HEADS, HG_DK, HG_DK), state),
    ]
    args = [zm, zm, zf, zm, zm, zf, lb_f.reshape(1, w), lb_b.reshape(1, w),
            jnp.asarray(tab_f, BF16), jnp.asarray(mask_f), jnp.asarray(tab_b, BF16),
            jnp.asarray(mask_b), s0_f, s0_b]
    kernel = _hgrn_kernel
    aliases = {}
    if prev_of is not None:
        in_specs += [pl.BlockSpec(memory_space=pl.ANY), pl.BlockSpec(memory_space=pl.ANY)]
        args += [prev_of, prev_ob]
        aliases = {14: 0, 15: 1}
        kernel = functools.partial(_hgrn_alias_kernel, 14)
    st_shape = jax.ShapeDtypeStruct((n_seq, HG_HEADS, HG_DK, HG_DK), F32)
    return pl.pallas_call(
        kernel,
        grid=(n_seq, nc),
        in_specs=in_specs,
        out_specs=[
            pl.BlockSpec((c, w), fwd(0)),
            pl.BlockSpec((c, w), bwd(0)),
            pl.BlockSpec((1, HG_HEADS, HG_DK, HG_DK), state),
            pl.BlockSpec((1, HG_HEADS, HG_DK, HG_DK), state),
        ],
        out_shape=[jax.ShapeDtypeStruct((out_rows, w), F32),
                   jax.ShapeDtypeStruct((out_rows, w), F32), st_shape, st_shape],
        scratch_shapes=[pltpu.VMEM((HG_HEADS, HG_DK, HG_DK), F32),
                        pltpu.VMEM((HG_HEADS, HG_DK, HG_DK), F32)],
        input_output_aliases=aliases,
        compiler_params=_cparams(2),
        name="hgrn2",
    )(*args)


def _hgrn_alias_kernel(n_in, *refs):
    return _hgrn_kernel(*refs[:n_in], *refs[n_in + 2:])


def kernel(x, c, ctx, c_ctx, w_mod, b_mod, norm1_w, norm2_w, w_in, hy_conv_w, hy_conv_b,
           hy_f_w1, hy_f_b1, hy_f_w2, hy_f_b2, hy_f_w3, hy_f_freq, hy_decay, hy_skip,
           hg_lower, hg_norm_w, w_br_hy, w_br_fn, w_br_hg, w_out,
           ffn_w1, ffn_w3, ffn_w2, moe_router, moe_w1, moe_w3, moe_w2, final_norm_w):
    batch, seq, d = x.shape
    l_ctx = ctx.shape[1]
    depth = w_in.shape[0]
    n_lat = batch * seq
    n_all = n_lat + batch * l_ctx

    lb_all = jnp.cumsum(jax.nn.softmax(hg_lower.astype(F32), axis=1), axis=1)
    lb_all = lb_all - lb_all[:, :1]

    cc = jnp.zeros((8, d), F32).at[:batch].set(c).at[batch].set(c_ctx)
    mods = _modulation(cc, w_mod, b_mod).reshape(depth, 8, 6, d)

    w1p = jnp.zeros((depth, LANE, HY_FILTER_HIDDEN), F32).at[:, :HY_POS_DIM].set(hy_f_w1)
    dft_lat = _hyena_dft(seq)
    dft_ctx = _hyena_dft(l_ctx)
    kf_lat = _hyena_filters(seq, w1p, hy_f_b1, hy_f_w2, hy_f_b2, hy_f_w3, hy_f_freq, hy_decay,
                            dft_lat[0], dft_lat[1])
    kf_ctx = _hyena_filters(l_ctx, w1p, hy_f_b1, hy_f_w2, hy_f_b2, hy_f_w3, hy_f_freq, hy_decay,
                            dft_ctx[0], dft_ctx[1])
    fn_lat = _fnet_dft(seq)
    fn_ctx = _fnet_dft(l_ctx)

    rows_x = jnp.concatenate([x.reshape(n_lat, d), ctx.reshape(batch * l_ctx, d)], axis=0)
    zero_state = jnp.zeros((batch, HG_HEADS, HG_DK, HG_DK), F32)
    tile_cap = math.gcd(seq, batch * l_ctx)
    tok = dict(seq=seq, batch=batch)
    tile = lambda t: dict(tm=min(t, tile_cap))
    lat = dict(seq_len=seq, n_seq=batch, row0=0, out_rows=n_all)
    cx = dict(seq_len=l_ctx, n_seq=batch, row0=n_lat, out_rows=n_all)

    for l in range(depth):
        last = l == depth - 1
        rows = n_lat if last else n_all
        m = mods[l]
        w_main = jnp.concatenate([w_in[l][:, GATE_OFF:], w_in[l][:, :HG_FF],
                                  w_in[l][:, HG_I:GATE_OFF]], axis=1).astype(BF16)
        w_forget = w_in[l][:, HG_FF:HG_I].astype(BF16)

        zm = _in_proj(rows_x, norm1_w[l], m, w_main, BF16, rows=n_all, **tok, **tile(1024))
        zf = _in_proj(rows_x, norm1_w[l], m, w_forget, F32, rows=n_all, **tok, **tile(1024))

        of, ob, s_f, s_b = _hgrn(zm, zf, lb_all[0, l], lb_all[1, l], zero_state, zero_state, **cx)
        of, ob, _, _ = _hgrn(zm, zf, lb_all[0, l], lb_all[1, l], s_f, s_b, prev_of=of,
                             prev_ob=ob, **lat)

        hy_args = (hy_conv_w[l], hy_conv_b[l])
        u = _hyena_longconv(zm, 0, zm, 1, *hy_args, kf_lat, l, 0, hy_skip[l], dft_lat, **lat)
        y_hy = _hyena_longconv(u, None, zm, 2, *hy_args, kf_lat, l, 1, hy_skip[l], dft_lat, **lat)
        y_fn = _fnet(zm, fn_lat, **lat)
        if not last:
            u = _hyena_longconv(zm, 0, zm, 1, *hy_args, kf_ctx, l, 0, hy_skip[l], dft_ctx,
                                prev_out=u, **cx)
            y_hy = _hyena_longconv(u, None, zm, 2, *hy_args, kf_ctx, l, 1, hy_skip[l], dft_ctx,
                                   prev_out=y_hy, **cx)
            y_fn = _fnet(zm, fn_ctx, prev_out=y_fn, **cx)

        rows_x = _merge_out(y_hy, y_fn, of, ob, zm, rows_x, hg_norm_w[l], m,
                            w_br_hy[l].astype(BF16), w_br_fn[l].astype(BF16),
                            w_br_hg[l].astype(BF16), w_out[l].astype(BF16), rows=rows,
                            **tok, **tile(256))

        if l % 2 == 0:
            i = l // 2
            rows_x = _ffn_dense(rows_x, norm2_w[l], m, ffn_w1[i].astype(BF16),
                                ffn_w3[i].astype(BF16), ffn_w2[i].astype(BF16), rows=rows,
                                **tok, **tile(512))
        else:
            i = l // 2
            rows_x = _moe_top2(rows_x, norm2_w[l], m, moe_router[i], moe_w1[i].astype(BF16),
                                moe_w3[i].astype(BF16), moe_w2[i].astype(BF16), rows=rows,
                                **tok, **tile(512))

    out = _final_norm(rows_x, final_norm_w, rows=n_lat, **tile(512))
    return out.reshape(batch, seq, d)
```

```python
import functools
import math

import numpy as np
import jax
import jax.numpy as jnp
from jax import lax
from jax.experimental import pallas as pl
from jax.experimental.pallas import tpu as pltpu

F32 = jnp.float32
BF16 = jnp.bfloat16
NORM_EPS = 1e-6

D_MODEL = 2048
DEPTH = 4
HY_W = 1024
HY_BANDS = 16
HY_POS_DIM = 2 * HY_BANDS + 1
HY_FILTER_HIDDEN = 64
FN_W = 1024
FN_GROUP_W = 128
HG_W = 1024
HG_HEADS = 8
HG_DK = 128
HG_CHUNK = 64
HG_LEVELS = 6
N_EXPERTS = 8
LANE = 128

HG_FF = 3 * HY_W + FN_W + 2 * HG_W
HG_I = HG_FF + 2 * HG_W
GATE_OFF = HG_I + HG_W
ZM_GATE = 0
ZM_HY = ZM_GATE + 3 * D_MODEL
ZM_FN = ZM_HY + 3 * HY_W
ZM_Q = ZM_FN + FN_W
ZM_G = ZM_Q + HG_W
ZM_I = ZM_G + HG_W
ZM_COLS = ZM_I + HG_W

VMEM_LIMIT = 56 * 1024 * 1024


def _cparams(n_axes, vmem=VMEM_LIMIT):
    return pltpu.CompilerParams(dimension_semantics=("arbitrary",) * n_axes, vmem_limit_bytes=vmem)


def _resident(shape, index_map):
    return pl.BlockSpec(shape, index_map, pipeline_mode=pl.Buffered(1))


def _silu(v):
    return v * jax.nn.sigmoid(v)


def _bdot(a, b):
    return jnp.dot(a, b, preferred_element_type=F32)


def _mod_kernel(c_ref, w_ref, b_ref, o_ref):
    c = _silu(c_ref[...])
    o_ref[0] = jnp.dot(c, w_ref[0], preferred_element_type=F32,
                       precision=lax.Precision.HIGHEST) + b_ref[0]


def _modulation(cc, w_mod, b_mod):
    depth, d, n6 = w_mod.shape
    tn = 2048
    return pl.pallas_call(
        _mod_kernel,
        grid=(depth, n6 // tn),
        in_specs=[
            pl.BlockSpec((8, d), lambda l, n: (0, 0)),
            pl.BlockSpec((1, d, tn), lambda l, n: (l, 0, n)),
            pl.BlockSpec((1, 1, tn), lambda l, n: (l, 0, n)),
        ],
        out_specs=pl.BlockSpec((1, 8, tn), lambda l, n: (l, 0, n)),
        out_shape=jax.ShapeDtypeStruct((depth, 8, n6), F32),
        compiler_params=_cparams(2),
        name="modulation",
    )(cc, w_mod, b_mod.reshape(depth, 1, n6))


def _norm_mod(x, nw, shift, scale):
    ms = jnp.mean(x * x, axis=-1, keepdims=True)
    return (x * lax.rsqrt(ms + NORM_EPS) * nw) * (1 + scale) + shift


def _group_of(row_block, tm, seq, batch):
    return jnp.minimum((row_block * tm) // seq, batch)


def _inproj_kernel(x_ref, nw_ref, mod_ref, w_ref, o_ref, h_scr):
    @pl.when(pl.program_id(1) == 0)
    def _():
        h = _norm_mod(x_ref[...], nw_ref[...], mod_ref[0, 0:1, :], mod_ref[0, 1:2, :])
        h_scr[...] = h.astype(BF16)

    o_ref[...] = _bdot(h_scr[...], w_ref[...]).astype(o_ref.dtype)


def _in_proj(x, nw, mods, w, out_dtype, *, rows, seq, batch, tm=1024, tn=1024):
    d = x.shape[1]
    n = w.shape[1]
    tm = min(tm, rows)
    return pl.pallas_call(
        _inproj_kernel,
        grid=(rows // tm, n // tn),
        in_specs=[
            pl.BlockSpec((tm, d), lambda m, j: (m, 0)),
            pl.BlockSpec((1, d), lambda m, j: (0, 0)),
            pl.BlockSpec((1, 6, d), lambda m, j: (_group_of(m, tm, seq, batch), 0, 0)),
            pl.BlockSpec((d, tn), lambda m, j: (0, j)),
        ],
        out_specs=pl.BlockSpec((tm, tn), lambda m, j: (m, j)),
        out_shape=jax.ShapeDtypeStruct((x.shape[0], n), out_dtype),
        scratch_shapes=[pltpu.VMEM((tm, d), BF16)],
        compiler_params=_cparams(2),
        name="in_proj",
    )(x, nw.reshape(1, d), mods, w)


def _merge_kernel(yhy_ref, yfn_ref, of_ref, ob_ref, g_ref, gate_ref, x_ref, hnw_ref, mod_ref,
                  wbhy_ref, wbfn_ref, wbhg_ref, wo_ref, o_ref):
    d = x_ref.shape[1]
    ohg = of_ref[...] + ob_ref[...]
    nw = hnw_ref[...]
    heads = []
    for h in range(ohg.shape[1] // HG_DK):
        oh = ohg[:, h * HG_DK:(h + 1) * HG_DK]
        ms = jnp.mean(oh * oh, axis=-1, keepdims=True)
        heads.append(oh * lax.rsqrt(ms + NORM_EPS) * nw)
    y_hg = jnp.concatenate(heads, axis=-1) * _silu(g_ref[...].astype(F32))
    gate = lambda i: jax.nn.sigmoid(gate_ref[:, i * d:(i + 1) * d].astype(F32))
    merged = gate(0) * _bdot(yhy_ref[...], wbhy_ref[...])
    merged += gate(1) * _bdot(yfn_ref[...], wbfn_ref[...])
    merged += gate(2) * _bdot(y_hg.astype(BF16), wbhg_ref[...])
    out = _bdot(merged.astype(BF16), wo_ref[...])
    o_ref[...] = x_ref[...] + mod_ref[0, 2:3, :] * out


def _merge_out(y_hy, y_fn, o_f, o_b, zm, x, hg_nw, mods, wb_hy, wb_fn, wb_hg, w_o,
               *, rows, seq, batch, tm=256):
    d = x.shape[1]
    w = y_hy.shape[1]
    tm = min(tm, rows)
    row = lambda m: (m, 0)
    const = lambda m: (0, 0)
    return pl.pallas_call(
        _merge_kernel,
        grid=(rows // tm,),
        in_specs=[
            pl.BlockSpec((tm, w), row),
            pl.BlockSpec((tm, w), row),
            pl.BlockSpec((tm, w), row),
            pl.BlockSpec((tm, w), row),
            pl.BlockSpec((tm, w), lambda m: (m, ZM_G // w)),
            pl.BlockSpec((tm, 3 * d), lambda m: (m, ZM_GATE // (3 * d))),
            pl.BlockSpec((tm, d), row),
            pl.BlockSpec((1, HG_DK), const),
            pl.BlockSpec((1, 6, d), lambda m: (_group_of(m, tm, seq, batch), 0, 0)),
            _resident((w, d), const),
            _resident((w, d), const),
            _resident((w, d), const),
            _resident((d, d), const),
        ],
        out_specs=pl.BlockSpec((tm, d), row),
        out_shape=jax.ShapeDtypeStruct((rows, d), F32),
        compiler_params=_cparams(1),
        name="merge_out",
    )(y_hy, y_fn, o_f, o_b, zm, zm, x, hg_nw.reshape(1, HG_DK), mods, wb_hy, wb_fn, wb_hg, w_o)


def _ffn_kernel(x_ref, nw_ref, mod_ref, w1_ref, w3_ref, w2_ref, o_ref, h_scr, acc_scr):
    f = pl.program_id(1)

    @pl.when(f == 0)
    def _():
        h = _norm_mod(x_ref[...], nw_ref[...], mod_ref[0, 3:4, :], mod_ref[0, 4:5, :])
        h_scr[...] = h.astype(BF16)
        acc_scr[...] = jnp.zeros_like(acc_scr)

    h = h_scr[...]
    act = _silu(_bdot(h, w1_ref[...])) * _bdot(h, w3_ref[...])
    acc_scr[...] += _bdot(act.astype(BF16), w2_ref[...])

    @pl.when(f == pl.num_programs(1) - 1)
    def _():
        o_ref[...] = x_ref[...] + mod_ref[0, 5:6, :] * acc_scr[...]


def _ffn_dense(x, nw, mods, w1, w3, w2, *, rows, seq, batch, tm=512, tf=512):
    d = x.shape[1]
    ff = w1.shape[1]
    tm = min(tm, rows)
    return pl.pallas_call(
        _ffn_kernel,
        grid=(rows // tm, ff // tf),
        in_specs=[
            pl.BlockSpec((tm, d), lambda m, f: (m, 0)),
            pl.BlockSpec((1, d), lambda m, f: (0, 0)),
            pl.BlockSpec((1, 6, d), lambda m, f: (_group_of(m, tm, seq, batch), 0, 0)),
            pl.BlockSpec((d, tf), lambda m, f: (0, f)),
            pl.BlockSpec((d, tf), lambda m, f: (0, f)),
            pl.BlockSpec((tf, d), lambda m, f: (f, 0)),
        ],
        out_specs=pl.BlockSpec((tm, d), lambda m, f: (m, 0)),
        out_shape=jax.ShapeDtypeStruct((rows, d), F32),
        scratch_shapes=[pltpu.VMEM((tm, d), BF16), pltpu.VMEM((tm, d), F32)],
        compiler_params=_cparams(2),
        name="ffn_dense",
    )(x, nw.reshape(1, d), mods, w1, w3, w2)


def _top2(logits, n_experts):
    lane = lax.broadcasted_iota(jnp.int32, logits.shape, 1).astype(F32)
    neg = jnp.float32(-jnp.inf)
    lg = jnp.where(lane < n_experts, logits, neg)
    m1 = jnp.max(lg, axis=-1, keepdims=True)
    i1 = jnp.min(jnp.where(lg == m1, lane, float(LANE)), axis=-1, keepdims=True)
    lg2 = jnp.where(lane == i1, neg, lg)
    m2 = jnp.max(lg2, axis=-1, keepdims=True)
    i2 = jnp.min(jnp.where(lg2 == m2, lane, float(LANE)), axis=-1, keepdims=True)
    e2 = jnp.exp(m2 - m1)
    return i1, i2, 1.0 / (1.0 + e2), e2 / (1.0 + e2)


def _route_kernel(x_ref, nw_ref, mod_ref, r_ref, h_ref, route_ref):
    h = _norm_mod(x_ref[...], nw_ref[...], mod_ref[0, 3:4, :], mod_ref[0, 4:5, :])
    logits = jnp.dot(h, r_ref[...], preferred_element_type=F32, precision=lax.Precision.HIGHEST)
    i1, i2, p1, p2 = _top2(logits, N_EXPERTS)
    lane = lax.broadcasted_iota(jnp.int32, logits.shape, 1)
    route = jnp.where(lane == 0, i1, jnp.where(lane == 1, i2,
                      jnp.where(lane == 2, p1, jnp.where(lane == 3, p2, 0.0))))
    h_ref[...] = h
    route_ref[...] = route


def _moe_route(x, nw, mods, router, *, rows, seq, batch, tm=512):
    d = x.shape[1]
    rpad = jnp.zeros((d, LANE), F32).at[:, :router.shape[1]].set(router)
    return pl.pallas_call(
        _route_kernel,
        grid=(rows // tm,),
        in_specs=[
            pl.BlockSpec((tm, d), lambda m: (m, 0)),
            pl.BlockSpec((1, d), lambda m: (0, 0)),
            pl.BlockSpec((1, 6, d), lambda m: (_group_of(m, tm, seq, batch), 0, 0)),
            pl.BlockSpec((d, LANE), lambda m: (0, 0)),
        ],
        out_specs=[pl.BlockSpec((tm, d), lambda m: (m, 0)),
                   pl.BlockSpec((tm, LANE), lambda m: (m, 0))],
        out_shape=[jax.ShapeDtypeStruct((rows, d), F32), jax.ShapeDtypeStruct((rows, LANE), F32)],
        compiler_params=_cparams(1),
        name="moe_route",
    )(x, nw.reshape(1, d), mods, rpad)


def _moe_plan(route, n_experts, tile):
    rows = route.shape[0]
    e = route[:, :2].astype(jnp.int32).reshape(-1)
    onehot = (e[:, None] == jnp.arange(n_experts, dtype=jnp.int32)[None, :]).astype(jnp.int32)
    before = jnp.cumsum(onehot, axis=0) - onehot
    rank = jnp.sum(before * onehot, axis=1)
    counts = jnp.sum(onehot, axis=0)
    padded = ((counts + tile - 1) // tile) * tile
    ends = jnp.cumsum(padded)
    starts = ends - padded
    pos = jnp.sum(starts[None, :] * onehot, axis=1) + rank
    n_tiles = (2 * rows) // tile + n_experts
    src = jnp.zeros((n_tiles * tile,), jnp.int32).at[pos].set(
        jnp.arange(2 * rows, dtype=jnp.int32) // 2)
    tile_start = jnp.arange(n_tiles, dtype=jnp.int32) * tile
    tile_expert = jnp.sum((tile_start[:, None] >= ends[None, :]).astype(jnp.int32), axis=1)
    tile_expert = jnp.minimum(tile_expert, n_experts - 1)
    n_used = (ends[-1] // tile).reshape(1)
    return pos, src, tile_expert, n_used


def _row_copy(src_hbm, src_row, dst, dst_row, sem):
    return pltpu.make_async_copy(src_hbm.at[pl.ds(src_row, 1)], dst.at[pl.ds(dst_row, 1)], sem)


def _gather_rows(idx_ref, base, stride, src_hbm, dst, sem, *, wait):
    def body(r, carry):
        copy = _row_copy(src_hbm, idx_ref[base + stride * r], dst, r, sem)
        copy.wait() if wait else copy.start()
        return carry

    lax.fori_loop(0, dst.shape[0], body, 0, unroll=8)


def _moe_ffn_kernel(te_ref, nu_ref, src_ref, h_hbm, w1_ref, w3_ref, w2_ref, o_ref,
                    rows_scr, h_scr, acc_scr, sems):
    t = pl.program_id(0)
    f = pl.program_id(1)
    last = pl.num_programs(1) - 1
    tile = o_ref.shape[0]
    n_used = nu_ref[0]
    used = t < n_used
    slot = t % 2

    def gather(tile_idx, buf, wait):
        _gather_rows(src_ref, tile_idx * tile, 1, h_hbm, rows_scr.at[buf], sems.at[buf], wait=wait)

    @pl.when(used & (f == 0) & (t == 0))
    def _():
        gather(t, slot, False)

    @pl.when(used & (f == 0))
    def _():
        gather(t, slot, True)
        h_scr[...] = rows_scr[slot].astype(BF16)
        acc_scr[...] = jnp.zeros_like(acc_scr)

    @pl.when((f == 0) & (t + 1 < n_used))
    def _():
        gather(t + 1, 1 - slot, False)

    @pl.when(used)
    def _():
        h = h_scr[...]
        act = _silu(_bdot(h, w1_ref[0])) * _bdot(h, w3_ref[0])
        acc_scr[...] += _bdot(act.astype(BF16), w2_ref[0])

    @pl.when(used & (f == last))
    def _():
        o_ref[...] = acc_scr[...]

    @pl.when(jnp.logical_not(used) & (f == last))
    def _():
        o_ref[...] = jnp.zeros_like(o_ref)


def _moe_ffn(h, src, tile_expert, n_used, w1, w3, w2, *, tile, tf=512):
    d = h.shape[1]
    n = src.shape[0]
    ff = w1.shape[2]
    nf = ff // tf
    fsel = lambda t, f, nu: jnp.where(t < nu[0], f, nf - 1)
    return pl.pallas_call(
        _moe_ffn_kernel,
        grid_spec=pltpu.PrefetchScalarGridSpec(
            num_scalar_prefetch=3,
            grid=(n // tile, nf),
            in_specs=[
                pl.BlockSpec(memory_space=pl.ANY),
                pl.BlockSpec((1, d, tf), lambda t, f, te, nu, sr: (te[t], 0, fsel(t, f, nu))),
                pl.BlockSpec((1, d, tf), lambda t, f, te, nu, sr: (te[t], 0, fsel(t, f, nu))),
                pl.BlockSpec((1, tf, d), lambda t, f, te, nu, sr: (te[t], fsel(t, f, nu), 0)),
            ],
            out_specs=pl.BlockSpec((tile, d), lambda t, f, te, nu, sr: (t, 0)),
            scratch_shapes=[pltpu.VMEM((2, tile, d), F32), pltpu.VMEM((tile, d), BF16),
                            pltpu.VMEM((tile, d), F32), pltpu.SemaphoreType.DMA((2,))],
        ),
        out_shape=jax.ShapeDtypeStruct((n, d), F32),
        compiler_params=_cparams(2),
        name="moe_ffn",
    )(tile_expert, n_used, src, h, w1, w3, w2)


def _combine_kernel(pos_ref, ys_hbm, x_ref, route_ref, mod_ref, o_ref, y1_scr, y2_scr, sem):
    base = 2 * pl.program_id(0) * x_ref.shape[0]
    _gather_rows(pos_ref, base, 2, ys_hbm, y1_scr, sem, wait=False)
    _gather_rows(pos_ref, base + 1, 2, ys_hbm, y2_scr, sem, wait=False)
    _gather_rows(pos_ref, base, 2, ys_hbm, y1_scr, sem, wait=True)
    _gather_rows(pos_ref, base + 1, 2, ys_hbm, y2_scr, sem, wait=True)
    mix = route_ref[:, 2:3] * y1_scr[...] + route_ref[:, 3:4] * y2_scr[...]
    o_ref[...] = x_ref[...] + mod_ref[0, 5:6, :] * mix


def _moe_combine(x, mods, route, ys, pos, *, rows, seq, batch, tm=256):
    d = x.shape[1]
    return pl.pallas_call(
        _combine_kernel,
        grid_spec=pltpu.PrefetchScalarGridSpec(
            num_scalar_prefetch=1,
            grid=(rows // tm,),
            in_specs=[
                pl.BlockSpec(memory_space=pl.ANY),
                pl.BlockSpec((tm, d), lambda m, pos: (m, 0)),
                pl.BlockSpec((tm, LANE), lambda m, pos: (m, 0)),
                pl.BlockSpec((1, 6, d), lambda m, pos: (_group_of(m, tm, seq, batch), 0, 0)),
            ],
            out_specs=pl.BlockSpec((tm, d), lambda m, pos: (m, 0)),
            scratch_shapes=[pltpu.VMEM((tm, d), F32), pltpu.VMEM((tm, d), F32),
                            pltpu.SemaphoreType.DMA(())],
        ),
        out_shape=jax.ShapeDtypeStruct((rows, d), F32),
        compiler_params=_cparams(1),
        name="moe_combine",
    )(pos, ys, x, route, mods)


def _moe_top2(x, nw, mods, router, w1, w3, w2, *, rows, seq, batch, tm, tile=512):
    tok = dict(rows=rows, seq=seq, batch=batch)
    h, route = _moe_route(x, nw, mods, router, tm=tm, **tok)
    pos, src, tile_expert, n_used = _moe_plan(route, w1.shape[0], tile)
    ys = _moe_ffn(h, src, tile_expert, n_used, w1, w3, w2, tile=tile)
    return _moe_combine(x, mods, route, ys, pos, tm=min(tm, 256), **tok)


def _final_norm_kernel(x_ref, w_ref, o_ref):
    x = x_ref[...]
    ms = jnp.mean(x * x, axis=-1, keepdims=True)
    o_ref[...] = x * lax.rsqrt(ms + NORM_EPS) * w_ref[...]


def _final_norm(x, w, *, rows, tm=512):
    d = x.shape[1]
    return pl.pallas_call(
        _final_norm_kernel,
        grid=(rows // tm,),
        in_specs=[pl.BlockSpec((tm, d), lambda m: (m, 0)), pl.BlockSpec((1, d), lambda m: (0, 0))],
        out_specs=pl.BlockSpec((tm, d), lambda m: (m, 0)),
        out_shape=jax.ShapeDtypeStruct((rows, d), F32),
        compiler_params=_cparams(1),
        name="final_norm",
    )(x, w.reshape(1, d))


def _dft_angles(n, period):
    k = lax.broadcasted_iota(jnp.int32, (n, n), 0)
    s = lax.broadcasted_iota(jnp.int32, (n, n), 1)
    return ((k * s) % period).astype(F32) * (2.0 * math.pi / period)


def _hyena_dft(seq_len):
    n = 2 * seq_len
    ang = _dft_angles(seq_len, n)
    row = lax.broadcasted_iota(jnp.int32, ang.shape, 0)
    col = lax.broadcasted_iota(jnp.int32, ang.shape, 1)
    cos = jnp.cos(ang)
    sin = jnp.sin(ang)
    fc = cos
    fs = jnp.where(row == 0, jnp.where(col % 2 == 0, 1.0, -1.0), sin)
    ic = jnp.where(col == 0, 1.0 / n, 2.0 / n * cos)
    isn = jnp.where(col == 0, jnp.where(row % 2 == 0, 1.0, -1.0) / n, 2.0 / n * sin)
    return fc.astype(BF16), fs.astype(BF16), ic.astype(BF16), isn.astype(BF16)


def _hyena_feats(seq_len):
    pos = np.arange(seq_len, dtype=np.float64)
    t = pos / max(seq_len - 1, 1)
    bands = np.linspace(1e-4, HY_BANDS - 1, HY_BANDS).astype(np.float32).astype(np.float64)
    ang = 2 * math.pi * pos[:, None] * bands[None, :] / seq_len
    feats = np.concatenate([t[:, None], np.cos(ang), -np.sin(ang)], axis=-1)
    out = np.zeros((seq_len, LANE), np.float32)
    out[:, :HY_POS_DIM] = feats
    return jnp.asarray(out)


def _split_dot(w, a):
    hi = a.astype(BF16)
    lo = (a - hi.astype(F32)).astype(BF16)
    return _bdot(w, hi) + _bdot(w, lo)


def _filter_kernel(feat_ref, w1_ref, b1_ref, w2_ref, b2_ref, fr_ref, w3f_ref, w3b_ref,
                   dcf_ref, dcb_ref, fc_ref, fs_ref, o_ref):
    hp = lax.Precision.HIGHEST
    feats = feat_ref[...]
    h = jnp.sin(fr_ref[0, 0:1, :] * (jnp.dot(feats, w1_ref[0], preferred_element_type=F32,
                                             precision=hp) + b1_ref[0]))
    h = jnp.sin(fr_ref[0, 1:2, :] * (jnp.dot(h, w2_ref[0], preferred_element_type=F32,
                                             precision=hp) + b2_ref[0]))
    t = feats[:, 0:1]
    hf = jnp.dot(h, w3f_ref[0], preferred_element_type=F32, precision=hp) * jnp.exp(-t * dcf_ref[0])
    hb = jnp.dot(h, w3b_ref[0], preferred_element_type=F32, precision=hp) * jnp.exp(-t * dcb_ref[0])
    row = lax.broadcasted_iota(jnp.int32, hf.shape, 0)
    hb = jnp.where(row == 0, 0.0, hb)
    ss = jnp.sum(hf * hf + hb * hb, axis=0, keepdims=True)
    scale = lax.rsqrt(ss + NORM_EPS)
    even = hf * scale + hb * scale
    odd = hf * scale - hb * scale
    kc = _split_dot(fc_ref[...], even)
    ks = _split_dot(fs_ref[...], odd)
    sign = jnp.where(row % 2 == 0, 1.0, -1.0)
    nyq = jnp.sum(even * sign, axis=0, keepdims=True)
    o_ref[0, 0] = kc
    o_ref[0, 1] = jnp.where(row == 0, nyq, ks)


def _hyena_filters(seq_len, w1p, b1, w2, b2, w3, freq, decay, fc, fs, *, tn=256):
    depth = w1p.shape[0]
    hid = HY_FILTER_HIDDEN
    nb = HY_W // tn
    feats = _hyena_feats(seq_len)
    col = lambda d: (lambda l, o, c: (l, 0, (2 * o + d) * nb + c))
    lyr = lambda l, o, c: (l, 0, 0)
    return pl.pallas_call(
        _filter_kernel,
        grid=(depth, 2, nb),
        in_specs=[
            pl.BlockSpec((seq_len, LANE), lambda l, o, c: (0, 0)),
            pl.BlockSpec((1, LANE, hid), lyr),
            pl.BlockSpec((1, 1, hid), lyr),
            pl.BlockSpec((1, hid, hid), lyr),
            pl.BlockSpec((1, 1, hid), lyr),
            pl.BlockSpec((1, 2, hid), lyr),
            pl.BlockSpec((1, hid, tn), col(0)),
            pl.BlockSpec((1, hid, tn), col(1)),
            pl.BlockSpec((1, 1, tn), col(0)),
            pl.BlockSpec((1, 1, tn), col(1)),
            _resident((seq_len, seq_len), lambda l, o, c: (0, 0)),
            _resident((seq_len, seq_len), lambda l, o, c: (0, 0)),
        ],
        out_specs=pl.BlockSpec((1, 2, seq_len, tn), lambda l, o, c: (l, 0, 0, o * nb + c)),
        out_shape=jax.ShapeDtypeStruct((depth, 2, seq_len, 2 * HY_W), F32),
        compiler_params=_cparams(3),
        name="hyena_filters",
    )(feats, w1p, b1.reshape(depth, 1, hid), w2, b2.reshape(depth, 1, hid), freq,
      w3, w3, decay.reshape(depth, 1, 4 * HY_W), decay.reshape(depth, 1, 4 * HY_W), fc, fs)


def _conv3(z, w_ref, b_ref):
    n = z.shape[0]
    row = lax.broadcasted_iota(jnp.int32, z.shape, 0)
    prev = jnp.where(row == 0, 0.0, pltpu.roll(z, 1, 0))
    nxt = jnp.where(row == n - 1, 0.0, pltpu.roll(z, n - 1, 0))
    return prev * w_ref[0:1, :] + z * w_ref[1:2, :] + nxt * w_ref[2:3, :] + b_ref[...]


def _longconv_kernel(a_ref, g_ref, aw_ref, ab_ref, gw_ref, gb_ref, kf_ref, skip_ref,
                     fc_ref, fs_ref, ic_ref, is_ref, o_ref, acc_scr, *, conv_a, fblk):
    a = a_ref[...].astype(F32)
    if conv_a:
        a = _conv3(a, aw_ref, ab_ref)
    a16 = a.astype(BF16)
    n = a.shape[0]
    for j in range(n // fblk):
        rows = slice(j * fblk, (j + 1) * fblk)
        ac = _bdot(fc_ref[rows, :], a16)
        asn = _bdot(fs_ref[rows, :], a16)
        kc = kf_ref[0, 0, rows, :]
        ks = kf_ref[0, 1, rows, :]
        cross = asn * ks
        yc = ac * kc - cross
        ys = ac * ks + asn * kc
        if j == 0:
            row = lax.broadcasted_iota(jnp.int32, ac.shape, 0)
            yc = jnp.where(row == 0, ac * kc, yc)
            ys = jnp.where(row == 0, cross, ys)
        part = _bdot(ic_ref[:, rows], yc.astype(BF16)) + _bdot(is_ref[:, rows], ys.astype(BF16))
        if j == 0:
            acc_scr[...] = part
        else:
            acc_scr[...] += part
    gate = _conv3(g_ref[...].astype(F32), gw_ref, gb_ref)
    o_ref[...] = (gate * (acc_scr[...] + a * skip_ref[0])).astype(o_ref.dtype)


def _hyena_longconv(a_src, a_part, zm, gate_part, conv_w, conv_b, kf, layer, order, skip,
                    dft, *, seq_len, n_seq, row0, out_rows, prev_out=None, tn=256):
    fc, fs, ic, isn = dft
    nb = HY_W // tn
    rb0 = row0 // seq_len
    fblk = min(512, seq_len)
    const = lambda c, b: (0, 0)
    conv_a = a_part is not None
    aw_blk = a_part * nb if conv_a else 0
    a_blk = ZM_HY // tn + aw_blk if conv_a else 0
    gw_blk = gate_part * nb
    g_blk = ZM_HY // tn + gw_blk
    kernel = functools.partial(_longconv_kernel, conv_a=conv_a, fblk=fblk)
    in_specs = [
        pl.BlockSpec((seq_len, tn), lambda c, b: (rb0 + b, a_blk + c)),
        pl.BlockSpec((seq_len, tn), lambda c, b: (rb0 + b, g_blk + c)),
        pl.BlockSpec((3, tn), lambda c, b: (0, aw_blk + c)),
        pl.BlockSpec((1, tn), lambda c, b: (0, aw_blk + c)),
        pl.BlockSpec((3, tn), lambda c, b: (0, gw_blk + c)),
        pl.BlockSpec((1, tn), lambda c, b: (0, gw_blk + c)),
        pl.BlockSpec((1, 2, seq_len, tn), lambda c, b: (layer, 0, 0, order * nb + c)),
        pl.BlockSpec((1, 1, tn), lambda c, b: (order, 0, c)),
        _resident((seq_len, seq_len), const),
        _resident((seq_len, seq_len), const),
        _resident((seq_len, seq_len), const),
        _resident((seq_len, seq_len), const),
    ]
    args = [a_src, zm, conv_w, conv_b.reshape(1, -1), conv_w, conv_b.reshape(1, -1), kf,
            skip.reshape(2, 1, HY_W), fc, fs, ic, isn]
    aliases = {}
    if prev_out is not None:
        in_specs.append(pl.BlockSpec(memory_space=pl.ANY))
        args.append(prev_out)
        aliases = {len(args) - 1: 0}
        kernel = functools.partial(_drop_last_input, kernel, 12)
    return pl.pallas_call(
        kernel,
        grid=(nb, n_seq),
        in_specs=in_specs,
        out_specs=pl.BlockSpec((seq_len, tn), lambda c, b: (rb0 + b, c)),
        out_shape=jax.ShapeDtypeStruct((out_rows, HY_W), BF16),
        scratch_shapes=[pltpu.VMEM((seq_len, tn), F32)],
        input_output_aliases=aliases,
        compiler_params=_cparams(2),
        name="hyena_longconv",
    )(*args)


def _drop_last_input(kernel, n_in, *refs):
    return kernel(*refs[:n_in], *refs[n_in + 1:])


def _fnet_dft(seq_len):
    ang_l = _dft_angles(seq_len, seq_len)
    ang_w = _dft_angles(FN_GROUP_W, FN_GROUP_W)
    scale = 1.0 / math.sqrt(seq_len * FN_GROUP_W)
    return ((jnp.cos(ang_l) * scale).astype(BF16), (jnp.sin(ang_l) * scale).astype(BF16),
            jnp.cos(ang_w).astype(BF16), jnp.sin(ang_w).astype(BF16))


def _fnet_kernel(z_ref, cl_ref, sl_ref, cw_ref, sw_ref, o_ref):
    z = z_ref[...]
    gc, gs = [], []
    for j in range(z.shape[1] // FN_GROUP_W):
        zj = z[:, j * FN_GROUP_W:(j + 1) * FN_GROUP_W]
        gc.append(_bdot(zj, cw_ref[...]).astype(BF16))
        gs.append(_bdot(zj, sw_ref[...]).astype(BF16))
    gc = jnp.concatenate(gc, axis=-1)
    gs = jnp.concatenate(gs, axis=-1)
    o_ref[...] = (_bdot(cl_ref[...], gc) - _bdot(sl_ref[...], gs)).astype(o_ref.dtype)


def _fnet(zm, dft, *, seq_len, n_seq, row0, out_rows, prev_out=None, tn=256):
    cl, sl, cw, sw = dft
    nb = FN_W // tn
    rb0 = row0 // seq_len
    c0 = ZM_FN // tn
    const = lambda c, b: (0, 0)
    in_specs = [
        pl.BlockSpec((seq_len, tn), lambda c, b: (rb0 + b, c0 + c)),
        _resident((seq_len, seq_len), const),
        _resident((seq_len, seq_len), const),
        pl.BlockSpec((FN_GROUP_W, FN_GROUP_W), const),
        pl.BlockSpec((FN_GROUP_W, FN_GROUP_W), const),
    ]
    args = [zm, cl, sl, cw, sw]
    kernel = _fnet_kernel
    aliases = {}
    if prev_out is not None:
        in_specs.append(pl.BlockSpec(memory_space=pl.ANY))
        args.append(prev_out)
        aliases = {len(args) - 1: 0}
        kernel = functools.partial(_drop_last_input, _fnet_kernel, 5)
    return pl.pallas_call(
        kernel,
        grid=(nb, n_seq),
        in_specs=in_specs,
        out_specs=pl.BlockSpec((seq_len, tn), lambda c, b: (rb0 + b, c)),
        out_shape=jax.ShapeDtypeStruct((out_rows, FN_W), BF16),
        input_output_aliases=aliases,
        compiler_params=_cparams(2),
        name="fnet",
    )(*args)


def _hgrn_tables(reverse):
    c = HG_CHUNK
    tri = np.zeros((c, c), np.float32)
    for t in range(c):
        if reverse:
            tri[t, t:] = 1.0
        else:
            tri[t, :t + 1] = 1.0
    mats = [tri]
    masks = []
    for lvl in range(HG_LEVELS):
        half = (c // 2) >> lvl
        sel = np.zeros((c, c), np.float32)
        mask = np.zeros((c, c), np.float32)
        for t in range(c):
            start = (t // (2 * half)) * 2 * half
            mid = start + half
            sel[t, mid if reverse else mid - 1] = 1.0
            for s in range(start, start + 2 * half):
                if reverse and t < mid <= s:
                    mask[t, s] = 1.0
                if (not reverse) and s < mid <= t:
                    mask[t, s] = 1.0
        mats.append(tri - sel @ tri)
        masks.append(mask)
    return np.concatenate(mats, axis=0), np.stack(masks, axis=0)


def _hgrn_direction(q_ref, v_ref, zf_ref, lb_ref, tab_ref, mask_ref, st_scr, o_ref, end_row):
    c = HG_CHUNK
    lb = lb_ref[...]
    f = lb + (1.0 - lb) * jax.nn.sigmoid(zf_ref[...])
    lf = jnp.log(f)
    kk = 1.0 - f
    p1 = lf.astype(BF16)
    r1 = lf - p1.astype(F32)
    p2 = r1.astype(BF16)
    p3 = (r1 - p2.astype(F32)).astype(BF16)
    dec = _bdot(tab_ref[...], jnp.concatenate([p1, p2, p3], axis=0))
    q = q_ref[...].astype(F32)
    v = v_ref[...]
    for h in range(HG_HEADS):
        cols = slice(h * HG_DK, (h + 1) * HG_DK)
        qh = q[:, cols]
        kh = kk[:, cols]
        vh = v[:, cols]
        b = dec[0:c, cols]
        st = st_scr[h]
        qd = (qh * jnp.exp(b)).astype(BF16)
        o = _bdot(qd, st.astype(BF16))
        o += jnp.sum(qh * kh, axis=-1, keepdims=True) * vh.astype(F32)
        att = jnp.zeros((c, c), F32)
        for lvl in range(HG_LEVELS):
            e = jnp.exp(-jnp.abs(dec[(lvl + 1) * c:(lvl + 2) * c, cols]))
            a = lax.dot_general((qh * e).astype(BF16), (kh * e).astype(BF16),
                                (((1,), (1,)), ((), ())), preferred_element_type=F32)
            att += a * mask_ref[lvl]
        o += _bdot(att.astype(BF16), vh)
        o_ref[:, cols] = o
        b_end = b[end_row:end_row + 1, :]
        kd = (kh * jnp.exp(b_end - b)).astype(BF16)
        upd = lax.dot_general(kd, vh, (((0,), (0,)), ((), ())), preferred_element_type=F32)
        decay = jnp.transpose(jnp.broadcast_to(jnp.exp(b_end), (HG_DK, HG_DK)))
        st_scr[h] = decay * st + upd


def _hgrn_kernel(qf_ref, vf_ref, zff_ref, qb_ref, vb_ref, zfb_ref, lbf_ref, lbb_ref,
                 tabf_ref, maskf_ref, tabb_ref, maskb_ref, s0f_ref, s0b_ref,
                 of_ref, ob_ref, sf_ref, sb_ref, stf_scr, stb_scr):
    ci = pl.program_id(1)

    @pl.when(ci == 0)
    def _():
        stf_scr[...] = s0f_ref[0]
        stb_scr[...] = s0b_ref[0]

    _hgrn_direction(qf_ref, vf_ref, zff_ref, lbf_ref, tabf_ref, maskf_ref, stf_scr, of_ref,
                    HG_CHUNK - 1)
    _hgrn_direction(qb_ref, vb_ref, zfb_ref, lbb_ref, tabb_ref, maskb_ref, stb_scr, ob_ref, 0)

    @pl.when(ci == pl.num_programs(1) - 1)
    def _():
        sf_ref[0] = stf_scr[...]
        sb_ref[0] = stb_scr[...]


def _hgrn(zm, zf, lb_f, lb_b, s0_f, s0_b, *, seq_len, n_seq, row0, out_rows,
          prev_of=None, prev_ob=None):
    c = HG_CHUNK
    nc = seq_len // c
    rb0 = row0 // c
    w = HG_W
    tab_f, mask_f = _hgrn_tables(False)
    tab_b, mask_b = _hgrn_tables(True)
    fwd = lambda col: (lambda b, i: (rb0 + b * nc + i, col))
    bwd = lambda col: (lambda b, i: (rb0 + b * nc + nc - 1 - i, col))
    const2 = lambda b, i: (0, 0)
    const3 = lambda b, i: (0, 0, 0)
    state = lambda b, i: (b, 0, 0, 0)
    nt = (1 + HG_LEVELS) * c
    in_specs = [
        pl.BlockSpec((c, w), fwd(ZM_Q // w)),
        pl.BlockSpec((c, w), fwd(ZM_I // w)),
        pl.BlockSpec((c, w), fwd(0)),
        pl.BlockSpec((c, w), bwd(ZM_Q // w)),
        pl.BlockSpec((c, w), bwd(ZM_I // w)),
        pl.BlockSpec((c, w), bwd(1)),
        pl.BlockSpec((1, w), const2),
        pl.BlockSpec((1, w), const2),
        pl.BlockSpec((nt, 3 * c), const2),
        pl.BlockSpec((HG_LEVELS, c, c), const3),
        pl.BlockSpec((nt, 3 * c), const2),
        pl.BlockSpec((HG_LEVELS, c, c), const3),
        pl.BlockSpec((1, HG_HEADS, HG_DK, HG_DK), state),
        pl.BlockSpec((1, HG_HEADS, HG_DK, HG_DK), state),
    ]
    args = [zm, zm, zf, zm, zm, zf, lb_f.reshape(1, w), lb_b.reshape(1, w),
            jnp.asarray(np.tile(tab_f, (1, 3)), BF16), jnp.asarray(mask_f),
            jnp.asarray(np.tile(tab_b, (1, 3)), BF16), jnp.asarray(mask_b), s0_f, s0_b]
    kernel = _hgrn_kernel
    aliases = {}
    if prev_of is not None:
        in_specs += [pl.BlockSpec(memory_space=pl.ANY), pl.BlockSpec(memory_space=pl.ANY)]
        args += [prev_of, prev_ob]
        aliases = {14: 0, 15: 1}
        kernel = functools.partial(_hgrn_alias_kernel, 14)
    st_shape = jax.ShapeDtypeStruct((n_seq, HG_HEADS, HG_DK, HG_DK), F32)
    return pl.pallas_call(
        kernel,
        grid=(n_seq, nc),
        in_specs=in_specs,
        out_specs=[
            pl.BlockSpec((c, w), fwd(0)),
            pl.BlockSpec((c, w), bwd(0)),
            pl.BlockSpec((1, HG_HEADS, HG_DK, HG_DK), state),
            pl.BlockSpec((1, HG_HEADS, HG_DK, HG_DK), state),
        ],
        out_shape=[jax.ShapeDtypeStruct((out_rows, w), F32),
                   jax.ShapeDtypeStruct((out_rows, w), F32), st_shape, st_shape],
        scratch_shapes=[pltpu.VMEM((HG_HEADS, HG_DK, HG_DK), F32),
                        pltpu.VMEM((HG_HEADS, HG_DK, HG_DK), F32)],
        input_output_aliases=aliases,
        compiler_params=_cparams(2),
        name="hgrn2",
    )(*args)


def _hgrn_alias_kernel(n_in, *refs):
    return _hgrn_kernel(*refs[:n_in], *refs[n_in + 2:])


def kernel(x, c, ctx, c_ctx, w_mod, b_mod, norm1_w, norm2_w, w_in, hy_conv_w, hy_conv_b,
           hy_f_w1, hy_f_b1, hy_f_w2, hy_f_b2, hy_f_w3, hy_f_freq, hy_decay, hy_skip,
           hg_lower, hg_norm_w, w_br_hy, w_br_fn, w_br_hg, w_out,
           ffn_w1, ffn_w3, ffn_w2, moe_router, moe_w1, moe_w3, moe_w2, final_norm_w):
    batch, seq, d = x.shape
    l_ctx = ctx.shape[1]
    depth = w_in.shape[0]
    n_lat = batch * seq
    n_all = n_lat + batch * l_ctx

    lb_all = jnp.cumsum(jax.nn.softmax(hg_lower.astype(F32), axis=1), axis=1)
    lb_all = lb_all - lb_all[:, :1]

    cc = jnp.zeros((8, d), F32).at[:batch].set(c).at[batch].set(c_ctx)
    mods = _modulation(cc, w_mod, b_mod).reshape(depth, 8, 6, d)

    w1p = jnp.zeros((depth, LANE, HY_FILTER_HIDDEN), F32).at[:, :HY_POS_DIM].set(hy_f_w1)
    dft_lat = _hyena_dft(seq)
    dft_ctx = _hyena_dft(l_ctx)
    kf_lat = _hyena_filters(seq, w1p, hy_f_b1, hy_f_w2, hy_f_b2, hy_f_w3, hy_f_freq, hy_decay,
                            dft_lat[0], dft_lat[1])
    kf_ctx = _hyena_filters(l_ctx, w1p, hy_f_b1, hy_f_w2, hy_f_b2, hy_f_w3, hy_f_freq, hy_decay,
                            dft_ctx[0], dft_ctx[1])
    fn_lat = _fnet_dft(seq)
    fn_ctx = _fnet_dft(l_ctx)

    rows_x = jnp.concatenate([x.reshape(n_lat, d), ctx.reshape(batch * l_ctx, d)], axis=0)
    zero_state = jnp.zeros((batch, HG_HEADS, HG_DK, HG_DK), F32)
    tile_cap = math.gcd(seq, batch * l_ctx)
    tok = dict(seq=seq, batch=batch)
    tile = lambda t: dict(tm=min(t, tile_cap))
    lat = dict(seq_len=seq, n_seq=batch, row0=0, out_rows=n_all)
    cx = dict(seq_len=l_ctx, n_seq=batch, row0=n_lat, out_rows=n_all)

    for l in range(depth):
        last = l == depth - 1
        rows = n_lat if last else n_all
        m = mods[l]
        w_main = jnp.concatenate([w_in[l][:, GATE_OFF:], w_in[l][:, :HG_FF],
                                  w_in[l][:, HG_I:GATE_OFF]], axis=1).astype(BF16)
        w_forget = w_in[l][:, HG_FF:HG_I].astype(BF16)

        zm = _in_proj(rows_x, norm1_w[l], m, w_main, BF16, rows=n_all, **tok, **tile(1024))
        zf = _in_proj(rows_x, norm1_w[l], m, w_forget, F32, rows=n_all, **tok, **tile(1024))

        of, ob, s_f, s_b = _hgrn(zm, zf, lb_all[0, l], lb_all[1, l], zero_state, zero_state, **cx)
        of, ob, _, _ = _hgrn(zm, zf, lb_all[0, l], lb_all[1, l], s_f, s_b, prev_of=of,
                             prev_ob=ob, **lat)

        hy_args = (hy_conv_w[l], hy_conv_b[l])
        mix = dict(lat, out_rows=rows)
        u = _hyena_longconv(zm, 0, zm, 1, *hy_args, kf_lat, l, 0, hy_skip[l], dft_lat, **mix)
        y_hy = _hyena_longconv(u, None, zm, 2, *hy_args, kf_lat, l, 1, hy_skip[l], dft_lat, **mix)
        y_fn = _fnet(zm, fn_lat, **mix)
        if not last:
            u = _hyena_longconv(zm, 0, zm, 1, *hy_args, kf_ctx, l, 0, hy_skip[l], dft_ctx,
                                prev_out=u, **cx)
            y_hy = _hyena_longconv(u, None, zm, 2, *hy_args, kf_ctx, l, 1, hy_skip[l], dft_ctx,
                                   prev_out=y_hy, **cx)
            y_fn = _fnet(zm, fn_ctx, prev_out=y_fn, **cx)

        rows_x = _merge_out(y_hy, y_fn, of, ob, zm, rows_x, hg_norm_w[l], m,
                            w_br_hy[l].astype(BF16), w_br_fn[l].astype(BF16),
                            w_br_hg[l].astype(BF16), w_out[l].astype(BF16), rows=rows,
                            **tok, **tile(256))

        if l % 2 == 0:
            i = l // 2
            rows_x = _ffn_dense(rows_x, norm2_w[l], m, ffn_w1[i].astype(BF16),
                                ffn_w3[i].astype(BF16), ffn_w2[i].astype(BF16), rows=rows,
                                **tok, **tile(512))
        else:
            i = l // 2
            rows_x = _moe_top2(rows_x, norm2_w[l], m, moe_router[i], moe_w1[i].astype(BF16),
                                moe_w3[i].astype(BF16), moe_w2[i].astype(BF16), rows=rows,
                                **tok, **tile(512))

    out = _final_norm(rows_x, final_norm_w, rows=n_lat, **tile(512))
    return out.reshape(batch, seq, d)
```

```python
import functools
import math

import numpy as np
import jax
import jax.numpy as jnp
from jax import lax
from jax.experimental import pallas as pl
from jax.experimental.pallas import tpu as pltpu

F32 = jnp.float32
BF16 = jnp.bfloat16
NORM_EPS = 1e-6

D_MODEL = 2048
DEPTH = 4
HY_W = 1024
HY_BANDS = 16
HY_POS_DIM = 2 * HY_BANDS + 1
HY_FILTER_HIDDEN = 64
FN_W = 1024
FN_GROUP_W = 128
HG_W = 1024
HG_HEADS = 8
HG_DK = 128
HG_CHUNK = 64
HG_LEVELS = 6
N_EXPERTS = 8
LANE = 128

HG_FF = 3 * HY_W + FN_W + 2 * HG_W
HG_I = HG_FF + 2 * HG_W
GATE_OFF = HG_I + HG_W
ZM_GATE = 0
ZM_HY = ZM_GATE + 3 * D_MODEL
ZM_FN = ZM_HY + 3 * HY_W
ZM_Q = ZM_FN + FN_W
ZM_G = ZM_Q + HG_W
ZM_I = ZM_G + HG_W
ZM_COLS = ZM_I + HG_W

VMEM_LIMIT = 56 * 1024 * 1024


def _cparams(n_axes, vmem=VMEM_LIMIT):
    return pltpu.CompilerParams(dimension_semantics=("arbitrary",) * n_axes, vmem_limit_bytes=vmem)


def _resident(shape, index_map):
    return pl.BlockSpec(shape, index_map, pipeline_mode=pl.Buffered(1))


def _silu(v):
    return v * jax.nn.sigmoid(v)


def _bdot(a, b):
    return jnp.dot(a, b, preferred_element_type=F32)


def _mod_kernel(c_ref, w_ref, b_ref, o_ref):
    c = _silu(c_ref[...])
    o_ref[0] = jnp.dot(c, w_ref[0], preferred_element_type=F32,
                       precision=lax.Precision.HIGHEST) + b_ref[0]


def _modulation(cc, w_mod, b_mod):
    depth, d, n6 = w_mod.shape
    tn = 2048
    return pl.pallas_call(
        _mod_kernel,
        grid=(depth, n6 // tn),
        in_specs=[
            pl.BlockSpec((8, d), lambda l, n: (0, 0)),
            pl.BlockSpec((1, d, tn), lambda l, n: (l, 0, n)),
            pl.BlockSpec((1, 1, tn), lambda l, n: (l, 0, n)),
        ],
        out_specs=pl.BlockSpec((1, 8, tn), lambda l, n: (l, 0, n)),
        out_shape=jax.ShapeDtypeStruct((depth, 8, n6), F32),
        compiler_params=_cparams(2),
        name="modulation",
    )(cc, w_mod, b_mod.reshape(depth, 1, n6))


def _norm_mod(x, nw, shift, scale):
    ms = jnp.mean(x * x, axis=-1, keepdims=True)
    return (x * lax.rsqrt(ms + NORM_EPS) * nw) * (1 + scale) + shift


def _group_of(row_block, tm, seq, batch):
    return jnp.minimum((row_block * tm) // seq, batch)


def _inproj_kernel(x_ref, nw_ref, mod_ref, w_ref, o_ref, h_scr):
    @pl.when(pl.program_id(1) == 0)
    def _():
        h = _norm_mod(x_ref[...], nw_ref[...], mod_ref[0, 0:1, :], mod_ref[0, 1:2, :])
        h_scr[...] = h.astype(BF16)

    o_ref[...] = _bdot(h_scr[...], w_ref[...]).astype(o_ref.dtype)


def _in_proj(x, nw, mods, w, out_dtype, *, rows, seq, batch, tm=1024, tn=1024):
    d = x.shape[1]
    n = w.shape[1]
    tm = min(tm, rows)
    return pl.pallas_call(
        _inproj_kernel,
        grid=(rows // tm, n // tn),
        in_specs=[
            pl.BlockSpec((tm, d), lambda m, j: (m, 0)),
            pl.BlockSpec((1, d), lambda m, j: (0, 0)),
            pl.BlockSpec((1, 6, d), lambda m, j: (_group_of(m, tm, seq, batch), 0, 0)),
            pl.BlockSpec((d, tn), lambda m, j: (0, j)),
        ],
        out_specs=pl.BlockSpec((tm, tn), lambda m, j: (m, j)),
        out_shape=jax.ShapeDtypeStruct((x.shape[0], n), out_dtype),
        scratch_shapes=[pltpu.VMEM((tm, d), BF16)],
        compiler_params=_cparams(2),
        name="in_proj",
    )(x, nw.reshape(1, d), mods, w)


def _merge_kernel(yhy_ref, yfn_ref, of_ref, ob_ref, g_ref, gate_ref, x_ref, hnw_ref, mod_ref,
                  wbhy_ref, wbfn_ref, wbhg_ref, wo_ref, o_ref):
    d = x_ref.shape[1]
    ohg = of_ref[...] + ob_ref[...]
    nw = hnw_ref[...]
    heads = []
    for h in range(ohg.shape[1] // HG_DK):
        oh = ohg[:, h * HG_DK:(h + 1) * HG_DK]
        ms = jnp.mean(oh * oh, axis=-1, keepdims=True)
        heads.append(oh * lax.rsqrt(ms + NORM_EPS) * nw)
    y_hg = jnp.concatenate(heads, axis=-1) * _silu(g_ref[...].astype(F32))
    gate = lambda i: jax.nn.sigmoid(gate_ref[:, i * d:(i + 1) * d].astype(F32))
    merged = gate(0) * _bdot(yhy_ref[...], wbhy_ref[...])
    merged += gate(1) * _bdot(yfn_ref[...], wbfn_ref[...])
    merged += gate(2) * _bdot(y_hg.astype(BF16), wbhg_ref[...])
    out = _bdot(merged.astype(BF16), wo_ref[...])
    o_ref[...] = x_ref[...] + mod_ref[0, 2:3, :] * out


def _merge_out(y_hy, y_fn, o_f, o_b, zm, x, hg_nw, mods, wb_hy, wb_fn, wb_hg, w_o,
               *, rows, seq, batch, tm=256):
    d = x.shape[1]
    w = y_hy.shape[1]
    tm = min(tm, rows)
    row = lambda m: (m, 0)
    const = lambda m: (0, 0)
    return pl.pallas_call(
        _merge_kernel,
        grid=(rows // tm,),
        in_specs=[
            pl.BlockSpec((tm, w), row),
            pl.BlockSpec((tm, w), row),
            pl.BlockSpec((tm, w), row),
            pl.BlockSpec((tm, w), row),
            pl.BlockSpec((tm, w), lambda m: (m, ZM_G // w)),
            pl.BlockSpec((tm, 3 * d), lambda m: (m, ZM_GATE // (3 * d))),
            pl.BlockSpec((tm, d), row),
            pl.BlockSpec((1, HG_DK), const),
            pl.BlockSpec((1, 6, d), lambda m: (_group_of(m, tm, seq, batch), 0, 0)),
            _resident((w, d), const),
            _resident((w, d), const),
            _resident((w, d), const),
            _resident((d, d), const),
        ],
        out_specs=pl.BlockSpec((tm, d), row),
        out_shape=jax.ShapeDtypeStruct((rows, d), F32),
        compiler_params=_cparams(1),
        name="merge_out",
    )(y_hy, y_fn, o_f, o_b, zm, zm, x, hg_nw.reshape(1, HG_DK), mods, wb_hy, wb_fn, wb_hg, w_o)


def _ffn_kernel(x_ref, nw_ref, mod_ref, w1_ref, w3_ref, w2_ref, o_ref, h_scr, acc_scr):
    f = pl.program_id(1)

    @pl.when(f == 0)
    def _():
        h = _norm_mod(x_ref[...], nw_ref[...], mod_ref[0, 3:4, :], mod_ref[0, 4:5, :])
        h_scr[...] = h.astype(BF16)
        acc_scr[...] = jnp.zeros_like(acc_scr)

    h = h_scr[...]
    act = _silu(_bdot(h, w1_ref[...])) * _bdot(h, w3_ref[...])
    acc_scr[...] += _bdot(act.astype(BF16), w2_ref[...])

    @pl.when(f == pl.num_programs(1) - 1)
    def _():
        o_ref[...] = x_ref[...] + mod_ref[0, 5:6, :] * acc_scr[...]


def _ffn_dense(x, nw, mods, w1, w3, w2, *, rows, seq, batch, tm=512, tf=512):
    d = x.shape[1]
    ff = w1.shape[1]
    tm = min(tm, rows)
    return pl.pallas_call(
        _ffn_kernel,
        grid=(rows // tm, ff // tf),
        in_specs=[
            pl.BlockSpec((tm, d), lambda m, f: (m, 0)),
            pl.BlockSpec((1, d), lambda m, f: (0, 0)),
            pl.BlockSpec((1, 6, d), lambda m, f: (_group_of(m, tm, seq, batch), 0, 0)),
            pl.BlockSpec((d, tf), lambda m, f: (0, f)),
            pl.BlockSpec((d, tf), lambda m, f: (0, f)),
            pl.BlockSpec((tf, d), lambda m, f: (f, 0)),
        ],
        out_specs=pl.BlockSpec((tm, d), lambda m, f: (m, 0)),
        out_shape=jax.ShapeDtypeStruct((rows, d), F32),
        scratch_shapes=[pltpu.VMEM((tm, d), BF16), pltpu.VMEM((tm, d), F32)],
        compiler_params=_cparams(2),
        name="ffn_dense",
    )(x, nw.reshape(1, d), mods, w1, w3, w2)


def _top2(logits, n_experts):
    lane = lax.broadcasted_iota(jnp.int32, logits.shape, 1).astype(F32)
    neg = jnp.float32(-jnp.inf)
    lg = jnp.where(lane < n_experts, logits, neg)
    m1 = jnp.max(lg, axis=-1, keepdims=True)
    i1 = jnp.min(jnp.where(lg == m1, lane, float(LANE)), axis=-1, keepdims=True)
    lg2 = jnp.where(lane == i1, neg, lg)
    m2 = jnp.max(lg2, axis=-1, keepdims=True)
    i2 = jnp.min(jnp.where(lg2 == m2, lane, float(LANE)), axis=-1, keepdims=True)
    e2 = jnp.exp(m2 - m1)
    return i1, i2, 1.0 / (1.0 + e2), e2 / (1.0 + e2)


def _route_kernel(x_ref, nw_ref, mod_ref, r_ref, h_ref, route_ref):
    h = _norm_mod(x_ref[...], nw_ref[...], mod_ref[0, 3:4, :], mod_ref[0, 4:5, :])
    logits = jnp.dot(h, r_ref[...], preferred_element_type=F32, precision=lax.Precision.HIGHEST)
    i1, i2, p1, p2 = _top2(logits, N_EXPERTS)
    lane = lax.broadcasted_iota(jnp.int32, logits.shape, 1)
    route = jnp.where(lane == 0, i1, jnp.where(lane == 1, i2,
                      jnp.where(lane == 2, p1, jnp.where(lane == 3, p2, 0.0))))
    h_ref[...] = h
    route_ref[...] = route


def _moe_route(x, nw, mods, router, *, rows, seq, batch, tm=512):
    d = x.shape[1]
    rpad = jnp.zeros((d, LANE), F32).at[:, :router.shape[1]].set(router)
    return pl.pallas_call(
        _route_kernel,
        grid=(rows // tm,),
        in_specs=[
            pl.BlockSpec((tm, d), lambda m: (m, 0)),
            pl.BlockSpec((1, d), lambda m: (0, 0)),
            pl.BlockSpec((1, 6, d), lambda m: (_group_of(m, tm, seq, batch), 0, 0)),
            pl.BlockSpec((d, LANE), lambda m: (0, 0)),
        ],
        out_specs=[pl.BlockSpec((tm, d), lambda m: (m, 0)),
                   pl.BlockSpec((tm, LANE), lambda m: (m, 0))],
        out_shape=[jax.ShapeDtypeStruct((rows, d), F32), jax.ShapeDtypeStruct((rows, LANE), F32)],
        compiler_params=_cparams(1),
        name="moe_route",
    )(x, nw.reshape(1, d), mods, rpad)


def _moe_plan(route, n_experts, tile):
    rows = route.shape[0]
    e = route[:, :2].astype(jnp.int32).reshape(-1)
    onehot = (e[:, None] == jnp.arange(n_experts, dtype=jnp.int32)[None, :]).astype(jnp.int32)
    before = jnp.cumsum(onehot, axis=0) - onehot
    rank = jnp.sum(before * onehot, axis=1)
    counts = jnp.sum(onehot, axis=0)
    padded = ((counts + tile - 1) // tile) * tile
    ends = jnp.cumsum(padded)
    starts = ends - padded
    pos = jnp.sum(starts[None, :] * onehot, axis=1) + rank
    n_tiles = (2 * rows) // tile + n_experts
    src = jnp.zeros((n_tiles * tile,), jnp.int32).at[pos].set(
        jnp.arange(2 * rows, dtype=jnp.int32) // 2)
    tile_start = jnp.arange(n_tiles, dtype=jnp.int32) * tile
    tile_expert = jnp.sum((tile_start[:, None] >= ends[None, :]).astype(jnp.int32), axis=1)
    tile_expert = jnp.minimum(tile_expert, n_experts - 1)
    n_used = (ends[-1] // tile).reshape(1)
    return pos, src, tile_expert, n_used


def _row_copy(src_hbm, src_row, dst, dst_row, sem):
    return pltpu.make_async_copy(src_hbm.at[pl.ds(src_row, 1)], dst.at[pl.ds(dst_row, 1)], sem)


def _gather_rows(idx_ref, base, stride, src_hbm, dst, sem, *, wait):
    if wait:
        pltpu.make_async_copy(src_hbm.at[pl.ds(0, dst.shape[0])], dst, sem).wait()
        return

    def body(r, carry):
        _row_copy(src_hbm, idx_ref[base + stride * r], dst, r, sem).start()
        return carry

    lax.fori_loop(0, dst.shape[0], body, 0, unroll=8)


def _moe_ffn_kernel(te_ref, nu_ref, src_ref, h_hbm, w1_ref, w3_ref, w2_ref, o_ref,
                    rows_scr, h_scr, acc_scr, sems):
    t = pl.program_id(0)
    f = pl.program_id(1)
    last = pl.num_programs(1) - 1
    tile = o_ref.shape[0]
    n_used = nu_ref[0]
    used = t < n_used
    slot = t % 2

    def gather(tile_idx, buf, wait):
        _gather_rows(src_ref, tile_idx * tile, 1, h_hbm, rows_scr.at[buf], sems.at[buf], wait=wait)

    @pl.when(used & (f == 0) & (t == 0))
    def _():
        gather(t, slot, False)

    @pl.when(used & (f == 0))
    def _():
        gather(t, slot, True)
        h_scr[...] = rows_scr[slot].astype(BF16)
        acc_scr[...] = jnp.zeros_like(acc_scr)

    @pl.when((f == 0) & (t + 1 < n_used))
    def _():
        gather(t + 1, 1 - slot, False)

    @pl.when(used)
    def _():
        h = h_scr[...]
        act = _silu(_bdot(h, w1_ref[0])) * _bdot(h, w3_ref[0])
        acc_scr[...] += _bdot(act.astype(BF16), w2_ref[0])

    @pl.when(used & (f == last))
    def _():
        o_ref[...] = acc_scr[...]

    @pl.when(jnp.logical_not(used) & (f == last))
    def _():
        o_ref[...] = jnp.zeros_like(o_ref)


def _moe_ffn(h, src, tile_expert, n_used, w1, w3, w2, *, tile, tf=512):
    d = h.shape[1]
    n = src.shape[0]
    ff = w1.shape[2]
    nf = ff // tf
    fsel = lambda t, f, nu: jnp.where(t < nu[0], f, nf - 1)
    return pl.pallas_call(
        _moe_ffn_kernel,
        grid_spec=pltpu.PrefetchScalarGridSpec(
            num_scalar_prefetch=3,
            grid=(n // tile, nf),
            in_specs=[
                pl.BlockSpec(memory_space=pl.ANY),
                pl.BlockSpec((1, d, tf), lambda t, f, te, nu, sr: (te[t], 0, fsel(t, f, nu))),
                pl.BlockSpec((1, d, tf), lambda t, f, te, nu, sr: (te[t], 0, fsel(t, f, nu))),
                pl.BlockSpec((1, tf, d), lambda t, f, te, nu, sr: (te[t], fsel(t, f, nu), 0)),
            ],
            out_specs=pl.BlockSpec((tile, d), lambda t, f, te, nu, sr: (t, 0)),
            scratch_shapes=[pltpu.VMEM((2, tile, d), F32), pltpu.VMEM((tile, d), BF16),
                            pltpu.VMEM((tile, d), F32), pltpu.SemaphoreType.DMA((2,))],
        ),
        out_shape=jax.ShapeDtypeStruct((n, d), F32),
        compiler_params=_cparams(2),
        name="moe_ffn",
    )(tile_expert, n_used, src, h, w1, w3, w2)


def _combine_kernel(pos_ref, ys_hbm, x_ref, route_ref, mod_ref, o_ref, y1_scr, y2_scr, sem):
    base = 2 * pl.program_id(0) * x_ref.shape[0]
    _gather_rows(pos_ref, base, 2, ys_hbm, y1_scr, sem, wait=False)
    _gather_rows(pos_ref, base + 1, 2, ys_hbm, y2_scr, sem, wait=False)
    _gather_rows(pos_ref, base, 2, ys_hbm, y1_scr, sem, wait=True)
    _gather_rows(pos_ref, base + 1, 2, ys_hbm, y2_scr, sem, wait=True)
    mix = route_ref[:, 2:3] * y1_scr[...] + route_ref[:, 3:4] * y2_scr[...]
    o_ref[...] = x_ref[...] + mod_ref[0, 5:6, :] * mix


def _moe_combine(x, mods, route, ys, pos, *, rows, seq, batch, tm=256):
    d = x.shape[1]
    return pl.pallas_call(
        _combine_kernel,
        grid_spec=pltpu.PrefetchScalarGridSpec(
            num_scalar_prefetch=1,
            grid=(rows // tm,),
            in_specs=[
                pl.BlockSpec(memory_space=pl.ANY),
                pl.BlockSpec((tm, d), lambda m, pos: (m, 0)),
                pl.BlockSpec((tm, LANE), lambda m, pos: (m, 0)),
                pl.BlockSpec((1, 6, d), lambda m, pos: (_group_of(m, tm, seq, batch), 0, 0)),
            ],
            out_specs=pl.BlockSpec((tm, d), lambda m, pos: (m, 0)),
            scratch_shapes=[pltpu.VMEM((tm, d), F32), pltpu.VMEM((tm, d), F32),
                            pltpu.SemaphoreType.DMA(())],
        ),
        out_shape=jax.ShapeDtypeStruct((rows, d), F32),
        compiler_params=_cparams(1),
        name="moe_combine",
    )(pos, ys, x, route, mods)


def _moe_top2(x, nw, mods, router, w1, w3, w2, *, rows, seq, batch, tm, tile=512):
    tok = dict(rows=rows, seq=seq, batch=batch)
    h, route = _moe_route(x, nw, mods, router, tm=tm, **tok)
    pos, src, tile_expert, n_used = _moe_plan(route, w1.shape[0], tile)
    ys = _moe_ffn(h, src, tile_expert, n_used, w1, w3, w2, tile=tile)
    return _moe_combine(x, mods, route, ys, pos, tm=min(tm, 256), **tok)


def _final_norm_kernel(x_ref, w_ref, o_ref):
    x = x_ref[...]
    ms = jnp.mean(x * x, axis=-1, keepdims=True)
    o_ref[...] = x * lax.rsqrt(ms + NORM_EPS) * w_ref[...]


def _final_norm(x, w, *, rows, tm=512):
    d = x.shape[1]
    return pl.pallas_call(
        _final_norm_kernel,
        grid=(rows // tm,),
        in_specs=[pl.BlockSpec((tm, d), lambda m: (m, 0)), pl.BlockSpec((1, d), lambda m: (0, 0))],
        out_specs=pl.BlockSpec((tm, d), lambda m: (m, 0)),
        out_shape=jax.ShapeDtypeStruct((rows, d), F32),
        compiler_params=_cparams(1),
        name="final_norm",
    )(x, w.reshape(1, d))


def _dft_angles(n, period):
    k = lax.broadcasted_iota(jnp.int32, (n, n), 0)
    s = lax.broadcasted_iota(jnp.int32, (n, n), 1)
    return ((k * s) % period).astype(F32) * (2.0 * math.pi / period)


def _hyena_dft(seq_len):
    n = 2 * seq_len
    ang = _dft_angles(seq_len, n)
    row = lax.broadcasted_iota(jnp.int32, ang.shape, 0)
    col = lax.broadcasted_iota(jnp.int32, ang.shape, 1)
    cos = jnp.cos(ang)
    sin = jnp.sin(ang)
    fc = cos
    fs = jnp.where(row == 0, jnp.where(col % 2 == 0, 1.0, -1.0), sin)
    ic = jnp.where(col == 0, 1.0 / n, 2.0 / n * cos)
    isn = jnp.where(col == 0, jnp.where(row % 2 == 0, 1.0, -1.0) / n, 2.0 / n * sin)
    return fc.astype(BF16), fs.astype(BF16), ic.astype(BF16), isn.astype(BF16)


def _hyena_feats(seq_len):
    pos = np.arange(seq_len, dtype=np.float64)
    t = pos / max(seq_len - 1, 1)
    bands = np.linspace(1e-4, HY_BANDS - 1, HY_BANDS).astype(np.float32).astype(np.float64)
    ang = 2 * math.pi * pos[:, None] * bands[None, :] / seq_len
    feats = np.concatenate([t[:, None], np.cos(ang), -np.sin(ang)], axis=-1)
    out = np.zeros((seq_len, LANE), np.float32)
    out[:, :HY_POS_DIM] = feats
    return jnp.asarray(out)


def _split_dot(w, a):
    hi = a.astype(BF16)
    lo = (a - hi.astype(F32)).astype(BF16)
    return _bdot(w, hi) + _bdot(w, lo)


def _split2(a):
    hi = a.astype(BF16)
    return hi, (a - hi.astype(F32)).astype(BF16)


def _dot3(a, b):
    a_hi, a_lo = _split2(a)
    b_hi, b_lo = _split2(b)
    return _bdot(a_hi, b_hi) + _bdot(a_hi, b_lo) + _bdot(a_lo, b_hi)


def _filter_kernel(feat_ref, w1_ref, b1_ref, w2_ref, b2_ref, fr_ref, w3f_ref, w3b_ref,
                   dcf_ref, dcb_ref, fc_ref, fs_ref, o_ref, h_scr):
    @pl.when((pl.program_id(1) == 0) & (pl.program_id(2) == 0))
    def _():
        hp = lax.Precision.HIGHEST
        h1 = jnp.sin(fr_ref[0, 0:1, :] * (jnp.dot(feat_ref[...], w1_ref[0], precision=hp,
                                                  preferred_element_type=F32) + b1_ref[0]))
        h_scr[...] = jnp.sin(fr_ref[0, 1:2, :] * (jnp.dot(h1, w2_ref[0], precision=hp,
                                                          preferred_element_type=F32) + b2_ref[0]))

    h = h_scr[...]
    t = feat_ref[:, 0:1]
    hf = _dot3(h, w3f_ref[0]) * jnp.exp(-t * dcf_ref[0])
    hb = _dot3(h, w3b_ref[0]) * jnp.exp(-t * dcb_ref[0])
    row = lax.broadcasted_iota(jnp.int32, hf.shape, 0)
    hb = jnp.where(row == 0, 0.0, hb)
    ss = jnp.sum(hf * hf + hb * hb, axis=0, keepdims=True)
    scale = lax.rsqrt(ss + NORM_EPS)
    even = hf * scale + hb * scale
    odd = hf * scale - hb * scale
    kc = _split_dot(fc_ref[...], even)
    ks = _split_dot(fs_ref[...], odd)
    sign = jnp.where(row % 2 == 0, 1.0, -1.0)
    nyq = jnp.sum(even * sign, axis=0, keepdims=True)
    o_ref[0, 0] = kc
    o_ref[0, 1] = jnp.where(row == 0, nyq, ks)


def _hyena_filters(seq_len, w1p, b1, w2, b2, w3, freq, decay, fc, fs, *, tn=256):
    depth = w1p.shape[0]
    hid = HY_FILTER_HIDDEN
    nb = HY_W // tn
    feats = _hyena_feats(seq_len)
    col = lambda d: (lambda l, o, c: (l, 0, (2 * o + d) * nb + c))
    lyr = lambda l, o, c: (l, 0, 0)
    return pl.pallas_call(
        _filter_kernel,
        grid=(depth, 2, nb),
        in_specs=[
            pl.BlockSpec((seq_len, LANE), lambda l, o, c: (0, 0)),
            pl.BlockSpec((1, LANE, hid), lyr),
            pl.BlockSpec((1, 1, hid), lyr),
            pl.BlockSpec((1, hid, hid), lyr),
            pl.BlockSpec((1, 1, hid), lyr),
            pl.BlockSpec((1, 2, hid), lyr),
            pl.BlockSpec((1, hid, tn), col(0)),
            pl.BlockSpec((1, hid, tn), col(1)),
            pl.BlockSpec((1, 1, tn), col(0)),
            pl.BlockSpec((1, 1, tn), col(1)),
            _resident((seq_len, seq_len), lambda l, o, c: (0, 0)),
            _resident((seq_len, seq_len), lambda l, o, c: (0, 0)),
        ],
        out_specs=pl.BlockSpec((1, 2, seq_len, tn), lambda l, o, c: (l, 0, 0, o * nb + c)),
        out_shape=jax.ShapeDtypeStruct((depth, 2, seq_len, 2 * HY_W), F32),
        scratch_shapes=[pltpu.VMEM((seq_len, hid), F32)],
        compiler_params=_cparams(3),
        name="hyena_filters",
    )(feats, w1p, b1.reshape(depth, 1, hid), w2, b2.reshape(depth, 1, hid), freq,
      w3, w3, decay.reshape(depth, 1, 4 * HY_W), decay.reshape(depth, 1, 4 * HY_W), fc, fs)


def _conv3(z, w_ref, b_ref):
    n = z.shape[0]
    row = lax.broadcasted_iota(jnp.int32, z.shape, 0)
    prev = jnp.where(row == 0, 0.0, pltpu.roll(z, 1, 0))
    nxt = jnp.where(row == n - 1, 0.0, pltpu.roll(z, n - 1, 0))
    return prev * w_ref[0:1, :] + z * w_ref[1:2, :] + nxt * w_ref[2:3, :] + b_ref[...]


def _longconv_kernel(a_ref, g_ref, aw_ref, ab_ref, gw_ref, gb_ref, kf_ref, skip_ref,
                     fc_ref, fs_ref, ic_ref, is_ref, o_ref, acc_scr, *, conv_a, fblk):
    a = a_ref[...].astype(F32)
    if conv_a:
        a = _conv3(a, aw_ref, ab_ref)
    a16 = a.astype(BF16)
    n = a.shape[0]
    for j in range(n // fblk):
        rows = slice(j * fblk, (j + 1) * fblk)
        ac = _bdot(fc_ref[rows, :], a16)
        asn = _bdot(fs_ref[rows, :], a16)
        kc = kf_ref[0, 0, rows, :]
        ks = kf_ref[0, 1, rows, :]
        cross = asn * ks
        yc = ac * kc - cross
        ys = ac * ks + asn * kc
        if j == 0:
            row = lax.broadcasted_iota(jnp.int32, ac.shape, 0)
            yc = jnp.where(row == 0, ac * kc, yc)
            ys = jnp.where(row == 0, cross, ys)
        part = _bdot(ic_ref[:, rows], yc.astype(BF16)) + _bdot(is_ref[:, rows], ys.astype(BF16))
        if j == 0:
            acc_scr[...] = part
        else:
            acc_scr[...] += part
    gate = _conv3(g_ref[...].astype(F32), gw_ref, gb_ref)
    o_ref[...] = (gate * (acc_scr[...] + a * skip_ref[0])).astype(o_ref.dtype)


def _hyena_longconv(a_src, a_part, zm, gate_part, conv_w, conv_b, kf, layer, order, skip,
                    dft, *, seq_len, n_seq, row0, out_rows, prev_out=None, tn=256):
    fc, fs, ic, isn = dft
    nb = HY_W // tn
    rb0 = row0 // seq_len
    fblk = min(512, seq_len)
    const = lambda c, b: (0, 0)
    conv_a = a_part is not None
    aw_blk = a_part * nb if conv_a else 0
    a_blk = ZM_HY // tn + aw_blk if conv_a else 0
    gw_blk = gate_part * nb
    g_blk = ZM_HY // tn + gw_blk
    kernel = functools.partial(_longconv_kernel, conv_a=conv_a, fblk=fblk)
    in_specs = [
        pl.BlockSpec((seq_len, tn), lambda c, b: (rb0 + b, a_blk + c)),
        pl.BlockSpec((seq_len, tn), lambda c, b: (rb0 + b, g_blk + c)),
        pl.BlockSpec((3, tn), lambda c, b: (0, aw_blk + c)),
        pl.BlockSpec((1, tn), lambda c, b: (0, aw_blk + c)),
        pl.BlockSpec((3, tn), lambda c, b: (0, gw_blk + c)),
        pl.BlockSpec((1, tn), lambda c, b: (0, gw_blk + c)),
        pl.BlockSpec((1, 2, seq_len, tn), lambda c, b: (layer, 0, 0, order * nb + c)),
        pl.BlockSpec((1, 1, tn), lambda c, b: (order, 0, c)),
        _resident((seq_len, seq_len), const),
        _resident((seq_len, seq_len), const),
        _resident((seq_len, seq_len), const),
        _resident((seq_len, seq_len), const),
    ]
    args = [a_src, zm, conv_w, conv_b.reshape(1, -1), conv_w, conv_b.reshape(1, -1), kf,
            skip.reshape(2, 1, HY_W), fc, fs, ic, isn]
    aliases = {}
    if prev_out is not None:
        in_specs.append(pl.BlockSpec(memory_space=pl.ANY))
        args.append(prev_out)
        aliases = {len(args) - 1: 0}
        kernel = functools.partial(_drop_last_input, kernel, 12)
    return pl.pallas_call(
        kernel,
        grid=(nb, n_seq),
        in_specs=in_specs,
        out_specs=pl.BlockSpec((seq_len, tn), lambda c, b: (rb0 + b, c)),
        out_shape=jax.ShapeDtypeStruct((out_rows, HY_W), BF16),
        scratch_shapes=[pltpu.VMEM((seq_len, tn), F32)],
        input_output_aliases=aliases,
        compiler_params=_cparams(2),
        name="hyena_longconv",
    )(*args)


def _drop_last_input(kernel, n_in, *refs):
    return kernel(*refs[:n_in], *refs[n_in + 1:])


def _fnet_dft(seq_len):
    ang_l = _dft_angles(seq_len, seq_len)
    ang_w = _dft_angles(FN_GROUP_W, FN_GROUP_W)
    scale = 1.0 / math.sqrt(seq_len * FN_GROUP_W)
    return ((jnp.cos(ang_l) * scale).astype(BF16), (jnp.sin(ang_l) * scale).astype(BF16),
            jnp.cos(ang_w).astype(BF16), jnp.sin(ang_w).astype(BF16))


def _fnet_kernel(z_ref, cl_ref, sl_ref, cw_ref, sw_ref, o_ref):
    z = z_ref[...]
    gc, gs = [], []
    for j in range(z.shape[1] // FN_GROUP_W):
        zj = z[:, j * FN_GROUP_W:(j + 1) * FN_GROUP_W]
        gc.append(_bdot(zj, cw_ref[...]).astype(BF16))
        gs.append(_bdot(zj, sw_ref[...]).astype(BF16))
    gc = jnp.concatenate(gc, axis=-1)
    gs = jnp.concatenate(gs, axis=-1)
    o_ref[...] = (_bdot(cl_ref[...], gc) - _bdot(sl_ref[...], gs)).astype(o_ref.dtype)


def _fnet(zm, dft, *, seq_len, n_seq, row0, out_rows, prev_out=None, tn=256):
    cl, sl, cw, sw = dft
    nb = FN_W // tn
    rb0 = row0 // seq_len
    c0 = ZM_FN // tn
    const = lambda c, b: (0, 0)
    in_specs = [
        pl.BlockSpec((seq_len, tn), lambda c, b: (rb0 + b, c0 + c)),
        _resident((seq_len, seq_len), const),
        _resident((seq_len, seq_len), const),
        pl.BlockSpec((FN_GROUP_W, FN_GROUP_W), const),
        pl.BlockSpec((FN_GROUP_W, FN_GROUP_W), const),
    ]
    args = [zm, cl, sl, cw, sw]
    kernel = _fnet_kernel
    aliases = {}
    if prev_out is not None:
        in_specs.append(pl.BlockSpec(memory_space=pl.ANY))
        args.append(prev_out)
        aliases = {len(args) - 1: 0}
        kernel = functools.partial(_drop_last_input, _fnet_kernel, 5)
    return pl.pallas_call(
        kernel,
        grid=(nb, n_seq),
        in_specs=in_specs,
        out_specs=pl.BlockSpec((seq_len, tn), lambda c, b: (rb0 + b, c)),
        out_shape=jax.ShapeDtypeStruct((out_rows, FN_W), BF16),
        input_output_aliases=aliases,
        compiler_params=_cparams(2),
        name="fnet",
    )(*args)


def _hgrn_tables(reverse):
    c = HG_CHUNK
    tri = np.zeros((c, c), np.float32)
    for t in range(c):
        if reverse:
            tri[t, t:] = 1.0
        else:
            tri[t, :t + 1] = 1.0
    mats = [tri]
    masks = []
    for lvl in range(HG_LEVELS):
        half = (c // 2) >> lvl
        sel = np.zeros((c, c), np.float32)
        mask = np.zeros((c, c), np.float32)
        for t in range(c):
            start = (t // (2 * half)) * 2 * half
            mid = start + half
            sel[t, mid if reverse else mid - 1] = 1.0
            for s in range(start, start + 2 * half):
                if reverse and t < mid <= s:
                    mask[t, s] = 1.0
                if (not reverse) and s < mid <= t:
                    mask[t, s] = 1.0
        mats.append(tri - sel @ tri)
        masks.append(mask)
    half = c // 2
    sel = np.zeros((c, c), np.float32)
    mask = np.zeros((c, c), np.float32)
    for t in range(c):
        start = (t // half) * half
        sel[t, start + half // 2] = 1.0
        for s in range(start, start + half):
            if (s >= t) if reverse else (s <= t):
                mask[t, s] = 1.0
    mats.append(tri - sel @ tri)
    masks.append(mask)
    mats = [mats[0], mats[1], mats[-1]] + mats[2:-1]
    return np.concatenate(mats, axis=0), np.stack(masks, axis=0)


HG_BOUNDED_BLOCKS = 3
HG_MID_BLOCK = 2


def _hgrn_level_block(lvl):
    return 1 if lvl == 0 else HG_BOUNDED_BLOCKS + lvl - 1


def _hgrn_decays(zf_ref, lb_ref, tab_ref, k_scr, p_scr, dec_scr):
    lb = lb_ref[...]
    f = lb + (1.0 - lb) * jax.nn.sigmoid(zf_ref[...])
    lf = jnp.log2(f)
    k_scr[...] = (1.0 - f).astype(BF16)
    p1 = lf.astype(BF16)
    r1 = lf - p1.astype(F32)
    p2 = r1.astype(BF16)
    p3 = (r1 - p2.astype(F32)).astype(BF16)
    p_scr[...] = jnp.concatenate([p1, p2, p3], axis=0)
    n = HG_BOUNDED_BLOCKS * HG_CHUNK
    dec_scr[0:n, :] = _bdot(tab_ref[0:n, :], p_scr[...])


def _hgrn_more_decays(tab_ref, p_scr, dec_scr):
    n = HG_BOUNDED_BLOCKS * HG_CHUNK
    dec_scr[n:, :] = _bdot(tab_ref[n:, :], p_scr[...])


def _nt_dot(a, b):
    return lax.dot_general(a, b, (((1,), (1,)), ((), ())), preferred_element_type=F32)


def _hgrn_direction(q_ref, v_ref, k_scr, dec_scr, mask_ref, st_scr, o_ref, end_row, bounded):
    c = HG_CHUNK
    heads = lambda ref, r0: jnp.stack(
        [ref[r0:r0 + c, h * HG_DK:(h + 1) * HG_DK] for h in range(HG_HEADS)], axis=0)
    bmm = lambda eq, a, b: jnp.einsum(eq, a, b, preferred_element_type=F32)
    q = heads(q_ref, 0)
    k = heads(k_scr, 0)
    v = heads(v_ref, 0)
    b = heads(dec_scr, 0)
    st = st_scr[...]
    o = bmm('htk,hkv->htv', q * jnp.exp2(b).astype(BF16), st.astype(BF16))
    level = lambda l: jnp.exp2(-jnp.abs(heads(dec_scr, _hgrn_level_block(l) * c))).astype(BF16)
    if bounded:
        e = level(0)
        att = jnp.where(mask_ref[0] > 0.5, bmm('htk,hsk->hts', q * e, k * e), 0.0)
        d = heads(dec_scr, HG_MID_BLOCK * c)
        inner = bmm('htk,hsk->hts', q * jnp.exp2(d).astype(BF16), k * jnp.exp2(-d).astype(BF16))
        att += jnp.where(mask_ref[HG_LEVELS] > 0.5, inner, 0.0)
    else:
        diag = jnp.sum(q.astype(F32) * k.astype(F32), axis=-1, keepdims=True)
        o += diag * v.astype(F32)
        att = jnp.zeros((HG_HEADS, c, c), F32)
        for lvl in range(HG_LEVELS):
            e = level(lvl)
            att += bmm('htk,hsk->hts', q * e, k * e) * mask_ref[lvl]
    o += bmm('hts,hsv->htv', att.astype(BF16), v)
    for h in range(HG_HEADS):
        o_ref[:, h * HG_DK:(h + 1) * HG_DK] = o[h]
    b_end = b[:, end_row:end_row + 1, :]
    upd = bmm('htk,htv->hkv', k * jnp.exp2(b_end - b).astype(BF16), v)
    decay = jnp.swapaxes(jnp.broadcast_to(jnp.exp2(b_end), (HG_HEADS, HG_DK, HG_DK)), 1, 2)
    st_scr[...] = decay * st + upd


HG_MAX_LOG2_SPAN = 100.0
HG_MAX_ABS_Q = 1e6


def _hgrn_kernel(qf_ref, vf_ref, zff_ref, qb_ref, vb_ref, zfb_ref, lbf_ref, lbb_ref,
                 tabf_ref, maskf_ref, tabb_ref, maskb_ref, s0f_ref, s0b_ref,
                 of_ref, ob_ref, sf_ref, sb_ref, stf_scr, stb_scr, kf_scr, kb_scr,
                 pf_scr, pb_scr, decf_scr, decb_scr):
    ci = pl.program_id(1)
    c = HG_CHUNK

    @pl.when(ci == 0)
    def _():
        stf_scr[...] = s0f_ref[0]
        stb_scr[...] = s0b_ref[0]

    _hgrn_decays(zff_ref, lbf_ref, tabf_ref, kf_scr, pf_scr, decf_scr)
    _hgrn_decays(zfb_ref, lbb_ref, tabb_ref, kb_scr, pb_scr, decb_scr)
    mid = slice(HG_MID_BLOCK * c, (HG_MID_BLOCK + 1) * c)
    span = jnp.maximum(jnp.max(jnp.abs(decf_scr[mid, :])), jnp.max(jnp.abs(decb_scr[mid, :])))
    q_abs = jnp.maximum(jnp.max(jnp.abs(qf_ref[...].astype(F32))),
                        jnp.max(jnp.abs(qb_ref[...].astype(F32))))
    bounded = (span < HG_MAX_LOG2_SPAN) & (q_abs < HG_MAX_ABS_Q)

    def run(flag):
        _hgrn_direction(qf_ref, vf_ref, kf_scr, decf_scr, maskf_ref, stf_scr, of_ref, c - 1, flag)
        _hgrn_direction(qb_ref, vb_ref, kb_scr, decb_scr, maskb_ref, stb_scr, ob_ref, 0, flag)

    @pl.when(bounded)
    def _():
        run(True)

    @pl.when(jnp.logical_not(bounded))
    def _():
        _hgrn_more_decays(tabf_ref, pf_scr, decf_scr)
        _hgrn_more_decays(tabb_ref, pb_scr, decb_scr)
        run(False)

    @pl.when(ci == pl.num_programs(1) - 1)
    def _():
        sf_ref[0] = stf_scr[...]
        sb_ref[0] = stb_scr[...]


def _hgrn(zm, zf, lb_f, lb_b, s0_f, s0_b, *, seq_len, n_seq, row0, out_rows,
          prev_of=None, prev_ob=None):
    c = HG_CHUNK
    nc = seq_len // c
    rb0 = row0 // c
    w = HG_W
    tab_f, mask_f = _hgrn_tables(False)
    tab_b, mask_b = _hgrn_tables(True)
    fwd = lambda col: (lambda b, i: (rb0 + b * nc + i, col))
    bwd = lambda col: (lambda b, i: (rb0 + b * nc + nc - 1 - i, col))
    const2 = lambda b, i: (0, 0)
    const3 = lambda b, i: (0, 0, 0)
    state = lambda b, i: (b, 0, 0, 0)
    nt = (2 + HG_LEVELS) * c
    in_specs = [
        pl.BlockSpec((c, w), fwd(ZM_Q // w)),
        pl.BlockSpec((c, w), fwd(ZM_I // w)),
        pl.BlockSpec((c, w), fwd(0)),
        pl.BlockSpec((c, w), bwd(ZM_Q // w)),
        pl.BlockSpec((c, w), bwd(ZM_I // w)),
        pl.BlockSpec((c, w), bwd(1)),
        pl.BlockSpec((1, w), const2),
        pl.BlockSpec((1, w), const2),
        pl.BlockSpec((nt, 3 * c), const2),
        pl.BlockSpec((HG_LEVELS + 1, c, c), const3),
        pl.BlockSpec((nt, 3 * c), const2),
        pl.BlockSpec((HG_LEVELS + 1, c, c), const3),
        pl.BlockSpec((1, HG_HEADS, HG_DK, HG_DK), state),
        pl.BlockSpec((1, HG_HEADS, HG_DK, HG_DK), state),
    ]
    args = [zm, zm, zf, zm, zm, zf, lb_f.reshape(1, w), lb_b.reshape(1, w),
            jnp.asarray(np.tile(tab_f, (1, 3)), BF16), jnp.asarray(mask_f),
            jnp.asarray(np.tile(tab_b, (1, 3)), BF16), jnp.asarray(mask_b), s0_f, s0_b]
    kernel = _hgrn_kernel
    aliases = {}
    if prev_of is not None:
        in_specs += [pl.BlockSpec(memory_space=pl.ANY), pl.BlockSpec(memory_space=pl.ANY)]
        args += [prev_of, prev_ob]
        aliases = {14: 0, 15: 1}
        kernel = functools.partial(_hgrn_alias_kernel, 14)
    st_shape = jax.ShapeDtypeStruct((n_seq, HG_HEADS, HG_DK, HG_DK), F32)
    return pl.pallas_call(
        kernel,
        grid=(n_seq, nc),
        in_specs=in_specs,
        out_specs=[
            pl.BlockSpec((c, w), fwd(0)),
            pl.BlockSpec((c, w), bwd(0)),
            pl.BlockSpec((1, HG_HEADS, HG_DK, HG_DK), state),
            pl.BlockSpec((1, HG_HEADS, HG_DK, HG_DK), state),
        ],
        out_shape=[jax.ShapeDtypeStruct((out_rows, w), F32),
                   jax.ShapeDtypeStruct((out_rows, w), F32), st_shape, st_shape],
        scratch_shapes=[pltpu.VMEM((HG_HEADS, HG_DK, HG_DK), F32),
                        pltpu.VMEM((HG_HEADS, HG_DK, HG_DK), F32),
                        pltpu.VMEM((c, w), BF16), pltpu.VMEM((c, w), BF16),
                        pltpu.VMEM((3 * c, w), BF16), pltpu.VMEM((3 * c, w), BF16),
                        pltpu.VMEM((nt, w), F32), pltpu.VMEM((nt, w), F32)],
        input_output_aliases=aliases,
        compiler_params=_cparams(2),
        name="hgrn2",
    )(*args)


def _hgrn_alias_kernel(n_in, *refs):
    return _hgrn_kernel(*refs[:n_in], *refs[n_in + 2:])


def kernel(x, c, ctx, c_ctx, w_mod, b_mod, norm1_w, norm2_w, w_in, hy_conv_w, hy_conv_b,
           hy_f_w1, hy_f_b1, hy_f_w2, hy_f_b2, hy_f_w3, hy_f_freq, hy_decay, hy_skip,
           hg_lower, hg_norm_w, w_br_hy, w_br_fn, w_br_hg, w_out,
           ffn_w1, ffn_w3, ffn_w2, moe_router, moe_w1, moe_w3, moe_w2, final_norm_w):
    batch, seq, d = x.shape
    l_ctx = ctx.shape[1]
    depth = w_in.shape[0]
    n_lat = batch * seq
    n_all = n_lat + batch * l_ctx

    lb_all = jnp.cumsum(jax.nn.softmax(hg_lower.astype(F32), axis=1), axis=1)
    lb_all = lb_all - lb_all[:, :1]

    cc = jnp.zeros((8, d), F32).at[:batch].set(c).at[batch].set(c_ctx)
    mods = _modulation(cc, w_mod, b_mod).reshape(depth, 8, 6, d)

    w1p = jnp.zeros((depth, LANE, HY_FILTER_HIDDEN), F32).at[:, :HY_POS_DIM].set(hy_f_w1)
    dft_lat = _hyena_dft(seq)
    dft_ctx = _hyena_dft(l_ctx)
    kf_lat = _hyena_filters(seq, w1p, hy_f_b1, hy_f_w2, hy_f_b2, hy_f_w3, hy_f_freq, hy_decay,
                            dft_lat[0], dft_lat[1])
    kf_ctx = _hyena_filters(l_ctx, w1p, hy_f_b1, hy_f_w2, hy_f_b2, hy_f_w3, hy_f_freq, hy_decay,
                            dft_ctx[0], dft_ctx[1])
    fn_lat = _fnet_dft(seq)
    fn_ctx = _fnet_dft(l_ctx)

    rows_x = jnp.concatenate([x.reshape(n_lat, d), ctx.reshape(batch * l_ctx, d)], axis=0)
    zero_state = jnp.zeros((batch, HG_HEADS, HG_DK, HG_DK), F32)
    tile_cap = math.gcd(seq, batch * l_ctx)
    tok = dict(seq=seq, batch=batch)
    tile = lambda t: dict(tm=min(t, tile_cap))
    lat = dict(seq_len=seq, n_seq=batch, row0=0, out_rows=n_all)
    cx = dict(seq_len=l_ctx, n_seq=batch, row0=n_lat, out_rows=n_all)

    for l in range(depth):
        last = l == depth - 1
        rows = n_lat if last else n_all
        m = mods[l]
        w_main = jnp.concatenate([w_in[l][:, GATE_OFF:], w_in[l][:, :HG_FF],
                                  w_in[l][:, HG_I:GATE_OFF]], axis=1).astype(BF16)
        w_forget = w_in[l][:, HG_FF:HG_I].astype(BF16)

        zm = _in_proj(rows_x, norm1_w[l], m, w_main, BF16, rows=n_all, **tok, **tile(1024))
        zf = _in_proj(rows_x, norm1_w[l], m, w_forget, F32, rows=n_all, **tok, **tile(1024))

        of, ob, s_f, s_b = _hgrn(zm, zf, lb_all[0, l], lb_all[1, l], zero_state, zero_state, **cx)
        of, ob, _, _ = _hgrn(zm, zf, lb_all[0, l], lb_all[1, l], s_f, s_b, prev_of=of,
                             prev_ob=ob, **lat)

        hy_args = (hy_conv_w[l], hy_conv_b[l])
        mix = dict(lat, out_rows=rows)
        u = _hyena_longconv(zm, 0, zm, 1, *hy_args, kf_lat, l, 0, hy_skip[l], dft_lat, **mix)
        y_hy = _hyena_longconv(u, None, zm, 2, *hy_args, kf_lat, l, 1, hy_skip[l], dft_lat, **mix)
        y_fn = _fnet(zm, fn_lat, **mix)
        if not last:
            u = _hyena_longconv(zm, 0, zm, 1, *hy_args, kf_ctx, l, 0, hy_skip[l], dft_ctx,
                                prev_out=u, **cx)
            y_hy = _hyena_longconv(u, None, zm, 2, *hy_args, kf_ctx, l, 1, hy_skip[l], dft_ctx,
                                   prev_out=y_hy, **cx)
            y_fn = _fnet(zm, fn_ctx, prev_out=y_fn, **cx)

        rows_x = _merge_out(y_hy, y_fn, of, ob, zm, rows_x, hg_norm_w[l], m,
                            w_br_hy[l].astype(BF16), w_br_fn[l].astype(BF16),
                            w_br_hg[l].astype(BF16), w_out[l].astype(BF16), rows=rows,
                            **tok, **tile(256))

        if l % 2 == 0:
            i = l // 2
            rows_x = _ffn_dense(rows_x, norm2_w[l], m, ffn_w1[i].astype(BF16),
                                ffn_w3[i].astype(BF16), ffn_w2[i].astype(BF16), rows=rows,
                                **tok, **tile(512))
        else:
            i = l // 2
            rows_x = _moe_top2(rows_x, norm2_w[l], m, moe_router[i], moe_w1[i].astype(BF16),
                                moe_w3[i].astype(BF16), moe_w2[i].astype(BF16), rows=rows,
                                **tok, **tile(512))

    out = _final_norm(rows_x, final_norm_w, rows=n_lat, **tile(512))
    return out.reshape(batch, seq, d)
```

```python
import functools
import math

import numpy as np
import jax
import jax.numpy as jnp
from jax import lax
from jax.experimental import pallas as pl
from jax.experimental.pallas import tpu as pltpu

F32 = jnp.float32
BF16 = jnp.bfloat16
NORM_EPS = 1e-6

D_MODEL = 2048
DEPTH = 4
HY_W = 1024
HY_BANDS = 16
HY_POS_DIM = 2 * HY_BANDS + 1
HY_FILTER_HIDDEN = 64
FN_W = 1024
FN_GROUP_W = 128
HG_W = 1024
HG_HEADS = 8
HG_DK = 128
HG_CHUNK = 64
HG_LEVELS = 6
N_EXPERTS = 8
LANE = 128

HG_FF = 3 * HY_W + FN_W + 2 * HG_W
HG_I = HG_FF + 2 * HG_W
ZM_HY = 0
ZM_FN = ZM_HY + 3 * HY_W
ZM_Q = ZM_FN + FN_W
ZM_G = ZM_Q + HG_W
ZM_I = HG_I
ZM_GATE = HG_I + HG_W
ZM_COLS = ZM_GATE + 3 * D_MODEL

VMEM_LIMIT = 56 * 1024 * 1024


def _cparams(n_axes, vmem=VMEM_LIMIT):
    return pltpu.CompilerParams(dimension_semantics=("arbitrary",) * n_axes, vmem_limit_bytes=vmem)


def _resident(shape, index_map):
    return pl.BlockSpec(shape, index_map, pipeline_mode=pl.Buffered(1))


def _silu(v):
    return v * jax.nn.sigmoid(v)


def _bdot(a, b):
    return jnp.dot(a, b, preferred_element_type=F32)


def _mod_kernel(c_ref, w_ref, b_ref, o_ref):
    c = _silu(c_ref[...])
    o_ref[0] = jnp.dot(c, w_ref[0], preferred_element_type=F32,
                       precision=lax.Precision.HIGHEST) + b_ref[0]


def _modulation(cc, w_mod, b_mod):
    depth, d, n6 = w_mod.shape
    tn = 2048
    return pl.pallas_call(
        _mod_kernel,
        grid=(depth, n6 // tn),
        in_specs=[
            pl.BlockSpec((8, d), lambda l, n: (0, 0)),
            pl.BlockSpec((1, d, tn), lambda l, n: (l, 0, n)),
            pl.BlockSpec((1, 1, tn), lambda l, n: (l, 0, n)),
        ],
        out_specs=pl.BlockSpec((1, 8, tn), lambda l, n: (l, 0, n)),
        out_shape=jax.ShapeDtypeStruct((depth, 8, n6), F32),
        compiler_params=_cparams(2),
        name="modulation",
    )(cc, w_mod, b_mod.reshape(depth, 1, n6))


def _norm_mod(x, nw, shift, scale):
    ms = jnp.mean(x * x, axis=-1, keepdims=True)
    return (x * lax.rsqrt(ms + NORM_EPS) * nw) * (1 + scale) + shift


def _group_of(row_block, tm, seq, batch):
    return jnp.minimum((row_block * tm) // seq, batch)


def _norm_rows_kernel(x_ref, nw_ref, mod_ref, o_ref):
    h = _norm_mod(x_ref[...], nw_ref[...], mod_ref[0, 0:1, :], mod_ref[0, 1:2, :])
    o_ref[...] = h.astype(o_ref.dtype)


def _norm_rows(x, nw, mods, *, seq, batch, tm=512):
    rows, d = x.shape
    return pl.pallas_call(
        _norm_rows_kernel,
        grid=(rows // tm,),
        in_specs=[
            pl.BlockSpec((tm, d), lambda m: (m, 0)),
            pl.BlockSpec((1, d), lambda m: (0, 0)),
            pl.BlockSpec((1, 6, d), lambda m: (_group_of(m, tm, seq, batch), 0, 0)),
        ],
        out_specs=pl.BlockSpec((tm, d), lambda m: (m, 0)),
        out_shape=jax.ShapeDtypeStruct((rows, d), BF16),
        compiler_params=_cparams(1),
        name="norm_rows",
    )(x, nw.reshape(1, d), mods)


def _inproj_kernel(h_ref, w_ref, z_ref, zf_ref, w_scr, *, n_main):
    @pl.when(pl.program_id(1) == 0)
    def _():
        w_scr[...] = w_ref[0].astype(BF16)

    r = _bdot(h_ref[...], w_scr[...])

    @pl.when(pl.program_id(0) < n_main)
    def _():
        z_ref[...] = r.astype(z_ref.dtype)

    @pl.when(pl.program_id(0) >= n_main)
    def _():
        zf_ref[...] = r


def _in_proj(h, w_in, layer, *, tm=1024, tn=1024):
    rows, d = h.shape
    n_cols = w_in.shape[2]
    nb = n_cols // tn
    f0 = HG_FF // tn
    nf = 2 * HG_W // tn
    n_main = nb - nf
    nm = rows // tm
    col = lambda j: jnp.where(j < f0, j, jnp.where(j < n_main, j + nf, j - n_main + f0))
    z_idx = lambda j, m: (jnp.where(j < n_main, m, nm - 1), col(jnp.minimum(j, n_main - 1)))
    zf_idx = lambda j, m: (jnp.where(j < n_main, 0, m), jnp.maximum(j - n_main, 0))
    return pl.pallas_call(
        functools.partial(_inproj_kernel, n_main=n_main),
        grid=(nb, nm),
        in_specs=[
            pl.BlockSpec((tm, d), lambda j, m: (m, 0)),
            pl.BlockSpec((1, d, tn), lambda j, m: (layer, 0, col(j))),
        ],
        out_specs=[pl.BlockSpec((tm, tn), z_idx), pl.BlockSpec((tm, tn), zf_idx)],
        out_shape=[jax.ShapeDtypeStruct((rows, n_cols), BF16),
                   jax.ShapeDtypeStruct((rows, nf * tn), F32)],
        scratch_shapes=[pltpu.VMEM((d, tn), BF16)],
        compiler_params=_cparams(2),
        name="in_proj",
    )(h, w_in)


def _merge_kernel(yhy_ref, yfn_ref, of_ref, ob_ref, g_ref, ga0_ref, ga1_ref, gb0_ref, gb1_ref,
                  gc0_ref, gc1_ref, x_ref, hnw_ref, mod_ref,
                  wbhy_ref, wbfn_ref, wbhg_ref, wo_ref, o_ref):
    d = x_ref.shape[1]
    ohg = of_ref[...] + ob_ref[...]
    nw = hnw_ref[...]
    heads = []
    for h in range(ohg.shape[1] // HG_DK):
        oh = ohg[:, h * HG_DK:(h + 1) * HG_DK]
        ms = jnp.mean(oh * oh, axis=-1, keepdims=True)
        heads.append(oh * lax.rsqrt(ms + NORM_EPS) * nw)
    y_hg = jnp.concatenate(heads, axis=-1) * _silu(g_ref[...].astype(F32))
    gate = lambda lo, hi: jax.nn.sigmoid(
        jnp.concatenate([lo[...], hi[...]], axis=-1).astype(F32))
    merged = gate(ga0_ref, ga1_ref) * _bdot(yhy_ref[...], wbhy_ref[...])
    merged += gate(gb0_ref, gb1_ref) * _bdot(yfn_ref[...], wbfn_ref[...])
    merged += gate(gc0_ref, gc1_ref) * _bdot(y_hg.astype(BF16), wbhg_ref[...])
    out = _bdot(merged.astype(BF16), wo_ref[...])
    o_ref[...] = x_ref[...] + mod_ref[0, 2:3, :] * out


def _merge_out(y_hy, y_fn, o_f, o_b, zm, x, hg_nw, mods, wb_hy, wb_fn, wb_hg, w_o,
               *, rows, seq, batch, tm=256):
    d = x.shape[1]
    w = y_hy.shape[1]
    tm = min(tm, rows)
    row = lambda m: (m, 0)
    const = lambda m: (0, 0)
    gate0 = ZM_GATE // (d // 2)
    return pl.pallas_call(
        _merge_kernel,
        grid=(rows // tm,),
        in_specs=[
            pl.BlockSpec((tm, w), row),
            pl.BlockSpec((tm, w), row),
            pl.BlockSpec((tm, w), row),
            pl.BlockSpec((tm, w), row),
            pl.BlockSpec((tm, w), lambda m: (m, ZM_G // w)),
            *[pl.BlockSpec((tm, d // 2), functools.partial(lambda i, m: (m, i), gate0 + i))
              for i in range(6)],
            pl.BlockSpec((tm, d), row),
            pl.BlockSpec((1, HG_DK), const),
            pl.BlockSpec((1, 6, d), lambda m: (_group_of(m, tm, seq, batch), 0, 0)),
            _resident((w, d), const),
            _resident((w, d), const),
            _resident((w, d), const),
            _resident((d, d), const),
        ],
        out_specs=pl.BlockSpec((tm, d), row),
        out_shape=jax.ShapeDtypeStruct((rows, d), F32),
        compiler_params=_cparams(1),
        name="merge_out",
    )(y_hy, y_fn, o_f, o_b, zm, *([zm] * 6), x, hg_nw.reshape(1, HG_DK), mods,
      wb_hy, wb_fn, wb_hg, w_o)


def _ffn_kernel(x_ref, nw_ref, mod_ref, w1_ref, w3_ref, w2_ref, o_ref, h_scr, acc_scr):
    f = pl.program_id(1)

    @pl.when(f == 0)
    def _():
        h = _norm_mod(x_ref[...], nw_ref[...], mod_ref[0, 3:4, :], mod_ref[0, 4:5, :])
        h_scr[...] = h.astype(BF16)
        acc_scr[...] = jnp.zeros_like(acc_scr)

    h = h_scr[...]
    act = _silu(_bdot(h, w1_ref[...])) * _bdot(h, w3_ref[...])
    acc_scr[...] += _bdot(act.astype(BF16), w2_ref[...])

    @pl.when(f == pl.num_programs(1) - 1)
    def _():
        o_ref[...] = x_ref[...] + mod_ref[0, 5:6, :] * acc_scr[...]


def _ffn_dense(x, nw, mods, w1, w3, w2, *, rows, seq, batch, tm=512, tf=512):
    d = x.shape[1]
    ff = w1.shape[1]
    tm = min(tm, rows)
    return pl.pallas_call(
        _ffn_kernel,
        grid=(rows // tm, ff // tf),
        in_specs=[
            pl.BlockSpec((tm, d), lambda m, f: (m, 0)),
            pl.BlockSpec((1, d), lambda m, f: (0, 0)),
            pl.BlockSpec((1, 6, d), lambda m, f: (_group_of(m, tm, seq, batch), 0, 0)),
            pl.BlockSpec((d, tf), lambda m, f: (0, f)),
            pl.BlockSpec((d, tf), lambda m, f: (0, f)),
            pl.BlockSpec((tf, d), lambda m, f: (f, 0)),
        ],
        out_specs=pl.BlockSpec((tm, d), lambda m, f: (m, 0)),
        out_shape=jax.ShapeDtypeStruct((rows, d), F32),
        scratch_shapes=[pltpu.VMEM((tm, d), BF16), pltpu.VMEM((tm, d), F32)],
        compiler_params=_cparams(2),
        name="ffn_dense",
    )(x, nw.reshape(1, d), mods, w1, w3, w2)


def _top2(logits, n_experts):
    lane = lax.broadcasted_iota(jnp.int32, logits.shape, 1).astype(F32)
    neg = jnp.float32(-jnp.inf)
    lg = jnp.where(lane < n_experts, logits, neg)
    m1 = jnp.max(lg, axis=-1, keepdims=True)
    i1 = jnp.min(jnp.where(lg == m1, lane, float(LANE)), axis=-1, keepdims=True)
    lg2 = jnp.where(lane == i1, neg, lg)
    m2 = jnp.max(lg2, axis=-1, keepdims=True)
    i2 = jnp.min(jnp.where(lg2 == m2, lane, float(LANE)), axis=-1, keepdims=True)
    e2 = jnp.exp(m2 - m1)
    return i1, i2, 1.0 / (1.0 + e2), e2 / (1.0 + e2)


def _route_kernel(x_ref, nw_ref, mod_ref, r_ref, h_ref, route_ref):
    h = _norm_mod(x_ref[...], nw_ref[...], mod_ref[0, 3:4, :], mod_ref[0, 4:5, :])
    logits = jnp.dot(h, r_ref[...], preferred_element_type=F32, precision=lax.Precision.HIGHEST)
    i1, i2, p1, p2 = _top2(logits, N_EXPERTS)
    lane = lax.broadcasted_iota(jnp.int32, logits.shape, 1)
    route = jnp.where(lane == 0, i1, jnp.where(lane == 1, i2,
                      jnp.where(lane == 2, p1, jnp.where(lane == 3, p2, 0.0))))
    h_ref[...] = h
    route_ref[...] = route


def _moe_route(x, nw, mods, router, *, rows, seq, batch, tm=512):
    d = x.shape[1]
    rpad = jnp.zeros((d, LANE), F32).at[:, :router.shape[1]].set(router)
    return pl.pallas_call(
        _route_kernel,
        grid=(rows // tm,),
        in_specs=[
            pl.BlockSpec((tm, d), lambda m: (m, 0)),
            pl.BlockSpec((1, d), lambda m: (0, 0)),
            pl.BlockSpec((1, 6, d), lambda m: (_group_of(m, tm, seq, batch), 0, 0)),
            pl.BlockSpec((d, LANE), lambda m: (0, 0)),
        ],
        out_specs=[pl.BlockSpec((tm, d), lambda m: (m, 0)),
                   pl.BlockSpec((tm, LANE), lambda m: (m, 0))],
        out_shape=[jax.ShapeDtypeStruct((rows, d), F32), jax.ShapeDtypeStruct((rows, LANE), F32)],
        compiler_params=_cparams(1),
        name="moe_route",
    )(x, nw.reshape(1, d), mods, rpad)


def _moe_plan(route, n_experts, tile):
    rows = route.shape[0]
    e = route[:, :2].astype(jnp.int32).reshape(-1)
    onehot = (e[:, None] == jnp.arange(n_experts, dtype=jnp.int32)[None, :]).astype(jnp.int32)
    before = jnp.cumsum(onehot, axis=0) - onehot
    rank = jnp.sum(before * onehot, axis=1)
    counts = jnp.sum(onehot, axis=0)
    padded = ((counts + tile - 1) // tile) * tile
    ends = jnp.cumsum(padded)
    starts = ends - padded
    pos = jnp.sum(starts[None, :] * onehot, axis=1) + rank
    n_tiles = (2 * rows) // tile + n_experts
    src = jnp.zeros((n_tiles * tile,), jnp.int32).at[pos].set(
        jnp.arange(2 * rows, dtype=jnp.int32) // 2)
    tile_start = jnp.arange(n_tiles, dtype=jnp.int32) * tile
    tile_expert = jnp.sum((tile_start[:, None] >= ends[None, :]).astype(jnp.int32), axis=1)
    tile_expert = jnp.minimum(tile_expert, n_experts - 1)
    n_used = (ends[-1] // tile).reshape(1)
    return pos, src, tile_expert, n_used


def _row_copy(src_hbm, src_row, dst, dst_row, sem):
    return pltpu.make_async_copy(src_hbm.at[pl.ds(src_row, 1)], dst.at[pl.ds(dst_row, 1)], sem)


def _gather_rows(idx_ref, base, stride, src_hbm, dst, sem, *, wait):
    if wait:
        pltpu.make_async_copy(src_hbm.at[pl.ds(0, dst.shape[0])], dst, sem).wait()
        return

    def body(r, carry):
        _row_copy(src_hbm, idx_ref[base + stride * r], dst, r, sem).start()
        return carry

    lax.fori_loop(0, dst.shape[0], body, 0, unroll=8)


def _moe_ffn_kernel(te_ref, nu_ref, src_ref, h_hbm, w1_ref, w3_ref, w2_ref, o_ref,
                    rows_scr, h_scr, acc_scr, sems):
    t = pl.program_id(0)
    f = pl.program_id(1)
    last = pl.num_programs(1) - 1
    tile = o_ref.shape[0]
    n_used = nu_ref[0]
    used = t < n_used
    slot = t % 2

    def gather(tile_idx, buf, wait):
        _gather_rows(src_ref, tile_idx * tile, 1, h_hbm, rows_scr.at[buf], sems.at[buf], wait=wait)

    @pl.when(used & (f == 0) & (t == 0))
    def _():
        gather(t, slot, False)

    @pl.when(used & (f == 0))
    def _():
        gather(t, slot, True)
        h_scr[...] = rows_scr[slot].astype(BF16)
        acc_scr[...] = jnp.zeros_like(acc_scr)

    @pl.when((f == 0) & (t + 1 < n_used))
    def _():
        gather(t + 1, 1 - slot, False)

    @pl.when(used)
    def _():
        h = h_scr[...]
        act = _silu(_bdot(h, w1_ref[0])) * _bdot(h, w3_ref[0])
        acc_scr[...] += _bdot(act.astype(BF16), w2_ref[0])

    @pl.when(used & (f == last))
    def _():
        o_ref[...] = acc_scr[...]

    @pl.when(jnp.logical_not(used) & (f == last))
    def _():
        o_ref[...] = jnp.zeros_like(o_ref)


def _moe_ffn(h, src, tile_expert, n_used, w1, w3, w2, *, tile, tf=512):
    d = h.shape[1]
    n = src.shape[0]
    ff = w1.shape[2]
    nf = ff // tf
    fsel = lambda t, f, nu: jnp.where(t < nu[0], f, nf - 1)
    return pl.pallas_call(
        _moe_ffn_kernel,
        grid_spec=pltpu.PrefetchScalarGridSpec(
            num_scalar_prefetch=3,
            grid=(n // tile, nf),
            in_specs=[
                pl.BlockSpec(memory_space=pl.ANY),
                pl.BlockSpec((1, d, tf), lambda t, f, te, nu, sr: (te[t], 0, fsel(t, f, nu))),
                pl.BlockSpec((1, d, tf), lambda t, f, te, nu, sr: (te[t], 0, fsel(t, f, nu))),
                pl.BlockSpec((1, tf, d), lambda t, f, te, nu, sr: (te[t], fsel(t, f, nu), 0)),
            ],
            out_specs=pl.BlockSpec((tile, d), lambda t, f, te, nu, sr: (t, 0)),
            scratch_shapes=[pltpu.VMEM((2, tile, d), F32), pltpu.VMEM((tile, d), BF16),
                            pltpu.VMEM((tile, d), F32), pltpu.SemaphoreType.DMA((2,))],
        ),
        out_shape=jax.ShapeDtypeStruct((n, d), F32),
        compiler_params=_cparams(2),
        name="moe_ffn",
    )(tile_expert, n_used, src, h, w1, w3, w2)


def _combine_kernel(pos_ref, ys_hbm, x_ref, route_ref, mod_ref, o_ref, y1_scr, y2_scr, sem):
    base = 2 * pl.program_id(0) * x_ref.shape[0]
    _gather_rows(pos_ref, base, 2, ys_hbm, y1_scr, sem, wait=False)
    _gather_rows(pos_ref, base + 1, 2, ys_hbm, y2_scr, sem, wait=False)
    _gather_rows(pos_ref, base, 2, ys_hbm, y1_scr, sem, wait=True)
    _gather_rows(pos_ref, base + 1, 2, ys_hbm, y2_scr, sem, wait=True)
    mix = route_ref[:, 2:3] * y1_scr[...] + route_ref[:, 3:4] * y2_scr[...]
    o_ref[...] = x_ref[...] + mod_ref[0, 5:6, :] * mix


def _moe_combine(x, mods, route, ys, pos, *, rows, seq, batch, tm=256):
    d = x.shape[1]
    return pl.pallas_call(
        _combine_kernel,
        grid_spec=pltpu.PrefetchScalarGridSpec(
            num_scalar_prefetch=1,
            grid=(rows // tm,),
            in_specs=[
                pl.BlockSpec(memory_space=pl.ANY),
                pl.BlockSpec((tm, d), lambda m, pos: (m, 0)),
                pl.BlockSpec((tm, LANE), lambda m, pos: (m, 0)),
                pl.BlockSpec((1, 6, d), lambda m, pos: (_group_of(m, tm, seq, batch), 0, 0)),
            ],
            out_specs=pl.BlockSpec((tm, d), lambda m, pos: (m, 0)),
            scratch_shapes=[pltpu.VMEM((tm, d), F32), pltpu.VMEM((tm, d), F32),
                            pltpu.SemaphoreType.DMA(())],
        ),
        out_shape=jax.ShapeDtypeStruct((rows, d), F32),
        compiler_params=_cparams(1),
        name="moe_combine",
    )(pos, ys, x, route, mods)


def _moe_top2(x, nw, mods, router, w1, w3, w2, *, rows, seq, batch, tm, tile=512):
    tok = dict(rows=rows, seq=seq, batch=batch)
    h, route = _moe_route(x, nw, mods, router, tm=tm, **tok)
    pos, src, tile_expert, n_used = _moe_plan(route, w1.shape[0], tile)
    ys = _moe_ffn(h, src, tile_expert, n_used, w1, w3, w2, tile=tile)
    return _moe_combine(x, mods, route, ys, pos, tm=min(tm, 256), **tok)


def _final_norm_kernel(x_ref, w_ref, o_ref):
    x = x_ref[...]
    ms = jnp.mean(x * x, axis=-1, keepdims=True)
    o_ref[...] = x * lax.rsqrt(ms + NORM_EPS) * w_ref[...]


def _final_norm(x, w, *, rows, tm=512):
    d = x.shape[1]
    return pl.pallas_call(
        _final_norm_kernel,
        grid=(rows // tm,),
        in_specs=[pl.BlockSpec((tm, d), lambda m: (m, 0)), pl.BlockSpec((1, d), lambda m: (0, 0))],
        out_specs=pl.BlockSpec((tm, d), lambda m: (m, 0)),
        out_shape=jax.ShapeDtypeStruct((rows, d), F32),
        compiler_params=_cparams(1),
        name="final_norm",
    )(x, w.reshape(1, d))


def _dft_angles(n, period):
    k = lax.broadcasted_iota(jnp.int32, (n, n), 0)
    s = lax.broadcasted_iota(jnp.int32, (n, n), 1)
    return ((k * s) % period).astype(F32) * (2.0 * math.pi / period)


def _hyena_dft(seq_len):
    n = 2 * seq_len
    ang = _dft_angles(seq_len, n)
    row = lax.broadcasted_iota(jnp.int32, ang.shape, 0)
    col = lax.broadcasted_iota(jnp.int32, ang.shape, 1)
    cos = jnp.cos(ang)
    sin = jnp.sin(ang)
    fc = cos
    fs = jnp.where(row == 0, jnp.where(col % 2 == 0, 1.0, -1.0), sin)
    ic = jnp.where(col == 0, 1.0 / n, 2.0 / n * cos)
    isn = jnp.where(col == 0, jnp.where(row % 2 == 0, 1.0, -1.0) / n, 2.0 / n * sin)
    return fc.astype(BF16), fs.astype(BF16), ic.astype(BF16), isn.astype(BF16)


def _hyena_feats(seq_len):
    pos = np.arange(seq_len, dtype=np.float64)
    t = pos / max(seq_len - 1, 1)
    bands = np.linspace(1e-4, HY_BANDS - 1, HY_BANDS).astype(np.float32).astype(np.float64)
    ang = 2 * math.pi * pos[:, None] * bands[None, :] / seq_len
    feats = np.concatenate([t[:, None], np.cos(ang), -np.sin(ang)], axis=-1)
    out = np.zeros((seq_len, LANE), np.float32)
    out[:, :HY_POS_DIM] = feats
    return jnp.asarray(out)


def _split_dot(w, a):
    hi = a.astype(BF16)
    lo = (a - hi.astype(F32)).astype(BF16)
    return _bdot(w, hi) + _bdot(w, lo)


def _split2(a):
    hi = a.astype(BF16)
    return hi, (a - hi.astype(F32)).astype(BF16)


def _dot3(a, b):
    a_hi, a_lo = _split2(a)
    b_hi, b_lo = _split2(b)
    return _bdot(a_hi, b_hi) + _bdot(a_hi, b_lo) + _bdot(a_lo, b_hi)


def _filter_kernel(feat_ref, w1_ref, b1_ref, w2_ref, b2_ref, fr_ref, w3f_ref, w3b_ref,
                   dcf_ref, dcb_ref, fc_ref, fs_ref, o_ref, h_scr):
    @pl.when((pl.program_id(1) == 0) & (pl.program_id(2) == 0))
    def _():
        hp = lax.Precision.HIGHEST
        h1 = jnp.sin(fr_ref[0, 0:1, :] * (jnp.dot(feat_ref[...], w1_ref[0], precision=hp,
                                                  preferred_element_type=F32) + b1_ref[0]))
        h_scr[...] = jnp.sin(fr_ref[0, 1:2, :] * (jnp.dot(h1, w2_ref[0], precision=hp,
                                                          preferred_element_type=F32) + b2_ref[0]))

    h = h_scr[...]
    t = feat_ref[:, 0:1]
    hf = _dot3(h, w3f_ref[0]) * jnp.exp(-t * dcf_ref[0])
    hb = _dot3(h, w3b_ref[0]) * jnp.exp(-t * dcb_ref[0])
    row = lax.broadcasted_iota(jnp.int32, hf.shape, 0)
    hb = jnp.where(row == 0, 0.0, hb)
    ss = jnp.sum(hf * hf + hb * hb, axis=0, keepdims=True)
    scale = lax.rsqrt(ss + NORM_EPS)
    even = hf * scale + hb * scale
    odd = hf * scale - hb * scale
    kc = _split_dot(fc_ref[...], even)
    ks = _split_dot(fs_ref[...], odd)
    sign = jnp.where(row % 2 == 0, 1.0, -1.0)
    nyq = jnp.sum(even * sign, axis=0, keepdims=True)
    o_ref[0, 0] = kc
    o_ref[0, 1] = jnp.where(row == 0, nyq, ks)


def _hyena_filters(seq_len, w1p, b1, w2, b2, w3, freq, decay, fc, fs, *, tn=256):
    depth = w1p.shape[0]
    hid = HY_FILTER_HIDDEN
    nb = HY_W // tn
    feats = _hyena_feats(seq_len)
    col = lambda d: (lambda l, o, c: (l, 0, (2 * o + d) * nb + c))
    lyr = lambda l, o, c: (l, 0, 0)
    return pl.pallas_call(
        _filter_kernel,
        grid=(depth, 2, nb),
        in_specs=[
            pl.BlockSpec((seq_len, LANE), lambda l, o, c: (0, 0)),
            pl.BlockSpec((1, LANE, hid), lyr),
            pl.BlockSpec((1, 1, hid), lyr),
            pl.BlockSpec((1, hid, hid), lyr),
            pl.BlockSpec((1, 1, hid), lyr),
            pl.BlockSpec((1, 2, hid), lyr),
            pl.BlockSpec((1, hid, tn), col(0)),
            pl.BlockSpec((1, hid, tn), col(1)),
            pl.BlockSpec((1, 1, tn), col(0)),
            pl.BlockSpec((1, 1, tn), col(1)),
            _resident((seq_len, seq_len), lambda l, o, c: (0, 0)),
            _resident((seq_len, seq_len), lambda l, o, c: (0, 0)),
        ],
        out_specs=pl.BlockSpec((1, 2, seq_len, tn), lambda l, o, c: (l, 0, 0, o * nb + c)),
        out_shape=jax.ShapeDtypeStruct((depth, 2, seq_len, 2 * HY_W), F32),
        scratch_shapes=[pltpu.VMEM((seq_len, hid), F32)],
        compiler_params=_cparams(3),
        name="hyena_filters",
    )(feats, w1p, b1.reshape(depth, 1, hid), w2, b2.reshape(depth, 1, hid), freq,
      w3, w3, decay.reshape(depth, 1, 4 * HY_W), decay.reshape(depth, 1, 4 * HY_W), fc, fs)


def _conv3(z, w_ref, b_ref):
    n = z.shape[0]
    row = lax.broadcasted_iota(jnp.int32, z.shape, 0)
    prev = jnp.where(row == 0, 0.0, pltpu.roll(z, 1, 0))
    nxt = jnp.where(row == n - 1, 0.0, pltpu.roll(z, n - 1, 0))
    return prev * w_ref[0:1, :] + z * w_ref[1:2, :] + nxt * w_ref[2:3, :] + b_ref[...]


def _longconv_kernel(a_ref, g_ref, aw_ref, ab_ref, gw_ref, gb_ref, kf_ref, skip_ref,
                     fc_ref, fs_ref, ic_ref, is_ref, o_ref, y_scr, *, conv_a, fblk):
    a = a_ref[...].astype(F32)
    if conv_a:
        a = _conv3(a, aw_ref, ab_ref)
    a16 = a.astype(BF16)
    n = a.shape[0]
    for j in range(n // fblk):
        rows = slice(j * fblk, (j + 1) * fblk)
        ac = _bdot(fc_ref[rows, :], a16)
        asn = _bdot(fs_ref[rows, :], a16)
        kc = kf_ref[0, 0, rows, :]
        ks = kf_ref[0, 1, rows, :]
        cross = asn * ks
        yc = ac * kc - cross
        ys = ac * ks + asn * kc
        if j == 0:
            row = lax.broadcasted_iota(jnp.int32, ac.shape, 0)
            yc = jnp.where(row == 0, ac * kc, yc)
            ys = jnp.where(row == 0, cross, ys)
        y_scr[rows, :] = yc.astype(BF16)
        y_scr[n + j * fblk:n + (j + 1) * fblk, :] = ys.astype(BF16)
    conv = _bdot(ic_ref[...], y_scr[0:n, :]) + _bdot(is_ref[...], y_scr[n:2 * n, :])
    gate = _conv3(g_ref[...].astype(F32), gw_ref, gb_ref)
    o_ref[...] = (gate * (conv + a * skip_ref[0])).astype(o_ref.dtype)


def _hyena_longconv(a_src, a_part, zm, gate_part, conv_w, conv_b, kf, layer, order, skip,
                    dft, *, seq_len, n_seq, row0, out_rows, prev_out=None, tn=256):
    fc, fs, ic, isn = dft
    nb = HY_W // tn
    rb0 = row0 // seq_len
    fblk = min(512, seq_len)
    const = lambda c, b: (0, 0)
    conv_a = a_part is not None
    aw_blk = a_part * nb if conv_a else 0
    a_blk = ZM_HY // tn + aw_blk if conv_a else 0
    gw_blk = gate_part * nb
    g_blk = ZM_HY // tn + gw_blk
    kernel = functools.partial(_longconv_kernel, conv_a=conv_a, fblk=fblk)
    in_specs = [
        pl.BlockSpec((seq_len, tn), lambda c, b: (rb0 + b, a_blk + c)),
        pl.BlockSpec((seq_len, tn), lambda c, b: (rb0 + b, g_blk + c)),
        pl.BlockSpec((3, tn), lambda c, b: (0, aw_blk + c)),
        pl.BlockSpec((1, tn), lambda c, b: (0, aw_blk + c)),
        pl.BlockSpec((3, tn), lambda c, b: (0, gw_blk + c)),
        pl.BlockSpec((1, tn), lambda c, b: (0, gw_blk + c)),
        _resident((1, 2, seq_len, tn), lambda c, b: (layer, 0, 0, order * nb + c)),
        pl.BlockSpec((1, 1, tn), lambda c, b: (order, 0, c)),
        _resident((seq_len, seq_len), const),
        _resident((seq_len, seq_len), const),
        _resident((seq_len, seq_len), const),
        _resident((seq_len, seq_len), const),
    ]
    args = [a_src, zm, conv_w, conv_b.reshape(1, -1), conv_w, conv_b.reshape(1, -1), kf,
            skip.reshape(2, 1, HY_W), fc, fs, ic, isn]
    aliases = {}
    if prev_out is not None:
        in_specs.append(pl.BlockSpec(memory_space=pl.ANY))
        args.append(prev_out)
        aliases = {len(args) - 1: 0}
        kernel = functools.partial(_drop_last_input, kernel, 12)
    return pl.pallas_call(
        kernel,
        grid=(nb, n_seq),
        in_specs=in_specs,
        out_specs=pl.BlockSpec((seq_len, tn), lambda c, b: (rb0 + b, c)),
        out_shape=jax.ShapeDtypeStruct((out_rows, HY_W), BF16),
        scratch_shapes=[pltpu.VMEM((2 * seq_len, tn), BF16)],
        input_output_aliases=aliases,
        compiler_params=_cparams(2),
        name="hyena_longconv",
    )(*args)


def _drop_last_input(kernel, n_in, *refs):
    return kernel(*refs[:n_in], *refs[n_in + 1:])


def _fnet_dft(seq_len):
    ang_l = _dft_angles(seq_len, seq_len)
    ang_w = _dft_angles(FN_GROUP_W, FN_GROUP_W)
    scale = 1.0 / math.sqrt(seq_len * FN_GROUP_W)
    return ((jnp.cos(ang_l) * scale).astype(BF16), (jnp.sin(ang_l) * scale).astype(BF16),
            jnp.cos(ang_w).astype(BF16), jnp.sin(ang_w).astype(BF16))


def _fnet_kernel(z_ref, cl_ref, sl_ref, cw_ref, sw_ref, o_ref):
    z = z_ref[...]
    gc, gs = [], []
    for j in range(z.shape[1] // FN_GROUP_W):
        zj = z[:, j * FN_GROUP_W:(j + 1) * FN_GROUP_W]
        gc.append(_bdot(zj, cw_ref[...]).astype(BF16))
        gs.append(_bdot(zj, sw_ref[...]).astype(BF16))
    gc = jnp.concatenate(gc, axis=-1)
    gs = jnp.concatenate(gs, axis=-1)
    o_ref[...] = (_bdot(cl_ref[...], gc) - _bdot(sl_ref[...], gs)).astype(o_ref.dtype)


def _fnet(zm, dft, *, seq_len, n_seq, row0, out_rows, prev_out=None, tn=256):
    cl, sl, cw, sw = dft
    nb = FN_W // tn
    rb0 = row0 // seq_len
    c0 = ZM_FN // tn
    const = lambda c, b: (0, 0)
    in_specs = [
        pl.BlockSpec((seq_len, tn), lambda c, b: (rb0 + b, c0 + c)),
        _resident((seq_len, seq_len), const),
        _resident((seq_len, seq_len), const),
        pl.BlockSpec((FN_GROUP_W, FN_GROUP_W), const),
        pl.BlockSpec((FN_GROUP_W, FN_GROUP_W), const),
    ]
    args = [zm, cl, sl, cw, sw]
    kernel = _fnet_kernel
    aliases = {}
    if prev_out is not None:
        in_specs.append(pl.BlockSpec(memory_space=pl.ANY))
        args.append(prev_out)
        aliases = {len(args) - 1: 0}
        kernel = functools.partial(_drop_last_input, _fnet_kernel, 5)
    return pl.pallas_call(
        kernel,
        grid=(nb, n_seq),
        in_specs=in_specs,
        out_specs=pl.BlockSpec((seq_len, tn), lambda c, b: (rb0 + b, c)),
        out_shape=jax.ShapeDtypeStruct((out_rows, FN_W), BF16),
        input_output_aliases=aliases,
        compiler_params=_cparams(2),
        name="fnet",
    )(*args)


def _hgrn_tables(reverse):
    c = HG_CHUNK
    tri = np.zeros((c, c), np.float32)
    for t in range(c):
        if reverse:
            tri[t, t:] = 1.0
        else:
            tri[t, :t + 1] = 1.0
    mats = [tri]
    masks = []
    for lvl in range(HG_LEVELS):
        half = (c // 2) >> lvl
        sel = np.zeros((c, c), np.float32)
        mask = np.zeros((c, c), np.float32)
        for t in range(c):
            start = (t // (2 * half)) * 2 * half
            mid = start + half
            sel[t, mid if reverse else mid - 1] = 1.0
            for s in range(start, start + 2 * half):
                if reverse and t < mid <= s:
                    mask[t, s] = 1.0
                if (not reverse) and s < mid <= t:
                    mask[t, s] = 1.0
        mats.append(tri - sel @ tri)
        masks.append(mask)
    half = c // 2
    sel = np.zeros((c, c), np.float32)
    mask = np.zeros((c, c), np.float32)
    for t in range(c):
        start = (t // half) * half
        sel[t, start + half // 2] = 1.0
        for s in range(start, start + half):
            if (s >= t) if reverse else (s <= t):
                mask[t, s] = 1.0
    mats.append(tri - sel @ tri)
    masks.append(mask)
    mats = [mats[0], mats[1], mats[-1]] + mats[2:-1]
    return np.concatenate(mats, axis=0), np.stack(masks, axis=0)


HG_BOUNDED_BLOCKS = 3
HG_MID_BLOCK = 2


def _hgrn_level_block(lvl):
    return 1 if lvl == 0 else HG_BOUNDED_BLOCKS + lvl - 1


def _hgrn_decays(zf_ref, lb_ref, tab_ref, k_scr, p_scr, dec_scr):
    lb = lb_ref[...]
    f = lb + (1.0 - lb) * jax.nn.sigmoid(zf_ref[...])
    lf = jnp.log2(f)
    k_scr[...] = (1.0 - f).astype(BF16)
    p1 = lf.astype(BF16)
    r1 = lf - p1.astype(F32)
    p2 = r1.astype(BF16)
    p3 = (r1 - p2.astype(F32)).astype(BF16)
    p_scr[...] = jnp.concatenate([p1, p2, p3], axis=0)
    n = HG_BOUNDED_BLOCKS * HG_CHUNK
    dec_scr[0:n, :] = _bdot(tab_ref[0:n, :], p_scr[...])


def _hgrn_more_decays(tab_ref, p_scr, dec_scr):
    n = HG_BOUNDED_BLOCKS * HG_CHUNK
    dec_scr[n:, :] = _bdot(tab_ref[n:, :], p_scr[...])


def _nt_dot(a, b):
    return lax.dot_general(a, b, (((1,), (1,)), ((), ())), preferred_element_type=F32)


def _hgrn_direction(q_ref, v_ref, k_scr, dec_scr, mask_ref, st_scr, o_ref, end_row, bounded):
    c = HG_CHUNK
    heads = lambda ref, r0: jnp.stack(
        [ref[r0:r0 + c, h * HG_DK:(h + 1) * HG_DK] for h in range(HG_HEADS)], axis=0)
    bmm = lambda eq, a, b: jnp.einsum(eq, a, b, preferred_element_type=F32)
    q = heads(q_ref, 0)
    k = heads(k_scr, 0)
    v = heads(v_ref, 0)
    b = heads(dec_scr, 0)
    st = st_scr[...]
    o = bmm('htk,hkv->htv', q * jnp.exp2(b).astype(BF16), st.astype(BF16))
    level = lambda l: jnp.exp2(-jnp.abs(heads(dec_scr, _hgrn_level_block(l) * c))).astype(BF16)
    if bounded:
        e = level(0)
        att = jnp.where(mask_ref[0] > 0.5, bmm('htk,hsk->hts', q * e, k * e), 0.0)
        d = heads(dec_scr, HG_MID_BLOCK * c)
        inner = bmm('htk,hsk->hts', q * jnp.exp2(d).astype(BF16), k * jnp.exp2(-d).astype(BF16))
        att += jnp.where(mask_ref[HG_LEVELS] > 0.5, inner, 0.0)
    else:
        diag = jnp.sum(q.astype(F32) * k.astype(F32), axis=-1, keepdims=True)
        o += diag * v.astype(F32)
        att = jnp.zeros((HG_HEADS, c, c), F32)
        for lvl in range(HG_LEVELS):
            e = level(lvl)
            att += bmm('htk,hsk->hts', q * e, k * e) * mask_ref[lvl]
    o += bmm('hts,hsv->htv', att.astype(BF16), v)
    for h in range(HG_HEADS):
        o_ref[:, h * HG_DK:(h + 1) * HG_DK] = o[h]
    b_end = b[:, end_row:end_row + 1, :]
    upd = bmm('htk,htv->hkv', k * jnp.exp2(b_end - b).astype(BF16), v)
    decay = jnp.swapaxes(jnp.broadcast_to(jnp.exp2(b_end), (HG_HEADS, HG_DK, HG_DK)), 1, 2)
    st_scr[...] = decay * st + upd


HG_MAX_LOG2_SPAN = 100.0
HG_MAX_ABS_Q = 1e6


def _hgrn_kernel(qf_ref, vf_ref, zff_ref, qb_ref, vb_ref, zfb_ref, lbf_ref, lbb_ref,
                 tabf_ref, maskf_ref, tabb_ref, maskb_ref, s0f_ref, s0b_ref,
                 of_ref, ob_ref, sf_ref, sb_ref, stf_scr, stb_scr, kf_scr, kb_scr,
                 pf_scr, pb_scr, decf_scr, decb_scr):
    ci = pl.program_id(1)
    c = HG_CHUNK

    @pl.when(ci == 0)
    def _():
        stf_scr[...] = s0f_ref[0]
        stb_scr[...] = s0b_ref[0]

    _hgrn_decays(zff_ref, lbf_ref, tabf_ref, kf_scr, pf_scr, decf_scr)
    _hgrn_decays(zfb_ref, lbb_ref, tabb_ref, kb_scr, pb_scr, decb_scr)
    mid = slice(HG_MID_BLOCK * c, (HG_MID_BLOCK + 1) * c)
    span = jnp.maximum(jnp.max(jnp.abs(decf_scr[mid, :])), jnp.max(jnp.abs(decb_scr[mid, :])))
    q_abs = jnp.maximum(jnp.max(jnp.abs(qf_ref[...].astype(F32))),
                        jnp.max(jnp.abs(qb_ref[...].astype(F32))))
    bounded = (span < HG_MAX_LOG2_SPAN) & (q_abs < HG_MAX_ABS_Q)

    def run(flag):
        _hgrn_direction(qf_ref, vf_ref, kf_scr, decf_scr, maskf_ref, stf_scr, of_ref, c - 1, flag)
        _hgrn_direction(qb_ref, vb_ref, kb_scr, decb_scr, maskb_ref, stb_scr, ob_ref, 0, flag)

    @pl.when(bounded)
    def _():
        run(True)

    @pl.when(jnp.logical_not(bounded))
    def _():
        _hgrn_more_decays(tabf_ref, pf_scr, decf_scr)
        _hgrn_more_decays(tabb_ref, pb_scr, decb_scr)
        run(False)

    @pl.when(ci == pl.num_programs(1) - 1)
    def _():
        sf_ref[0] = stf_scr[...]
        sb_ref[0] = stb_scr[...]


def _hgrn(zm, zf, lb_f, lb_b, s0_f, s0_b, *, seq_len, n_seq, row0, out_rows,
          prev_of=None, prev_ob=None):
    c = HG_CHUNK
    nc = seq_len // c
    rb0 = row0 // c
    w = HG_W
    tab_f, mask_f = _hgrn_tables(False)
    tab_b, mask_b = _hgrn_tables(True)
    fwd = lambda col: (lambda b, i: (rb0 + b * nc + i, col))
    bwd = lambda col: (lambda b, i: (rb0 + b * nc + nc - 1 - i, col))
    const2 = lambda b, i: (0, 0)
    const3 = lambda b, i: (0, 0, 0)
    state = lambda b, i: (b, 0, 0, 0)
    nt = (2 + HG_LEVELS) * c
    in_specs = [
        pl.BlockSpec((c, w), fwd(ZM_Q // w)),
        pl.BlockSpec((c, w), fwd(ZM_I // w)),
        pl.BlockSpec((c, w), fwd(0)),
        pl.BlockSpec((c, w), bwd(ZM_Q // w)),
        pl.BlockSpec((c, w), bwd(ZM_I // w)),
        pl.BlockSpec((c, w), bwd(1)),
        pl.BlockSpec((1, w), const2),
        pl.BlockSpec((1, w), const2),
        pl.BlockSpec((nt, 3 * c), const2),
        pl.BlockSpec((HG_LEVELS + 1, c, c), const3),
        pl.BlockSpec((nt, 3 * c), const2),
        pl.BlockSpec((HG_LEVELS + 1, c, c), const3),
        pl.BlockSpec((1, HG_HEADS, HG_DK, HG_DK), state),
        pl.BlockSpec((1, HG_HEADS, HG_DK, HG_DK), state),
    ]
    args = [zm, zm, zf, zm, zm, zf, lb_f.reshape(1, w), lb_b.reshape(1, w),
            jnp.asarray(np.tile(tab_f, (1, 3)), BF16), jnp.asarray(mask_f),
            jnp.asarray(np.tile(tab_b, (1, 3)), BF16), jnp.asarray(mask_b), s0_f, s0_b]
    kernel = _hgrn_kernel
    aliases = {}
    if prev_of is not None:
        in_specs += [pl.BlockSpec(memory_space=pl.ANY), pl.BlockSpec(memory_space=pl.ANY)]
        args += [prev_of, prev_ob]
        aliases = {14: 0, 15: 1}
        kernel = functools.partial(_hgrn_alias_kernel, 14)
    st_shape = jax.ShapeDtypeStruct((n_seq, HG_HEADS, HG_DK, HG_DK), F32)
    return pl.pallas_call(
        kernel,
        grid=(n_seq, nc),
        in_specs=in_specs,
        out_specs=[
            pl.BlockSpec((c, w), fwd(0)),
            pl.BlockSpec((c, w), bwd(0)),
            pl.BlockSpec((1, HG_HEADS, HG_DK, HG_DK), state),
            pl.BlockSpec((1, HG_HEADS, HG_DK, HG_DK), state),
        ],
        out_shape=[jax.ShapeDtypeStruct((out_rows, w), F32),
                   jax.ShapeDtypeStruct((out_rows, w), F32), st_shape, st_shape],
        scratch_shapes=[pltpu.VMEM((HG_HEADS, HG_DK, HG_DK), F32),
                        pltpu.VMEM((HG_HEADS, HG_DK, HG_DK), F32),
                        pltpu.VMEM((c, w), BF16), pltpu.VMEM((c, w), BF16),
                        pltpu.VMEM((3 * c, w), BF16), pltpu.VMEM((3 * c, w), BF16),
                        pltpu.VMEM((nt, w), F32), pltpu.VMEM((nt, w), F32)],
        input_output_aliases=aliases,
        compiler_params=_cparams(2),
        name="hgrn2",
    )(*args)


def _hgrn_alias_kernel(n_in, *refs):
    return _hgrn_kernel(*refs[:n_in], *refs[n_in + 2:])


def kernel(x, c, ctx, c_ctx, w_mod, b_mod, norm1_w, norm2_w, w_in, hy_conv_w, hy_conv_b,
           hy_f_w1, hy_f_b1, hy_f_w2, hy_f_b2, hy_f_w3, hy_f_freq, hy_decay, hy_skip,
           hg_lower, hg_norm_w, w_br_hy, w_br_fn, w_br_hg, w_out,
           ffn_w1, ffn_w3, ffn_w2, moe_router, moe_w1, moe_w3, moe_w2, final_norm_w):
    batch, seq, d = x.shape
    l_ctx = ctx.shape[1]
    depth = w_in.shape[0]
    n_lat = batch * seq
    n_all = n_lat + batch * l_ctx

    lb_all = jnp.cumsum(jax.nn.softmax(hg_lower.astype(F32), axis=1), axis=1)
    lb_all = lb_all - lb_all[:, :1]

    cc = jnp.zeros((8, d), F32).at[:batch].set(c).at[batch].set(c_ctx)
    mods = _modulation(cc, w_mod, b_mod).reshape(depth, 8, 6, d)

    w1p = jnp.zeros((depth, LANE, HY_FILTER_HIDDEN), F32).at[:, :HY_POS_DIM].set(hy_f_w1)
    dft_lat = _hyena_dft(seq)
    dft_ctx = _hyena_dft(l_ctx)
    kf_lat = _hyena_filters(seq, w1p, hy_f_b1, hy_f_w2, hy_f_b2, hy_f_w3, hy_f_freq, hy_decay,
                            dft_lat[0], dft_lat[1])
    kf_ctx = _hyena_filters(l_ctx, w1p, hy_f_b1, hy_f_w2, hy_f_b2, hy_f_w3, hy_f_freq, hy_decay,
                            dft_ctx[0], dft_ctx[1])
    fn_lat = _fnet_dft(seq)
    fn_ctx = _fnet_dft(l_ctx)

    rows_x = jnp.concatenate([x.reshape(n_lat, d), ctx.reshape(batch * l_ctx, d)], axis=0)
    zero_state = jnp.zeros((batch, HG_HEADS, HG_DK, HG_DK), F32)
    tile_cap = math.gcd(seq, batch * l_ctx)
    tok = dict(seq=seq, batch=batch)
    tile = lambda t: dict(tm=min(t, tile_cap))
    lat = dict(seq_len=seq, n_seq=batch, row0=0, out_rows=n_all)
    cx = dict(seq_len=l_ctx, n_seq=batch, row0=n_lat, out_rows=n_all)

    for l in range(depth):
        last = l == depth - 1
        rows = n_lat if last else n_all
        m = mods[l]

        h = _norm_rows(rows_x, norm1_w[l], m, **tok, **tile(512))
        zm, zf = _in_proj(h, w_in, l, **tile(1024))

        of, ob, s_f, s_b = _hgrn(zm, zf, lb_all[0, l], lb_all[1, l], zero_state, zero_state, **cx)
        of, ob, _, _ = _hgrn(zm, zf, lb_all[0, l], lb_all[1, l], s_f, s_b, prev_of=of,
                             prev_ob=ob, **lat)

        hy_args = (hy_conv_w[l], hy_conv_b[l])
        mix = dict(lat, out_rows=rows)
        u = _hyena_longconv(zm, 0, zm, 1, *hy_args, kf_lat, l, 0, hy_skip[l], dft_lat, **mix)
        y_hy = _hyena_longconv(u, None, zm, 2, *hy_args, kf_lat, l, 1, hy_skip[l], dft_lat, **mix)
        y_fn = _fnet(zm, fn_lat, **mix)
        if not last:
            u = _hyena_longconv(zm, 0, zm, 1, *hy_args, kf_ctx, l, 0, hy_skip[l], dft_ctx,
                                prev_out=u, **cx)
            y_hy = _hyena_longconv(u, None, zm, 2, *hy_args, kf_ctx, l, 1, hy_skip[l], dft_ctx,
                                   prev_out=y_hy, **cx)
            y_fn = _fnet(zm, fn_ctx, prev_out=y_fn, **cx)

        rows_x = _merge_out(y_hy, y_fn, of, ob, zm, rows_x, hg_norm_w[l], m,
                            w_br_hy[l].astype(BF16), w_br_fn[l].astype(BF16),
                            w_br_hg[l].astype(BF16), w_out[l].astype(BF16), rows=rows,
                            **tok, **tile(256))

        if l % 2 == 0:
            i = l // 2
            rows_x = _ffn_dense(rows_x, norm2_w[l], m, ffn_w1[i].astype(BF16),
                                ffn_w3[i].astype(BF16), ffn_w2[i].astype(BF16), rows=rows,
                                **tok, **tile(512))
        else:
            i = l // 2
            rows_x = _moe_top2(rows_x, norm2_w[l], m, moe_router[i], moe_w1[i].astype(BF16),
                                moe_w3[i].astype(BF16), moe_w2[i].astype(BF16), rows=rows,
                                **tok, **tile(512))

    out = _final_norm(rows_x, final_norm_w, rows=n_lat, **tile(512))
    return out.reshape(batch, seq, d)
```

```python
import functools
import math

import numpy as np
import jax
import jax.numpy as jnp
from jax import lax
from jax.experimental import pallas as pl
from jax.experimental.pallas import tpu as pltpu

F32 = jnp.float32
BF16 = jnp.bfloat16
NORM_EPS = 1e-6

D_MODEL = 2048
DEPTH = 4
HY_W = 1024
HY_BANDS = 16
HY_POS_DIM = 2 * HY_BANDS + 1
HY_FILTER_HIDDEN = 64
FN_W = 1024
FN_GROUP_W = 128
HG_W = 1024
HG_HEADS = 8
HG_DK = 128
HG_CHUNK = 64
HG_LEVELS = 6
N_EXPERTS = 8
LANE = 128

HG_FF = 3 * HY_W + FN_W + 2 * HG_W
HG_I = HG_FF + 2 * HG_W
ZM_HY = 0
ZM_FN = ZM_HY + 3 * HY_W
ZM_Q = ZM_FN + FN_W
ZM_G = ZM_Q + HG_W
ZM_I = HG_I
ZM_GATE = HG_I + HG_W
ZM_COLS = ZM_GATE + 3 * D_MODEL

VMEM_LIMIT = 56 * 1024 * 1024


def _cparams(n_axes, vmem=VMEM_LIMIT):
    return pltpu.CompilerParams(dimension_semantics=("arbitrary",) * n_axes, vmem_limit_bytes=vmem)


def _resident(shape, index_map):
    return pl.BlockSpec(shape, index_map, pipeline_mode=pl.Buffered(1))


def _silu(v):
    return v * jax.nn.sigmoid(v)


def _bdot(a, b):
    return jnp.dot(a, b, preferred_element_type=F32)


def _mod_kernel(c_ref, w_ref, b_ref, o_ref):
    c = _silu(c_ref[...])
    o_ref[0] = jnp.dot(c, w_ref[0], preferred_element_type=F32,
                       precision=lax.Precision.HIGHEST) + b_ref[0]


def _modulation(cc, w_mod, b_mod):
    depth, d, n6 = w_mod.shape
    tn = 2048
    return pl.pallas_call(
        _mod_kernel,
        grid=(depth, n6 // tn),
        in_specs=[
            pl.BlockSpec((8, d), lambda l, n: (0, 0)),
            pl.BlockSpec((1, d, tn), lambda l, n: (l, 0, n)),
            pl.BlockSpec((1, 1, tn), lambda l, n: (l, 0, n)),
        ],
        out_specs=pl.BlockSpec((1, 8, tn), lambda l, n: (l, 0, n)),
        out_shape=jax.ShapeDtypeStruct((depth, 8, n6), F32),
        compiler_params=_cparams(2),
        name="modulation",
    )(cc, w_mod, b_mod.reshape(depth, 1, n6))


def _norm_mod(x, nw, shift, scale):
    ms = jnp.mean(x * x, axis=-1, keepdims=True)
    return (x * lax.rsqrt(ms + NORM_EPS) * nw) * (1 + scale) + shift


def _group_of(row_block, tm, seq, batch):
    return jnp.minimum((row_block * tm) // seq, batch)


def _norm_rows_kernel(x_ref, nw_ref, mod_ref, o_ref):
    h = _norm_mod(x_ref[...], nw_ref[...], mod_ref[0, 0:1, :], mod_ref[0, 1:2, :])
    o_ref[...] = h.astype(o_ref.dtype)


def _norm_rows(x, nw, mods, *, seq, batch, tm=512):
    rows, d = x.shape
    return pl.pallas_call(
        _norm_rows_kernel,
        grid=(rows // tm,),
        in_specs=[
            pl.BlockSpec((tm, d), lambda m: (m, 0)),
            pl.BlockSpec((1, d), lambda m: (0, 0)),
            pl.BlockSpec((1, 6, d), lambda m: (_group_of(m, tm, seq, batch), 0, 0)),
        ],
        out_specs=pl.BlockSpec((tm, d), lambda m: (m, 0)),
        out_shape=jax.ShapeDtypeStruct((rows, d), BF16),
        compiler_params=_cparams(1),
        name="norm_rows",
    )(x, nw.reshape(1, d), mods)


def _inproj_kernel(h_ref, w_ref, z_ref, zf_ref, w_scr, *, n_main):
    @pl.when(pl.program_id(1) == 0)
    def _():
        w_scr[...] = w_ref[0].astype(BF16)

    r = _bdot(h_ref[...], w_scr[...])

    @pl.when(pl.program_id(0) < n_main)
    def _():
        z_ref[...] = r.astype(z_ref.dtype)

    @pl.when(pl.program_id(0) >= n_main)
    def _():
        zf_ref[...] = r


def _in_proj(h, w_in, layer, *, tm=1024, tn=1024):
    rows, d = h.shape
    n_cols = w_in.shape[2]
    nb = n_cols // tn
    f0 = HG_FF // tn
    nf = 2 * HG_W // tn
    n_main = nb - nf
    nm = rows // tm
    col = lambda j: jnp.where(j < f0, j, jnp.where(j < n_main, j + nf, j - n_main + f0))
    z_idx = lambda j, m: (jnp.where(j < n_main, m, nm - 1), col(jnp.minimum(j, n_main - 1)))
    zf_idx = lambda j, m: (jnp.where(j < n_main, 0, m), jnp.maximum(j - n_main, 0))
    return pl.pallas_call(
        functools.partial(_inproj_kernel, n_main=n_main),
        grid=(nb, nm),
        in_specs=[
            pl.BlockSpec((tm, d), lambda j, m: (m, 0)),
            pl.BlockSpec((1, d, tn), lambda j, m: (layer, 0, col(j))),
        ],
        out_specs=[pl.BlockSpec((tm, tn), z_idx), pl.BlockSpec((tm, tn), zf_idx)],
        out_shape=[jax.ShapeDtypeStruct((rows, n_cols), BF16),
                   jax.ShapeDtypeStruct((rows, nf * tn), F32)],
        scratch_shapes=[pltpu.VMEM((d, tn), BF16)],
        compiler_params=_cparams(2),
        name="in_proj",
    )(h, w_in)


def _merge_kernel(yhy_ref, yfn_ref, of_ref, ob_ref, g_ref, ga0_ref, ga1_ref, gb0_ref, gb1_ref,
                  gc0_ref, gc1_ref, x_ref, hnw_ref, mod_ref,
                  wbhy_ref, wbfn_ref, wbhg_ref, wo_ref, o_ref):
    d = x_ref.shape[1]
    ohg = of_ref[...] + ob_ref[...]
    nw = hnw_ref[...]
    heads = []
    for h in range(ohg.shape[1] // HG_DK):
        oh = ohg[:, h * HG_DK:(h + 1) * HG_DK]
        ms = jnp.mean(oh * oh, axis=-1, keepdims=True)
        heads.append(oh * lax.rsqrt(ms + NORM_EPS) * nw)
    y_hg = jnp.concatenate(heads, axis=-1) * _silu(g_ref[...].astype(F32))
    gate = lambda lo, hi: jax.nn.sigmoid(
        jnp.concatenate([lo[...], hi[...]], axis=-1).astype(F32))
    merged = gate(ga0_ref, ga1_ref) * _bdot(yhy_ref[...], wbhy_ref[...])
    merged += gate(gb0_ref, gb1_ref) * _bdot(yfn_ref[...], wbfn_ref[...])
    merged += gate(gc0_ref, gc1_ref) * _bdot(y_hg.astype(BF16), wbhg_ref[...])
    out = _bdot(merged.astype(BF16), wo_ref[...])
    o_ref[...] = x_ref[...] + mod_ref[0, 2:3, :] * out


def _merge_out(y_hy, y_fn, o_f, o_b, zm, x, hg_nw, mods, wb_hy, wb_fn, wb_hg, w_o,
               *, rows, seq, batch, tm=256):
    d = x.shape[1]
    w = y_hy.shape[1]
    tm = min(tm, rows)
    row = lambda m: (m, 0)
    const = lambda m: (0, 0)
    gate0 = ZM_GATE // (d // 2)
    return pl.pallas_call(
        _merge_kernel,
        grid=(rows // tm,),
        in_specs=[
            pl.BlockSpec((tm, w), row),
            pl.BlockSpec((tm, w), row),
            pl.BlockSpec((tm, w), row),
            pl.BlockSpec((tm, w), row),
            pl.BlockSpec((tm, w), lambda m: (m, ZM_G // w)),
            *[pl.BlockSpec((tm, d // 2), functools.partial(lambda i, m: (m, i), gate0 + i))
              for i in range(6)],
            pl.BlockSpec((tm, d), row),
            pl.BlockSpec((1, HG_DK), const),
            pl.BlockSpec((1, 6, d), lambda m: (_group_of(m, tm, seq, batch), 0, 0)),
            _resident((w, d), const),
            _resident((w, d), const),
            _resident((w, d), const),
            _resident((d, d), const),
        ],
        out_specs=pl.BlockSpec((tm, d), row),
        out_shape=jax.ShapeDtypeStruct((rows, d), F32),
        compiler_params=_cparams(1),
        name="merge_out",
    )(y_hy, y_fn, o_f, o_b, zm, *([zm] * 6), x, hg_nw.reshape(1, HG_DK), mods,
      wb_hy, wb_fn, wb_hg, w_o)


def _ffn_kernel(x_ref, nw_ref, mod_ref, w1_ref, w3_ref, w2_ref, o_ref, h_scr, acc_scr):
    f = pl.program_id(1)

    @pl.when(f == 0)
    def _():
        h = _norm_mod(x_ref[...], nw_ref[...], mod_ref[0, 3:4, :], mod_ref[0, 4:5, :])
        h_scr[...] = h.astype(BF16)
        acc_scr[...] = jnp.zeros_like(acc_scr)

    h = h_scr[...]
    act = _silu(_bdot(h, w1_ref[...])) * _bdot(h, w3_ref[...])
    acc_scr[...] += _bdot(act.astype(BF16), w2_ref[...])

    @pl.when(f == pl.num_programs(1) - 1)
    def _():
        o_ref[...] = x_ref[...] + mod_ref[0, 5:6, :] * acc_scr[...]


def _ffn_dense(x, nw, mods, w1, w3, w2, *, rows, seq, batch, tm=512, tf=512):
    d = x.shape[1]
    ff = w1.shape[1]
    tm = min(tm, rows)
    return pl.pallas_call(
        _ffn_kernel,
        grid=(rows // tm, ff // tf),
        in_specs=[
            pl.BlockSpec((tm, d), lambda m, f: (m, 0)),
            pl.BlockSpec((1, d), lambda m, f: (0, 0)),
            pl.BlockSpec((1, 6, d), lambda m, f: (_group_of(m, tm, seq, batch), 0, 0)),
            pl.BlockSpec((d, tf), lambda m, f: (0, f)),
            pl.BlockSpec((d, tf), lambda m, f: (0, f)),
            pl.BlockSpec((tf, d), lambda m, f: (f, 0)),
        ],
        out_specs=pl.BlockSpec((tm, d), lambda m, f: (m, 0)),
        out_shape=jax.ShapeDtypeStruct((rows, d), F32),
        scratch_shapes=[pltpu.VMEM((tm, d), BF16), pltpu.VMEM((tm, d), F32)],
        compiler_params=_cparams(2),
        name="ffn_dense",
    )(x, nw.reshape(1, d), mods, w1, w3, w2)


def _top2(logits, n_experts):
    lane = lax.broadcasted_iota(jnp.int32, logits.shape, 1).astype(F32)
    neg = jnp.float32(-jnp.inf)
    lg = jnp.where(lane < n_experts, logits, neg)
    m1 = jnp.max(lg, axis=-1, keepdims=True)
    i1 = jnp.min(jnp.where(lg == m1, lane, float(LANE)), axis=-1, keepdims=True)
    lg2 = jnp.where(lane == i1, neg, lg)
    m2 = jnp.max(lg2, axis=-1, keepdims=True)
    i2 = jnp.min(jnp.where(lg2 == m2, lane, float(LANE)), axis=-1, keepdims=True)
    e2 = jnp.exp(m2 - m1)
    return i1, i2, 1.0 / (1.0 + e2), e2 / (1.0 + e2)


def _route_kernel(x_ref, nw_ref, mod_ref, r_ref, h_ref, route_ref):
    h = _norm_mod(x_ref[...], nw_ref[...], mod_ref[0, 3:4, :], mod_ref[0, 4:5, :])
    logits = jnp.dot(h, r_ref[...], preferred_element_type=F32, precision=lax.Precision.HIGHEST)
    i1, i2, p1, p2 = _top2(logits, N_EXPERTS)
    lane = lax.broadcasted_iota(jnp.int32, logits.shape, 1)
    route = jnp.where(lane == 0, i1, jnp.where(lane == 1, i2,
                      jnp.where(lane == 2, p1, jnp.where(lane == 3, p2, 0.0))))
    h_ref[...] = h
    route_ref[...] = route


def _moe_route(x, nw, mods, router, *, rows, seq, batch, tm=512):
    d = x.shape[1]
    rpad = jnp.zeros((d, LANE), F32).at[:, :router.shape[1]].set(router)
    return pl.pallas_call(
        _route_kernel,
        grid=(rows // tm,),
        in_specs=[
            pl.BlockSpec((tm, d), lambda m: (m, 0)),
            pl.BlockSpec((1, d), lambda m: (0, 0)),
            pl.BlockSpec((1, 6, d), lambda m: (_group_of(m, tm, seq, batch), 0, 0)),
            pl.BlockSpec((d, LANE), lambda m: (0, 0)),
        ],
        out_specs=[pl.BlockSpec((tm, d), lambda m: (m, 0)),
                   pl.BlockSpec((tm, LANE), lambda m: (m, 0))],
        out_shape=[jax.ShapeDtypeStruct((rows, d), F32), jax.ShapeDtypeStruct((rows, LANE), F32)],
        compiler_params=_cparams(1),
        name="moe_route",
    )(x, nw.reshape(1, d), mods, rpad)


def _moe_plan(route, n_experts, tile):
    rows = route.shape[0]
    e = route[:, :2].astype(jnp.int32).reshape(-1)
    onehot = (e[:, None] == jnp.arange(n_experts, dtype=jnp.int32)[None, :]).astype(jnp.int32)
    before = jnp.cumsum(onehot, axis=0) - onehot
    rank = jnp.sum(before * onehot, axis=1)
    counts = jnp.sum(onehot, axis=0)
    padded = ((counts + tile - 1) // tile) * tile
    ends = jnp.cumsum(padded)
    starts = ends - padded
    pos = jnp.sum(starts[None, :] * onehot, axis=1) + rank
    n_tiles = (2 * rows) // tile + n_experts
    src = jnp.zeros((n_tiles * tile,), jnp.int32).at[pos].set(
        jnp.arange(2 * rows, dtype=jnp.int32) // 2)
    tile_start = jnp.arange(n_tiles, dtype=jnp.int32) * tile
    tile_expert = jnp.sum((tile_start[:, None] >= ends[None, :]).astype(jnp.int32), axis=1)
    tile_expert = jnp.minimum(tile_expert, n_experts - 1)
    n_used = (ends[-1] // tile).reshape(1)
    return pos, src, tile_expert, n_used


def _row_copy(src_hbm, src_row, dst, dst_row, sem):
    return pltpu.make_async_copy(src_hbm.at[pl.ds(src_row, 1)], dst.at[pl.ds(dst_row, 1)], sem)


def _gather_rows(idx_ref, base, stride, src_hbm, dst, sem, *, wait):
    if wait:
        pltpu.make_async_copy(src_hbm.at[pl.ds(0, dst.shape[0])], dst, sem).wait()
        return

    def body(r, carry):
        _row_copy(src_hbm, idx_ref[base + stride * r], dst, r, sem).start()
        return carry

    lax.fori_loop(0, dst.shape[0], body, 0, unroll=8)


def _moe_ffn_kernel(te_ref, nu_ref, src_ref, h_hbm, w1_ref, w3_ref, w2_ref, o_ref,
                    rows_scr, h_scr, sems):
    t = pl.program_id(0)
    f = pl.program_id(1)
    last = pl.num_programs(1) - 1
    tile = o_ref.shape[0]
    n_used = nu_ref[0]
    used = t < n_used
    slot = t % 2

    def gather(tile_idx, buf, wait):
        _gather_rows(src_ref, tile_idx * tile, 1, h_hbm, rows_scr.at[buf], sems.at[buf], wait=wait)

    @pl.when(used & (f == 0) & (t == 0))
    def _():
        gather(t, slot, False)

    @pl.when(used & (f == 0))
    def _():
        gather(t, slot, True)
        h_scr[...] = rows_scr[slot].astype(BF16)
        o_ref[...] = jnp.zeros_like(o_ref)

    @pl.when((f == 0) & (t + 1 < n_used))
    def _():
        gather(t + 1, 1 - slot, False)

    @pl.when(used)
    def _():
        h = h_scr[...]
        act = _silu(_bdot(h, w1_ref[0, 0].astype(BF16))) * _bdot(h, w3_ref[0, 0].astype(BF16))
        o_ref[...] += _bdot(act.astype(BF16), w2_ref[0, 0].astype(BF16))

    @pl.when(jnp.logical_not(used) & (f == last))
    def _():
        o_ref[...] = jnp.zeros_like(o_ref)


def _moe_ffn(h, src, tile_expert, n_used, w1, w3, w2, layer, *, tile, tf=512):
    d = h.shape[1]
    n = src.shape[0]
    ff = w1.shape[3]
    nf = ff // tf
    fsel = lambda t, f, nu: jnp.where(t < nu[0], f, nf - 1)
    up = lambda t, f, te, nu, sr: (layer, te[t], 0, fsel(t, f, nu))
    down = lambda t, f, te, nu, sr: (layer, te[t], fsel(t, f, nu), 0)
    return pl.pallas_call(
        _moe_ffn_kernel,
        grid_spec=pltpu.PrefetchScalarGridSpec(
            num_scalar_prefetch=3,
            grid=(n // tile, nf),
            in_specs=[
                pl.BlockSpec(memory_space=pl.ANY),
                pl.BlockSpec((1, 1, d, tf), up),
                pl.BlockSpec((1, 1, d, tf), up),
                pl.BlockSpec((1, 1, tf, d), down),
            ],
            out_specs=pl.BlockSpec((tile, d), lambda t, f, te, nu, sr: (t, 0)),
            scratch_shapes=[pltpu.VMEM((2, tile, d), F32), pltpu.VMEM((tile, d), BF16),
                            pltpu.SemaphoreType.DMA((2,))],
        ),
        out_shape=jax.ShapeDtypeStruct((n, d), F32),
        compiler_params=_cparams(2),
        name="moe_ffn",
    )(tile_expert, n_used, src, h, w1, w3, w2)


def _combine_kernel(pos_ref, ys_hbm, x_ref, route_ref, mod_ref, o_ref, y1_scr, y2_scr, sem):
    base = 2 * pl.program_id(0) * x_ref.shape[0]
    _gather_rows(pos_ref, base, 2, ys_hbm, y1_scr, sem, wait=False)
    _gather_rows(pos_ref, base + 1, 2, ys_hbm, y2_scr, sem, wait=False)
    _gather_rows(pos_ref, base, 2, ys_hbm, y1_scr, sem, wait=True)
    _gather_rows(pos_ref, base + 1, 2, ys_hbm, y2_scr, sem, wait=True)
    mix = route_ref[:, 2:3] * y1_scr[...] + route_ref[:, 3:4] * y2_scr[...]
    o_ref[...] = x_ref[...] + mod_ref[0, 5:6, :] * mix


def _moe_combine(x, mods, route, ys, pos, *, rows, seq, batch, tm=256):
    d = x.shape[1]
    return pl.pallas_call(
        _combine_kernel,
        grid_spec=pltpu.PrefetchScalarGridSpec(
            num_scalar_prefetch=1,
            grid=(rows // tm,),
            in_specs=[
                pl.BlockSpec(memory_space=pl.ANY),
                pl.BlockSpec((tm, d), lambda m, pos: (m, 0)),
                pl.BlockSpec((tm, LANE), lambda m, pos: (m, 0)),
                pl.BlockSpec((1, 6, d), lambda m, pos: (_group_of(m, tm, seq, batch), 0, 0)),
            ],
            out_specs=pl.BlockSpec((tm, d), lambda m, pos: (m, 0)),
            scratch_shapes=[pltpu.VMEM((tm, d), F32), pltpu.VMEM((tm, d), F32),
                            pltpu.SemaphoreType.DMA(())],
        ),
        out_shape=jax.ShapeDtypeStruct((rows, d), F32),
        compiler_params=_cparams(1),
        name="moe_combine",
    )(pos, ys, x, route, mods)


def _moe_top2(x, nw, mods, router, w1, w3, w2, layer, *, rows, seq, batch, tm, tile=512):
    tok = dict(rows=rows, seq=seq, batch=batch)
    h, route = _moe_route(x, nw, mods, router, tm=tm, **tok)
    pos, src, tile_expert, n_used = _moe_plan(route, w1.shape[1], tile)
    ys = _moe_ffn(h, src, tile_expert, n_used, w1, w3, w2, layer, tile=tile)
    return _moe_combine(x, mods, route, ys, pos, tm=min(tm, 256), **tok)


def _final_norm_kernel(x_ref, w_ref, o_ref):
    x = x_ref[...]
    ms = jnp.mean(x * x, axis=-1, keepdims=True)
    o_ref[...] = x * lax.rsqrt(ms + NORM_EPS) * w_ref[...]


def _final_norm(x, w, *, rows, tm=512):
    d = x.shape[1]
    return pl.pallas_call(
        _final_norm_kernel,
        grid=(rows // tm,),
        in_specs=[pl.BlockSpec((tm, d), lambda m: (m, 0)), pl.BlockSpec((1, d), lambda m: (0, 0))],
        out_specs=pl.BlockSpec((tm, d), lambda m: (m, 0)),
        out_shape=jax.ShapeDtypeStruct((rows, d), F32),
        compiler_params=_cparams(1),
        name="final_norm",
    )(x, w.reshape(1, d))


def _dft_angles(n, period):
    k = lax.broadcasted_iota(jnp.int32, (n, n), 0)
    s = lax.broadcasted_iota(jnp.int32, (n, n), 1)
    return ((k * s) % period).astype(F32) * (2.0 * math.pi / period)


def _hyena_dft(seq_len):
    n = 2 * seq_len
    ang = _dft_angles(seq_len, n)
    row = lax.broadcasted_iota(jnp.int32, ang.shape, 0)
    col = lax.broadcasted_iota(jnp.int32, ang.shape, 1)
    cos = jnp.cos(ang)
    sin = jnp.sin(ang)
    fc = cos
    fs = jnp.where(row == 0, jnp.where(col % 2 == 0, 1.0, -1.0), sin)
    ic = jnp.where(col == 0, 1.0 / n, 2.0 / n * cos)
    isn = jnp.where(col == 0, jnp.where(row % 2 == 0, 1.0, -1.0) / n, 2.0 / n * sin)
    return fc.astype(BF16), fs.astype(BF16), ic.astype(BF16), isn.astype(BF16)


def _hyena_feats(seq_len):
    pos = np.arange(seq_len, dtype=np.float64)
    t = pos / max(seq_len - 1, 1)
    bands = np.linspace(1e-4, HY_BANDS - 1, HY_BANDS).astype(np.float32).astype(np.float64)
    ang = 2 * math.pi * pos[:, None] * bands[None, :] / seq_len
    feats = np.concatenate([t[:, None], np.cos(ang), -np.sin(ang)], axis=-1)
    out = np.zeros((seq_len, LANE), np.float32)
    out[:, :HY_POS_DIM] = feats
    return jnp.asarray(out)


def _split_dot(w, a):
    hi = a.astype(BF16)
    lo = (a - hi.astype(F32)).astype(BF16)
    return _bdot(w, hi) + _bdot(w, lo)


def _split2(a):
    hi = a.astype(BF16)
    return hi, (a - hi.astype(F32)).astype(BF16)


def _dot3(a, b):
    a_hi, a_lo = _split2(a)
    b_hi, b_lo = _split2(b)
    return _bdot(a_hi, b_hi) + _bdot(a_hi, b_lo) + _bdot(a_lo, b_hi)


def _filter_kernel(feat_ref, w1_ref, b1_ref, w2_ref, b2_ref, fr_ref, w3f_ref, w3b_ref,
                   dcf_ref, dcb_ref, fc_ref, fs_ref, o_ref, h_scr):
    @pl.when((pl.program_id(1) == 0) & (pl.program_id(2) == 0))
    def _():
        hp = lax.Precision.HIGHEST
        h1 = jnp.sin(fr_ref[0, 0:1, :] * (jnp.dot(feat_ref[...], w1_ref[0], precision=hp,
                                                  preferred_element_type=F32) + b1_ref[0]))
        h_scr[...] = jnp.sin(fr_ref[0, 1:2, :] * (jnp.dot(h1, w2_ref[0], precision=hp,
                                                          preferred_element_type=F32) + b2_ref[0]))

    h = h_scr[...]
    t = feat_ref[:, 0:1]
    hf = _dot3(h, w3f_ref[0]) * jnp.exp(-t * dcf_ref[0])
    hb = _dot3(h, w3b_ref[0]) * jnp.exp(-t * dcb_ref[0])
    row = lax.broadcasted_iota(jnp.int32, hf.shape, 0)
    hb = jnp.where(row == 0, 0.0, hb)
    ss = jnp.sum(hf * hf + hb * hb, axis=0, keepdims=True)
    scale = lax.rsqrt(ss + NORM_EPS)
    even = hf * scale + hb * scale
    odd = hf * scale - hb * scale
    kc = _split_dot(fc_ref[...], even)
    ks = _split_dot(fs_ref[...], odd)
    sign = jnp.where(row % 2 == 0, 1.0, -1.0)
    nyq = jnp.sum(even * sign, axis=0, keepdims=True)
    o_ref[0, 0] = kc
    o_ref[0, 1] = jnp.where(row == 0, nyq, ks)


def _hyena_filters(seq_len, w1p, b1, w2, b2, w3, freq, decay, fc, fs, *, tn=256):
    depth = w1p.shape[0]
    hid = HY_FILTER_HIDDEN
    nb = HY_W // tn
    feats = _hyena_feats(seq_len)
    col = lambda d: (lambda l, o, c: (l, 0, (2 * o + d) * nb + c))
    lyr = lambda l, o, c: (l, 0, 0)
    return pl.pallas_call(
        _filter_kernel,
        grid=(depth, 2, nb),
        in_specs=[
            pl.BlockSpec((seq_len, LANE), lambda l, o, c: (0, 0)),
            pl.BlockSpec((1, LANE, hid), lyr),
            pl.BlockSpec((1, 1, hid), lyr),
            pl.BlockSpec((1, hid, hid), lyr),
            pl.BlockSpec((1, 1, hid), lyr),
            pl.BlockSpec((1, 2, hid), lyr),
            pl.BlockSpec((1, hid, tn), col(0)),
            pl.BlockSpec((1, hid, tn), col(1)),
            pl.BlockSpec((1, 1, tn), col(0)),
            pl.BlockSpec((1, 1, tn), col(1)),
            _resident((seq_len, seq_len), lambda l, o, c: (0, 0)),
            _resident((seq_len, seq_len), lambda l, o, c: (0, 0)),
        ],
        out_specs=pl.BlockSpec((1, 2, seq_len, tn), lambda l, o, c: (l, 0, 0, o * nb + c)),
        out_shape=jax.ShapeDtypeStruct((depth, 2, seq_len, 2 * HY_W), F32),
        scratch_shapes=[pltpu.VMEM((seq_len, hid), F32)],
        compiler_params=_cparams(3),
        name="hyena_filters",
    )(feats, w1p, b1.reshape(depth, 1, hid), w2, b2.reshape(depth, 1, hid), freq,
      w3, w3, decay.reshape(depth, 1, 4 * HY_W), decay.reshape(depth, 1, 4 * HY_W), fc, fs)


def _conv3(z, w_ref, b_ref):
    n = z.shape[0]
    row = lax.broadcasted_iota(jnp.int32, z.shape, 0)
    prev = jnp.where(row == 0, 0.0, pltpu.roll(z, 1, 0))
    nxt = jnp.where(row == n - 1, 0.0, pltpu.roll(z, n - 1, 0))
    return prev * w_ref[0:1, :] + z * w_ref[1:2, :] + nxt * w_ref[2:3, :] + b_ref[...]


def _longconv_kernel(a_ref, g_ref, aw_ref, ab_ref, gw_ref, gb_ref, kf_ref, skip_ref,
                     fc_ref, fs_ref, ic_ref, is_ref, o_ref, y_scr, *, conv_a, fblk):
    a = a_ref[...].astype(F32)
    if conv_a:
        a = _conv3(a, aw_ref, ab_ref)
    a16 = a.astype(BF16)
    n = a.shape[0]
    for j in range(n // fblk):
        rows = slice(j * fblk, (j + 1) * fblk)
        ac = _bdot(fc_ref[rows, :], a16)
        asn = _bdot(fs_ref[rows, :], a16)
        kc = kf_ref[0, 0, rows, :]
        ks = kf_ref[0, 1, rows, :]
        cross = asn * ks
        yc = ac * kc - cross
        ys = ac * ks + asn * kc
        if j == 0:
            row = lax.broadcasted_iota(jnp.int32, ac.shape, 0)
            yc = jnp.where(row == 0, ac * kc, yc)
            ys = jnp.where(row == 0, cross, ys)
        y_scr[rows, :] = yc.astype(BF16)
        y_scr[n + j * fblk:n + (j + 1) * fblk, :] = ys.astype(BF16)
    conv = _bdot(ic_ref[...], y_scr[0:n, :]) + _bdot(is_ref[...], y_scr[n:2 * n, :])
    gate = _conv3(g_ref[...].astype(F32), gw_ref, gb_ref)
    o_ref[...] = (gate * (conv + a * skip_ref[0])).astype(o_ref.dtype)


def _hyena_longconv(a_src, a_part, zm, gate_part, conv_w, conv_b, kf, layer, order, skip,
                    dft, *, seq_len, n_seq, row0, out_rows, prev_out=None, tn=256):
    fc, fs, ic, isn = dft
    nb = HY_W // tn
    rb0 = row0 // seq_len
    fblk = min(512, seq_len)
    const = lambda c, b: (0, 0)
    conv_a = a_part is not None
    aw_blk = a_part * nb if conv_a else 0
    a_blk = ZM_HY // tn + aw_blk if conv_a else 0
    gw_blk = gate_part * nb
    g_blk = ZM_HY // tn + gw_blk
    kernel = functools.partial(_longconv_kernel, conv_a=conv_a, fblk=fblk)
    in_specs = [
        pl.BlockSpec((seq_len, tn), lambda c, b: (rb0 + b, a_blk + c)),
        pl.BlockSpec((seq_len, tn), lambda c, b: (rb0 + b, g_blk + c)),
        pl.BlockSpec((3, tn), lambda c, b: (0, aw_blk + c)),
        pl.BlockSpec((1, tn), lambda c, b: (0, aw_blk + c)),
        pl.BlockSpec((3, tn), lambda c, b: (0, gw_blk + c)),
        pl.BlockSpec((1, tn), lambda c, b: (0, gw_blk + c)),
        _resident((1, 2, seq_len, tn), lambda c, b: (layer, 0, 0, order * nb + c)),
        pl.BlockSpec((1, 1, tn), lambda c, b: (order, 0, c)),
        _resident((seq_len, seq_len), const),
        _resident((seq_len, seq_len), const),
        _resident((seq_len, seq_len), const),
        _resident((seq_len, seq_len), const),
    ]
    args = [a_src, zm, conv_w, conv_b.reshape(1, -1), conv_w, conv_b.reshape(1, -1), kf,
            skip.reshape(2, 1, HY_W), fc, fs, ic, isn]
    aliases = {}
    if prev_out is not None:
        in_specs.append(pl.BlockSpec(memory_space=pl.ANY))
        args.append(prev_out)
        aliases = {len(args) - 1: 0}
        kernel = functools.partial(_drop_last_input, kernel, 12)
    return pl.pallas_call(
        kernel,
        grid=(nb, n_seq),
        in_specs=in_specs,
        out_specs=pl.BlockSpec((seq_len, tn), lambda c, b: (rb0 + b, c)),
        out_shape=jax.ShapeDtypeStruct((out_rows, HY_W), BF16),
        scratch_shapes=[pltpu.VMEM((2 * seq_len, tn), BF16)],
        input_output_aliases=aliases,
        compiler_params=_cparams(2),
        name="hyena_longconv",
    )(*args)


def _drop_last_input(kernel, n_in, *refs):
    return kernel(*refs[:n_in], *refs[n_in + 1:])


def _fnet_dft(seq_len):
    ang_l = _dft_angles(seq_len, seq_len)
    ang_w = _dft_angles(FN_GROUP_W, FN_GROUP_W)
    scale = 1.0 / math.sqrt(seq_len * FN_GROUP_W)
    return ((jnp.cos(ang_l) * scale).astype(BF16), (jnp.sin(ang_l) * scale).astype(BF16),
            jnp.cos(ang_w).astype(BF16), jnp.sin(ang_w).astype(BF16))


def _fnet_kernel(z_ref, cl_ref, sl_ref, cw_ref, sw_ref, o_ref):
    z = z_ref[...]
    gc, gs = [], []
    for j in range(z.shape[1] // FN_GROUP_W):
        zj = z[:, j * FN_GROUP_W:(j + 1) * FN_GROUP_W]
        gc.append(_bdot(zj, cw_ref[...]).astype(BF16))
        gs.append(_bdot(zj, sw_ref[...]).astype(BF16))
    gc = jnp.concatenate(gc, axis=-1)
    gs = jnp.concatenate(gs, axis=-1)
    o_ref[...] = (_bdot(cl_ref[...], gc) - _bdot(sl_ref[...], gs)).astype(o_ref.dtype)


def _fnet(zm, dft, *, seq_len, n_seq, row0, out_rows, prev_out=None, tn=256):
    cl, sl, cw, sw = dft
    nb = FN_W // tn
    rb0 = row0 // seq_len
    c0 = ZM_FN // tn
    const = lambda c, b: (0, 0)
    in_specs = [
        pl.BlockSpec((seq_len, tn), lambda c, b: (rb0 + b, c0 + c)),
        _resident((seq_len, seq_len), const),
        _resident((seq_len, seq_len), const),
        pl.BlockSpec((FN_GROUP_W, FN_GROUP_W), const),
        pl.BlockSpec((FN_GROUP_W, FN_GROUP_W), const),
    ]
    args = [zm, cl, sl, cw, sw]
    kernel = _fnet_kernel
    aliases = {}
    if prev_out is not None:
        in_specs.append(pl.BlockSpec(memory_space=pl.ANY))
        args.append(prev_out)
        aliases = {len(args) - 1: 0}
        kernel = functools.partial(_drop_last_input, _fnet_kernel, 5)
    return pl.pallas_call(
        kernel,
        grid=(nb, n_seq),
        in_specs=in_specs,
        out_specs=pl.BlockSpec((seq_len, tn), lambda c, b: (rb0 + b, c)),
        out_shape=jax.ShapeDtypeStruct((out_rows, FN_W), BF16),
        input_output_aliases=aliases,
        compiler_params=_cparams(2),
        name="fnet",
    )(*args)


def _hgrn_tables(reverse):
    c = HG_CHUNK
    tri = np.zeros((c, c), np.float32)
    for t in range(c):
        if reverse:
            tri[t, t:] = 1.0
        else:
            tri[t, :t + 1] = 1.0
    mats = [tri]
    masks = []
    for lvl in range(HG_LEVELS):
        half = (c // 2) >> lvl
        sel = np.zeros((c, c), np.float32)
        mask = np.zeros((c, c), np.float32)
        for t in range(c):
            start = (t // (2 * half)) * 2 * half
            mid = start + half
            sel[t, mid if reverse else mid - 1] = 1.0
            for s in range(start, start + 2 * half):
                if reverse and t < mid <= s:
                    mask[t, s] = 1.0
                if (not reverse) and s < mid <= t:
                    mask[t, s] = 1.0
        mats.append(tri - sel @ tri)
        masks.append(mask)
    half = c // 2
    sel = np.zeros((c, c), np.float32)
    mask = np.zeros((c, c), np.float32)
    for t in range(c):
        start = (t // half) * half
        sel[t, start + half // 2] = 1.0
        for s in range(start, start + half):
            if (s >= t) if reverse else (s <= t):
                mask[t, s] = 1.0
    mats.append(tri - sel @ tri)
    masks.append(mask)
    mats = [mats[0], mats[1], mats[-1]] + mats[2:-1]
    return np.concatenate(mats, axis=0), np.stack(masks, axis=0)


HG_BOUNDED_BLOCKS = 3
HG_MID_BLOCK = 2


def _hgrn_level_block(lvl):
    return 1 if lvl == 0 else HG_BOUNDED_BLOCKS + lvl - 1


def _hgrn_decays(zf_ref, lb_ref, tab_ref, k_scr, p_scr, dec_scr):
    lb = lb_ref[...]
    f = lb + (1.0 - lb) * jax.nn.sigmoid(zf_ref[...])
    lf = jnp.log2(f)
    k_scr[...] = (1.0 - f).astype(BF16)
    p1 = lf.astype(BF16)
    r1 = lf - p1.astype(F32)
    p2 = r1.astype(BF16)
    p3 = (r1 - p2.astype(F32)).astype(BF16)
    p_scr[...] = jnp.concatenate([p1, p2, p3], axis=0)
    n = HG_BOUNDED_BLOCKS * HG_CHUNK
    dec_scr[0:n, :] = _bdot(tab_ref[0:n, :], p_scr[...])


def _hgrn_more_decays(tab_ref, p_scr, dec_scr):
    n = HG_BOUNDED_BLOCKS * HG_CHUNK
    dec_scr[n:, :] = _bdot(tab_ref[n:, :], p_scr[...])


def _nt_dot(a, b):
    return lax.dot_general(a, b, (((1,), (1,)), ((), ())), preferred_element_type=F32)


def _hgrn_direction(q_ref, v_ref, k_scr, dec_scr, mask_ref, st_scr, o_ref, end_row, bounded):
    c = HG_CHUNK
    heads = lambda ref, r0: jnp.stack(
        [ref[r0:r0 + c, h * HG_DK:(h + 1) * HG_DK] for h in range(HG_HEADS)], axis=0)
    bmm = lambda eq, a, b: jnp.einsum(eq, a, b, preferred_element_type=F32)
    q = heads(q_ref, 0)
    k = heads(k_scr, 0)
    v = heads(v_ref, 0)
    b = heads(dec_scr, 0)
    st = st_scr[...]
    o = bmm('htk,hkv->htv', q * jnp.exp2(b).astype(BF16), st.astype(BF16))
    level = lambda l: jnp.exp2(-jnp.abs(heads(dec_scr, _hgrn_level_block(l) * c))).astype(BF16)
    if bounded:
        e = level(0)
        att = jnp.where(mask_ref[0] > 0.5, bmm('htk,hsk->hts', q * e, k * e), 0.0)
        d = heads(dec_scr, HG_MID_BLOCK * c)
        inner = bmm('htk,hsk->hts', q * jnp.exp2(d).astype(BF16), k * jnp.exp2(-d).astype(BF16))
        att += jnp.where(mask_ref[HG_LEVELS] > 0.5, inner, 0.0)
    else:
        diag = jnp.sum(q.astype(F32) * k.astype(F32), axis=-1, keepdims=True)
        o += diag * v.astype(F32)
        att = jnp.zeros((HG_HEADS, c, c), F32)
        for lvl in range(HG_LEVELS):
            e = level(lvl)
            att += bmm('htk,hsk->hts', q * e, k * e) * mask_ref[lvl]
    o += bmm('hts,hsv->htv', att.astype(BF16), v)
    for h in range(HG_HEADS):
        o_ref[:, h * HG_DK:(h + 1) * HG_DK] = o[h]
    b_end = b[:, end_row:end_row + 1, :]
    upd = bmm('htk,htv->hkv', k * jnp.exp2(b_end - b).astype(BF16), v)
    decay = jnp.swapaxes(jnp.broadcast_to(jnp.exp2(b_end), (HG_HEADS, HG_DK, HG_DK)), 1, 2)
    st_scr[...] = decay * st + upd


HG_MAX_LOG2_SPAN = 100.0
HG_MAX_ABS_Q = 1e6


def _hgrn_kernel(qf_ref, vf_ref, zff_ref, qb_ref, vb_ref, zfb_ref, lbf_ref, lbb_ref,
                 tabf_ref, maskf_ref, tabb_ref, maskb_ref, s0f_ref, s0b_ref,
                 of_ref, ob_ref, sf_ref, sb_ref, stf_scr, stb_scr, kf_scr, kb_scr,
                 pf_scr, pb_scr, decf_scr, decb_scr):
    ci = pl.program_id(1)
    c = HG_CHUNK

    @pl.when(ci == 0)
    def _():
        stf_scr[...] = s0f_ref[0]
        stb_scr[...] = s0b_ref[0]

    _hgrn_decays(zff_ref, lbf_ref, tabf_ref, kf_scr, pf_scr, decf_scr)
    _hgrn_decays(zfb_ref, lbb_ref, tabb_ref, kb_scr, pb_scr, decb_scr)
    mid = slice(HG_MID_BLOCK * c, (HG_MID_BLOCK + 1) * c)
    span = jnp.maximum(jnp.max(jnp.abs(decf_scr[mid, :])), jnp.max(jnp.abs(decb_scr[mid, :])))
    q_abs = jnp.maximum(jnp.max(jnp.abs(qf_ref[...].astype(F32))),
                        jnp.max(jnp.abs(qb_ref[...].astype(F32))))
    bounded = (span < HG_MAX_LOG2_SPAN) & (q_abs < HG_MAX_ABS_Q)

    def run(flag):
        _hgrn_direction(qf_ref, vf_ref, kf_scr, decf_scr, maskf_ref, stf_scr, of_ref, c - 1, flag)
        _hgrn_direction(qb_ref, vb_ref, kb_scr, decb_scr, maskb_ref, stb_scr, ob_ref, 0, flag)

    @pl.when(bounded)
    def _():
        run(True)

    @pl.when(jnp.logical_not(bounded))
    def _():
        _hgrn_more_decays(tabf_ref, pf_scr, decf_scr)
        _hgrn_more_decays(tabb_ref, pb_scr, decb_scr)
        run(False)

    @pl.when(ci == pl.num_programs(1) - 1)
    def _():
        sf_ref[0] = stf_scr[...]
        sb_ref[0] = stb_scr[...]


def _hgrn(zm, zf, lb_f, lb_b, s0_f, s0_b, *, seq_len, n_seq, row0, out_rows,
          prev_of=None, prev_ob=None):
    c = HG_CHUNK
    nc = seq_len // c
    rb0 = row0 // c
    w = HG_W
    tab_f, mask_f = _hgrn_tables(False)
    tab_b, mask_b = _hgrn_tables(True)
    fwd = lambda col: (lambda b, i: (rb0 + b * nc + i, col))
    bwd = lambda col: (lambda b, i: (rb0 + b * nc + nc - 1 - i, col))
    const2 = lambda b, i: (0, 0)
    const3 = lambda b, i: (0, 0, 0)
    state = lambda b, i: (b, 0, 0, 0)
    nt = (2 + HG_LEVELS) * c
    in_specs = [
        pl.BlockSpec((c, w), fwd(ZM_Q // w)),
        pl.BlockSpec((c, w), fwd(ZM_I // w)),
        pl.BlockSpec((c, w), fwd(0)),
        pl.BlockSpec((c, w), bwd(ZM_Q // w)),
        pl.BlockSpec((c, w), bwd(ZM_I // w)),
        pl.BlockSpec((c, w), bwd(1)),
        pl.BlockSpec((1, w), const2),
        pl.BlockSpec((1, w), const2),
        pl.BlockSpec((nt, 3 * c), const2),
        pl.BlockSpec((HG_LEVELS + 1, c, c), const3),
        pl.BlockSpec((nt, 3 * c), const2),
        pl.BlockSpec((HG_LEVELS + 1, c, c), const3),
        pl.BlockSpec((1, HG_HEADS, HG_DK, HG_DK), state),
        pl.BlockSpec((1, HG_HEADS, HG_DK, HG_DK), state),
    ]
    args = [zm, zm, zf, zm, zm, zf, lb_f.reshape(1, w), lb_b.reshape(1, w),
            jnp.asarray(np.tile(tab_f, (1, 3)), BF16), jnp.asarray(mask_f),
            jnp.asarray(np.tile(tab_b, (1, 3)), BF16), jnp.asarray(mask_b), s0_f, s0_b]
    kernel = _hgrn_kernel
    aliases = {}
    if prev_of is not None:
        in_specs += [pl.BlockSpec(memory_space=pl.ANY), pl.BlockSpec(memory_space=pl.ANY)]
        args += [prev_of, prev_ob]
        aliases = {14: 0, 15: 1}
        kernel = functools.partial(_hgrn_alias_kernel, 14)
    st_shape = jax.ShapeDtypeStruct((n_seq, HG_HEADS, HG_DK, HG_DK), F32)
    return pl.pallas_call(
        kernel,
        grid=(n_seq, nc),
        in_specs=in_specs,
        out_specs=[
            pl.BlockSpec((c, w), fwd(0)),
            pl.BlockSpec((c, w), bwd(0)),
            pl.BlockSpec((1, HG_HEADS, HG_DK, HG_DK), state),
            pl.BlockSpec((1, HG_HEADS, HG_DK, HG_DK), state),
        ],
        out_shape=[jax.ShapeDtypeStruct((out_rows, w), F32),
                   jax.ShapeDtypeStruct((out_rows, w), F32), st_shape, st_shape],
        scratch_shapes=[pltpu.VMEM((HG_HEADS, HG_DK, HG_DK), F32),
                        pltpu.VMEM((HG_HEADS, HG_DK, HG_DK), F32),
                        pltpu.VMEM((c, w), BF16), pltpu.VMEM((c, w), BF16),
                        pltpu.VMEM((3 * c, w), BF16), pltpu.VMEM((3 * c, w), BF16),
                        pltpu.VMEM((nt, w), F32), pltpu.VMEM((nt, w), F32)],
        input_output_aliases=aliases,
        compiler_params=_cparams(2),
        name="hgrn2",
    )(*args)


def _hgrn_alias_kernel(n_in, *refs):
    return _hgrn_kernel(*refs[:n_in], *refs[n_in + 2:])


def kernel(x, c, ctx, c_ctx, w_mod, b_mod, norm1_w, norm2_w, w_in, hy_conv_w, hy_conv_b,
           hy_f_w1, hy_f_b1, hy_f_w2, hy_f_b2, hy_f_w3, hy_f_freq, hy_decay, hy_skip,
           hg_lower, hg_norm_w, w_br_hy, w_br_fn, w_br_hg, w_out,
           ffn_w1, ffn_w3, ffn_w2, moe_router, moe_w1, moe_w3, moe_w2, final_norm_w):
    batch, seq, d = x.shape
    l_ctx = ctx.shape[1]
    depth = w_in.shape[0]
    n_lat = batch * seq
    n_all = n_lat + batch * l_ctx

    lb_all = jnp.cumsum(jax.nn.softmax(hg_lower.astype(F32), axis=1), axis=1)
    lb_all = lb_all - lb_all[:, :1]

    cc = jnp.zeros((8, d), F32).at[:batch].set(c).at[batch].set(c_ctx)
    mods = _modulation(cc, w_mod, b_mod).reshape(depth, 8, 6, d)

    w1p = jnp.zeros((depth, LANE, HY_FILTER_HIDDEN), F32).at[:, :HY_POS_DIM].set(hy_f_w1)
    dft_lat = _hyena_dft(seq)
    dft_ctx = _hyena_dft(l_ctx)
    kf_lat = _hyena_filters(seq, w1p, hy_f_b1, hy_f_w2, hy_f_b2, hy_f_w3, hy_f_freq, hy_decay,
                            dft_lat[0], dft_lat[1])
    kf_ctx = _hyena_filters(l_ctx, w1p, hy_f_b1, hy_f_w2, hy_f_b2, hy_f_w3, hy_f_freq, hy_decay,
                            dft_ctx[0], dft_ctx[1])
    fn_lat = _fnet_dft(seq)
    fn_ctx = _fnet_dft(l_ctx)

    rows_x = jnp.concatenate([x.reshape(n_lat, d), ctx.reshape(batch * l_ctx, d)], axis=0)
    zero_state = jnp.zeros((batch, HG_HEADS, HG_DK, HG_DK), F32)
    tile_cap = math.gcd(seq, batch * l_ctx)
    tok = dict(seq=seq, batch=batch)
    tile = lambda t: dict(tm=min(t, tile_cap))
    lat = dict(seq_len=seq, n_seq=batch, row0=0, out_rows=n_all)
    cx = dict(seq_len=l_ctx, n_seq=batch, row0=n_lat, out_rows=n_all)

    for l in range(depth):
        last = l == depth - 1
        rows = n_lat if last else n_all
        m = mods[l]

        h = _norm_rows(rows_x, norm1_w[l], m, **tok, **tile(512))
        zm, zf = _in_proj(h, w_in, l, **tile(1024))

        of, ob, s_f, s_b = _hgrn(zm, zf, lb_all[0, l], lb_all[1, l], zero_state, zero_state, **cx)
        of, ob, _, _ = _hgrn(zm, zf, lb_all[0, l], lb_all[1, l], s_f, s_b, prev_of=of,
                             prev_ob=ob, **lat)

        hy_args = (hy_conv_w[l], hy_conv_b[l])
        mix = dict(lat, out_rows=rows)
        u = _hyena_longconv(zm, 0, zm, 1, *hy_args, kf_lat, l, 0, hy_skip[l], dft_lat, **mix)
        y_hy = _hyena_longconv(u, None, zm, 2, *hy_args, kf_lat, l, 1, hy_skip[l], dft_lat, **mix)
        y_fn = _fnet(zm, fn_lat, **mix)
        if not last:
            u = _hyena_longconv(zm, 0, zm, 1, *hy_args, kf_ctx, l, 0, hy_skip[l], dft_ctx,
                                prev_out=u, **cx)
            y_hy = _hyena_longconv(u, None, zm, 2, *hy_args, kf_ctx, l, 1, hy_skip[l], dft_ctx,
                                   prev_out=y_hy, **cx)
            y_fn = _fnet(zm, fn_ctx, prev_out=y_fn, **cx)

        rows_x = _merge_out(y_hy, y_fn, of, ob, zm, rows_x, hg_norm_w[l], m,
                            w_br_hy[l].astype(BF16), w_br_fn[l].astype(BF16),
                            w_br_hg[l].astype(BF16), w_out[l].astype(BF16), rows=rows,
                            **tok, **tile(256))

        if l % 2 == 0:
            i = l // 2
            rows_x = _ffn_dense(rows_x, norm2_w[l], m, ffn_w1[i].astype(BF16),
                                ffn_w3[i].astype(BF16), ffn_w2[i].astype(BF16), rows=rows,
                                **tok, **tile(512))
        else:
            i = l // 2
            rows_x = _moe_top2(rows_x, norm2_w[l], m, moe_router[i], moe_w1, moe_w3, moe_w2, i,
                               rows=rows, **tok, **tile(512))

    out = _final_norm(rows_x, final_norm_w, rows=n_lat, **tile(512))
    return out.reshape(batch, seq, d)
```

```python
import functools
import math

import numpy as np
import jax
import jax.numpy as jnp
from jax import lax
from jax.experimental import pallas as pl
from jax.experimental.pallas import tpu as pltpu

F32 = jnp.float32
BF16 = jnp.bfloat16
NORM_EPS = 1e-6

D_MODEL = 2048
DEPTH = 4
HY_W = 1024
HY_BANDS = 16
HY_POS_DIM = 2 * HY_BANDS + 1
HY_FILTER_HIDDEN = 64
FN_W = 1024
FN_GROUP_W = 128
HG_W = 1024
HG_HEADS = 8
HG_DK = 128
HG_CHUNK = 64
HG_LEVELS = 6
N_EXPERTS = 8
LANE = 128

HG_FF = 3 * HY_W + FN_W + 2 * HG_W
HG_I = HG_FF + 2 * HG_W
ZM_HY = 0
ZM_FN = ZM_HY + 3 * HY_W
ZM_Q = ZM_FN + FN_W
ZM_G = ZM_Q + HG_W
ZM_I = HG_I
ZM_GATE = HG_I + HG_W
ZM_COLS = ZM_GATE + 3 * D_MODEL

VMEM_LIMIT = 56 * 1024 * 1024


def _cparams(n_axes, vmem=VMEM_LIMIT):
    return pltpu.CompilerParams(dimension_semantics=("arbitrary",) * n_axes, vmem_limit_bytes=vmem)


def _resident(shape, index_map):
    return pl.BlockSpec(shape, index_map, pipeline_mode=pl.Buffered(1))


def _silu(v):
    return v * jax.nn.sigmoid(v)


def _bdot(a, b):
    return jnp.dot(a, b, preferred_element_type=F32)


def _mod_kernel(c_ref, w_ref, b_ref, o_ref):
    c = _silu(c_ref[...])
    o_ref[0] = jnp.dot(c, w_ref[0], preferred_element_type=F32,
                       precision=lax.Precision.HIGHEST) + b_ref[0]


def _modulation(cc, w_mod, b_mod):
    depth, d, n6 = w_mod.shape
    tn = 2048
    return pl.pallas_call(
        _mod_kernel,
        grid=(depth, n6 // tn),
        in_specs=[
            pl.BlockSpec((8, d), lambda l, n: (0, 0)),
            pl.BlockSpec((1, d, tn), lambda l, n: (l, 0, n)),
            pl.BlockSpec((1, 1, tn), lambda l, n: (l, 0, n)),
        ],
        out_specs=pl.BlockSpec((1, 8, tn), lambda l, n: (l, 0, n)),
        out_shape=jax.ShapeDtypeStruct((depth, 8, n6), F32),
        compiler_params=_cparams(2),
        name="modulation",
    )(cc, w_mod, b_mod.reshape(depth, 1, n6))


def _norm_mod(x, nw, shift, scale):
    ms = jnp.mean(x * x, axis=-1, keepdims=True)
    return (x * lax.rsqrt(ms + NORM_EPS) * nw) * (1 + scale) + shift


def _group_of(row_block, tm, seq, batch):
    return jnp.minimum((row_block * tm) // seq, batch)


def _norm_rows_kernel(x_ref, nw_ref, mod_ref, o_ref):
    h = _norm_mod(x_ref[...], nw_ref[...], mod_ref[0, 0:1, :], mod_ref[0, 1:2, :])
    o_ref[...] = h.astype(o_ref.dtype)


def _norm_rows(x, nw, mods, *, seq, batch, tm=512):
    rows, d = x.shape
    return pl.pallas_call(
        _norm_rows_kernel,
        grid=(rows // tm,),
        in_specs=[
            pl.BlockSpec((tm, d), lambda m: (m, 0)),
            pl.BlockSpec((1, d), lambda m: (0, 0)),
            pl.BlockSpec((1, 6, d), lambda m: (_group_of(m, tm, seq, batch), 0, 0)),
        ],
        out_specs=pl.BlockSpec((tm, d), lambda m: (m, 0)),
        out_shape=jax.ShapeDtypeStruct((rows, d), BF16),
        compiler_params=_cparams(1),
        name="norm_rows",
    )(x, nw.reshape(1, d), mods)


def _inproj_kernel(h_ref, w_ref, z_ref, zf_ref, w_scr, *, n_main):
    @pl.when(pl.program_id(1) == 0)
    def _():
        w_scr[...] = w_ref[0].astype(BF16)

    r = _bdot(h_ref[...], w_scr[...])

    @pl.when(pl.program_id(0) < n_main)
    def _():
        z_ref[...] = r.astype(z_ref.dtype)

    @pl.when(pl.program_id(0) >= n_main)
    def _():
        zf_ref[...] = r


def _in_proj(h, w_in, layer, *, tm=1024, tn=1024):
    rows, d = h.shape
    n_cols = w_in.shape[2]
    nb = n_cols // tn
    f0 = HG_FF // tn
    nf = 2 * HG_W // tn
    n_main = nb - nf
    nm = rows // tm
    col = lambda j: jnp.where(j < f0, j, jnp.where(j < n_main, j + nf, j - n_main + f0))
    z_idx = lambda j, m: (jnp.where(j < n_main, m, nm - 1), col(jnp.minimum(j, n_main - 1)))
    zf_idx = lambda j, m: (jnp.where(j < n_main, 0, m), jnp.maximum(j - n_main, 0))
    return pl.pallas_call(
        functools.partial(_inproj_kernel, n_main=n_main),
        grid=(nb, nm),
        in_specs=[
            pl.BlockSpec((tm, d), lambda j, m: (m, 0)),
            pl.BlockSpec((1, d, tn), lambda j, m: (layer, 0, col(j))),
        ],
        out_specs=[pl.BlockSpec((tm, tn), z_idx), pl.BlockSpec((tm, tn), zf_idx)],
        out_shape=[jax.ShapeDtypeStruct((rows, n_cols), BF16),
                   jax.ShapeDtypeStruct((rows, nf * tn), F32)],
        scratch_shapes=[pltpu.VMEM((d, tn), BF16)],
        compiler_params=_cparams(2),
        name="in_proj",
    )(h, w_in)


def _merge_kernel(yhy_ref, yfn_ref, of_ref, ob_ref, g_ref, ga0_ref, ga1_ref, gb0_ref, gb1_ref,
                  gc0_ref, gc1_ref, x_ref, hnw_ref, mod_ref,
                  wbhy_ref, wbfn_ref, wbhg_ref, wo_ref, o_ref):
    d = x_ref.shape[1]
    ohg = of_ref[...] + ob_ref[...]
    nw = hnw_ref[...]
    heads = []
    for h in range(ohg.shape[1] // HG_DK):
        oh = ohg[:, h * HG_DK:(h + 1) * HG_DK]
        ms = jnp.mean(oh * oh, axis=-1, keepdims=True)
        heads.append(oh * lax.rsqrt(ms + NORM_EPS) * nw)
    y_hg = jnp.concatenate(heads, axis=-1) * _silu(g_ref[...].astype(F32))
    gate = lambda lo, hi: jax.nn.sigmoid(
        jnp.concatenate([lo[...], hi[...]], axis=-1).astype(F32))
    merged = gate(ga0_ref, ga1_ref) * _bdot(yhy_ref[...], wbhy_ref[...])
    merged += gate(gb0_ref, gb1_ref) * _bdot(yfn_ref[...], wbfn_ref[...])
    merged += gate(gc0_ref, gc1_ref) * _bdot(y_hg.astype(BF16), wbhg_ref[...])
    out = _bdot(merged.astype(BF16), wo_ref[...])
    o_ref[...] = x_ref[...] + mod_ref[0, 2:3, :] * out


def _merge_out(y_hy, y_fn, o_f, o_b, zm, x, hg_nw, mods, wb_hy, wb_fn, wb_hg, w_o,
               *, rows, seq, batch, tm=256):
    d = x.shape[1]
    w = y_hy.shape[1]
    tm = min(tm, rows)
    row = lambda m: (m, 0)
    const = lambda m: (0, 0)
    gate0 = ZM_GATE // (d // 2)
    return pl.pallas_call(
        _merge_kernel,
        grid=(rows // tm,),
        in_specs=[
            pl.BlockSpec((tm, w), row),
            pl.BlockSpec((tm, w), row),
            pl.BlockSpec((tm, w), row),
            pl.BlockSpec((tm, w), row),
            pl.BlockSpec((tm, w), lambda m: (m, ZM_G // w)),
            *[pl.BlockSpec((tm, d // 2), functools.partial(lambda i, m: (m, i), gate0 + i))
              for i in range(6)],
            pl.BlockSpec((tm, d), row),
            pl.BlockSpec((1, HG_DK), const),
            pl.BlockSpec((1, 6, d), lambda m: (_group_of(m, tm, seq, batch), 0, 0)),
            _resident((w, d), const),
            _resident((w, d), const),
            _resident((w, d), const),
            _resident((d, d), const),
        ],
        out_specs=pl.BlockSpec((tm, d), row),
        out_shape=jax.ShapeDtypeStruct((rows, d), F32),
        compiler_params=_cparams(1),
        name="merge_out",
    )(y_hy, y_fn, o_f, o_b, zm, *([zm] * 6), x, hg_nw.reshape(1, HG_DK), mods,
      wb_hy, wb_fn, wb_hg, w_o)


def _ffn_kernel(x_ref, nw_ref, mod_ref, w1_ref, w3_ref, w2_ref, o_ref, h_scr, acc_scr):
    f = pl.program_id(1)

    @pl.when(f == 0)
    def _():
        h = _norm_mod(x_ref[...], nw_ref[...], mod_ref[0, 3:4, :], mod_ref[0, 4:5, :])
        h_scr[...] = h.astype(BF16)
        acc_scr[...] = jnp.zeros_like(acc_scr)

    h = h_scr[...]
    act = _silu(_bdot(h, w1_ref[...])) * _bdot(h, w3_ref[...])
    acc_scr[...] += _bdot(act.astype(BF16), w2_ref[...])

    @pl.when(f == pl.num_programs(1) - 1)
    def _():
        o_ref[...] = x_ref[...] + mod_ref[0, 5:6, :] * acc_scr[...]


def _ffn_dense(x, nw, mods, w1, w3, w2, *, rows, seq, batch, tm=512, tf=512):
    d = x.shape[1]
    ff = w1.shape[1]
    tm = min(tm, rows)
    return pl.pallas_call(
        _ffn_kernel,
        grid=(rows // tm, ff // tf),
        in_specs=[
            pl.BlockSpec((tm, d), lambda m, f: (m, 0)),
            pl.BlockSpec((1, d), lambda m, f: (0, 0)),
            pl.BlockSpec((1, 6, d), lambda m, f: (_group_of(m, tm, seq, batch), 0, 0)),
            pl.BlockSpec((d, tf), lambda m, f: (0, f)),
            pl.BlockSpec((d, tf), lambda m, f: (0, f)),
            pl.BlockSpec((tf, d), lambda m, f: (f, 0)),
        ],
        out_specs=pl.BlockSpec((tm, d), lambda m, f: (m, 0)),
        out_shape=jax.ShapeDtypeStruct((rows, d), F32),
        scratch_shapes=[pltpu.VMEM((tm, d), BF16), pltpu.VMEM((tm, d), F32)],
        compiler_params=_cparams(2),
        name="ffn_dense",
    )(x, nw.reshape(1, d), mods, w1, w3, w2)


def _top2(logits, n_experts):
    lane = lax.broadcasted_iota(jnp.int32, logits.shape, 1).astype(F32)
    neg = jnp.float32(-jnp.inf)
    lg = jnp.where(lane < n_experts, logits, neg)
    m1 = jnp.max(lg, axis=-1, keepdims=True)
    i1 = jnp.min(jnp.where(lg == m1, lane, float(LANE)), axis=-1, keepdims=True)
    lg2 = jnp.where(lane == i1, neg, lg)
    m2 = jnp.max(lg2, axis=-1, keepdims=True)
    i2 = jnp.min(jnp.where(lg2 == m2, lane, float(LANE)), axis=-1, keepdims=True)
    e2 = jnp.exp(m2 - m1)
    return i1, i2, 1.0 / (1.0 + e2), e2 / (1.0 + e2)


def _route_kernel(x_ref, nw_ref, mod_ref, r_ref, h_ref, route_ref):
    h = _norm_mod(x_ref[...], nw_ref[...], mod_ref[0, 3:4, :], mod_ref[0, 4:5, :])
    logits = jnp.dot(h, r_ref[...], preferred_element_type=F32, precision=lax.Precision.HIGHEST)
    i1, i2, p1, p2 = _top2(logits, N_EXPERTS)
    lane = lax.broadcasted_iota(jnp.int32, logits.shape, 1)
    route = jnp.where(lane == 0, i1, jnp.where(lane == 1, i2,
                      jnp.where(lane == 2, p1, jnp.where(lane == 3, p2, 0.0))))
    h_ref[...] = h
    route_ref[...] = route


def _moe_route(x, nw, mods, router, *, rows, seq, batch, tm=512):
    d = x.shape[1]
    rpad = jnp.zeros((d, LANE), F32).at[:, :router.shape[1]].set(router)
    return pl.pallas_call(
        _route_kernel,
        grid=(rows // tm,),
        in_specs=[
            pl.BlockSpec((tm, d), lambda m: (m, 0)),
            pl.BlockSpec((1, d), lambda m: (0, 0)),
            pl.BlockSpec((1, 6, d), lambda m: (_group_of(m, tm, seq, batch), 0, 0)),
            pl.BlockSpec((d, LANE), lambda m: (0, 0)),
        ],
        out_specs=[pl.BlockSpec((tm, d), lambda m: (m, 0)),
                   pl.BlockSpec((tm, LANE), lambda m: (m, 0))],
        out_shape=[jax.ShapeDtypeStruct((rows, d), F32), jax.ShapeDtypeStruct((rows, LANE), F32)],
        compiler_params=_cparams(1),
        name="moe_route",
    )(x, nw.reshape(1, d), mods, rpad)


def _moe_plan(route, n_experts, tile):
    rows = route.shape[0]
    e = route[:, :2].astype(jnp.int32).reshape(-1)
    onehot = (e[:, None] == jnp.arange(n_experts, dtype=jnp.int32)[None, :]).astype(jnp.int32)
    before = jnp.cumsum(onehot, axis=0) - onehot
    rank = jnp.sum(before * onehot, axis=1)
    counts = jnp.sum(onehot, axis=0)
    padded = ((counts + tile - 1) // tile) * tile
    ends = jnp.cumsum(padded)
    starts = ends - padded
    pos = jnp.sum(starts[None, :] * onehot, axis=1) + rank
    n_tiles = (2 * rows) // tile + n_experts
    src = jnp.zeros((n_tiles * tile,), jnp.int32).at[pos].set(
        jnp.arange(2 * rows, dtype=jnp.int32) // 2)
    tile_start = jnp.arange(n_tiles, dtype=jnp.int32) * tile
    tile_expert = jnp.sum((tile_start[:, None] >= ends[None, :]).astype(jnp.int32), axis=1)
    tile_expert = jnp.minimum(tile_expert, n_experts - 1)
    n_used = (ends[-1] // tile).reshape(1)
    return pos, src, tile_expert, n_used


def _row_copy(src_hbm, src_row, dst, dst_row, sem):
    return pltpu.make_async_copy(src_hbm.at[pl.ds(src_row, 1)], dst.at[pl.ds(dst_row, 1)], sem)


def _gather_rows(idx_ref, base, stride, src_hbm, dst, sem, *, wait):
    if wait:
        pltpu.make_async_copy(src_hbm.at[pl.ds(0, dst.shape[0])], dst, sem).wait()
        return

    def body(r, carry):
        _row_copy(src_hbm, idx_ref[base + stride * r], dst, r, sem).start()
        return carry

    lax.fori_loop(0, dst.shape[0], body, 0, unroll=8)


def _moe_ffn_kernel(te_ref, nu_ref, src_ref, h_hbm, w1_ref, w3_ref, w2_ref, o_ref,
                    rows_scr, h_scr, sems):
    t = pl.program_id(0)
    f = pl.program_id(1)
    last = pl.num_programs(1) - 1
    tile = o_ref.shape[0]
    n_used = nu_ref[0]
    used = t < n_used
    slot = t % 2

    def gather(tile_idx, buf, wait):
        _gather_rows(src_ref, tile_idx * tile, 1, h_hbm, rows_scr.at[buf], sems.at[buf], wait=wait)

    @pl.when(used & (f == 0) & (t == 0))
    def _():
        gather(t, slot, False)

    @pl.when(used & (f == 0))
    def _():
        gather(t, slot, True)
        h_scr[...] = rows_scr[slot].astype(BF16)
        o_ref[...] = jnp.zeros_like(o_ref)

    @pl.when((f == 0) & (t + 1 < n_used))
    def _():
        gather(t + 1, 1 - slot, False)

    @pl.when(used)
    def _():
        h = h_scr[...]
        act = _silu(_bdot(h, w1_ref[0, 0].astype(BF16))) * _bdot(h, w3_ref[0, 0].astype(BF16))
        o_ref[...] += _bdot(act.astype(BF16), w2_ref[0, 0].astype(BF16))

    @pl.when(jnp.logical_not(used) & (f == last))
    def _():
        o_ref[...] = jnp.zeros_like(o_ref)


def _moe_ffn(h, src, tile_expert, n_used, w1, w3, w2, layer, *, tile, tf=512):
    d = h.shape[1]
    n = src.shape[0]
    ff = w1.shape[3]
    nf = ff // tf
    fsel = lambda t, f, nu: jnp.where(t < nu[0], f, nf - 1)
    up = lambda t, f, te, nu, sr: (layer, te[t], 0, fsel(t, f, nu))
    down = lambda t, f, te, nu, sr: (layer, te[t], fsel(t, f, nu), 0)
    return pl.pallas_call(
        _moe_ffn_kernel,
        grid_spec=pltpu.PrefetchScalarGridSpec(
            num_scalar_prefetch=3,
            grid=(n // tile, nf),
            in_specs=[
                pl.BlockSpec(memory_space=pl.ANY),
                pl.BlockSpec((1, 1, d, tf), up),
                pl.BlockSpec((1, 1, d, tf), up),
                pl.BlockSpec((1, 1, tf, d), down),
            ],
            out_specs=pl.BlockSpec((tile, d), lambda t, f, te, nu, sr: (t, 0)),
            scratch_shapes=[pltpu.VMEM((2, tile, d), F32), pltpu.VMEM((tile, d), BF16),
                            pltpu.SemaphoreType.DMA((2,))],
        ),
        out_shape=jax.ShapeDtypeStruct((n, d), F32),
        compiler_params=_cparams(2),
        name="moe_ffn",
    )(tile_expert, n_used, src, h, w1, w3, w2)


def _combine_kernel(pos_ref, ys_hbm, x_ref, route_ref, mod_ref, o_ref, y1_scr, y2_scr, sem):
    base = 2 * pl.program_id(0) * x_ref.shape[0]
    _gather_rows(pos_ref, base, 2, ys_hbm, y1_scr, sem, wait=False)
    _gather_rows(pos_ref, base + 1, 2, ys_hbm, y2_scr, sem, wait=False)
    _gather_rows(pos_ref, base, 2, ys_hbm, y1_scr, sem, wait=True)
    _gather_rows(pos_ref, base + 1, 2, ys_hbm, y2_scr, sem, wait=True)
    mix = route_ref[:, 2:3] * y1_scr[...] + route_ref[:, 3:4] * y2_scr[...]
    o_ref[...] = x_ref[...] + mod_ref[0, 5:6, :] * mix


def _moe_combine(x, mods, route, ys, pos, *, rows, seq, batch, tm=256):
    d = x.shape[1]
    return pl.pallas_call(
        _combine_kernel,
        grid_spec=pltpu.PrefetchScalarGridSpec(
            num_scalar_prefetch=1,
            grid=(rows // tm,),
            in_specs=[
                pl.BlockSpec(memory_space=pl.ANY),
                pl.BlockSpec((tm, d), lambda m, pos: (m, 0)),
                pl.BlockSpec((tm, LANE), lambda m, pos: (m, 0)),
                pl.BlockSpec((1, 6, d), lambda m, pos: (_group_of(m, tm, seq, batch), 0, 0)),
            ],
            out_specs=pl.BlockSpec((tm, d), lambda m, pos: (m, 0)),
            scratch_shapes=[pltpu.VMEM((tm, d), F32), pltpu.VMEM((tm, d), F32),
                            pltpu.SemaphoreType.DMA(())],
        ),
        out_shape=jax.ShapeDtypeStruct((rows, d), F32),
        compiler_params=_cparams(1),
        name="moe_combine",
    )(pos, ys, x, route, mods)


def _moe_top2(x, nw, mods, router, w1, w3, w2, layer, *, rows, seq, batch, tm, tile=512):
    tok = dict(rows=rows, seq=seq, batch=batch)
    h, route = _moe_route(x, nw, mods, router, tm=tm, **tok)
    pos, src, tile_expert, n_used = _moe_plan(route, w1.shape[1], tile)
    ys = _moe_ffn(h, src, tile_expert, n_used, w1, w3, w2, layer, tile=tile)
    return _moe_combine(x, mods, route, ys, pos, tm=min(tm, 256), **tok)


def _final_norm_kernel(x_ref, w_ref, o_ref):
    x = x_ref[...]
    ms = jnp.mean(x * x, axis=-1, keepdims=True)
    o_ref[...] = x * lax.rsqrt(ms + NORM_EPS) * w_ref[...]


def _final_norm(x, w, *, rows, tm=512):
    d = x.shape[1]
    return pl.pallas_call(
        _final_norm_kernel,
        grid=(rows // tm,),
        in_specs=[pl.BlockSpec((tm, d), lambda m: (m, 0)), pl.BlockSpec((1, d), lambda m: (0, 0))],
        out_specs=pl.BlockSpec((tm, d), lambda m: (m, 0)),
        out_shape=jax.ShapeDtypeStruct((rows, d), F32),
        compiler_params=_cparams(1),
        name="final_norm",
    )(x, w.reshape(1, d))


def _dft_angles(n, period):
    k = lax.broadcasted_iota(jnp.int32, (n, n), 0)
    s = lax.broadcasted_iota(jnp.int32, (n, n), 1)
    return ((k * s) % period).astype(F32) * (2.0 * math.pi / period)


def _hyena_dft(seq_len):
    n = 2 * seq_len
    half = seq_len // 2
    row = lax.broadcasted_iota(jnp.int32, (seq_len, seq_len), 0)
    col = lax.broadcasted_iota(jnp.int32, (seq_len, seq_len), 1)
    freq = lambda i: jnp.where(i < half, 2 * i, 2 * (i - half) + 1)
    to_angle = lambda prod: (prod % n).astype(F32) * (2.0 * math.pi / n)
    ang_f = to_angle(freq(row) * col)
    ang_i = to_angle(row * freq(col))
    fc = jnp.cos(ang_f)
    fs = jnp.where(row == 0, jnp.where(col % 2 == 0, 1.0, -1.0), jnp.sin(ang_f))
    ic = jnp.where(col == 0, 1.0 / n, 2.0 / n * jnp.cos(ang_i))
    isn = jnp.where(col == 0, jnp.where(row % 2 == 0, 1.0, -1.0) / n, 2.0 / n * jnp.sin(ang_i))
    r = lax.broadcasted_iota(jnp.int32, (half, half), 0)
    c = lax.broadcasted_iota(jnp.int32, (half, half), 1)
    flip = jnp.where((r >= 1) & (c == half - r), 1.0, 0.0)
    return (fc.astype(BF16), fs.astype(BF16), ic.astype(BF16), isn.astype(BF16),
            flip.astype(BF16))


def _hyena_feats(seq_len):
    pos = np.arange(seq_len, dtype=np.float64)
    t = pos / max(seq_len - 1, 1)
    bands = np.linspace(1e-4, HY_BANDS - 1, HY_BANDS).astype(np.float32).astype(np.float64)
    ang = 2 * math.pi * pos[:, None] * bands[None, :] / seq_len
    feats = np.concatenate([t[:, None], np.cos(ang), -np.sin(ang)], axis=-1)
    out = np.zeros((seq_len, LANE), np.float32)
    out[:, :HY_POS_DIM] = feats
    return jnp.asarray(out)


def _split2(a):
    hi = a.astype(BF16)
    return hi, (a - hi.astype(F32)).astype(BF16)


def _dot3(a, b):
    a_hi, a_lo = _split2(a)
    b_hi, b_lo = _split2(b)
    return _bdot(a_hi, b_hi) + _bdot(a_hi, b_lo) + _bdot(a_lo, b_hi)


def _filter_kernel(feat_ref, w1_ref, b1_ref, w2_ref, b2_ref, fr_ref, w3f_ref, w3b_ref,
                   dcf_ref, dcb_ref, fc_ref, fs_ref, o_ref, h_scr):
    @pl.when((pl.program_id(1) == 0) & (pl.program_id(2) == 0))
    def _():
        hp = lax.Precision.HIGHEST
        h1 = jnp.sin(fr_ref[0, 0:1, :] * (jnp.dot(feat_ref[...], w1_ref[0], precision=hp,
                                                  preferred_element_type=F32) + b1_ref[0]))
        h_scr[...] = jnp.sin(fr_ref[0, 1:2, :] * (jnp.dot(h1, w2_ref[0], precision=hp,
                                                          preferred_element_type=F32) + b2_ref[0]))

    h = h_scr[...]
    t = feat_ref[:, 0:1]
    hf = _dot3(h, w3f_ref[0]) * jnp.exp(-t * dcf_ref[0])
    hb = _dot3(h, w3b_ref[0]) * jnp.exp(-t * dcb_ref[0])
    row = lax.broadcasted_iota(jnp.int32, hf.shape, 0)
    hb = jnp.where(row == 0, 0.0, hb)
    ss = jnp.sum(hf * hf + hb * hb, axis=0, keepdims=True)
    scale = lax.rsqrt(ss + NORM_EPS)
    even = hf * scale + hb * scale
    odd = hf * scale - hb * scale
    kc = _bdot(fc_ref[...], even.astype(BF16))
    ks = _bdot(fs_ref[...], odd.astype(BF16))
    sign = jnp.where(row % 2 == 0, 1.0, -1.0)
    nyq = jnp.sum(even * sign, axis=0, keepdims=True)
    o_ref[0, 0] = kc
    o_ref[0, 1] = jnp.where(row == 0, nyq, ks)


def _hyena_filters(seq_len, w1p, b1, w2, b2, w3, freq, decay, fc, fs, *, tn=256):
    depth = w1p.shape[0]
    hid = HY_FILTER_HIDDEN
    nb = HY_W // tn
    feats = _hyena_feats(seq_len)
    col = lambda d: (lambda l, o, c: (l, 0, (2 * o + d) * nb + c))
    lyr = lambda l, o, c: (l, 0, 0)
    return pl.pallas_call(
        _filter_kernel,
        grid=(depth, 2, nb),
        in_specs=[
            pl.BlockSpec((seq_len, LANE), lambda l, o, c: (0, 0)),
            pl.BlockSpec((1, LANE, hid), lyr),
            pl.BlockSpec((1, 1, hid), lyr),
            pl.BlockSpec((1, hid, hid), lyr),
            pl.BlockSpec((1, 1, hid), lyr),
            pl.BlockSpec((1, 2, hid), lyr),
            pl.BlockSpec((1, hid, tn), col(0)),
            pl.BlockSpec((1, hid, tn), col(1)),
            pl.BlockSpec((1, 1, tn), col(0)),
            pl.BlockSpec((1, 1, tn), col(1)),
            _resident((seq_len, seq_len), lambda l, o, c: (0, 0)),
            _resident((seq_len, seq_len), lambda l, o, c: (0, 0)),
        ],
        out_specs=pl.BlockSpec((1, 2, seq_len, tn), lambda l, o, c: (l, 0, 0, o * nb + c)),
        out_shape=jax.ShapeDtypeStruct((depth, 2, seq_len, 2 * HY_W), F32),
        scratch_shapes=[pltpu.VMEM((seq_len, hid), F32)],
        compiler_params=_cparams(3),
        name="hyena_filters",
    )(feats, w1p, b1.reshape(depth, 1, hid), w2, b2.reshape(depth, 1, hid), freq,
      w3, w3, decay.reshape(depth, 1, 4 * HY_W), decay.reshape(depth, 1, 4 * HY_W), fc, fs)


def _conv3(z, w_ref, b_ref):
    n = z.shape[0]
    row = lax.broadcasted_iota(jnp.int32, z.shape, 0)
    prev = jnp.where(row == 0, 0.0, pltpu.roll(z, 1, 0))
    nxt = jnp.where(row == n - 1, 0.0, pltpu.roll(z, n - 1, 0))
    return prev * w_ref[0:1, :] + z * w_ref[1:2, :] + nxt * w_ref[2:3, :] + b_ref[...]


def _longconv_kernel(a_ref, g_ref, aw_ref, ab_ref, gw_ref, gb_ref, kf_ref, skip_ref,
                     fc_ref, fs_ref, ic_ref, is_ref, flip_ref, o_ref, *, conv_a):
    a = a_ref[...].astype(F32)
    if conv_a:
        a = _conv3(a, aw_ref, ab_ref)
    n = a.shape[0]
    h = n // 2
    lo = slice(0, h)
    hi = slice(h, n)
    k = lax.broadcasted_iota(jnp.int32, (h, a.shape[1]), 0)
    alt = jnp.where(k % 2 == 0, 1.0, -1.0)
    first = k == 0
    a_lo = a[lo]
    a_hi = a[hi]
    a_mid = a_hi[0:1, :]
    rev = _bdot(flip_ref[...], a_hi.astype(BF16))
    even_in = (a_lo + rev).astype(BF16)
    odd_in = (a_lo - rev).astype(BF16)
    nyq = jnp.sum(alt * (a_lo + a_hi), axis=0, keepdims=True)
    ac_e = _bdot(fc_ref[lo, lo], even_in) + alt * a_mid
    as_e = jnp.where(first, nyq, _bdot(fs_ref[lo, lo], odd_in))
    ac_o = _bdot(fc_ref[hi, lo], odd_in)
    as_o = _bdot(fs_ref[hi, lo], even_in) + alt * a_mid
    kc_e = kf_ref[0, 0, lo, :]
    ks_e = kf_ref[0, 1, lo, :]
    kc_o = kf_ref[0, 0, hi, :]
    ks_o = kf_ref[0, 1, hi, :]
    cross = as_e * ks_e
    yc_e = jnp.where(first, ac_e * kc_e, ac_e * kc_e - cross)
    ys_e = jnp.where(first, 0.0, ac_e * ks_e + as_e * kc_e)
    y_nyq = cross[0:1, :]
    yc_o = ac_o * kc_o - as_o * ks_o
    ys_o = ac_o * ks_o + as_o * kc_o
    sym = (_bdot(ic_ref[lo, lo], yc_e.astype(BF16)) + _bdot(is_ref[lo, hi], ys_o.astype(BF16))
           + alt * (y_nyq * (0.5 / n)))
    anti = _bdot(ic_ref[lo, hi], yc_o.astype(BF16)) + _bdot(is_ref[lo, lo], ys_e.astype(BF16))
    y_mid = (jnp.sum(alt * (yc_e + ys_o), axis=0, keepdims=True) * (1.0 / n)
             - yc_e[0:1, :] * (0.5 / n) + y_nyq * (0.5 / n))
    y_hi = jnp.where(first, y_mid, _bdot(flip_ref[...], (sym - anti).astype(BF16)))
    conv = jnp.concatenate([sym + anti, y_hi], axis=0)
    gate = _conv3(g_ref[...].astype(F32), gw_ref, gb_ref)
    o_ref[...] = (gate * (conv + a * skip_ref[0])).astype(o_ref.dtype)


def _hyena_longconv(a_src, a_part, zm, gate_part, conv_w, conv_b, kf, layer, order, skip,
                    dft, *, seq_len, n_seq, row0, out_rows, prev_out=None, tn=256):
    fc, fs, ic, isn, flip = dft
    assert seq_len % 4 == 0
    nb = HY_W // tn
    rb0 = row0 // seq_len
    const = lambda c, b: (0, 0)
    conv_a = a_part is not None
    aw_blk = a_part * nb if conv_a else 0
    a_blk = ZM_HY // tn + aw_blk if conv_a else 0
    gw_blk = gate_part * nb
    g_blk = ZM_HY // tn + gw_blk
    kernel = functools.partial(_longconv_kernel, conv_a=conv_a)
    in_specs = [
        pl.BlockSpec((seq_len, tn), lambda c, b: (rb0 + b, a_blk + c)),
        pl.BlockSpec((seq_len, tn), lambda c, b: (rb0 + b, g_blk + c)),
        pl.BlockSpec((3, tn), lambda c, b: (0, aw_blk + c)),
        pl.BlockSpec((1, tn), lambda c, b: (0, aw_blk + c)),
        pl.BlockSpec((3, tn), lambda c, b: (0, gw_blk + c)),
        pl.BlockSpec((1, tn), lambda c, b: (0, gw_blk + c)),
        _resident((1, 2, seq_len, tn), lambda c, b: (layer, 0, 0, order * nb + c)),
        pl.BlockSpec((1, 1, tn), lambda c, b: (order, 0, c)),
        _resident((seq_len, seq_len), const),
        _resident((seq_len, seq_len), const),
        _resident((seq_len, seq_len), const),
        _resident((seq_len, seq_len), const),
        _resident((seq_len // 2, seq_len // 2), const),
    ]
    args = [a_src, zm, conv_w, conv_b.reshape(1, -1), conv_w, conv_b.reshape(1, -1), kf,
            skip.reshape(2, 1, HY_W), fc, fs, ic, isn, flip]
    aliases = {}
    if prev_out is not None:
        in_specs.append(pl.BlockSpec(memory_space=pl.ANY))
        args.append(prev_out)
        aliases = {len(args) - 1: 0}
        kernel = functools.partial(_drop_last_input, kernel, 13)
    return pl.pallas_call(
        kernel,
        grid=(nb, n_seq),
        in_specs=in_specs,
        out_specs=pl.BlockSpec((seq_len, tn), lambda c, b: (rb0 + b, c)),
        out_shape=jax.ShapeDtypeStruct((out_rows, HY_W), BF16),
        input_output_aliases=aliases,
        compiler_params=_cparams(2),
        name="hyena_longconv",
    )(*args)


def _drop_last_input(kernel, n_in, *refs):
    return kernel(*refs[:n_in], *refs[n_in + 1:])


def _fnet_dft(seq_len):
    ang_l = _dft_angles(seq_len, seq_len)
    ang_w = _dft_angles(FN_GROUP_W, FN_GROUP_W)
    scale = 1.0 / math.sqrt(seq_len * FN_GROUP_W)
    return ((jnp.cos(ang_l) * scale).astype(BF16), (jnp.sin(ang_l) * scale).astype(BF16),
            jnp.cos(ang_w).astype(BF16), jnp.sin(ang_w).astype(BF16))


def _fnet_kernel(z_ref, cl_ref, sl_ref, cw_ref, sw_ref, o_ref):
    z = z_ref[...]
    gc, gs = [], []
    for j in range(z.shape[1] // FN_GROUP_W):
        zj = z[:, j * FN_GROUP_W:(j + 1) * FN_GROUP_W]
        gc.append(_bdot(zj, cw_ref[...]).astype(BF16))
        gs.append(_bdot(zj, sw_ref[...]).astype(BF16))
    gc = jnp.concatenate(gc, axis=-1)
    gs = jnp.concatenate(gs, axis=-1)
    o_ref[...] = (_bdot(cl_ref[...], gc) - _bdot(sl_ref[...], gs)).astype(o_ref.dtype)


def _fnet(zm, dft, *, seq_len, n_seq, row0, out_rows, prev_out=None, tn=256):
    cl, sl, cw, sw = dft
    nb = FN_W // tn
    rb0 = row0 // seq_len
    c0 = ZM_FN // tn
    const = lambda c, b: (0, 0)
    in_specs = [
        pl.BlockSpec((seq_len, tn), lambda c, b: (rb0 + b, c0 + c)),
        _resident((seq_len, seq_len), const),
        _resident((seq_len, seq_len), const),
        pl.BlockSpec((FN_GROUP_W, FN_GROUP_W), const),
        pl.BlockSpec((FN_GROUP_W, FN_GROUP_W), const),
    ]
    args = [zm, cl, sl, cw, sw]
    kernel = _fnet_kernel
    aliases = {}
    if prev_out is not None:
        in_specs.append(pl.BlockSpec(memory_space=pl.ANY))
        args.append(prev_out)
        aliases = {len(args) - 1: 0}
        kernel = functools.partial(_drop_last_input, _fnet_kernel, 5)
    return pl.pallas_call(
        kernel,
        grid=(nb, n_seq),
        in_specs=in_specs,
        out_specs=pl.BlockSpec((seq_len, tn), lambda c, b: (rb0 + b, c)),
        out_shape=jax.ShapeDtypeStruct((out_rows, FN_W), BF16),
        input_output_aliases=aliases,
        compiler_params=_cparams(2),
        name="fnet",
    )(*args)


def _hgrn_tables(reverse):
    c = HG_CHUNK
    tri = np.zeros((c, c), np.float32)
    for t in range(c):
        if reverse:
            tri[t, t:] = 1.0
        else:
            tri[t, :t + 1] = 1.0
    mats = [tri]
    masks = []
    for lvl in range(HG_LEVELS):
        half = (c // 2) >> lvl
        sel = np.zeros((c, c), np.float32)
        mask = np.zeros((c, c), np.float32)
        for t in range(c):
            start = (t // (2 * half)) * 2 * half
            mid = start + half
            sel[t, mid if reverse else mid - 1] = 1.0
            for s in range(start, start + 2 * half):
                if reverse and t < mid <= s:
                    mask[t, s] = 1.0
                if (not reverse) and s < mid <= t:
                    mask[t, s] = 1.0
        mats.append(tri - sel @ tri)
        masks.append(mask)
    half = c // 2
    sel = np.zeros((c, c), np.float32)
    mask = np.zeros((c, c), np.float32)
    for t in range(c):
        start = (t // half) * half
        sel[t, start + half // 2] = 1.0
        for s in range(start, start + half):
            if (s >= t) if reverse else (s <= t):
                mask[t, s] = 1.0
    mats.append(tri - sel @ tri)
    masks.append(mask)
    mats = [mats[0], mats[1], mats[-1]] + mats[2:-1]
    return np.concatenate(mats, axis=0), np.stack(masks, axis=0)


HG_BOUNDED_BLOCKS = 3
HG_MID_BLOCK = 2


def _hgrn_level_block(lvl):
    return 1 if lvl == 0 else HG_BOUNDED_BLOCKS + lvl - 1


def _hgrn_decays(zf_ref, lb_ref, tab_ref, k_scr, p_scr, dec_scr):
    lb = lb_ref[...]
    f = lb + (1.0 - lb) * jax.nn.sigmoid(zf_ref[...])
    lf = jnp.log2(f)
    k_scr[...] = (1.0 - f).astype(BF16)
    p1 = lf.astype(BF16)
    r1 = lf - p1.astype(F32)
    p2 = r1.astype(BF16)
    p3 = (r1 - p2.astype(F32)).astype(BF16)
    p_scr[...] = jnp.concatenate([p1, p2, p3], axis=0)
    n = HG_BOUNDED_BLOCKS * HG_CHUNK
    dec_scr[0:n, :] = _bdot(tab_ref[0:n, :], p_scr[...])


def _hgrn_more_decays(tab_ref, p_scr, dec_scr):
    n = HG_BOUNDED_BLOCKS * HG_CHUNK
    dec_scr[n:, :] = _bdot(tab_ref[n:, :], p_scr[...])


def _nt_dot(a, b):
    return lax.dot_general(a, b, (((1,), (1,)), ((), ())), preferred_element_type=F32)


def _hgrn_direction(q_ref, v_ref, k_scr, dec_scr, mask_ref, st_scr, o_ref, end_row, bounded):
    c = HG_CHUNK
    heads = lambda ref, r0: jnp.stack(
        [ref[r0:r0 + c, h * HG_DK:(h + 1) * HG_DK] for h in range(HG_HEADS)], axis=0)
    bmm = lambda eq, a, b: jnp.einsum(eq, a, b, preferred_element_type=F32)
    q = heads(q_ref, 0)
    k = heads(k_scr, 0)
    v = heads(v_ref, 0)
    b = heads(dec_scr, 0)
    st = st_scr[...]
    o = bmm('htk,hkv->htv', q * jnp.exp2(b).astype(BF16), st.astype(BF16))
    level = lambda l: jnp.exp2(-jnp.abs(heads(dec_scr, _hgrn_level_block(l) * c))).astype(BF16)
    if bounded:
        e = level(0)
        att = jnp.where(mask_ref[0] > 0.5, bmm('htk,hsk->hts', q * e, k * e), 0.0)
        d = heads(dec_scr, HG_MID_BLOCK * c)
        inner = bmm('htk,hsk->hts', q * jnp.exp2(d).astype(BF16), k * jnp.exp2(-d).astype(BF16))
        att += jnp.where(mask_ref[HG_LEVELS] > 0.5, inner, 0.0)
    else:
        diag = jnp.sum(q.astype(F32) * k.astype(F32), axis=-1, keepdims=True)
        o += diag * v.astype(F32)
        att = jnp.zeros((HG_HEADS, c, c), F32)
        for lvl in range(HG_LEVELS):
            e = level(lvl)
            att += bmm('htk,hsk->hts', q * e, k * e) * mask_ref[lvl]
    o += bmm('hts,hsv->htv', att.astype(BF16), v)
    for h in range(HG_HEADS):
        o_ref[:, h * HG_DK:(h + 1) * HG_DK] = o[h]
    b_end = b[:, end_row:end_row + 1, :]
    upd = bmm('htk,htv->hkv', k * jnp.exp2(b_end - b).astype(BF16), v)
    decay = jnp.swapaxes(jnp.broadcast_to(jnp.exp2(b_end), (HG_HEADS, HG_DK, HG_DK)), 1, 2)
    st_scr[...] = decay * st + upd


HG_MAX_LOG2_SPAN = 100.0
HG_MAX_ABS_Q = 1e6


def _hgrn_kernel(qf_ref, vf_ref, zff_ref, qb_ref, vb_ref, zfb_ref, lbf_ref, lbb_ref,
                 tabf_ref, maskf_ref, tabb_ref, maskb_ref, s0f_ref, s0b_ref,
                 of_ref, ob_ref, sf_ref, sb_ref, stf_scr, stb_scr, kf_scr, kb_scr,
                 pf_scr, pb_scr, decf_scr, decb_scr):
    ci = pl.program_id(1)
    c = HG_CHUNK

    @pl.when(ci == 0)
    def _():
        stf_scr[...] = s0f_ref[0]
        stb_scr[...] = s0b_ref[0]

    _hgrn_decays(zff_ref, lbf_ref, tabf_ref, kf_scr, pf_scr, decf_scr)
    _hgrn_decays(zfb_ref, lbb_ref, tabb_ref, kb_scr, pb_scr, decb_scr)
    mid = slice(HG_MID_BLOCK * c, (HG_MID_BLOCK + 1) * c)
    span = jnp.maximum(jnp.max(jnp.abs(decf_scr[mid, :])), jnp.max(jnp.abs(decb_scr[mid, :])))
    q_abs = jnp.maximum(jnp.max(jnp.abs(qf_ref[...].astype(F32))),
                        jnp.max(jnp.abs(qb_ref[...].astype(F32))))
    bounded = (span < HG_MAX_LOG2_SPAN) & (q_abs < HG_MAX_ABS_Q)

    def run(flag):
        _hgrn_direction(qf_ref, vf_ref, kf_scr, decf_scr, maskf_ref, stf_scr, of_ref, c - 1, flag)
        _hgrn_direction(qb_ref, vb_ref, kb_scr, decb_scr, maskb_ref, stb_scr, ob_ref, 0, flag)

    @pl.when(bounded)
    def _():
        run(True)

    @pl.when(jnp.logical_not(bounded))
    def _():
        _hgrn_more_decays(tabf_ref, pf_scr, decf_scr)
        _hgrn_more_decays(tabb_ref, pb_scr, decb_scr)
        run(False)

    @pl.when(ci == pl.num_programs(1) - 1)
    def _():
        sf_ref[0] = stf_scr[...]
        sb_ref[0] = stb_scr[...]


def _hgrn(zm, zf, lb_f, lb_b, s0_f, s0_b, *, seq_len, n_seq, row0, out_rows,
          prev_of=None, prev_ob=None):
    c = HG_CHUNK
    nc = seq_len // c
    rb0 = row0 // c
    w = HG_W
    tab_f, mask_f = _hgrn_tables(False)
    tab_b, mask_b = _hgrn_tables(True)
    fwd = lambda col: (lambda b, i: (rb0 + b * nc + i, col))
    bwd = lambda col: (lambda b, i: (rb0 + b * nc + nc - 1 - i, col))
    const2 = lambda b, i: (0, 0)
    const3 = lambda b, i: (0, 0, 0)
    state = lambda b, i: (b, 0, 0, 0)
    nt = (2 + HG_LEVELS) * c
    in_specs = [
        pl.BlockSpec((c, w), fwd(ZM_Q // w)),
        pl.BlockSpec((c, w), fwd(ZM_I // w)),
        pl.BlockSpec((c, w), fwd(0)),
        pl.BlockSpec((c, w), bwd(ZM_Q // w)),
        pl.BlockSpec((c, w), bwd(ZM_I // w)),
        pl.BlockSpec((c, w), bwd(1)),
        pl.BlockSpec((1, w), const2),
        pl.BlockSpec((1, w), const2),
        pl.BlockSpec((nt, 3 * c), const2),
        pl.BlockSpec((HG_LEVELS + 1, c, c), const3),
        pl.BlockSpec((nt, 3 * c), const2),
        pl.BlockSpec((HG_LEVELS + 1, c, c), const3),
        pl.BlockSpec((1, HG_HEADS, HG_DK, HG_DK), state),
        pl.BlockSpec((1, HG_HEADS, HG_DK, HG_DK), state),
    ]
    args = [zm, zm, zf, zm, zm, zf, lb_f.reshape(1, w), lb_b.reshape(1, w),
            jnp.asarray(np.tile(tab_f, (1, 3)), BF16), jnp.asarray(mask_f),
            jnp.asarray(np.tile(tab_b, (1, 3)), BF16), jnp.asarray(mask_b), s0_f, s0_b]
    kernel = _hgrn_kernel
    aliases = {}
    if prev_of is not None:
        in_specs += [pl.BlockSpec(memory_space=pl.ANY), pl.BlockSpec(memory_space=pl.ANY)]
        args += [prev_of, prev_ob]
        aliases = {14: 0, 15: 1}
        kernel = functools.partial(_hgrn_alias_kernel, 14)
    st_shape = jax.ShapeDtypeStruct((n_seq, HG_HEADS, HG_DK, HG_DK), F32)
    return pl.pallas_call(
        kernel,
        grid=(n_seq, nc),
        in_specs=in_specs,
        out_specs=[
            pl.BlockSpec((c, w), fwd(0)),
            pl.BlockSpec((c, w), bwd(0)),
            pl.BlockSpec((1, HG_HEADS, HG_DK, HG_DK), state),
            pl.BlockSpec((1, HG_HEADS, HG_DK, HG_DK), state),
        ],
        out_shape=[jax.ShapeDtypeStruct((out_rows, w), F32),
                   jax.ShapeDtypeStruct((out_rows, w), F32), st_shape, st_shape],
        scratch_shapes=[pltpu.VMEM((HG_HEADS, HG_DK, HG_DK), F32),
                        pltpu.VMEM((HG_HEADS, HG_DK, HG_DK), F32),
                        pltpu.VMEM((c, w), BF16), pltpu.VMEM((c, w), BF16),
                        pltpu.VMEM((3 * c, w), BF16), pltpu.VMEM((3 * c, w), BF16),
                        pltpu.VMEM((nt, w), F32), pltpu.VMEM((nt, w), F32)],
        input_output_aliases=aliases,
        compiler_params=_cparams(2),
        name="hgrn2",
    )(*args)


def _hgrn_alias_kernel(n_in, *refs):
    return _hgrn_kernel(*refs[:n_in], *refs[n_in + 2:])


def kernel(x, c, ctx, c_ctx, w_mod, b_mod, norm1_w, norm2_w, w_in, hy_conv_w, hy_conv_b,
           hy_f_w1, hy_f_b1, hy_f_w2, hy_f_b2, hy_f_w3, hy_f_freq, hy_decay, hy_skip,
           hg_lower, hg_norm_w, w_br_hy, w_br_fn, w_br_hg, w_out,
           ffn_w1, ffn_w3, ffn_w2, moe_router, moe_w1, moe_w3, moe_w2, final_norm_w):
    batch, seq, d = x.shape
    l_ctx = ctx.shape[1]
    depth = w_in.shape[0]
    n_lat = batch * seq
    n_all = n_lat + batch * l_ctx

    lb_all = jnp.cumsum(jax.nn.softmax(hg_lower.astype(F32), axis=1), axis=1)
    lb_all = lb_all - lb_all[:, :1]

    cc = jnp.zeros((8, d), F32).at[:batch].set(c).at[batch].set(c_ctx)
    mods = _modulation(cc, w_mod, b_mod).reshape(depth, 8, 6, d)

    w1p = jnp.zeros((depth, LANE, HY_FILTER_HIDDEN), F32).at[:, :HY_POS_DIM].set(hy_f_w1)
    dft_lat = _hyena_dft(seq)
    dft_ctx = _hyena_dft(l_ctx)
    kf_lat = _hyena_filters(seq, w1p, hy_f_b1, hy_f_w2, hy_f_b2, hy_f_w3, hy_f_freq, hy_decay,
                            dft_lat[0], dft_lat[1])
    kf_ctx = _hyena_filters(l_ctx, w1p, hy_f_b1, hy_f_w2, hy_f_b2, hy_f_w3, hy_f_freq, hy_decay,
                            dft_ctx[0], dft_ctx[1])
    fn_lat = _fnet_dft(seq)
    fn_ctx = _fnet_dft(l_ctx)

    rows_x = jnp.concatenate([x.reshape(n_lat, d), ctx.reshape(batch * l_ctx, d)], axis=0)
    zero_state = jnp.zeros((batch, HG_HEADS, HG_DK, HG_DK), F32)
    tile_cap = math.gcd(seq, batch * l_ctx)
    tok = dict(seq=seq, batch=batch)
    tile = lambda t: dict(tm=min(t, tile_cap))
    lat = dict(seq_len=seq, n_seq=batch, row0=0, out_rows=n_all)
    cx = dict(seq_len=l_ctx, n_seq=batch, row0=n_lat, out_rows=n_all)

    for l in range(depth):
        last = l == depth - 1
        rows = n_lat if last else n_all
        m = mods[l]

        h = _norm_rows(rows_x, norm1_w[l], m, **tok, **tile(512))
        zm, zf = _in_proj(h, w_in, l, **tile(1024))

        of, ob, s_f, s_b = _hgrn(zm, zf, lb_all[0, l], lb_all[1, l], zero_state, zero_state, **cx)
        of, ob, _, _ = _hgrn(zm, zf, lb_all[0, l], lb_all[1, l], s_f, s_b, prev_of=of,
                             prev_ob=ob, **lat)

        hy_args = (hy_conv_w[l], hy_conv_b[l])
        mix = dict(lat, out_rows=rows)
        u = _hyena_longconv(zm, 0, zm, 1, *hy_args, kf_lat, l, 0, hy_skip[l], dft_lat, **mix)
        y_hy = _hyena_longconv(u, None, zm, 2, *hy_args, kf_lat, l, 1, hy_skip[l], dft_lat, **mix)
        y_fn = _fnet(zm, fn_lat, **mix)
        if not last:
            u = _hyena_longconv(zm, 0, zm, 1, *hy_args, kf_ctx, l, 0, hy_skip[l], dft_ctx,
                                prev_out=u, **cx)
            y_hy = _hyena_longconv(u, None, zm, 2, *hy_args, kf_ctx, l, 1, hy_skip[l], dft_ctx,
                                   prev_out=y_hy, **cx)
            y_fn = _fnet(zm, fn_ctx, prev_out=y_fn, **cx)

        rows_x = _merge_out(y_hy, y_fn, of, ob, zm, rows_x, hg_norm_w[l], m,
                            w_br_hy[l].astype(BF16), w_br_fn[l].astype(BF16),
                            w_br_hg[l].astype(BF16), w_out[l].astype(BF16), rows=rows,
                            **tok, **tile(256))

        if l % 2 == 0:
            i = l // 2
            rows_x = _ffn_dense(rows_x, norm2_w[l], m, ffn_w1[i].astype(BF16),
                                ffn_w3[i].astype(BF16), ffn_w2[i].astype(BF16), rows=rows,
                                **tok, **tile(512))
        else:
            i = l // 2
            rows_x = _moe_top2(rows_x, norm2_w[l], m, moe_router[i], moe_w1, moe_w3, moe_w2, i,
                               rows=rows, **tok, **tile(512))

    out = _final_norm(rows_x, final_norm_w, rows=n_lat, **tile(512))
    return out.reshape(batch, seq, d)
```

```python
import functools
import math

import numpy as np
import jax
import jax.numpy as jnp
from jax import lax
from jax.experimental import pallas as pl
from jax.experimental.pallas import tpu as pltpu

F32 = jnp.float32
BF16 = jnp.bfloat16
NORM_EPS = 1e-6

D_MODEL = 2048
DEPTH = 4
HY_W = 1024
HY_BANDS = 16
HY_POS_DIM = 2 * HY_BANDS + 1
HY_FILTER_HIDDEN = 64
FN_W = 1024
FN_GROUP_W = 128
HG_W = 1024
HG_HEADS = 8
HG_DK = 128
HG_CHUNK = 64
HG_LEVELS = 6
N_EXPERTS = 8
LANE = 128

HG_FF = 3 * HY_W + FN_W + 2 * HG_W
HG_I = HG_FF + 2 * HG_W
ZM_HY = 0
ZM_FN = ZM_HY + 3 * HY_W
ZM_Q = ZM_FN + FN_W
ZM_G = ZM_Q + HG_W
ZM_I = HG_I
ZM_GATE = HG_I + HG_W
ZM_COLS = ZM_GATE + 3 * D_MODEL

VMEM_LIMIT = 56 * 1024 * 1024


def _cparams(n_axes, vmem=VMEM_LIMIT):
    return pltpu.CompilerParams(dimension_semantics=("arbitrary",) * n_axes, vmem_limit_bytes=vmem)


def _resident(shape, index_map):
    return pl.BlockSpec(shape, index_map, pipeline_mode=pl.Buffered(1))


def _silu(v):
    return v * jax.nn.sigmoid(v)


def _bdot(a, b):
    return jnp.dot(a, b, preferred_element_type=F32)


def _mod_kernel(c_ref, w_ref, b_ref, o_ref):
    c = _silu(c_ref[...])
    o_ref[0] = jnp.dot(c, w_ref[0], preferred_element_type=F32,
                       precision=lax.Precision.HIGHEST) + b_ref[0]


def _modulation(cc, w_mod, b_mod):
    depth, d, n6 = w_mod.shape
    tn = 2048
    return pl.pallas_call(
        _mod_kernel,
        grid=(depth, n6 // tn),
        in_specs=[
            pl.BlockSpec((8, d), lambda l, n: (0, 0)),
            pl.BlockSpec((1, d, tn), lambda l, n: (l, 0, n)),
            pl.BlockSpec((1, 1, tn), lambda l, n: (l, 0, n)),
        ],
        out_specs=pl.BlockSpec((1, 8, tn), lambda l, n: (l, 0, n)),
        out_shape=jax.ShapeDtypeStruct((depth, 8, n6), F32),
        compiler_params=_cparams(2),
        name="modulation",
    )(cc, w_mod, b_mod.reshape(depth, 1, n6))


def _norm_mod(x, nw, shift, scale):
    ms = jnp.mean(x * x, axis=-1, keepdims=True)
    return (x * lax.rsqrt(ms + NORM_EPS) * nw) * (1 + scale) + shift


def _group_of(row_block, tm, seq, batch):
    return jnp.minimum((row_block * tm) // seq, batch)


def _norm_rows_kernel(x_ref, nw_ref, mod_ref, o_ref):
    h = _norm_mod(x_ref[...], nw_ref[...], mod_ref[0, 0:1, :], mod_ref[0, 1:2, :])
    o_ref[...] = h.astype(o_ref.dtype)


def _norm_rows(x, nw, mods, *, seq, batch, tm=512):
    rows, d = x.shape
    return pl.pallas_call(
        _norm_rows_kernel,
        grid=(rows // tm,),
        in_specs=[
            pl.BlockSpec((tm, d), lambda m: (m, 0)),
            pl.BlockSpec((1, d), lambda m: (0, 0)),
            pl.BlockSpec((1, 6, d), lambda m: (_group_of(m, tm, seq, batch), 0, 0)),
        ],
        out_specs=pl.BlockSpec((tm, d), lambda m: (m, 0)),
        out_shape=jax.ShapeDtypeStruct((rows, d), BF16),
        compiler_params=_cparams(1),
        name="norm_rows",
    )(x, nw.reshape(1, d), mods)


def _inproj_kernel(h_ref, w_ref, z_ref, zf_ref, w_scr, *, n_main):
    @pl.when(pl.program_id(1) == 0)
    def _():
        w_scr[...] = w_ref[0].astype(BF16)

    r = _bdot(h_ref[...], w_scr[...])

    @pl.when(pl.program_id(0) < n_main)
    def _():
        z_ref[...] = r.astype(z_ref.dtype)

    @pl.when(pl.program_id(0) >= n_main)
    def _():
        zf_ref[...] = r


def _in_proj(h, w_in, layer, *, tm=1024, tn=1024):
    rows, d = h.shape
    n_cols = w_in.shape[2]
    nb = n_cols // tn
    f0 = HG_FF // tn
    nf = 2 * HG_W // tn
    n_main = nb - nf
    nm = rows // tm
    col = lambda j: jnp.where(j < f0, j, jnp.where(j < n_main, j + nf, j - n_main + f0))
    z_idx = lambda j, m: (jnp.where(j < n_main, m, nm - 1), col(jnp.minimum(j, n_main - 1)))
    zf_idx = lambda j, m: (jnp.where(j < n_main, 0, m), jnp.maximum(j - n_main, 0))
    return pl.pallas_call(
        functools.partial(_inproj_kernel, n_main=n_main),
        grid=(nb, nm),
        in_specs=[
            pl.BlockSpec((tm, d), lambda j, m: (m, 0)),
            pl.BlockSpec((1, d, tn), lambda j, m: (layer, 0, col(j))),
        ],
        out_specs=[pl.BlockSpec((tm, tn), z_idx), pl.BlockSpec((tm, tn), zf_idx)],
        out_shape=[jax.ShapeDtypeStruct((rows, n_cols), BF16),
                   jax.ShapeDtypeStruct((rows, nf * tn), F32)],
        scratch_shapes=[pltpu.VMEM((d, tn), BF16)],
        compiler_params=_cparams(2),
        name="in_proj",
    )(h, w_in)


def _merge_kernel(yhy_ref, yfn_ref, of_ref, ob_ref, g_ref, ga0_ref, ga1_ref, gb0_ref, gb1_ref,
                  gc0_ref, gc1_ref, x_ref, hnw_ref, mod_ref,
                  wbhy_ref, wbfn_ref, wbhg_ref, wo_ref, o_ref):
    d = x_ref.shape[1]
    ohg = of_ref[...] + ob_ref[...]
    nw = hnw_ref[...]
    heads = []
    for h in range(ohg.shape[1] // HG_DK):
        oh = ohg[:, h * HG_DK:(h + 1) * HG_DK]
        ms = jnp.mean(oh * oh, axis=-1, keepdims=True)
        heads.append(oh * lax.rsqrt(ms + NORM_EPS) * nw)
    y_hg = jnp.concatenate(heads, axis=-1) * _silu(g_ref[...].astype(F32))
    gate = lambda lo, hi: jax.nn.sigmoid(
        jnp.concatenate([lo[...], hi[...]], axis=-1).astype(F32))
    merged = gate(ga0_ref, ga1_ref) * _bdot(yhy_ref[...], wbhy_ref[...])
    merged += gate(gb0_ref, gb1_ref) * _bdot(yfn_ref[...], wbfn_ref[...])
    merged += gate(gc0_ref, gc1_ref) * _bdot(y_hg.astype(BF16), wbhg_ref[...])
    out = _bdot(merged.astype(BF16), wo_ref[...])
    o_ref[...] = x_ref[...] + mod_ref[0, 2:3, :] * out


def _merge_out(y_hy, y_fn, o_f, o_b, zm, x, hg_nw, mods, wb_hy, wb_fn, wb_hg, w_o,
               *, rows, seq, batch, tm=256):
    d = x.shape[1]
    w = y_hy.shape[1]
    tm = min(tm, rows)
    row = lambda m: (m, 0)
    const = lambda m: (0, 0)
    gate0 = ZM_GATE // (d // 2)
    return pl.pallas_call(
        _merge_kernel,
        grid=(rows // tm,),
        in_specs=[
            pl.BlockSpec((tm, w), row),
            pl.BlockSpec((tm, w), row),
            pl.BlockSpec((tm, w), row),
            pl.BlockSpec((tm, w), row),
            pl.BlockSpec((tm, w), lambda m: (m, ZM_G // w)),
            *[pl.BlockSpec((tm, d // 2), functools.partial(lambda i, m: (m, i), gate0 + i))
              for i in range(6)],
            pl.BlockSpec((tm, d), row),
            pl.BlockSpec((1, HG_DK), const),
            pl.BlockSpec((1, 6, d), lambda m: (_group_of(m, tm, seq, batch), 0, 0)),
            _resident((w, d), const),
            _resident((w, d), const),
            _resident((w, d), const),
            _resident((d, d), const),
        ],
        out_specs=pl.BlockSpec((tm, d), row),
        out_shape=jax.ShapeDtypeStruct((rows, d), F32),
        compiler_params=_cparams(1),
        name="merge_out",
    )(y_hy, y_fn, o_f, o_b, zm, *([zm] * 6), x, hg_nw.reshape(1, HG_DK), mods,
      wb_hy, wb_fn, wb_hg, w_o)


def _ffn_kernel(x_ref, nw_ref, mod_ref, w1_ref, w3_ref, w2_ref, o_ref, h_scr, acc_scr):
    f = pl.program_id(1)

    @pl.when(f == 0)
    def _():
        h = _norm_mod(x_ref[...], nw_ref[...], mod_ref[0, 3:4, :], mod_ref[0, 4:5, :])
        h_scr[...] = h.astype(BF16)
        acc_scr[...] = jnp.zeros_like(acc_scr)

    h = h_scr[...]
    act = _silu(_bdot(h, w1_ref[...])) * _bdot(h, w3_ref[...])
    acc_scr[...] += _bdot(act.astype(BF16), w2_ref[...])

    @pl.when(f == pl.num_programs(1) - 1)
    def _():
        o_ref[...] = x_ref[...] + mod_ref[0, 5:6, :] * acc_scr[...]


def _ffn_dense(x, nw, mods, w1, w3, w2, *, rows, seq, batch, tm=512, tf=512):
    d = x.shape[1]
    ff = w1.shape[1]
    tm = min(tm, rows)
    return pl.pallas_call(
        _ffn_kernel,
        grid=(rows // tm, ff // tf),
        in_specs=[
            pl.BlockSpec((tm, d), lambda m, f: (m, 0)),
            pl.BlockSpec((1, d), lambda m, f: (0, 0)),
            pl.BlockSpec((1, 6, d), lambda m, f: (_group_of(m, tm, seq, batch), 0, 0)),
            pl.BlockSpec((d, tf), lambda m, f: (0, f)),
            pl.BlockSpec((d, tf), lambda m, f: (0, f)),
            pl.BlockSpec((tf, d), lambda m, f: (f, 0)),
        ],
        out_specs=pl.BlockSpec((tm, d), lambda m, f: (m, 0)),
        out_shape=jax.ShapeDtypeStruct((rows, d), F32),
        scratch_shapes=[pltpu.VMEM((tm, d), BF16), pltpu.VMEM((tm, d), F32)],
        compiler_params=_cparams(2),
        name="ffn_dense",
    )(x, nw.reshape(1, d), mods, w1, w3, w2)


def _top2(logits, n_experts):
    lane = lax.broadcasted_iota(jnp.int32, logits.shape, 1).astype(F32)
    neg = jnp.float32(-jnp.inf)
    lg = jnp.where(lane < n_experts, logits, neg)
    m1 = jnp.max(lg, axis=-1, keepdims=True)
    i1 = jnp.min(jnp.where(lg == m1, lane, float(LANE)), axis=-1, keepdims=True)
    lg2 = jnp.where(lane == i1, neg, lg)
    m2 = jnp.max(lg2, axis=-1, keepdims=True)
    i2 = jnp.min(jnp.where(lg2 == m2, lane, float(LANE)), axis=-1, keepdims=True)
    e2 = jnp.exp(m2 - m1)
    return i1, i2, 1.0 / (1.0 + e2), e2 / (1.0 + e2)


def _route_kernel(x_ref, nw_ref, mod_ref, r_ref, h_ref, route_ref):
    h = _norm_mod(x_ref[...], nw_ref[...], mod_ref[0, 3:4, :], mod_ref[0, 4:5, :])
    logits = jnp.dot(h, r_ref[...], preferred_element_type=F32, precision=lax.Precision.HIGHEST)
    i1, i2, p1, p2 = _top2(logits, N_EXPERTS)
    lane = lax.broadcasted_iota(jnp.int32, logits.shape, 1)
    route = jnp.where(lane == 0, i1, jnp.where(lane == 1, i2,
                      jnp.where(lane == 2, p1, jnp.where(lane == 3, p2, 0.0))))
    h_ref[...] = h
    route_ref[...] = route


def _moe_route(x, nw, mods, router, *, rows, seq, batch, tm=512):
    d = x.shape[1]
    rpad = jnp.zeros((d, LANE), F32).at[:, :router.shape[1]].set(router)
    return pl.pallas_call(
        _route_kernel,
        grid=(rows // tm,),
        in_specs=[
            pl.BlockSpec((tm, d), lambda m: (m, 0)),
            pl.BlockSpec((1, d), lambda m: (0, 0)),
            pl.BlockSpec((1, 6, d), lambda m: (_group_of(m, tm, seq, batch), 0, 0)),
            pl.BlockSpec((d, LANE), lambda m: (0, 0)),
        ],
        out_specs=[pl.BlockSpec((tm, d), lambda m: (m, 0)),
                   pl.BlockSpec((tm, LANE), lambda m: (m, 0))],
        out_shape=[jax.ShapeDtypeStruct((rows, d), F32), jax.ShapeDtypeStruct((rows, LANE), F32)],
        compiler_params=_cparams(1),
        name="moe_route",
    )(x, nw.reshape(1, d), mods, rpad)


def _moe_plan(route, n_experts, tile):
    rows = route.shape[0]
    e = route[:, :2].astype(jnp.int32).reshape(-1)
    onehot = (e[:, None] == jnp.arange(n_experts, dtype=jnp.int32)[None, :]).astype(jnp.int32)
    before = jnp.cumsum(onehot, axis=0) - onehot
    rank = jnp.sum(before * onehot, axis=1)
    counts = jnp.sum(onehot, axis=0)
    padded = ((counts + tile - 1) // tile) * tile
    ends = jnp.cumsum(padded)
    starts = ends - padded
    pos = jnp.sum(starts[None, :] * onehot, axis=1) + rank
    n_tiles = -(-2 * rows // tile) + n_experts
    src = jnp.zeros((n_tiles * tile,), jnp.int32).at[pos].set(
        jnp.arange(2 * rows, dtype=jnp.int32) // 2)
    tile_start = jnp.arange(n_tiles, dtype=jnp.int32) * tile
    tile_expert = jnp.sum((tile_start[:, None] >= ends[None, :]).astype(jnp.int32), axis=1)
    tile_expert = jnp.minimum(tile_expert, n_experts - 1)
    n_used = (ends[-1] // tile).reshape(1)
    return pos, src, tile_expert, n_used


def _row_copy(src_hbm, src_row, dst, dst_row, sem):
    return pltpu.make_async_copy(src_hbm.at[pl.ds(src_row, 1)], dst.at[pl.ds(dst_row, 1)], sem)


def _gather_rows(idx_ref, base, stride, src_hbm, dst, sem, *, wait):
    if wait:
        pltpu.make_async_copy(src_hbm.at[pl.ds(0, dst.shape[0])], dst, sem).wait()
        return

    def body(r, carry):
        _row_copy(src_hbm, idx_ref[base + stride * r], dst, r, sem).start()
        return carry

    lax.fori_loop(0, dst.shape[0], body, 0, unroll=8)


def _moe_ffn_kernel(te_ref, nu_ref, src_ref, h_hbm, w1_ref, w3_ref, w2_ref, o_ref,
                    rows_scr, h_scr, sems):
    t = pl.program_id(0)
    f = pl.program_id(1)
    last = pl.num_programs(1) - 1
    tile = o_ref.shape[0]
    n_used = nu_ref[0]
    used = t < n_used
    slot = t % 2

    def gather(tile_idx, buf, wait):
        _gather_rows(src_ref, tile_idx * tile, 1, h_hbm, rows_scr.at[buf], sems.at[buf], wait=wait)

    @pl.when(used & (f == 0) & (t == 0))
    def _():
        gather(t, slot, False)

    @pl.when(used & (f == 0))
    def _():
        gather(t, slot, True)
        h_scr[...] = rows_scr[slot].astype(BF16)
        o_ref[...] = jnp.zeros_like(o_ref)

    @pl.when((f == 0) & (t + 1 < n_used))
    def _():
        gather(t + 1, 1 - slot, False)

    @pl.when(used)
    def _():
        h = h_scr[...]
        act = _silu(_bdot(h, w1_ref[0, 0].astype(BF16))) * _bdot(h, w3_ref[0, 0].astype(BF16))
        o_ref[...] += _bdot(act.astype(BF16), w2_ref[0, 0].astype(BF16))

    @pl.when(jnp.logical_not(used) & (f == last))
    def _():
        o_ref[...] = jnp.zeros_like(o_ref)


def _moe_ffn(h, src, tile_expert, n_used, w1, w3, w2, layer, *, tile, tf=256):
    d = h.shape[1]
    n = src.shape[0]
    ff = w1.shape[3]
    nf = ff // tf
    fsel = lambda t, f, nu: jnp.where(t < nu[0], f, nf - 1)
    up = lambda t, f, te, nu, sr: (layer, te[t], 0, fsel(t, f, nu))
    down = lambda t, f, te, nu, sr: (layer, te[t], fsel(t, f, nu), 0)
    return pl.pallas_call(
        _moe_ffn_kernel,
        grid_spec=pltpu.PrefetchScalarGridSpec(
            num_scalar_prefetch=3,
            grid=(n // tile, nf),
            in_specs=[
                pl.BlockSpec(memory_space=pl.ANY),
                pl.BlockSpec((1, 1, d, tf), up),
                pl.BlockSpec((1, 1, d, tf), up),
                pl.BlockSpec((1, 1, tf, d), down),
            ],
            out_specs=pl.BlockSpec((tile, d), lambda t, f, te, nu, sr: (t, 0)),
            scratch_shapes=[pltpu.VMEM((2, tile, d), F32), pltpu.VMEM((tile, d), BF16),
                            pltpu.SemaphoreType.DMA((2,))],
        ),
        out_shape=jax.ShapeDtypeStruct((n, d), F32),
        compiler_params=_cparams(2),
        name="moe_ffn",
    )(tile_expert, n_used, src, h, w1, w3, w2)


def _combine_kernel(pos_ref, ys_hbm, x_ref, route_ref, mod_ref, o_ref, y1_scr, y2_scr, sem):
    base = 2 * pl.program_id(0) * x_ref.shape[0]
    _gather_rows(pos_ref, base, 2, ys_hbm, y1_scr, sem, wait=False)
    _gather_rows(pos_ref, base + 1, 2, ys_hbm, y2_scr, sem, wait=False)
    _gather_rows(pos_ref, base, 2, ys_hbm, y1_scr, sem, wait=True)
    _gather_rows(pos_ref, base + 1, 2, ys_hbm, y2_scr, sem, wait=True)
    mix = route_ref[:, 2:3] * y1_scr[...] + route_ref[:, 3:4] * y2_scr[...]
    o_ref[...] = x_ref[...] + mod_ref[0, 5:6, :] * mix


def _moe_combine(x, mods, route, ys, pos, *, rows, seq, batch, tm=256):
    d = x.shape[1]
    return pl.pallas_call(
        _combine_kernel,
        grid_spec=pltpu.PrefetchScalarGridSpec(
            num_scalar_prefetch=1,
            grid=(rows // tm,),
            in_specs=[
                pl.BlockSpec(memory_space=pl.ANY),
                pl.BlockSpec((tm, d), lambda m, pos: (m, 0)),
                pl.BlockSpec((tm, LANE), lambda m, pos: (m, 0)),
                pl.BlockSpec((1, 6, d), lambda m, pos: (_group_of(m, tm, seq, batch), 0, 0)),
            ],
            out_specs=pl.BlockSpec((tm, d), lambda m, pos: (m, 0)),
            scratch_shapes=[pltpu.VMEM((tm, d), F32), pltpu.VMEM((tm, d), F32),
                            pltpu.SemaphoreType.DMA(())],
        ),
        out_shape=jax.ShapeDtypeStruct((rows, d), F32),
        compiler_params=_cparams(1),
        name="moe_combine",
    )(pos, ys, x, route, mods)


def _moe_top2(x, nw, mods, router, w1, w3, w2, layer, *, rows, seq, batch, tm, tile=768):
    tok = dict(rows=rows, seq=seq, batch=batch)
    h, route = _moe_route(x, nw, mods, router, tm=tm, **tok)
    pos, src, tile_expert, n_used = _moe_plan(route, w1.shape[1], tile)
    ys = _moe_ffn(h, src, tile_expert, n_used, w1, w3, w2, layer, tile=tile)
    return _moe_combine(x, mods, route, ys, pos, tm=min(tm, 256), **tok)


def _final_norm_kernel(x_ref, w_ref, o_ref):
    x = x_ref[...]
    ms = jnp.mean(x * x, axis=-1, keepdims=True)
    o_ref[...] = x * lax.rsqrt(ms + NORM_EPS) * w_ref[...]


def _final_norm(x, w, *, rows, tm=512):
    d = x.shape[1]
    return pl.pallas_call(
        _final_norm_kernel,
        grid=(rows // tm,),
        in_specs=[pl.BlockSpec((tm, d), lambda m: (m, 0)), pl.BlockSpec((1, d), lambda m: (0, 0))],
        out_specs=pl.BlockSpec((tm, d), lambda m: (m, 0)),
        out_shape=jax.ShapeDtypeStruct((rows, d), F32),
        compiler_params=_cparams(1),
        name="final_norm",
    )(x, w.reshape(1, d))


def _dft_angles(n, period):
    k = lax.broadcasted_iota(jnp.int32, (n, n), 0)
    s = lax.broadcasted_iota(jnp.int32, (n, n), 1)
    return ((k * s) % period).astype(F32) * (2.0 * math.pi / period)


def _hyena_dft(seq_len):
    n = 2 * seq_len
    half = seq_len // 2
    row = lax.broadcasted_iota(jnp.int32, (seq_len, seq_len), 0)
    col = lax.broadcasted_iota(jnp.int32, (seq_len, seq_len), 1)
    freq = lambda i: jnp.where(i < half, 2 * i, 2 * (i - half) + 1)
    to_angle = lambda prod: (prod % n).astype(F32) * (2.0 * math.pi / n)
    ang_f = to_angle(freq(row) * col)
    ang_i = to_angle(row * freq(col))
    fc = jnp.cos(ang_f)
    fs = jnp.where(row == 0, jnp.where(col % 2 == 0, 1.0, -1.0), jnp.sin(ang_f))
    ic = jnp.where(col == 0, 1.0 / n, 2.0 / n * jnp.cos(ang_i))
    isn = jnp.where(col == 0, jnp.where(row % 2 == 0, 1.0, -1.0) / n, 2.0 / n * jnp.sin(ang_i))
    return (fc.astype(BF16), fs.astype(BF16), ic.astype(BF16), isn.astype(BF16),
            _flip_matrix(half))


def _hyena_feats(seq_len):
    pos = np.arange(seq_len, dtype=np.float64)
    t = pos / max(seq_len - 1, 1)
    bands = np.linspace(1e-4, HY_BANDS - 1, HY_BANDS).astype(np.float32).astype(np.float64)
    ang = 2 * math.pi * pos[:, None] * bands[None, :] / seq_len
    feats = np.concatenate([t[:, None], np.cos(ang), -np.sin(ang)], axis=-1)
    out = np.zeros((seq_len, LANE), np.float32)
    out[:, :HY_POS_DIM] = feats
    return jnp.asarray(out)


def _split2(a):
    hi = a.astype(BF16)
    return hi, (a - hi.astype(F32)).astype(BF16)


def _dot3(a, b):
    a_hi, a_lo = _split2(a)
    b_hi, b_lo = _split2(b)
    return _bdot(a_hi, b_hi) + _bdot(a_hi, b_lo) + _bdot(a_lo, b_hi)


def _filter_kernel(feat_ref, w1_ref, b1_ref, w2_ref, b2_ref, fr_ref, w3f_ref, w3b_ref,
                   dcf_ref, dcb_ref, fc_ref, fs_ref, o_ref, h_scr):
    @pl.when((pl.program_id(1) == 0) & (pl.program_id(2) == 0))
    def _():
        hp = lax.Precision.HIGHEST
        h1 = jnp.sin(fr_ref[0, 0:1, :] * (jnp.dot(feat_ref[...], w1_ref[0], precision=hp,
                                                  preferred_element_type=F32) + b1_ref[0]))
        h_scr[...] = jnp.sin(fr_ref[0, 1:2, :] * (jnp.dot(h1, w2_ref[0], precision=hp,
                                                          preferred_element_type=F32) + b2_ref[0]))

    h = h_scr[...]
    t = feat_ref[:, 0:1]
    hf = _dot3(h, w3f_ref[0]) * jnp.exp(-t * dcf_ref[0])
    hb = _dot3(h, w3b_ref[0]) * jnp.exp(-t * dcb_ref[0])
    row = lax.broadcasted_iota(jnp.int32, hf.shape, 0)
    hb = jnp.where(row == 0, 0.0, hb)
    ss = jnp.sum(hf * hf + hb * hb, axis=0, keepdims=True)
    scale = lax.rsqrt(ss + NORM_EPS)
    even = hf * scale + hb * scale
    odd = hf * scale - hb * scale
    kc = _bdot(fc_ref[...], even.astype(BF16))
    ks = _bdot(fs_ref[...], odd.astype(BF16))
    sign = jnp.where(row % 2 == 0, 1.0, -1.0)
    nyq = jnp.sum(even * sign, axis=0, keepdims=True)
    o_ref[0, 0] = kc
    o_ref[0, 1] = jnp.where(row == 0, nyq, ks)


def _hyena_filters(seq_len, w1p, b1, w2, b2, w3, freq, decay, fc, fs, *, tn=256):
    depth = w1p.shape[0]
    hid = HY_FILTER_HIDDEN
    nb = HY_W // tn
    feats = _hyena_feats(seq_len)
    col = lambda d: (lambda l, o, c: (l, 0, (2 * o + d) * nb + c))
    lyr = lambda l, o, c: (l, 0, 0)
    return pl.pallas_call(
        _filter_kernel,
        grid=(depth, 2, nb),
        in_specs=[
            pl.BlockSpec((seq_len, LANE), lambda l, o, c: (0, 0)),
            pl.BlockSpec((1, LANE, hid), lyr),
            pl.BlockSpec((1, 1, hid), lyr),
            pl.BlockSpec((1, hid, hid), lyr),
            pl.BlockSpec((1, 1, hid), lyr),
            pl.BlockSpec((1, 2, hid), lyr),
            pl.BlockSpec((1, hid, tn), col(0)),
            pl.BlockSpec((1, hid, tn), col(1)),
            pl.BlockSpec((1, 1, tn), col(0)),
            pl.BlockSpec((1, 1, tn), col(1)),
            _resident((seq_len, seq_len), lambda l, o, c: (0, 0)),
            _resident((seq_len, seq_len), lambda l, o, c: (0, 0)),
        ],
        out_specs=pl.BlockSpec((1, 2, seq_len, tn), lambda l, o, c: (l, 0, 0, o * nb + c)),
        out_shape=jax.ShapeDtypeStruct((depth, 2, seq_len, 2 * HY_W), F32),
        scratch_shapes=[pltpu.VMEM((seq_len, hid), F32)],
        compiler_params=_cparams(3),
        name="hyena_filters",
    )(feats, w1p, b1.reshape(depth, 1, hid), w2, b2.reshape(depth, 1, hid), freq,
      w3, w3, decay.reshape(depth, 1, 4 * HY_W), decay.reshape(depth, 1, 4 * HY_W), fc, fs)


def _conv3(z, w_ref, b_ref):
    n = z.shape[0]
    row = lax.broadcasted_iota(jnp.int32, z.shape, 0)
    prev = jnp.where(row == 0, 0.0, pltpu.roll(z, 1, 0))
    nxt = jnp.where(row == n - 1, 0.0, pltpu.roll(z, n - 1, 0))
    return prev * w_ref[0:1, :] + z * w_ref[1:2, :] + nxt * w_ref[2:3, :] + b_ref[...]


def _longconv_kernel(a_ref, g_ref, aw_ref, ab_ref, gw_ref, gb_ref, kf_ref, skip_ref,
                     fc_ref, fs_ref, ic_ref, is_ref, flip_ref, o_ref, *, conv_a):
    a = a_ref[...].astype(F32)
    if conv_a:
        a = _conv3(a, aw_ref, ab_ref)
    n = a.shape[0]
    h = n // 2
    lo = slice(0, h)
    hi = slice(h, n)
    k = lax.broadcasted_iota(jnp.int32, (h, a.shape[1]), 0)
    alt = jnp.where(k % 2 == 0, 1.0, -1.0)
    first = k == 0
    a_lo = a[lo]
    a_hi = a[hi]
    a_mid = a_hi[0:1, :]
    rev = _bdot(flip_ref[...], a_hi.astype(BF16))
    even_in = (a_lo + rev).astype(BF16)
    odd_in = (a_lo - rev).astype(BF16)
    nyq = jnp.sum(alt * (a_lo + a_hi), axis=0, keepdims=True)
    ac_e = _bdot(fc_ref[lo, lo], even_in) + alt * a_mid
    as_e = jnp.where(first, nyq, _bdot(fs_ref[lo, lo], odd_in))
    ac_o = _bdot(fc_ref[hi, lo], odd_in)
    as_o = _bdot(fs_ref[hi, lo], even_in) + alt * a_mid
    kc_e = kf_ref[0, 0, lo, :]
    ks_e = kf_ref[0, 1, lo, :]
    kc_o = kf_ref[0, 0, hi, :]
    ks_o = kf_ref[0, 1, hi, :]
    cross = as_e * ks_e
    yc_e = jnp.where(first, ac_e * kc_e, ac_e * kc_e - cross)
    ys_e = jnp.where(first, 0.0, ac_e * ks_e + as_e * kc_e)
    y_nyq = cross[0:1, :]
    yc_o = ac_o * kc_o - as_o * ks_o
    ys_o = ac_o * ks_o + as_o * kc_o
    sym = (_bdot(ic_ref[lo, lo], yc_e.astype(BF16)) + _bdot(is_ref[lo, hi], ys_o.astype(BF16))
           + alt * (y_nyq * (0.5 / n)))
    anti = _bdot(ic_ref[lo, hi], yc_o.astype(BF16)) + _bdot(is_ref[lo, lo], ys_e.astype(BF16))
    y_mid = (jnp.sum(alt * (yc_e + ys_o), axis=0, keepdims=True) * (1.0 / n)
             - yc_e[0:1, :] * (0.5 / n) + y_nyq * (0.5 / n))
    y_hi = jnp.where(first, y_mid, _bdot(flip_ref[...], (sym - anti).astype(BF16)))
    conv = jnp.concatenate([sym + anti, y_hi], axis=0)
    gate = _conv3(g_ref[...].astype(F32), gw_ref, gb_ref)
    o_ref[...] = (gate * (conv + a * skip_ref[0])).astype(o_ref.dtype)


def _hyena_longconv(a_src, a_part, zm, gate_part, conv_w, conv_b, kf, layer, order, skip,
                    dft, *, seq_len, n_seq, row0, out_rows, prev_out=None, tn=256):
    fc, fs, ic, isn, flip = dft
    assert seq_len % 4 == 0
    nb = HY_W // tn
    rb0 = row0 // seq_len
    const = lambda c, b: (0, 0)
    conv_a = a_part is not None
    aw_blk = a_part * nb if conv_a else 0
    a_blk = ZM_HY // tn + aw_blk if conv_a else 0
    gw_blk = gate_part * nb
    g_blk = ZM_HY // tn + gw_blk
    kernel = functools.partial(_longconv_kernel, conv_a=conv_a)
    in_specs = [
        pl.BlockSpec((seq_len, tn), lambda c, b: (rb0 + b, a_blk + c)),
        pl.BlockSpec((seq_len, tn), lambda c, b: (rb0 + b, g_blk + c)),
        pl.BlockSpec((3, tn), lambda c, b: (0, aw_blk + c)),
        pl.BlockSpec((1, tn), lambda c, b: (0, aw_blk + c)),
        pl.BlockSpec((3, tn), lambda c, b: (0, gw_blk + c)),
        pl.BlockSpec((1, tn), lambda c, b: (0, gw_blk + c)),
        _resident((1, 2, seq_len, tn), lambda c, b: (layer, 0, 0, order * nb + c)),
        pl.BlockSpec((1, 1, tn), lambda c, b: (order, 0, c)),
        _resident((seq_len, seq_len), const),
        _resident((seq_len, seq_len), const),
        _resident((seq_len, seq_len), const),
        _resident((seq_len, seq_len), const),
        _resident((seq_len // 2, seq_len // 2), const),
    ]
    args = [a_src, zm, conv_w, conv_b.reshape(1, -1), conv_w, conv_b.reshape(1, -1), kf,
            skip.reshape(2, 1, HY_W), fc, fs, ic, isn, flip]
    aliases = {}
    if prev_out is not None:
        in_specs.append(pl.BlockSpec(memory_space=pl.ANY))
        args.append(prev_out)
        aliases = {len(args) - 1: 0}
        kernel = functools.partial(_drop_last_input, kernel, 13)
    return pl.pallas_call(
        kernel,
        grid=(nb, n_seq),
        in_specs=in_specs,
        out_specs=pl.BlockSpec((seq_len, tn), lambda c, b: (rb0 + b, c)),
        out_shape=jax.ShapeDtypeStruct((out_rows, HY_W), BF16),
        input_output_aliases=aliases,
        compiler_params=_cparams(2),
        name="hyena_longconv",
    )(*args)


def _drop_last_input(kernel, n_in, *refs):
    return kernel(*refs[:n_in], *refs[n_in + 1:])


def _flip_matrix(half):
    r = lax.broadcasted_iota(jnp.int32, (half, half), 0)
    c = lax.broadcasted_iota(jnp.int32, (half, half), 1)
    return jnp.where((r >= 1) & (c == half - r), 1.0, 0.0).astype(BF16)


def _fnet_scale(seq_len):
    return 1.0 / math.sqrt(seq_len * FN_GROUP_W)


def _fnet_dft(seq_len):
    half = seq_len // 2
    ang_l = _dft_angles(seq_len, seq_len)[:half]
    ang_w = _dft_angles(FN_GROUP_W, FN_GROUP_W)
    scale = _fnet_scale(seq_len)
    return ((jnp.cos(ang_l) * scale).astype(BF16), (jnp.sin(ang_l) * scale).astype(BF16),
            jnp.cos(ang_w).astype(BF16), jnp.sin(ang_w).astype(BF16), _flip_matrix(half))


def _fnet_kernel(z_ref, cl_ref, sl_ref, cw_ref, sw_ref, flip_ref, o_ref, *, scale):
    z = z_ref[...]
    gc, gs = [], []
    for j in range(z.shape[1] // FN_GROUP_W):
        zj = z[:, j * FN_GROUP_W:(j + 1) * FN_GROUP_W]
        gc.append(_bdot(zj, cw_ref[...]).astype(BF16))
        gs.append(_bdot(zj, sw_ref[...]).astype(BF16))
    gc = jnp.concatenate(gc, axis=-1)
    gs = jnp.concatenate(gs, axis=-1)
    p = _bdot(cl_ref[...], gc)
    q = _bdot(sl_ref[...], gs)
    pos = lax.broadcasted_iota(jnp.int32, gc.shape, 0)
    y_mid = scale * jnp.sum(jnp.where(pos % 2 == 0, 1.0, -1.0) * gc.astype(F32), axis=0,
                            keepdims=True)
    first = lax.broadcasted_iota(jnp.int32, p.shape, 0) == 0
    y_hi = jnp.where(first, y_mid, _bdot(flip_ref[...], (p + q).astype(BF16)))
    o_ref[...] = jnp.concatenate([p - q, y_hi], axis=0).astype(o_ref.dtype)


def _fnet(zm, dft, *, seq_len, n_seq, row0, out_rows, prev_out=None, tn=512):
    cl, sl, cw, sw, flip = dft
    nb = FN_W // tn
    rb0 = row0 // seq_len
    c0 = ZM_FN // tn
    half = seq_len // 2
    const = lambda c, b: (0, 0)
    in_specs = [
        pl.BlockSpec((seq_len, tn), lambda c, b: (rb0 + b, c0 + c)),
        _resident((half, seq_len), const),
        _resident((half, seq_len), const),
        pl.BlockSpec((FN_GROUP_W, FN_GROUP_W), const),
        pl.BlockSpec((FN_GROUP_W, FN_GROUP_W), const),
        _resident((half, half), const),
    ]
    args = [zm, cl, sl, cw, sw, flip]
    kernel = functools.partial(_fnet_kernel, scale=_fnet_scale(seq_len))
    aliases = {}
    if prev_out is not None:
        in_specs.append(pl.BlockSpec(memory_space=pl.ANY))
        args.append(prev_out)
        aliases = {len(args) - 1: 0}
        kernel = functools.partial(_drop_last_input, kernel, 6)
    return pl.pallas_call(
        kernel,
        grid=(nb, n_seq),
        in_specs=in_specs,
        out_specs=pl.BlockSpec((seq_len, tn), lambda c, b: (rb0 + b, c)),
        out_shape=jax.ShapeDtypeStruct((out_rows, FN_W), BF16),
        input_output_aliases=aliases,
        compiler_params=_cparams(2),
        name="fnet",
    )(*args)


def _hgrn_tables(reverse):
    c = HG_CHUNK
    tri = np.zeros((c, c), np.float32)
    for t in range(c):
        if reverse:
            tri[t, t:] = 1.0
        else:
            tri[t, :t + 1] = 1.0
    mats = [tri]
    masks = []
    for lvl in range(HG_LEVELS):
        half = (c // 2) >> lvl
        sel = np.zeros((c, c), np.float32)
        mask = np.zeros((c, c), np.float32)
        for t in range(c):
            start = (t // (2 * half)) * 2 * half
            mid = start + half
            sel[t, mid if reverse else mid - 1] = 1.0
            for s in range(start, start + 2 * half):
                if reverse and t < mid <= s:
                    mask[t, s] = 1.0
                if (not reverse) and s < mid <= t:
                    mask[t, s] = 1.0
        mats.append(tri - sel @ tri)
        masks.append(mask)
    half = c // 2
    sel = np.zeros((c, c), np.float32)
    mask = np.zeros((c, c), np.float32)
    for t in range(c):
        start = (t // half) * half
        sel[t, start + half // 2] = 1.0
        for s in range(start, start + half):
            if (s >= t) if reverse else (s <= t):
                mask[t, s] = 1.0
    mats.append(tri - sel @ tri)
    masks.append(mask)
    mats = [mats[0], mats[1], mats[-1]] + mats[2:-1]
    return np.concatenate(mats, axis=0), np.stack(masks, axis=0)


HG_BOUNDED_BLOCKS = 3
HG_MID_BLOCK = 2


def _hgrn_level_block(lvl):
    return 1 if lvl == 0 else HG_BOUNDED_BLOCKS + lvl - 1


def _hgrn_decays(zf_ref, lb_ref, tab_ref, k_scr, p_scr, dec_scr):
    lb = lb_ref[...]
    f = lb + (1.0 - lb) * jax.nn.sigmoid(zf_ref[...])
    lf = jnp.log2(f)
    k_scr[...] = (1.0 - f).astype(BF16)
    p1 = lf.astype(BF16)
    r1 = lf - p1.astype(F32)
    p2 = r1.astype(BF16)
    p3 = (r1 - p2.astype(F32)).astype(BF16)
    p_scr[...] = jnp.concatenate([p1, p2, p3], axis=0)
    n = HG_BOUNDED_BLOCKS * HG_CHUNK
    dec_scr[0:n, :] = _bdot(tab_ref[0:n, :], p_scr[...])


def _hgrn_more_decays(tab_ref, p_scr, dec_scr):
    n = HG_BOUNDED_BLOCKS * HG_CHUNK
    dec_scr[n:, :] = _bdot(tab_ref[n:, :], p_scr[...])


def _nt_dot(a, b):
    return lax.dot_general(a, b, (((1,), (1,)), ((), ())), preferred_element_type=F32)


def _hgrn_direction(q_ref, v_ref, k_scr, dec_scr, mask_ref, st_scr, o_ref, end_row, bounded):
    c = HG_CHUNK
    heads = lambda ref, r0: jnp.stack(
        [ref[r0:r0 + c, h * HG_DK:(h + 1) * HG_DK] for h in range(HG_HEADS)], axis=0)
    bmm = lambda eq, a, b: jnp.einsum(eq, a, b, preferred_element_type=F32)
    q = heads(q_ref, 0)
    k = heads(k_scr, 0)
    v = heads(v_ref, 0)
    b = heads(dec_scr, 0)
    st = st_scr[...]
    o = bmm('htk,hkv->htv', q * jnp.exp2(b).astype(BF16), st.astype(BF16))
    level = lambda l: jnp.exp2(-jnp.abs(heads(dec_scr, _hgrn_level_block(l) * c))).astype(BF16)
    if bounded:
        e = level(0)
        att = jnp.where(mask_ref[0] > 0.5, bmm('htk,hsk->hts', q * e, k * e), 0.0)
        d = heads(dec_scr, HG_MID_BLOCK * c)
        inner = bmm('htk,hsk->hts', q * jnp.exp2(d).astype(BF16), k * jnp.exp2(-d).astype(BF16))
        att += jnp.where(mask_ref[HG_LEVELS] > 0.5, inner, 0.0)
    else:
        diag = jnp.sum(q.astype(F32) * k.astype(F32), axis=-1, keepdims=True)
        o += diag * v.astype(F32)
        att = jnp.zeros((HG_HEADS, c, c), F32)
        for lvl in range(HG_LEVELS):
            e = level(lvl)
            att += bmm('htk,hsk->hts', q * e, k * e) * mask_ref[lvl]
    o += bmm('hts,hsv->htv', att.astype(BF16), v)
    for h in range(HG_HEADS):
        o_ref[:, h * HG_DK:(h + 1) * HG_DK] = o[h]
    b_end = b[:, end_row:end_row + 1, :]
    upd = bmm('htk,htv->hkv', k * jnp.exp2(b_end - b).astype(BF16), v)
    decay = jnp.swapaxes(jnp.broadcast_to(jnp.exp2(b_end), (HG_HEADS, HG_DK, HG_DK)), 1, 2)
    st_scr[...] = decay * st + upd


HG_MAX_LOG2_SPAN = 100.0
HG_MAX_ABS_Q = 1e6


def _hgrn_kernel(qf_ref, vf_ref, zff_ref, qb_ref, vb_ref, zfb_ref, lbf_ref, lbb_ref,
                 tabf_ref, maskf_ref, tabb_ref, maskb_ref, s0f_ref, s0b_ref,
                 of_ref, ob_ref, sf_ref, sb_ref, stf_scr, stb_scr, kf_scr, kb_scr,
                 pf_scr, pb_scr, decf_scr, decb_scr):
    ci = pl.program_id(1)
    c = HG_CHUNK

    @pl.when(ci == 0)
    def _():
        stf_scr[...] = s0f_ref[0]
        stb_scr[...] = s0b_ref[0]

    _hgrn_decays(zff_ref, lbf_ref, tabf_ref, kf_scr, pf_scr, decf_scr)
    _hgrn_decays(zfb_ref, lbb_ref, tabb_ref, kb_scr, pb_scr, decb_scr)
    mid = slice(HG_MID_BLOCK * c, (HG_MID_BLOCK + 1) * c)
    span = jnp.maximum(jnp.max(jnp.abs(decf_scr[mid, :])), jnp.max(jnp.abs(decb_scr[mid, :])))
    q_abs = jnp.maximum(jnp.max(jnp.abs(qf_ref[...].astype(F32))),
                        jnp.max(jnp.abs(qb_ref[...].astype(F32))))
    bounded = (span < HG_MAX_LOG2_SPAN) & (q_abs < HG_MAX_ABS_Q)

    def run(flag):
        _hgrn_direction(qf_ref, vf_ref, kf_scr, decf_scr, maskf_ref, stf_scr, of_ref, c - 1, flag)
        _hgrn_direction(qb_ref, vb_ref, kb_scr, decb_scr, maskb_ref, stb_scr, ob_ref, 0, flag)

    @pl.when(bounded)
    def _():
        run(True)

    @pl.when(jnp.logical_not(bounded))
    def _():
        _hgrn_more_decays(tabf_ref, pf_scr, decf_scr)
        _hgrn_more_decays(tabb_ref, pb_scr, decb_scr)
        run(False)

    @pl.when(ci == pl.num_programs(1) - 1)
    def _():
        sf_ref[0] = stf_scr[...]
        sb_ref[0] = stb_scr[...]


def _hgrn(zm, zf, lb_f, lb_b, s0_f, s0_b, *, seq_len, n_seq, row0, out_rows,
          prev_of=None, prev_ob=None):
    c = HG_CHUNK
    nc = seq_len // c
    rb0 = row0 // c
    w = HG_W
    tab_f, mask_f = _hgrn_tables(False)
    tab_b, mask_b = _hgrn_tables(True)
    fwd = lambda col: (lambda b, i: (rb0 + b * nc + i, col))
    bwd = lambda col: (lambda b, i: (rb0 + b * nc + nc - 1 - i, col))
    const2 = lambda b, i: (0, 0)
    const3 = lambda b, i: (0, 0, 0)
    state = lambda b, i: (b, 0, 0, 0)
    nt = (2 + HG_LEVELS) * c
    in_specs = [
        pl.BlockSpec((c, w), fwd(ZM_Q // w)),
        pl.BlockSpec((c, w), fwd(ZM_I // w)),
        pl.BlockSpec((c, w), fwd(0)),
        pl.BlockSpec((c, w), bwd(ZM_Q // w)),
        pl.BlockSpec((c, w), bwd(ZM_I // w)),
        pl.BlockSpec((c, w), bwd(1)),
        pl.BlockSpec((1, w), const2),
        pl.BlockSpec((1, w), const2),
        pl.BlockSpec((nt, 3 * c), const2),
        pl.BlockSpec((HG_LEVELS + 1, c, c), const3),
        pl.BlockSpec((nt, 3 * c), const2),
        pl.BlockSpec((HG_LEVELS + 1, c, c), const3),
        pl.BlockSpec((1, HG_HEADS, HG_DK, HG_DK), state),
        pl.BlockSpec((1, HG_HEADS, HG_DK, HG_DK), state),
    ]
    args = [zm, zm, zf, zm, zm, zf, lb_f.reshape(1, w), lb_b.reshape(1, w),
            jnp.asarray(np.tile(tab_f, (1, 3)), BF16), jnp.asarray(mask_f),
            jnp.asarray(np.tile(tab_b, (1, 3)), BF16), jnp.asarray(mask_b), s0_f, s0_b]
    kernel = _hgrn_kernel
    aliases = {}
    if prev_of is not None:
        in_specs += [pl.BlockSpec(memory_space=pl.ANY), pl.BlockSpec(memory_space=pl.ANY)]
        args += [prev_of, prev_ob]
        aliases = {14: 0, 15: 1}
        kernel = functools.partial(_hgrn_alias_kernel, 14)
    st_shape = jax.ShapeDtypeStruct((n_seq, HG_HEADS, HG_DK, HG_DK), F32)
    return pl.pallas_call(
        kernel,
        grid=(n_seq, nc),
        in_specs=in_specs,
        out_specs=[
            pl.BlockSpec((c, w), fwd(0)),
            pl.BlockSpec((c, w), bwd(0)),
            pl.BlockSpec((1, HG_HEADS, HG_DK, HG_DK), state),
            pl.BlockSpec((1, HG_HEADS, HG_DK, HG_DK), state),
        ],
        out_shape=[jax.ShapeDtypeStruct((out_rows, w), F32),
                   jax.ShapeDtypeStruct((out_rows, w), F32), st_shape, st_shape],
        scratch_shapes=[pltpu.VMEM((HG_HEADS, HG_DK, HG_DK), F32),
                        pltpu.VMEM((HG_HEADS, HG_DK, HG_DK), F32),
                        pltpu.VMEM((c, w), BF16), pltpu.VMEM((c, w), BF16),
                        pltpu.VMEM((3 * c, w), BF16), pltpu.VMEM((3 * c, w), BF16),
                        pltpu.VMEM((nt, w), F32), pltpu.VMEM((nt, w), F32)],
        input_output_aliases=aliases,
        compiler_params=_cparams(2),
        name="hgrn2",
    )(*args)


def _hgrn_alias_kernel(n_in, *refs):
    return _hgrn_kernel(*refs[:n_in], *refs[n_in + 2:])


def kernel(x, c, ctx, c_ctx, w_mod, b_mod, norm1_w, norm2_w, w_in, hy_conv_w, hy_conv_b,
           hy_f_w1, hy_f_b1, hy_f_w2, hy_f_b2, hy_f_w3, hy_f_freq, hy_decay, hy_skip,
           hg_lower, hg_norm_w, w_br_hy, w_br_fn, w_br_hg, w_out,
           ffn_w1, ffn_w3, ffn_w2, moe_router, moe_w1, moe_w3, moe_w2, final_norm_w):
    batch, seq, d = x.shape
    l_ctx = ctx.shape[1]
    depth = w_in.shape[0]
    n_lat = batch * seq
    n_all = n_lat + batch * l_ctx

    lb_all = jnp.cumsum(jax.nn.softmax(hg_lower.astype(F32), axis=1), axis=1)
    lb_all = lb_all - lb_all[:, :1]

    cc = jnp.zeros((8, d), F32).at[:batch].set(c).at[batch].set(c_ctx)
    mods = _modulation(cc, w_mod, b_mod).reshape(depth, 8, 6, d)

    w1p = jnp.zeros((depth, LANE, HY_FILTER_HIDDEN), F32).at[:, :HY_POS_DIM].set(hy_f_w1)
    dft_lat = _hyena_dft(seq)
    dft_ctx = _hyena_dft(l_ctx)
    kf_lat = _hyena_filters(seq, w1p, hy_f_b1, hy_f_w2, hy_f_b2, hy_f_w3, hy_f_freq, hy_decay,
                            dft_lat[0], dft_lat[1])
    kf_ctx = _hyena_filters(l_ctx, w1p, hy_f_b1, hy_f_w2, hy_f_b2, hy_f_w3, hy_f_freq, hy_decay,
                            dft_ctx[0], dft_ctx[1])
    fn_lat = _fnet_dft(seq)
    fn_ctx = _fnet_dft(l_ctx)

    rows_x = jnp.concatenate([x.reshape(n_lat, d), ctx.reshape(batch * l_ctx, d)], axis=0)
    zero_state = jnp.zeros((batch, HG_HEADS, HG_DK, HG_DK), F32)
    tile_cap = math.gcd(seq, batch * l_ctx)
    tok = dict(seq=seq, batch=batch)
    tile = lambda t: dict(tm=min(t, tile_cap))
    lat = dict(seq_len=seq, n_seq=batch, row0=0, out_rows=n_all)
    cx = dict(seq_len=l_ctx, n_seq=batch, row0=n_lat, out_rows=n_all)

    for l in range(depth):
        last = l == depth - 1
        rows = n_lat if last else n_all
        m = mods[l]

        h = _norm_rows(rows_x, norm1_w[l], m, **tok, **tile(512))
        zm, zf = _in_proj(h, w_in, l, **tile(1024))

        of, ob, s_f, s_b = _hgrn(zm, zf, lb_all[0, l], lb_all[1, l], zero_state, zero_state, **cx)
        of, ob, _, _ = _hgrn(zm, zf, lb_all[0, l], lb_all[1, l], s_f, s_b, prev_of=of,
                             prev_ob=ob, **lat)

        hy_args = (hy_conv_w[l], hy_conv_b[l])
        mix = dict(lat, out_rows=rows)
        u = _hyena_longconv(zm, 0, zm, 1, *hy_args, kf_lat, l, 0, hy_skip[l], dft_lat, **mix)
        y_hy = _hyena_longconv(u, None, zm, 2, *hy_args, kf_lat, l, 1, hy_skip[l], dft_lat, **mix)
        y_fn = _fnet(zm, fn_lat, **mix)
        if not last:
            u = _hyena_longconv(zm, 0, zm, 1, *hy_args, kf_ctx, l, 0, hy_skip[l], dft_ctx,
                                prev_out=u, **cx)
            y_hy = _hyena_longconv(u, None, zm, 2, *hy_args, kf_ctx, l, 1, hy_skip[l], dft_ctx,
                                   prev_out=y_hy, **cx)
            y_fn = _fnet(zm, fn_ctx, prev_out=y_fn, **cx)

        rows_x = _merge_out(y_hy, y_fn, of, ob, zm, rows_x, hg_norm_w[l], m,
                            w_br_hy[l].astype(BF16), w_br_fn[l].astype(BF16),
                            w_br_hg[l].astype(BF16), w_out[l].astype(BF16), rows=rows,
                            **tok, **tile(256))

        if l % 2 == 0:
            i = l // 2
            rows_x = _ffn_dense(rows_x, norm2_w[l], m, ffn_w1[i].astype(BF16),
                                ffn_w3[i].astype(BF16), ffn_w2[i].astype(BF16), rows=rows,
                                **tok, **tile(512))
        else:
            i = l // 2
            rows_x = _moe_top2(rows_x, norm2_w[l], m, moe_router[i], moe_w1, moe_w3, moe_w2, i,
                               rows=rows, **tok, **tile(512))

    out = _final_norm(rows_x, final_norm_w, rows=n_lat, **tile(512))
    return out.reshape(batch, seq, d)
```

```python
import functools
import math

import numpy as np
import jax
import jax.numpy as jnp
from jax import lax
from jax.experimental import pallas as pl
from jax.experimental.pallas import tpu as pltpu

F32 = jnp.float32
BF16 = jnp.bfloat16
NORM_EPS = 1e-6

D_MODEL = 2048
DEPTH = 4
HY_W = 1024
HY_BANDS = 16
HY_POS_DIM = 2 * HY_BANDS + 1
HY_FILTER_HIDDEN = 64
FN_W = 1024
FN_GROUP_W = 128
HG_W = 1024
HG_HEADS = 8
HG_DK = 128
HG_CHUNK = 64
HG_LEVELS = 6
N_EXPERTS = 8
LANE = 128

HG_FF = 3 * HY_W + FN_W + 2 * HG_W
HG_I = HG_FF + 2 * HG_W
ZM_HY = 0
ZM_FN = ZM_HY + 3 * HY_W
ZM_Q = ZM_FN + FN_W
ZM_G = ZM_Q + HG_W
ZM_I = HG_I
ZM_GATE = HG_I + HG_W
ZM_COLS = ZM_GATE + 3 * D_MODEL

VMEM_LIMIT = 56 * 1024 * 1024


def _cparams(n_axes, vmem=VMEM_LIMIT):
    return pltpu.CompilerParams(dimension_semantics=("arbitrary",) * n_axes, vmem_limit_bytes=vmem)


def _resident(shape, index_map):
    return pl.BlockSpec(shape, index_map, pipeline_mode=pl.Buffered(1))


def _silu(v):
    return v * jax.nn.sigmoid(v)


def _bdot(a, b):
    return jnp.dot(a, b, preferred_element_type=F32)


def _mod_kernel(c_ref, w_ref, b_ref, o_ref):
    c = _silu(c_ref[...])
    o_ref[0] = jnp.dot(c, w_ref[0], preferred_element_type=F32,
                       precision=lax.Precision.HIGHEST) + b_ref[0]


def _modulation(cc, w_mod, b_mod):
    depth, d, n6 = w_mod.shape
    tn = 2048
    return pl.pallas_call(
        _mod_kernel,
        grid=(depth, n6 // tn),
        in_specs=[
            pl.BlockSpec((8, d), lambda l, n: (0, 0)),
            pl.BlockSpec((1, d, tn), lambda l, n: (l, 0, n)),
            pl.BlockSpec((1, 1, tn), lambda l, n: (l, 0, n)),
        ],
        out_specs=pl.BlockSpec((1, 8, tn), lambda l, n: (l, 0, n)),
        out_shape=jax.ShapeDtypeStruct((depth, 8, n6), F32),
        compiler_params=_cparams(2),
        name="modulation",
    )(cc, w_mod, b_mod.reshape(depth, 1, n6))


def _norm_mod(x, nw, shift, scale):
    ms = jnp.mean(x * x, axis=-1, keepdims=True)
    return (x * lax.rsqrt(ms + NORM_EPS) * nw) * (1 + scale) + shift


def _group_of(row_block, tm, seq, batch):
    return jnp.minimum((row_block * tm) // seq, batch)


def _norm_rows_kernel(x_ref, nw_ref, mod_ref, o_ref):
    h = _norm_mod(x_ref[...], nw_ref[...], mod_ref[0, 0:1, :], mod_ref[0, 1:2, :])
    o_ref[...] = h.astype(o_ref.dtype)


def _norm_rows(x, nw, mods, *, seq, batch, tm=512):
    rows, d = x.shape
    return pl.pallas_call(
        _norm_rows_kernel,
        grid=(rows // tm,),
        in_specs=[
            pl.BlockSpec((tm, d), lambda m: (m, 0)),
            pl.BlockSpec((1, d), lambda m: (0, 0)),
            pl.BlockSpec((1, 6, d), lambda m: (_group_of(m, tm, seq, batch), 0, 0)),
        ],
        out_specs=pl.BlockSpec((tm, d), lambda m: (m, 0)),
        out_shape=jax.ShapeDtypeStruct((rows, d), BF16),
        compiler_params=_cparams(1),
        name="norm_rows",
    )(x, nw.reshape(1, d), mods)


def _inproj_kernel(h_ref, w_ref, z_ref, zf_ref, w_scr, *, n_main):
    @pl.when(pl.program_id(1) == 0)
    def _():
        w_scr[...] = w_ref[0].astype(BF16)

    r = _bdot(h_ref[...], w_scr[...])

    @pl.when(pl.program_id(0) < n_main)
    def _():
        z_ref[...] = r.astype(z_ref.dtype)

    @pl.when(pl.program_id(0) >= n_main)
    def _():
        zf_ref[...] = r


def _in_proj(h, w_in, layer, *, tm=1024, tn=1024):
    rows, d = h.shape
    n_cols = w_in.shape[2]
    nb = n_cols // tn
    f0 = HG_FF // tn
    nf = 2 * HG_W // tn
    n_main = nb - nf
    nm = rows // tm
    col = lambda j: jnp.where(j < f0, j, jnp.where(j < n_main, j + nf, j - n_main + f0))
    z_idx = lambda j, m: (jnp.where(j < n_main, m, nm - 1), col(jnp.minimum(j, n_main - 1)))
    zf_idx = lambda j, m: (jnp.where(j < n_main, 0, m), jnp.maximum(j - n_main, 0))
    return pl.pallas_call(
        functools.partial(_inproj_kernel, n_main=n_main),
        grid=(nb, nm),
        in_specs=[
            pl.BlockSpec((tm, d), lambda j, m: (m, 0)),
            pl.BlockSpec((1, d, tn), lambda j, m: (layer, 0, col(j))),
        ],
        out_specs=[pl.BlockSpec((tm, tn), z_idx), pl.BlockSpec((tm, tn), zf_idx)],
        out_shape=[jax.ShapeDtypeStruct((rows, n_cols), BF16),
                   jax.ShapeDtypeStruct((rows, nf * tn), F32)],
        scratch_shapes=[pltpu.VMEM((d, tn), BF16)],
        compiler_params=_cparams(2),
        name="in_proj",
    )(h, w_in)


def _merge_kernel(yhy_ref, yfn_ref, of_ref, ob_ref, g_ref, ga0_ref, ga1_ref, gb0_ref, gb1_ref,
                  gc0_ref, gc1_ref, x_ref, hnw_ref, mod_ref,
                  wbhy_ref, wbfn_ref, wbhg_ref, wo_ref, o_ref):
    d = x_ref.shape[1]
    ohg = of_ref[...] + ob_ref[...]
    nw = hnw_ref[...]
    heads = []
    for h in range(ohg.shape[1] // HG_DK):
        oh = ohg[:, h * HG_DK:(h + 1) * HG_DK]
        ms = jnp.mean(oh * oh, axis=-1, keepdims=True)
        heads.append(oh * lax.rsqrt(ms + NORM_EPS) * nw)
    y_hg = jnp.concatenate(heads, axis=-1) * _silu(g_ref[...].astype(F32))
    gate = lambda lo, hi: jax.nn.sigmoid(
        jnp.concatenate([lo[...], hi[...]], axis=-1).astype(F32))
    merged = gate(ga0_ref, ga1_ref) * _bdot(yhy_ref[...], wbhy_ref[...])
    merged += gate(gb0_ref, gb1_ref) * _bdot(yfn_ref[...], wbfn_ref[...])
    merged += gate(gc0_ref, gc1_ref) * _bdot(y_hg.astype(BF16), wbhg_ref[...])
    out = _bdot(merged.astype(BF16), wo_ref[...])
    o_ref[...] = x_ref[...] + mod_ref[0, 2:3, :] * out


def _merge_out(y_hy, y_fn, o_f, o_b, zm, x, hg_nw, mods, wb_hy, wb_fn, wb_hg, w_o,
               *, rows, seq, batch, tm=256):
    d = x.shape[1]
    w = y_hy.shape[1]
    tm = min(tm, rows)
    row = lambda m: (m, 0)
    const = lambda m: (0, 0)
    gate0 = ZM_GATE // (d // 2)
    return pl.pallas_call(
        _merge_kernel,
        grid=(rows // tm,),
        in_specs=[
            pl.BlockSpec((tm, w), row),
            pl.BlockSpec((tm, w), row),
            pl.BlockSpec((tm, w), row),
            pl.BlockSpec((tm, w), row),
            pl.BlockSpec((tm, w), lambda m: (m, ZM_G // w)),
            *[pl.BlockSpec((tm, d // 2), functools.partial(lambda i, m: (m, i), gate0 + i))
              for i in range(6)],
            pl.BlockSpec((tm, d), row),
            pl.BlockSpec((1, HG_DK), const),
            pl.BlockSpec((1, 6, d), lambda m: (_group_of(m, tm, seq, batch), 0, 0)),
            _resident((w, d), const),
            _resident((w, d), const),
            _resident((w, d), const),
            _resident((d, d), const),
        ],
        out_specs=pl.BlockSpec((tm, d), row),
        out_shape=jax.ShapeDtypeStruct((rows, d), F32),
        compiler_params=_cparams(1),
        name="merge_out",
    )(y_hy, y_fn, o_f, o_b, zm, *([zm] * 6), x, hg_nw.reshape(1, HG_DK), mods,
      wb_hy, wb_fn, wb_hg, w_o)


def _ffn_kernel(x_ref, nw_ref, mod_ref, w1_ref, w3_ref, w2_ref, o_ref, h_scr):
    f = pl.program_id(1)

    @pl.when(f == 0)
    def _():
        h = _norm_mod(x_ref[...], nw_ref[...], mod_ref[0, 3:4, :], mod_ref[0, 4:5, :])
        h_scr[...] = h.astype(BF16)
        o_ref[...] = jnp.zeros_like(o_ref)

    h = h_scr[...]
    act = _silu(_bdot(h, w1_ref[0].astype(BF16))) * _bdot(h, w3_ref[0].astype(BF16))
    o_ref[...] += _bdot(act.astype(BF16), w2_ref[0].astype(BF16))

    @pl.when(f == pl.num_programs(1) - 1)
    def _():
        o_ref[...] = x_ref[...] + mod_ref[0, 5:6, :] * o_ref[...]


def _ffn_dense(x, nw, mods, w1, w3, w2, layer, *, rows, seq, batch, tm=1024, tf=256):
    d = x.shape[1]
    ff = w1.shape[2]
    tm = min(tm, rows)
    return pl.pallas_call(
        _ffn_kernel,
        grid=(rows // tm, ff // tf),
        in_specs=[
            _resident((tm, d), lambda m, f: (m, 0)),
            pl.BlockSpec((1, d), lambda m, f: (0, 0)),
            pl.BlockSpec((1, 6, d), lambda m, f: (_group_of(m, tm, seq, batch), 0, 0)),
            pl.BlockSpec((1, d, tf), lambda m, f: (layer, 0, f)),
            pl.BlockSpec((1, d, tf), lambda m, f: (layer, 0, f)),
            pl.BlockSpec((1, tf, d), lambda m, f: (layer, f, 0)),
        ],
        out_specs=pl.BlockSpec((tm, d), lambda m, f: (m, 0)),
        out_shape=jax.ShapeDtypeStruct((rows, d), F32),
        scratch_shapes=[pltpu.VMEM((tm, d), BF16)],
        compiler_params=_cparams(2),
        name="ffn_dense",
    )(x, nw.reshape(1, d), mods, w1, w3, w2)


def _top2(logits, n_experts):
    lane = lax.broadcasted_iota(jnp.int32, logits.shape, 1).astype(F32)
    neg = jnp.float32(-jnp.inf)
    lg = jnp.where(lane < n_experts, logits, neg)
    m1 = jnp.max(lg, axis=-1, keepdims=True)
    i1 = jnp.min(jnp.where(lg == m1, lane, float(LANE)), axis=-1, keepdims=True)
    lg2 = jnp.where(lane == i1, neg, lg)
    m2 = jnp.max(lg2, axis=-1, keepdims=True)
    i2 = jnp.min(jnp.where(lg2 == m2, lane, float(LANE)), axis=-1, keepdims=True)
    e2 = jnp.exp(m2 - m1)
    return i1, i2, 1.0 / (1.0 + e2), e2 / (1.0 + e2)


def _route_kernel(x_ref, nw_ref, mod_ref, r_ref, h_ref, route_ref):
    h = _norm_mod(x_ref[...], nw_ref[...], mod_ref[0, 3:4, :], mod_ref[0, 4:5, :])
    logits = jnp.dot(h, r_ref[...], preferred_element_type=F32, precision=lax.Precision.HIGHEST)
    i1, i2, p1, p2 = _top2(logits, N_EXPERTS)
    lane = lax.broadcasted_iota(jnp.int32, logits.shape, 1)
    route = jnp.where(lane == 0, i1, jnp.where(lane == 1, i2,
                      jnp.where(lane == 2, p1, jnp.where(lane == 3, p2, 0.0))))
    h_ref[...] = h
    route_ref[...] = route


def _moe_route(x, nw, mods, router, *, rows, seq, batch, tm=512):
    d = x.shape[1]
    rpad = jnp.zeros((d, LANE), F32).at[:, :router.shape[1]].set(router)
    return pl.pallas_call(
        _route_kernel,
        grid=(rows // tm,),
        in_specs=[
            pl.BlockSpec((tm, d), lambda m: (m, 0)),
            pl.BlockSpec((1, d), lambda m: (0, 0)),
            pl.BlockSpec((1, 6, d), lambda m: (_group_of(m, tm, seq, batch), 0, 0)),
            pl.BlockSpec((d, LANE), lambda m: (0, 0)),
        ],
        out_specs=[pl.BlockSpec((tm, d), lambda m: (m, 0)),
                   pl.BlockSpec((tm, LANE), lambda m: (m, 0))],
        out_shape=[jax.ShapeDtypeStruct((rows, d), F32), jax.ShapeDtypeStruct((rows, LANE), F32)],
        compiler_params=_cparams(1),
        name="moe_route",
    )(x, nw.reshape(1, d), mods, rpad)


def _moe_plan(route, n_experts, tile):
    rows = route.shape[0]
    e = route[:, :2].astype(jnp.int32).reshape(-1)
    onehot = (e[:, None] == jnp.arange(n_experts, dtype=jnp.int32)[None, :]).astype(jnp.int32)
    before = jnp.cumsum(onehot, axis=0) - onehot
    rank = jnp.sum(before * onehot, axis=1)
    counts = jnp.sum(onehot, axis=0)
    padded = ((counts + tile - 1) // tile) * tile
    ends = jnp.cumsum(padded)
    starts = ends - padded
    pos = jnp.sum(starts[None, :] * onehot, axis=1) + rank
    n_tiles = -(-2 * rows // tile) + n_experts
    src = jnp.zeros((n_tiles * tile,), jnp.int32).at[pos].set(
        jnp.arange(2 * rows, dtype=jnp.int32) // 2)
    tile_start = jnp.arange(n_tiles, dtype=jnp.int32) * tile
    tile_expert = jnp.sum((tile_start[:, None] >= ends[None, :]).astype(jnp.int32), axis=1)
    tile_expert = jnp.minimum(tile_expert, n_experts - 1)
    n_used = (ends[-1] // tile).reshape(1)
    return pos, src, tile_expert, n_used


def _row_copy(src_hbm, src_row, dst, dst_row, sem):
    return pltpu.make_async_copy(src_hbm.at[pl.ds(src_row, 1)], dst.at[pl.ds(dst_row, 1)], sem)


def _gather_rows(idx_ref, base, stride, src_hbm, dst, sem, *, wait):
    if wait:
        pltpu.make_async_copy(src_hbm.at[pl.ds(0, dst.shape[0])], dst, sem).wait()
        return

    def body(r, carry):
        _row_copy(src_hbm, idx_ref[base + stride * r], dst, r, sem).start()
        return carry

    lax.fori_loop(0, dst.shape[0], body, 0, unroll=8)


def _moe_ffn_kernel(te_ref, nu_ref, src_ref, h_hbm, w1_ref, w3_ref, w2_ref, o_ref,
                    rows_scr, h_scr, sems):
    t = pl.program_id(0)
    f = pl.program_id(1)
    last = pl.num_programs(1) - 1
    tile = o_ref.shape[0]
    n_used = nu_ref[0]
    used = t < n_used
    slot = t % 2

    def gather(tile_idx, buf, wait):
        _gather_rows(src_ref, tile_idx * tile, 1, h_hbm, rows_scr.at[buf], sems.at[buf], wait=wait)

    @pl.when(used & (f == 0) & (t == 0))
    def _():
        gather(t, slot, False)

    @pl.when(used & (f == 0))
    def _():
        gather(t, slot, True)
        h_scr[...] = rows_scr[slot].astype(BF16)
        o_ref[...] = jnp.zeros_like(o_ref)

    @pl.when((f == 0) & (t + 1 < n_used))
    def _():
        gather(t + 1, 1 - slot, False)

    @pl.when(used)
    def _():
        h = h_scr[...]
        act = _silu(_bdot(h, w1_ref[0, 0].astype(BF16))) * _bdot(h, w3_ref[0, 0].astype(BF16))
        o_ref[...] += _bdot(act.astype(BF16), w2_ref[0, 0].astype(BF16))

    @pl.when(jnp.logical_not(used) & (f == last))
    def _():
        o_ref[...] = jnp.zeros_like(o_ref)


def _moe_ffn(h, src, tile_expert, n_used, w1, w3, w2, layer, *, tile, tf=256):
    d = h.shape[1]
    n = src.shape[0]
    ff = w1.shape[3]
    nf = ff // tf
    fsel = lambda t, f, nu: jnp.where(t < nu[0], f, nf - 1)
    up = lambda t, f, te, nu, sr: (layer, te[t], 0, fsel(t, f, nu))
    down = lambda t, f, te, nu, sr: (layer, te[t], fsel(t, f, nu), 0)
    return pl.pallas_call(
        _moe_ffn_kernel,
        grid_spec=pltpu.PrefetchScalarGridSpec(
            num_scalar_prefetch=3,
            grid=(n // tile, nf),
            in_specs=[
                pl.BlockSpec(memory_space=pl.ANY),
                pl.BlockSpec((1, 1, d, tf), up),
                pl.BlockSpec((1, 1, d, tf), up),
                pl.BlockSpec((1, 1, tf, d), down),
            ],
            out_specs=pl.BlockSpec((tile, d), lambda t, f, te, nu, sr: (t, 0)),
            scratch_shapes=[pltpu.VMEM((2, tile, d), F32), pltpu.VMEM((tile, d), BF16),
                            pltpu.SemaphoreType.DMA((2,))],
        ),
        out_shape=jax.ShapeDtypeStruct((n, d), F32),
        compiler_params=_cparams(2),
        name="moe_ffn",
    )(tile_expert, n_used, src, h, w1, w3, w2)


def _combine_kernel(pos_ref, ys_hbm, x_ref, route_ref, mod_ref, o_ref, y1_scr, y2_scr, sem):
    base = 2 * pl.program_id(0) * x_ref.shape[0]
    _gather_rows(pos_ref, base, 2, ys_hbm, y1_scr, sem, wait=False)
    _gather_rows(pos_ref, base + 1, 2, ys_hbm, y2_scr, sem, wait=False)
    _gather_rows(pos_ref, base, 2, ys_hbm, y1_scr, sem, wait=True)
    _gather_rows(pos_ref, base + 1, 2, ys_hbm, y2_scr, sem, wait=True)
    mix = route_ref[:, 2:3] * y1_scr[...] + route_ref[:, 3:4] * y2_scr[...]
    o_ref[...] = x_ref[...] + mod_ref[0, 5:6, :] * mix


def _moe_combine(x, mods, route, ys, pos, *, rows, seq, batch, tm=256):
    d = x.shape[1]
    return pl.pallas_call(
        _combine_kernel,
        grid_spec=pltpu.PrefetchScalarGridSpec(
            num_scalar_prefetch=1,
            grid=(rows // tm,),
            in_specs=[
                pl.BlockSpec(memory_space=pl.ANY),
                pl.BlockSpec((tm, d), lambda m, pos: (m, 0)),
                pl.BlockSpec((tm, LANE), lambda m, pos: (m, 0)),
                pl.BlockSpec((1, 6, d), lambda m, pos: (_group_of(m, tm, seq, batch), 0, 0)),
            ],
            out_specs=pl.BlockSpec((tm, d), lambda m, pos: (m, 0)),
            scratch_shapes=[pltpu.VMEM((tm, d), F32), pltpu.VMEM((tm, d), F32),
                            pltpu.SemaphoreType.DMA(())],
        ),
        out_shape=jax.ShapeDtypeStruct((rows, d), F32),
        compiler_params=_cparams(1),
        name="moe_combine",
    )(pos, ys, x, route, mods)


def _moe_top2(x, nw, mods, router, w1, w3, w2, layer, *, rows, seq, batch, tm, tile=768):
    tok = dict(rows=rows, seq=seq, batch=batch)
    h, route = _moe_route(x, nw, mods, router, tm=tm, **tok)
    pos, src, tile_expert, n_used = _moe_plan(route, w1.shape[1], tile)
    ys = _moe_ffn(h, src, tile_expert, n_used, w1, w3, w2, layer, tile=tile)
    return _moe_combine(x, mods, route, ys, pos, tm=min(tm, 256), **tok)


def _final_norm_kernel(x_ref, w_ref, o_ref):
    x = x_ref[...]
    ms = jnp.mean(x * x, axis=-1, keepdims=True)
    o_ref[...] = x * lax.rsqrt(ms + NORM_EPS) * w_ref[...]


def _final_norm(x, w, *, rows, tm=512):
    d = x.shape[1]
    return pl.pallas_call(
        _final_norm_kernel,
        grid=(rows // tm,),
        in_specs=[pl.BlockSpec((tm, d), lambda m: (m, 0)), pl.BlockSpec((1, d), lambda m: (0, 0))],
        out_specs=pl.BlockSpec((tm, d), lambda m: (m, 0)),
        out_shape=jax.ShapeDtypeStruct((rows, d), F32),
        compiler_params=_cparams(1),
        name="final_norm",
    )(x, w.reshape(1, d))


def _dft_angles(n, period):
    k = lax.broadcasted_iota(jnp.int32, (n, n), 0)
    s = lax.broadcasted_iota(jnp.int32, (n, n), 1)
    return ((k * s) % period).astype(F32) * (2.0 * math.pi / period)


def _hyena_dft(seq_len):
    n = 2 * seq_len
    half = seq_len // 2
    row = lax.broadcasted_iota(jnp.int32, (seq_len, seq_len), 0)
    col = lax.broadcasted_iota(jnp.int32, (seq_len, seq_len), 1)
    freq = lambda i: jnp.where(i < half, 2 * i, 2 * (i - half) + 1)
    to_angle = lambda prod: (prod % n).astype(F32) * (2.0 * math.pi / n)
    ang_f = to_angle(freq(row) * col)
    ang_i = to_angle(row * freq(col))
    fc = jnp.cos(ang_f)
    fs = jnp.where(row == 0, jnp.where(col % 2 == 0, 1.0, -1.0), jnp.sin(ang_f))
    ic = jnp.where(col == 0, 1.0 / n, 2.0 / n * jnp.cos(ang_i))
    isn = jnp.where(col == 0, jnp.where(row % 2 == 0, 1.0, -1.0) / n, 2.0 / n * jnp.sin(ang_i))
    return (fc.astype(BF16), fs.astype(BF16), ic.astype(BF16), isn.astype(BF16),
            _flip_matrix(half))


def _hyena_feats(seq_len):
    pos = np.arange(seq_len, dtype=np.float64)
    t = pos / max(seq_len - 1, 1)
    bands = np.linspace(1e-4, HY_BANDS - 1, HY_BANDS).astype(np.float32).astype(np.float64)
    ang = 2 * math.pi * pos[:, None] * bands[None, :] / seq_len
    feats = np.concatenate([t[:, None], np.cos(ang), -np.sin(ang)], axis=-1)
    out = np.zeros((seq_len, LANE), np.float32)
    out[:, :HY_POS_DIM] = feats
    return jnp.asarray(out)


def _split2(a):
    hi = a.astype(BF16)
    return hi, (a - hi.astype(F32)).astype(BF16)


def _dot3(a, b):
    a_hi, a_lo = _split2(a)
    b_hi, b_lo = _split2(b)
    return _bdot(a_hi, b_hi) + _bdot(a_hi, b_lo) + _bdot(a_lo, b_hi)


def _filter_kernel(feat_ref, w1_ref, b1_ref, w2_ref, b2_ref, fr_ref, w3f_ref, w3b_ref,
                   dcf_ref, dcb_ref, fc_ref, fs_ref, o_ref, h_scr):
    @pl.when((pl.program_id(1) == 0) & (pl.program_id(2) == 0))
    def _():
        hp = lax.Precision.HIGHEST
        h1 = jnp.sin(fr_ref[0, 0:1, :] * (jnp.dot(feat_ref[...], w1_ref[0], precision=hp,
                                                  preferred_element_type=F32) + b1_ref[0]))
        h_scr[...] = jnp.sin(fr_ref[0, 1:2, :] * (jnp.dot(h1, w2_ref[0], precision=hp,
                                                          preferred_element_type=F32) + b2_ref[0]))

    h = h_scr[...]
    t = feat_ref[:, 0:1]
    hf = _dot3(h, w3f_ref[0]) * jnp.exp(-t * dcf_ref[0])
    hb = _dot3(h, w3b_ref[0]) * jnp.exp(-t * dcb_ref[0])
    row = lax.broadcasted_iota(jnp.int32, hf.shape, 0)
    hb = jnp.where(row == 0, 0.0, hb)
    ss = jnp.sum(hf * hf + hb * hb, axis=0, keepdims=True)
    scale = lax.rsqrt(ss + NORM_EPS)
    even = hf * scale + hb * scale
    odd = hf * scale - hb * scale
    kc = _bdot(fc_ref[...], even.astype(BF16))
    ks = _bdot(fs_ref[...], odd.astype(BF16))
    sign = jnp.where(row % 2 == 0, 1.0, -1.0)
    nyq = jnp.sum(even * sign, axis=0, keepdims=True)
    o_ref[0, 0] = kc
    o_ref[0, 1] = jnp.where(row == 0, nyq, ks)


def _hyena_filters(seq_len, w1p, b1, w2, b2, w3, freq, decay, fc, fs, *, tn=256):
    depth = w1p.shape[0]
    hid = HY_FILTER_HIDDEN
    nb = HY_W // tn
    feats = _hyena_feats(seq_len)
    col = lambda d: (lambda l, o, c: (l, 0, (2 * o + d) * nb + c))
    lyr = lambda l, o, c: (l, 0, 0)
    return pl.pallas_call(
        _filter_kernel,
        grid=(depth, 2, nb),
        in_specs=[
            pl.BlockSpec((seq_len, LANE), lambda l, o, c: (0, 0)),
            pl.BlockSpec((1, LANE, hid), lyr),
            pl.BlockSpec((1, 1, hid), lyr),
            pl.BlockSpec((1, hid, hid), lyr),
            pl.BlockSpec((1, 1, hid), lyr),
            pl.BlockSpec((1, 2, hid), lyr),
            pl.BlockSpec((1, hid, tn), col(0)),
            pl.BlockSpec((1, hid, tn), col(1)),
            pl.BlockSpec((1, 1, tn), col(0)),
            pl.BlockSpec((1, 1, tn), col(1)),
            _resident((seq_len, seq_len), lambda l, o, c: (0, 0)),
            _resident((seq_len, seq_len), lambda l, o, c: (0, 0)),
        ],
        out_specs=pl.BlockSpec((1, 2, seq_len, tn), lambda l, o, c: (l, 0, 0, o * nb + c)),
        out_shape=jax.ShapeDtypeStruct((depth, 2, seq_len, 2 * HY_W), F32),
        scratch_shapes=[pltpu.VMEM((seq_len, hid), F32)],
        compiler_params=_cparams(3),
        name="hyena_filters",
    )(feats, w1p, b1.reshape(depth, 1, hid), w2, b2.reshape(depth, 1, hid), freq,
      w3, w3, decay.reshape(depth, 1, 4 * HY_W), decay.reshape(depth, 1, 4 * HY_W), fc, fs)


def _conv3(z, w_ref, b_ref):
    n = z.shape[0]
    row = lax.broadcasted_iota(jnp.int32, z.shape, 0)
    prev = jnp.where(row == 0, 0.0, pltpu.roll(z, 1, 0))
    nxt = jnp.where(row == n - 1, 0.0, pltpu.roll(z, n - 1, 0))
    return prev * w_ref[0:1, :] + z * w_ref[1:2, :] + nxt * w_ref[2:3, :] + b_ref[...]


def _longconv_kernel(a_ref, g_ref, aw_ref, ab_ref, gw_ref, gb_ref, kf_ref, skip_ref,
                     fc_ref, fs_ref, ic_ref, is_ref, flip_ref, o_ref, *, conv_a):
    a = a_ref[...].astype(F32)
    if conv_a:
        a = _conv3(a, aw_ref, ab_ref)
    n = a.shape[0]
    h = n // 2
    lo = slice(0, h)
    hi = slice(h, n)
    k = lax.broadcasted_iota(jnp.int32, (h, a.shape[1]), 0)
    alt = jnp.where(k % 2 == 0, 1.0, -1.0)
    first = k == 0
    a_lo = a[lo]
    a_hi = a[hi]
    a_mid = a_hi[0:1, :]
    rev = _bdot(flip_ref[...], a_hi.astype(BF16))
    even_in = (a_lo + rev).astype(BF16)
    odd_in = (a_lo - rev).astype(BF16)
    nyq = jnp.sum(alt * (a_lo + a_hi), axis=0, keepdims=True)
    ac_e = _bdot(fc_ref[lo, lo], even_in) + alt * a_mid
    as_e = jnp.where(first, nyq, _bdot(fs_ref[lo, lo], odd_in))
    ac_o = _bdot(fc_ref[hi, lo], odd_in)
    as_o = _bdot(fs_ref[hi, lo], even_in) + alt * a_mid
    kc_e = kf_ref[0, 0, lo, :]
    ks_e = kf_ref[0, 1, lo, :]
    kc_o = kf_ref[0, 0, hi, :]
    ks_o = kf_ref[0, 1, hi, :]
    cross = as_e * ks_e
    yc_e = jnp.where(first, ac_e * kc_e, ac_e * kc_e - cross)
    ys_e = jnp.where(first, 0.0, ac_e * ks_e + as_e * kc_e)
    y_nyq = cross[0:1, :]
    yc_o = ac_o * kc_o - as_o * ks_o
    ys_o = ac_o * ks_o + as_o * kc_o
    sym = (_bdot(ic_ref[lo, lo], yc_e.astype(BF16)) + _bdot(is_ref[lo, hi], ys_o.astype(BF16))
           + alt * (y_nyq * (0.5 / n)))
    anti = _bdot(ic_ref[lo, hi], yc_o.astype(BF16)) + _bdot(is_ref[lo, lo], ys_e.astype(BF16))
    y_mid = (jnp.sum(alt * (yc_e + ys_o), axis=0, keepdims=True) * (1.0 / n)
             - yc_e[0:1, :] * (0.5 / n) + y_nyq * (0.5 / n))
    y_hi = jnp.where(first, y_mid, _bdot(flip_ref[...], (sym - anti).astype(BF16)))
    conv = jnp.concatenate([sym + anti, y_hi], axis=0)
    gate = _conv3(g_ref[...].astype(F32), gw_ref, gb_ref)
    o_ref[...] = (gate * (conv + a * skip_ref[0])).astype(o_ref.dtype)


def _hyena_longconv(a_src, a_part, zm, gate_part, conv_w, conv_b, kf, layer, order, skip,
                    dft, *, seq_len, n_seq, row0, out_rows, prev_out=None, tn=256):
    fc, fs, ic, isn, flip = dft
    assert seq_len % 4 == 0
    nb = HY_W // tn
    rb0 = row0 // seq_len
    const = lambda c, b: (0, 0)
    conv_a = a_part is not None
    aw_blk = a_part * nb if conv_a else 0
    a_blk = ZM_HY // tn + aw_blk if conv_a else 0
    gw_blk = gate_part * nb
    g_blk = ZM_HY // tn + gw_blk
    kernel = functools.partial(_longconv_kernel, conv_a=conv_a)
    in_specs = [
        pl.BlockSpec((seq_len, tn), lambda c, b: (rb0 + b, a_blk + c)),
        pl.BlockSpec((seq_len, tn), lambda c, b: (rb0 + b, g_blk + c)),
        pl.BlockSpec((3, tn), lambda c, b: (0, aw_blk + c)),
        pl.BlockSpec((1, tn), lambda c, b: (0, aw_blk + c)),
        pl.BlockSpec((3, tn), lambda c, b: (0, gw_blk + c)),
        pl.BlockSpec((1, tn), lambda c, b: (0, gw_blk + c)),
        _resident((1, 2, seq_len, tn), lambda c, b: (layer, 0, 0, order * nb + c)),
        pl.BlockSpec((1, 1, tn), lambda c, b: (order, 0, c)),
        _resident((seq_len, seq_len), const),
        _resident((seq_len, seq_len), const),
        _resident((seq_len, seq_len), const),
        _resident((seq_len, seq_len), const),
        _resident((seq_len // 2, seq_len // 2), const),
    ]
    args = [a_src, zm, conv_w, conv_b.reshape(1, -1), conv_w, conv_b.reshape(1, -1), kf,
            skip.reshape(2, 1, HY_W), fc, fs, ic, isn, flip]
    aliases = {}
    if prev_out is not None:
        in_specs.append(pl.BlockSpec(memory_space=pl.ANY))
        args.append(prev_out)
        aliases = {len(args) - 1: 0}
        kernel = functools.partial(_drop_last_input, kernel, 13)
    return pl.pallas_call(
        kernel,
        grid=(nb, n_seq),
        in_specs=in_specs,
        out_specs=pl.BlockSpec((seq_len, tn), lambda c, b: (rb0 + b, c)),
        out_shape=jax.ShapeDtypeStruct((out_rows, HY_W), BF16),
        input_output_aliases=aliases,
        compiler_params=_cparams(2),
        name="hyena_longconv",
    )(*args)


def _drop_last_input(kernel, n_in, *refs):
    return kernel(*refs[:n_in], *refs[n_in + 1:])


def _flip_matrix(half):
    r = lax.broadcasted_iota(jnp.int32, (half, half), 0)
    c = lax.broadcasted_iota(jnp.int32, (half, half), 1)
    return jnp.where((r >= 1) & (c == half - r), 1.0, 0.0).astype(BF16)


def _fnet_scale(seq_len):
    return 1.0 / math.sqrt(seq_len * FN_GROUP_W)


def _fnet_dft(seq_len):
    half = seq_len // 2
    ang_l = _dft_angles(seq_len, seq_len)[:half]
    ang_w = _dft_angles(FN_GROUP_W, FN_GROUP_W)
    scale = _fnet_scale(seq_len)
    return ((jnp.cos(ang_l) * scale).astype(BF16), (jnp.sin(ang_l) * scale).astype(BF16),
            jnp.cos(ang_w).astype(BF16), jnp.sin(ang_w).astype(BF16), _flip_matrix(half))


def _fnet_kernel(z_ref, cl_ref, sl_ref, cw_ref, sw_ref, flip_ref, o_ref, *, scale):
    z = z_ref[...]
    gc, gs = [], []
    for j in range(z.shape[1] // FN_GROUP_W):
        zj = z[:, j * FN_GROUP_W:(j + 1) * FN_GROUP_W]
        gc.append(_bdot(zj, cw_ref[...]).astype(BF16))
        gs.append(_bdot(zj, sw_ref[...]).astype(BF16))
    gc = jnp.concatenate(gc, axis=-1)
    gs = jnp.concatenate(gs, axis=-1)
    p = _bdot(cl_ref[...], gc)
    q = _bdot(sl_ref[...], gs)
    pos = lax.broadcasted_iota(jnp.int32, gc.shape, 0)
    y_mid = scale * jnp.sum(jnp.where(pos % 2 == 0, 1.0, -1.0) * gc.astype(F32), axis=0,
                            keepdims=True)
    first = lax.broadcasted_iota(jnp.int32, p.shape, 0) == 0
    y_hi = jnp.where(first, y_mid, _bdot(flip_ref[...], (p + q).astype(BF16)))
    o_ref[...] = jnp.concatenate([p - q, y_hi], axis=0).astype(o_ref.dtype)


def _fnet(zm, dft, *, seq_len, n_seq, row0, out_rows, prev_out=None, tn=512):
    cl, sl, cw, sw, flip = dft
    nb = FN_W // tn
    rb0 = row0 // seq_len
    c0 = ZM_FN // tn
    half = seq_len // 2
    const = lambda c, b: (0, 0)
    in_specs = [
        pl.BlockSpec((seq_len, tn), lambda c, b: (rb0 + b, c0 + c)),
        _resident((half, seq_len), const),
        _resident((half, seq_len), const),
        pl.BlockSpec((FN_GROUP_W, FN_GROUP_W), const),
        pl.BlockSpec((FN_GROUP_W, FN_GROUP_W), const),
        _resident((half, half), const),
    ]
    args = [zm, cl, sl, cw, sw, flip]
    kernel = functools.partial(_fnet_kernel, scale=_fnet_scale(seq_len))
    aliases = {}
    if prev_out is not None:
        in_specs.append(pl.BlockSpec(memory_space=pl.ANY))
        args.append(prev_out)
        aliases = {len(args) - 1: 0}
        kernel = functools.partial(_drop_last_input, kernel, 6)
    return pl.pallas_call(
        kernel,
        grid=(nb, n_seq),
        in_specs=in_specs,
        out_specs=pl.BlockSpec((seq_len, tn), lambda c, b: (rb0 + b, c)),
        out_shape=jax.ShapeDtypeStruct((out_rows, FN_W), BF16),
        input_output_aliases=aliases,
        compiler_params=_cparams(2),
        name="fnet",
    )(*args)


def _hgrn_tables(reverse):
    c = HG_CHUNK
    tri = np.zeros((c, c), np.float32)
    for t in range(c):
        if reverse:
            tri[t, t:] = 1.0
        else:
            tri[t, :t + 1] = 1.0
    mats = [tri]
    masks = []
    for lvl in range(HG_LEVELS):
        half = (c // 2) >> lvl
        sel = np.zeros((c, c), np.float32)
        mask = np.zeros((c, c), np.float32)
        for t in range(c):
            start = (t // (2 * half)) * 2 * half
            mid = start + half
            sel[t, mid if reverse else mid - 1] = 1.0
            for s in range(start, start + 2 * half):
                if reverse and t < mid <= s:
                    mask[t, s] = 1.0
                if (not reverse) and s < mid <= t:
                    mask[t, s] = 1.0
        mats.append(tri - sel @ tri)
        masks.append(mask)
    half = c // 2
    sel = np.zeros((c, c), np.float32)
    mask = np.zeros((c, c), np.float32)
    for t in range(c):
        start = (t // half) * half
        sel[t, start + half // 2] = 1.0
        for s in range(start, start + half):
            if (s >= t) if reverse else (s <= t):
                mask[t, s] = 1.0
    mats.append(tri - sel @ tri)
    masks.append(mask)
    mats = [mats[0], mats[1], mats[-1]] + mats[2:-1]
    return np.concatenate(mats, axis=0), np.stack(masks, axis=0)


HG_BOUNDED_BLOCKS = 3
HG_MID_BLOCK = 2


def _hgrn_level_block(lvl):
    return 1 if lvl == 0 else HG_BOUNDED_BLOCKS + lvl - 1


def _hgrn_decays(zf_ref, lb_ref, tab_ref, k_scr, p_scr, dec_scr):
    lb = lb_ref[...]
    f = lb + (1.0 - lb) * jax.nn.sigmoid(zf_ref[...])
    lf = jnp.log2(f)
    k_scr[...] = (1.0 - f).astype(BF16)
    p1 = lf.astype(BF16)
    r1 = lf - p1.astype(F32)
    p2 = r1.astype(BF16)
    p3 = (r1 - p2.astype(F32)).astype(BF16)
    p_scr[...] = jnp.concatenate([p1, p2, p3], axis=0)
    n = HG_BOUNDED_BLOCKS * HG_CHUNK
    dec_scr[0:n, :] = _bdot(tab_ref[0:n, :], p_scr[...])


def _hgrn_more_decays(tab_ref, p_scr, dec_scr):
    n = HG_BOUNDED_BLOCKS * HG_CHUNK
    dec_scr[n:, :] = _bdot(tab_ref[n:, :], p_scr[...])


def _nt_dot(a, b):
    return lax.dot_general(a, b, (((1,), (1,)), ((), ())), preferred_element_type=F32)


def _hgrn_direction(q_ref, v_ref, k_scr, dec_scr, mask_ref, st_scr, o_ref, end_row, bounded):
    c = HG_CHUNK
    heads = lambda ref, r0: jnp.stack(
        [ref[r0:r0 + c, h * HG_DK:(h + 1) * HG_DK] for h in range(HG_HEADS)], axis=0)
    bmm = lambda eq, a, b: jnp.einsum(eq, a, b, preferred_element_type=F32)
    q = heads(q_ref, 0)
    k = heads(k_scr, 0)
    v = heads(v_ref, 0)
    b = heads(dec_scr, 0)
    st = st_scr[...]
    o = bmm('htk,hkv->htv', q * jnp.exp2(b).astype(BF16), st.astype(BF16))
    level = lambda l: jnp.exp2(-jnp.abs(heads(dec_scr, _hgrn_level_block(l) * c))).astype(BF16)
    if bounded:
        e = level(0)
        att = jnp.where(mask_ref[0] > 0.5, bmm('htk,hsk->hts', q * e, k * e), 0.0)
        d = heads(dec_scr, HG_MID_BLOCK * c)
        inner = bmm('htk,hsk->hts', q * jnp.exp2(d).astype(BF16), k * jnp.exp2(-d).astype(BF16))
        att += jnp.where(mask_ref[HG_LEVELS] > 0.5, inner, 0.0)
    else:
        diag = jnp.sum(q.astype(F32) * k.astype(F32), axis=-1, keepdims=True)
        o += diag * v.astype(F32)
        att = jnp.zeros((HG_HEADS, c, c), F32)
        for lvl in range(HG_LEVELS):
            e = level(lvl)
            att += bmm('htk,hsk->hts', q * e, k * e) * mask_ref[lvl]
    o += bmm('hts,hsv->htv', att.astype(BF16), v)
    for h in range(HG_HEADS):
        o_ref[:, h * HG_DK:(h + 1) * HG_DK] = o[h]
    b_end = b[:, end_row:end_row + 1, :]
    upd = bmm('htk,htv->hkv', k * jnp.exp2(b_end - b).astype(BF16), v)
    decay = jnp.swapaxes(jnp.broadcast_to(jnp.exp2(b_end), (HG_HEADS, HG_DK, HG_DK)), 1, 2)
    st_scr[...] = decay * st + upd


HG_MAX_LOG2_SPAN = 100.0
HG_MAX_ABS_Q = 1e6


def _hgrn_kernel(qf_ref, vf_ref, zff_ref, qb_ref, vb_ref, zfb_ref, lbf_ref, lbb_ref,
                 tabf_ref, maskf_ref, tabb_ref, maskb_ref, s0f_ref, s0b_ref,
                 of_ref, ob_ref, sf_ref, sb_ref, stf_scr, stb_scr, kf_scr, kb_scr,
                 pf_scr, pb_scr, decf_scr, decb_scr):
    ci = pl.program_id(1)
    c = HG_CHUNK

    @pl.when(ci == 0)
    def _():
        stf_scr[...] = s0f_ref[0]
        stb_scr[...] = s0b_ref[0]

    _hgrn_decays(zff_ref, lbf_ref, tabf_ref, kf_scr, pf_scr, decf_scr)
    _hgrn_decays(zfb_ref, lbb_ref, tabb_ref, kb_scr, pb_scr, decb_scr)
    mid = slice(HG_MID_BLOCK * c, (HG_MID_BLOCK + 1) * c)
    span = jnp.maximum(jnp.max(jnp.abs(decf_scr[mid, :])), jnp.max(jnp.abs(decb_scr[mid, :])))
    q_abs = jnp.maximum(jnp.max(jnp.abs(qf_ref[...].astype(F32))),
                        jnp.max(jnp.abs(qb_ref[...].astype(F32))))
    bounded = (span < HG_MAX_LOG2_SPAN) & (q_abs < HG_MAX_ABS_Q)

    def run(flag):
        _hgrn_direction(qf_ref, vf_ref, kf_scr, decf_scr, maskf_ref, stf_scr, of_ref, c - 1, flag)
        _hgrn_direction(qb_ref, vb_ref, kb_scr, decb_scr, maskb_ref, stb_scr, ob_ref, 0, flag)

    @pl.when(bounded)
    def _():
        run(True)

    @pl.when(jnp.logical_not(bounded))
    def _():
        _hgrn_more_decays(tabf_ref, pf_scr, decf_scr)
        _hgrn_more_decays(tabb_ref, pb_scr, decb_scr)
        run(False)

    @pl.when(ci == pl.num_programs(1) - 1)
    def _():
        sf_ref[0] = stf_scr[...]
        sb_ref[0] = stb_scr[...]


def _hgrn(zm, zf, lb_f, lb_b, s0_f, s0_b, *, seq_len, n_seq, row0, out_rows,
          prev_of=None, prev_ob=None):
    c = HG_CHUNK
    nc = seq_len // c
    rb0 = row0 // c
    w = HG_W
    tab_f, mask_f = _hgrn_tables(False)
    tab_b, mask_b = _hgrn_tables(True)
    fwd = lambda col: (lambda b, i: (rb0 + b * nc + i, col))
    bwd = lambda col: (lambda b, i: (rb0 + b * nc + nc - 1 - i, col))
    const2 = lambda b, i: (0, 0)
    const3 = lambda b, i: (0, 0, 0)
    state = lambda b, i: (b, 0, 0, 0)
    nt = (2 + HG_LEVELS) * c
    in_specs = [
        pl.BlockSpec((c, w), fwd(ZM_Q // w)),
        pl.BlockSpec((c, w), fwd(ZM_I // w)),
        pl.BlockSpec((c, w), fwd(0)),
        pl.BlockSpec((c, w), bwd(ZM_Q // w)),
        pl.BlockSpec((c, w), bwd(ZM_I // w)),
        pl.BlockSpec((c, w), bwd(1)),
        pl.BlockSpec((1, w), const2),
        pl.BlockSpec((1, w), const2),
        pl.BlockSpec((nt, 3 * c), const2),
        pl.BlockSpec((HG_LEVELS + 1, c, c), const3),
        pl.BlockSpec((nt, 3 * c), const2),
        pl.BlockSpec((HG_LEVELS + 1, c, c), const3),
        pl.BlockSpec((1, HG_HEADS, HG_DK, HG_DK), state),
        pl.BlockSpec((1, HG_HEADS, HG_DK, HG_DK), state),
    ]
    args = [zm, zm, zf, zm, zm, zf, lb_f.reshape(1, w), lb_b.reshape(1, w),
            jnp.asarray(np.tile(tab_f, (1, 3)), BF16), jnp.asarray(mask_f),
            jnp.asarray(np.tile(tab_b, (1, 3)), BF16), jnp.asarray(mask_b), s0_f, s0_b]
    kernel = _hgrn_kernel
    aliases = {}
    if prev_of is not None:
        in_specs += [pl.BlockSpec(memory_space=pl.ANY), pl.BlockSpec(memory_space=pl.ANY)]
        args += [prev_of, prev_ob]
        aliases = {14: 0, 15: 1}
        kernel = functools.partial(_hgrn_alias_kernel, 14)
    st_shape = jax.ShapeDtypeStruct((n_seq, HG_HEADS, HG_DK, HG_DK), F32)
    return pl.pallas_call(
        kernel,
        grid=(n_seq, nc),
        in_specs=in_specs,
        out_specs=[
            pl.BlockSpec((c, w), fwd(0)),
            pl.BlockSpec((c, w), bwd(0)),
            pl.BlockSpec((1, HG_HEADS, HG_DK, HG_DK), state),
            pl.BlockSpec((1, HG_HEADS, HG_DK, HG_DK), state),
        ],
        out_shape=[jax.ShapeDtypeStruct((out_rows, w), F32),
                   jax.ShapeDtypeStruct((out_rows, w), F32), st_shape, st_shape],
        scratch_shapes=[pltpu.VMEM((HG_HEADS, HG_DK, HG_DK), F32),
                        pltpu.VMEM((HG_HEADS, HG_DK, HG_DK), F32),
                        pltpu.VMEM((c, w), BF16), pltpu.VMEM((c, w), BF16),
                        pltpu.VMEM((3 * c, w), BF16), pltpu.VMEM((3 * c, w), BF16),
                        pltpu.VMEM((nt, w), F32), pltpu.VMEM((nt, w), F32)],
        input_output_aliases=aliases,
        compiler_params=_cparams(2),
        name="hgrn2",
    )(*args)


def _hgrn_alias_kernel(n_in, *refs):
    return _hgrn_kernel(*refs[:n_in], *refs[n_in + 2:])


def kernel(x, c, ctx, c_ctx, w_mod, b_mod, norm1_w, norm2_w, w_in, hy_conv_w, hy_conv_b,
           hy_f_w1, hy_f_b1, hy_f_w2, hy_f_b2, hy_f_w3, hy_f_freq, hy_decay, hy_skip,
           hg_lower, hg_norm_w, w_br_hy, w_br_fn, w_br_hg, w_out,
           ffn_w1, ffn_w3, ffn_w2, moe_router, moe_w1, moe_w3, moe_w2, final_norm_w):
    batch, seq, d = x.shape
    l_ctx = ctx.shape[1]
    depth = w_in.shape[0]
    n_lat = batch * seq
    n_all = n_lat + batch * l_ctx

    lb_all = jnp.cumsum(jax.nn.softmax(hg_lower.astype(F32), axis=1), axis=1)
    lb_all = lb_all - lb_all[:, :1]

    cc = jnp.zeros((8, d), F32).at[:batch].set(c).at[batch].set(c_ctx)
    mods = _modulation(cc, w_mod, b_mod).reshape(depth, 8, 6, d)

    w1p = jnp.zeros((depth, LANE, HY_FILTER_HIDDEN), F32).at[:, :HY_POS_DIM].set(hy_f_w1)
    dft_lat = _hyena_dft(seq)
    dft_ctx = _hyena_dft(l_ctx)
    kf_lat = _hyena_filters(seq, w1p, hy_f_b1, hy_f_w2, hy_f_b2, hy_f_w3, hy_f_freq, hy_decay,
                            dft_lat[0], dft_lat[1])
    kf_ctx = _hyena_filters(l_ctx, w1p, hy_f_b1, hy_f_w2, hy_f_b2, hy_f_w3, hy_f_freq, hy_decay,
                            dft_ctx[0], dft_ctx[1])
    fn_lat = _fnet_dft(seq)
    fn_ctx = _fnet_dft(l_ctx)

    rows_x = jnp.concatenate([x.reshape(n_lat, d), ctx.reshape(batch * l_ctx, d)], axis=0)
    zero_state = jnp.zeros((batch, HG_HEADS, HG_DK, HG_DK), F32)
    tile_cap = math.gcd(seq, batch * l_ctx)
    tok = dict(seq=seq, batch=batch)
    tile = lambda t: dict(tm=min(t, tile_cap))
    lat = dict(seq_len=seq, n_seq=batch, row0=0, out_rows=n_all)
    cx = dict(seq_len=l_ctx, n_seq=batch, row0=n_lat, out_rows=n_all)

    for l in range(depth):
        last = l == depth - 1
        rows = n_lat if last else n_all
        m = mods[l]

        h = _norm_rows(rows_x, norm1_w[l], m, **tok, **tile(512))
        zm, zf = _in_proj(h, w_in, l, **tile(1024))

        of, ob, s_f, s_b = _hgrn(zm, zf, lb_all[0, l], lb_all[1, l], zero_state, zero_state, **cx)
        of, ob, _, _ = _hgrn(zm, zf, lb_all[0, l], lb_all[1, l], s_f, s_b, prev_of=of,
                             prev_ob=ob, **lat)

        hy_args = (hy_conv_w[l], hy_conv_b[l])
        mix = dict(lat, out_rows=rows)
        u = _hyena_longconv(zm, 0, zm, 1, *hy_args, kf_lat, l, 0, hy_skip[l], dft_lat, **mix)
        y_hy = _hyena_longconv(u, None, zm, 2, *hy_args, kf_lat, l, 1, hy_skip[l], dft_lat, **mix)
        y_fn = _fnet(zm, fn_lat, **mix)
        if not last:
            u = _hyena_longconv(zm, 0, zm, 1, *hy_args, kf_ctx, l, 0, hy_skip[l], dft_ctx,
                                prev_out=u, **cx)
            y_hy = _hyena_longconv(u, None, zm, 2, *hy_args, kf_ctx, l, 1, hy_skip[l], dft_ctx,
                                   prev_out=y_hy, **cx)
            y_fn = _fnet(zm, fn_ctx, prev_out=y_fn, **cx)

        rows_x = _merge_out(y_hy, y_fn, of, ob, zm, rows_x, hg_norm_w[l], m,
                            w_br_hy[l].astype(BF16), w_br_fn[l].astype(BF16),
                            w_br_hg[l].astype(BF16), w_out[l].astype(BF16), rows=rows,
                            **tok, **tile(256))

        if l % 2 == 0:
            i = l // 2
            rows_x = _ffn_dense(rows_x, norm2_w[l], m, ffn_w1, ffn_w3, ffn_w2, i, rows=rows,
                                **tok, **tile(1024))
        else:
            i = l // 2
            rows_x = _moe_top2(rows_x, norm2_w[l], m, moe_router[i], moe_w1, moe_w3, moe_w2, i,
                               rows=rows, **tok, **tile(512))

    out = _final_norm(rows_x, final_norm_w, rows=n_lat, **tile(512))
    return out.reshape(batch, seq, d)
```

```python
import functools
import math

import numpy as np
import jax
import jax.numpy as jnp
from jax import lax
from jax.experimental import pallas as pl
from jax.experimental.pallas import tpu as pltpu

F32 = jnp.float32
BF16 = jnp.bfloat16
NORM_EPS = 1e-6

D_MODEL = 2048
DEPTH = 4
HY_W = 1024
HY_BANDS = 16
HY_POS_DIM = 2 * HY_BANDS + 1
HY_FILTER_HIDDEN = 64
FN_W = 1024
FN_GROUP_W = 128
HG_W = 1024
HG_HEADS = 8
HG_DK = 128
HG_CHUNK = 64
HG_LEVELS = 6
HG_STEP_CHUNKS = 2
N_EXPERTS = 8
LANE = 128

HG_FF = 3 * HY_W + FN_W + 2 * HG_W
HG_I = HG_FF + 2 * HG_W
ZM_HY = 0
ZM_FN = ZM_HY + 3 * HY_W
ZM_Q = ZM_FN + FN_W
ZM_G = ZM_Q + HG_W
ZM_I = HG_I
ZM_GATE = HG_I + HG_W
ZM_COLS = ZM_GATE + 3 * D_MODEL

VMEM_LIMIT = 56 * 1024 * 1024


def _cparams(n_axes, vmem=VMEM_LIMIT):
    return pltpu.CompilerParams(dimension_semantics=("arbitrary",) * n_axes, vmem_limit_bytes=vmem)


def _resident(shape, index_map):
    return pl.BlockSpec(shape, index_map, pipeline_mode=pl.Buffered(1))


def _silu(v):
    return v * jax.nn.sigmoid(v)


def _bdot(a, b):
    return jnp.dot(a, b, preferred_element_type=F32)


def _mod_kernel(c_ref, w_ref, b_ref, o_ref):
    c = _silu(c_ref[...])
    o_ref[0] = jnp.dot(c, w_ref[0], preferred_element_type=F32,
                       precision=lax.Precision.HIGHEST) + b_ref[0]


def _modulation(cc, w_mod, b_mod):
    depth, d, n6 = w_mod.shape
    tn = 2048
    return pl.pallas_call(
        _mod_kernel,
        grid=(depth, n6 // tn),
        in_specs=[
            pl.BlockSpec((8, d), lambda l, n: (0, 0)),
            pl.BlockSpec((1, d, tn), lambda l, n: (l, 0, n)),
            pl.BlockSpec((1, 1, tn), lambda l, n: (l, 0, n)),
        ],
        out_specs=pl.BlockSpec((1, 8, tn), lambda l, n: (l, 0, n)),
        out_shape=jax.ShapeDtypeStruct((depth, 8, n6), F32),
        compiler_params=_cparams(2),
        name="modulation",
    )(cc, w_mod, b_mod.reshape(depth, 1, n6))


def _norm_mod(x, nw, shift, scale):
    ms = jnp.mean(x * x, axis=-1, keepdims=True)
    return (x * lax.rsqrt(ms + NORM_EPS) * nw) * (1 + scale) + shift


def _group_of(row_block, tm, seq, batch):
    return jnp.minimum((row_block * tm) // seq, batch)


def _norm_rows_kernel(x_ref, nw_ref, mod_ref, o_ref):
    h = _norm_mod(x_ref[...], nw_ref[...], mod_ref[0, 0:1, :], mod_ref[0, 1:2, :])
    o_ref[...] = h.astype(o_ref.dtype)


def _norm_rows(x, nw, mods, *, seq, batch, tm=512):
    rows, d = x.shape
    return pl.pallas_call(
        _norm_rows_kernel,
        grid=(rows // tm,),
        in_specs=[
            pl.BlockSpec((tm, d), lambda m: (m, 0)),
            pl.BlockSpec((1, d), lambda m: (0, 0)),
            pl.BlockSpec((1, 6, d), lambda m: (_group_of(m, tm, seq, batch), 0, 0)),
        ],
        out_specs=pl.BlockSpec((tm, d), lambda m: (m, 0)),
        out_shape=jax.ShapeDtypeStruct((rows, d), BF16),
        compiler_params=_cparams(1),
        name="norm_rows",
    )(x, nw.reshape(1, d), mods)


def _inproj_kernel(h_ref, w_ref, z_ref, zf_ref, w_scr, *, n_main):
    @pl.when(pl.program_id(1) == 0)
    def _():
        w_scr[...] = w_ref[0].astype(BF16)

    r = _bdot(h_ref[...], w_scr[...])

    @pl.when(pl.program_id(0) < n_main)
    def _():
        z_ref[...] = r.astype(z_ref.dtype)

    @pl.when(pl.program_id(0) >= n_main)
    def _():
        zf_ref[...] = r


def _in_proj(h, w_in, layer, *, tm=1024, tn=1024):
    rows, d = h.shape
    n_cols = w_in.shape[2]
    nb = n_cols // tn
    f0 = HG_FF // tn
    nf = 2 * HG_W // tn
    n_main = nb - nf
    nm = rows // tm
    col = lambda j: jnp.where(j < f0, j, jnp.where(j < n_main, j + nf, j - n_main + f0))
    z_idx = lambda j, m: (jnp.where(j < n_main, m, nm - 1), col(jnp.minimum(j, n_main - 1)))
    zf_idx = lambda j, m: (jnp.where(j < n_main, 0, m), jnp.maximum(j - n_main, 0))
    return pl.pallas_call(
        functools.partial(_inproj_kernel, n_main=n_main),
        grid=(nb, nm),
        in_specs=[
            pl.BlockSpec((tm, d), lambda j, m: (m, 0)),
            pl.BlockSpec((1, d, tn), lambda j, m: (layer, 0, col(j))),
        ],
        out_specs=[pl.BlockSpec((tm, tn), z_idx), pl.BlockSpec((tm, tn), zf_idx)],
        out_shape=[jax.ShapeDtypeStruct((rows, n_cols), BF16),
                   jax.ShapeDtypeStruct((rows, nf * tn), F32)],
        scratch_shapes=[pltpu.VMEM((d, tn), BF16)],
        compiler_params=_cparams(2),
        name="in_proj",
    )(h, w_in)


def _merge_kernel(yhy_ref, yfn_ref, of_ref, ob_ref, g_ref, ga0_ref, ga1_ref, gb0_ref, gb1_ref,
                  gc0_ref, gc1_ref, x_ref, hnw_ref, mod_ref,
                  wbhy_ref, wbfn_ref, wbhg_ref, wo_ref, o_ref):
    d = x_ref.shape[1]
    ohg = of_ref[...] + ob_ref[...]
    nw = hnw_ref[...]
    heads = []
    for h in range(ohg.shape[1] // HG_DK):
        oh = ohg[:, h * HG_DK:(h + 1) * HG_DK]
        ms = jnp.mean(oh * oh, axis=-1, keepdims=True)
        heads.append(oh * lax.rsqrt(ms + NORM_EPS) * nw)
    y_hg = jnp.concatenate(heads, axis=-1) * _silu(g_ref[...].astype(F32))
    gate = lambda lo, hi: jax.nn.sigmoid(
        jnp.concatenate([lo[...], hi[...]], axis=-1).astype(F32))
    merged = gate(ga0_ref, ga1_ref) * _bdot(yhy_ref[...], wbhy_ref[...])
    merged += gate(gb0_ref, gb1_ref) * _bdot(yfn_ref[...], wbfn_ref[...])
    merged += gate(gc0_ref, gc1_ref) * _bdot(y_hg.astype(BF16), wbhg_ref[...])
    out = _bdot(merged.astype(BF16), wo_ref[...])
    o_ref[...] = x_ref[...] + mod_ref[0, 2:3, :] * out


def _merge_out(y_hy, y_fn, o_f, o_b, zm, x, hg_nw, mods, wb_hy, wb_fn, wb_hg, w_o,
               *, rows, seq, batch, tm=256):
    d = x.shape[1]
    w = y_hy.shape[1]
    tm = min(tm, rows)
    row = lambda m: (m, 0)
    const = lambda m: (0, 0)
    gate0 = ZM_GATE // (d // 2)
    return pl.pallas_call(
        _merge_kernel,
        grid=(rows // tm,),
        in_specs=[
            pl.BlockSpec((tm, w), row),
            pl.BlockSpec((tm, w), row),
            pl.BlockSpec((tm, w), row),
            pl.BlockSpec((tm, w), row),
            pl.BlockSpec((tm, w), lambda m: (m, ZM_G // w)),
            *[pl.BlockSpec((tm, d // 2), functools.partial(lambda i, m: (m, i), gate0 + i))
              for i in range(6)],
            pl.BlockSpec((tm, d), row),
            pl.BlockSpec((1, HG_DK), const),
            pl.BlockSpec((1, 6, d), lambda m: (_group_of(m, tm, seq, batch), 0, 0)),
            _resident((w, d), const),
            _resident((w, d), const),
            _resident((w, d), const),
            _resident((d, d), const),
        ],
        out_specs=pl.BlockSpec((tm, d), row),
        out_shape=jax.ShapeDtypeStruct((rows, d), F32),
        compiler_params=_cparams(1),
        name="merge_out",
    )(y_hy, y_fn, o_f, o_b, zm, *([zm] * 6), x, hg_nw.reshape(1, HG_DK), mods,
      wb_hy, wb_fn, wb_hg, w_o)


def _ffn_kernel(x_ref, nw_ref, mod_ref, w1_ref, w3_ref, w2_ref, o_ref, h_scr):
    f = pl.program_id(1)

    @pl.when(f == 0)
    def _():
        h = _norm_mod(x_ref[...], nw_ref[...], mod_ref[0, 3:4, :], mod_ref[0, 4:5, :])
        h_scr[...] = h.astype(BF16)
        o_ref[...] = jnp.zeros_like(o_ref)

    h = h_scr[...]
    act = _silu(_bdot(h, w1_ref[0].astype(BF16))) * _bdot(h, w3_ref[0].astype(BF16))
    o_ref[...] += _bdot(act.astype(BF16), w2_ref[0].astype(BF16))

    @pl.when(f == pl.num_programs(1) - 1)
    def _():
        o_ref[...] = x_ref[...] + mod_ref[0, 5:6, :] * o_ref[...]


def _ffn_dense(x, nw, mods, w1, w3, w2, layer, *, rows, seq, batch, tm=1024, tf=256):
    d = x.shape[1]
    ff = w1.shape[2]
    tm = min(tm, rows)
    return pl.pallas_call(
        _ffn_kernel,
        grid=(rows // tm, ff // tf),
        in_specs=[
            _resident((tm, d), lambda m, f: (m, 0)),
            pl.BlockSpec((1, d), lambda m, f: (0, 0)),
            pl.BlockSpec((1, 6, d), lambda m, f: (_group_of(m, tm, seq, batch), 0, 0)),
            pl.BlockSpec((1, d, tf), lambda m, f: (layer, 0, f)),
            pl.BlockSpec((1, d, tf), lambda m, f: (layer, 0, f)),
            pl.BlockSpec((1, tf, d), lambda m, f: (layer, f, 0)),
        ],
        out_specs=pl.BlockSpec((tm, d), lambda m, f: (m, 0)),
        out_shape=jax.ShapeDtypeStruct((rows, d), F32),
        scratch_shapes=[pltpu.VMEM((tm, d), BF16)],
        compiler_params=_cparams(2),
        name="ffn_dense",
    )(x, nw.reshape(1, d), mods, w1, w3, w2)


def _top2(logits, n_experts):
    lane = lax.broadcasted_iota(jnp.int32, logits.shape, 1).astype(F32)
    neg = jnp.float32(-jnp.inf)
    lg = jnp.where(lane < n_experts, logits, neg)
    m1 = jnp.max(lg, axis=-1, keepdims=True)
    i1 = jnp.min(jnp.where(lg == m1, lane, float(LANE)), axis=-1, keepdims=True)
    lg2 = jnp.where(lane == i1, neg, lg)
    m2 = jnp.max(lg2, axis=-1, keepdims=True)
    i2 = jnp.min(jnp.where(lg2 == m2, lane, float(LANE)), axis=-1, keepdims=True)
    e2 = jnp.exp(m2 - m1)
    return i1, i2, 1.0 / (1.0 + e2), e2 / (1.0 + e2)


def _route_kernel(x_ref, nw_ref, mod_ref, r_ref, h_ref, route_ref):
    h = _norm_mod(x_ref[...], nw_ref[...], mod_ref[0, 3:4, :], mod_ref[0, 4:5, :])
    logits = jnp.dot(h, r_ref[...], preferred_element_type=F32, precision=lax.Precision.HIGHEST)
    i1, i2, p1, p2 = _top2(logits, N_EXPERTS)
    lane = lax.broadcasted_iota(jnp.int32, logits.shape, 1)
    route = jnp.where(lane == 0, i1, jnp.where(lane == 1, i2,
                      jnp.where(lane == 2, p1, jnp.where(lane == 3, p2, 0.0))))
    h_ref[...] = h
    route_ref[...] = route


def _moe_route(x, nw, mods, router, *, rows, seq, batch, tm=512):
    d = x.shape[1]
    rpad = jnp.zeros((d, LANE), F32).at[:, :router.shape[1]].set(router)
    return pl.pallas_call(
        _route_kernel,
        grid=(rows // tm,),
        in_specs=[
            pl.BlockSpec((tm, d), lambda m: (m, 0)),
            pl.BlockSpec((1, d), lambda m: (0, 0)),
            pl.BlockSpec((1, 6, d), lambda m: (_group_of(m, tm, seq, batch), 0, 0)),
            pl.BlockSpec((d, LANE), lambda m: (0, 0)),
        ],
        out_specs=[pl.BlockSpec((tm, d), lambda m: (m, 0)),
                   pl.BlockSpec((tm, LANE), lambda m: (m, 0))],
        out_shape=[jax.ShapeDtypeStruct((rows, d), F32), jax.ShapeDtypeStruct((rows, LANE), F32)],
        compiler_params=_cparams(1),
        name="moe_route",
    )(x, nw.reshape(1, d), mods, rpad)


def _moe_plan(route, n_experts, tile):
    rows = route.shape[0]
    e = route[:, :2].astype(jnp.int32).reshape(-1)
    onehot = (e[:, None] == jnp.arange(n_experts, dtype=jnp.int32)[None, :]).astype(jnp.int32)
    before = jnp.cumsum(onehot, axis=0) - onehot
    rank = jnp.sum(before * onehot, axis=1)
    counts = jnp.sum(onehot, axis=0)
    padded = ((counts + tile - 1) // tile) * tile
    ends = jnp.cumsum(padded)
    starts = ends - padded
    pos = jnp.sum(starts[None, :] * onehot, axis=1) + rank
    n_tiles = -(-2 * rows // tile) + n_experts
    src = jnp.zeros((n_tiles * tile,), jnp.int32).at[pos].set(
        jnp.arange(2 * rows, dtype=jnp.int32) // 2)
    tile_start = jnp.arange(n_tiles, dtype=jnp.int32) * tile
    tile_expert = jnp.sum((tile_start[:, None] >= ends[None, :]).astype(jnp.int32), axis=1)
    tile_expert = jnp.minimum(tile_expert, n_experts - 1)
    n_used = (ends[-1] // tile).reshape(1)
    return pos, src, tile_expert, n_used


def _row_copy(src_hbm, src_row, dst, dst_row, sem):
    return pltpu.make_async_copy(src_hbm.at[pl.ds(src_row, 1)], dst.at[pl.ds(dst_row, 1)], sem)


def _gather_rows(idx_ref, base, stride, src_hbm, dst, sem, *, wait):
    if wait:
        pltpu.make_async_copy(src_hbm.at[pl.ds(0, dst.shape[0])], dst, sem).wait()
        return

    def body(r, carry):
        _row_copy(src_hbm, idx_ref[base + stride * r], dst, r, sem).start()
        return carry

    lax.fori_loop(0, dst.shape[0], body, 0, unroll=8)


def _moe_ffn_kernel(te_ref, nu_ref, src_ref, h_hbm, w1_ref, w3_ref, w2_ref, o_ref,
                    rows_scr, h_scr, sems):
    t = pl.program_id(0)
    f = pl.program_id(1)
    last = pl.num_programs(1) - 1
    tile = o_ref.shape[0]
    n_used = nu_ref[0]
    used = t < n_used
    slot = t % 2

    def gather(tile_idx, buf, wait):
        _gather_rows(src_ref, tile_idx * tile, 1, h_hbm, rows_scr.at[buf], sems.at[buf], wait=wait)

    @pl.when(used & (f == 0) & (t == 0))
    def _():
        gather(t, slot, False)

    @pl.when(used & (f == 0))
    def _():
        gather(t, slot, True)
        h_scr[...] = rows_scr[slot].astype(BF16)
        o_ref[...] = jnp.zeros_like(o_ref)

    @pl.when((f == 0) & (t + 1 < n_used))
    def _():
        gather(t + 1, 1 - slot, False)

    @pl.when(used)
    def _():
        h = h_scr[...]
        act = _silu(_bdot(h, w1_ref[0, 0].astype(BF16))) * _bdot(h, w3_ref[0, 0].astype(BF16))
        o_ref[...] += _bdot(act.astype(BF16), w2_ref[0, 0].astype(BF16))

    @pl.when(jnp.logical_not(used) & (f == last))
    def _():
        o_ref[...] = jnp.zeros_like(o_ref)


def _moe_ffn(h, src, tile_expert, n_used, w1, w3, w2, layer, *, tile, tf=256):
    d = h.shape[1]
    n = src.shape[0]
    ff = w1.shape[3]
    nf = ff // tf
    fsel = lambda t, f, nu: jnp.where(t < nu[0], f, nf - 1)
    up = lambda t, f, te, nu, sr: (layer, te[t], 0, fsel(t, f, nu))
    down = lambda t, f, te, nu, sr: (layer, te[t], fsel(t, f, nu), 0)
    return pl.pallas_call(
        _moe_ffn_kernel,
        grid_spec=pltpu.PrefetchScalarGridSpec(
            num_scalar_prefetch=3,
            grid=(n // tile, nf),
            in_specs=[
                pl.BlockSpec(memory_space=pl.ANY),
                pl.BlockSpec((1, 1, d, tf), up),
                pl.BlockSpec((1, 1, d, tf), up),
                pl.BlockSpec((1, 1, tf, d), down),
            ],
            out_specs=pl.BlockSpec((tile, d), lambda t, f, te, nu, sr: (t, 0)),
            scratch_shapes=[pltpu.VMEM((2, tile, d), F32), pltpu.VMEM((tile, d), BF16),
                            pltpu.SemaphoreType.DMA((2,))],
        ),
        out_shape=jax.ShapeDtypeStruct((n, d), F32),
        compiler_params=_cparams(2),
        name="moe_ffn",
    )(tile_expert, n_used, src, h, w1, w3, w2)


def _combine_kernel(pos_ref, ys_hbm, x_ref, route_ref, mod_ref, o_ref, y1_scr, y2_scr, sem):
    base = 2 * pl.program_id(0) * x_ref.shape[0]
    _gather_rows(pos_ref, base, 2, ys_hbm, y1_scr, sem, wait=False)
    _gather_rows(pos_ref, base + 1, 2, ys_hbm, y2_scr, sem, wait=False)
    _gather_rows(pos_ref, base, 2, ys_hbm, y1_scr, sem, wait=True)
    _gather_rows(pos_ref, base + 1, 2, ys_hbm, y2_scr, sem, wait=True)
    mix = route_ref[:, 2:3] * y1_scr[...] + route_ref[:, 3:4] * y2_scr[...]
    o_ref[...] = x_ref[...] + mod_ref[0, 5:6, :] * mix


def _moe_combine(x, mods, route, ys, pos, *, rows, seq, batch, tm=256):
    d = x.shape[1]
    return pl.pallas_call(
        _combine_kernel,
        grid_spec=pltpu.PrefetchScalarGridSpec(
            num_scalar_prefetch=1,
            grid=(rows // tm,),
            in_specs=[
                pl.BlockSpec(memory_space=pl.ANY),
                pl.BlockSpec((tm, d), lambda m, pos: (m, 0)),
                pl.BlockSpec((tm, LANE), lambda m, pos: (m, 0)),
                pl.BlockSpec((1, 6, d), lambda m, pos: (_group_of(m, tm, seq, batch), 0, 0)),
            ],
            out_specs=pl.BlockSpec((tm, d), lambda m, pos: (m, 0)),
            scratch_shapes=[pltpu.VMEM((tm, d), F32), pltpu.VMEM((tm, d), F32),
                            pltpu.SemaphoreType.DMA(())],
        ),
        out_shape=jax.ShapeDtypeStruct((rows, d), F32),
        compiler_params=_cparams(1),
        name="moe_combine",
    )(pos, ys, x, route, mods)


def _moe_top2(x, nw, mods, router, w1, w3, w2, layer, *, rows, seq, batch, tm, tile=768):
    tok = dict(rows=rows, seq=seq, batch=batch)
    h, route = _moe_route(x, nw, mods, router, tm=tm, **tok)
    pos, src, tile_expert, n_used = _moe_plan(route, w1.shape[1], tile)
    ys = _moe_ffn(h, src, tile_expert, n_used, w1, w3, w2, layer, tile=tile)
    return _moe_combine(x, mods, route, ys, pos, tm=min(tm, 256), **tok)


def _final_norm_kernel(x_ref, w_ref, o_ref):
    x = x_ref[...]
    ms = jnp.mean(x * x, axis=-1, keepdims=True)
    o_ref[...] = x * lax.rsqrt(ms + NORM_EPS) * w_ref[...]


def _final_norm(x, w, *, rows, tm=512):
    d = x.shape[1]
    return pl.pallas_call(
        _final_norm_kernel,
        grid=(rows // tm,),
        in_specs=[pl.BlockSpec((tm, d), lambda m: (m, 0)), pl.BlockSpec((1, d), lambda m: (0, 0))],
        out_specs=pl.BlockSpec((tm, d), lambda m: (m, 0)),
        out_shape=jax.ShapeDtypeStruct((rows, d), F32),
        compiler_params=_cparams(1),
        name="final_norm",
    )(x, w.reshape(1, d))


def _dft_angles(n, period):
    k = lax.broadcasted_iota(jnp.int32, (n, n), 0)
    s = lax.broadcasted_iota(jnp.int32, (n, n), 1)
    return ((k * s) % period).astype(F32) * (2.0 * math.pi / period)


def _hyena_dft(seq_len):
    n = 2 * seq_len
    half = seq_len // 2
    row = lax.broadcasted_iota(jnp.int32, (seq_len, seq_len), 0)
    col = lax.broadcasted_iota(jnp.int32, (seq_len, seq_len), 1)
    freq = lambda i: jnp.where(i < half, 2 * i, 2 * (i - half) + 1)
    to_angle = lambda prod: (prod % n).astype(F32) * (2.0 * math.pi / n)
    ang_f = to_angle(freq(row) * col)
    ang_i = to_angle(row * freq(col))
    fc = jnp.cos(ang_f)
    fs = jnp.where(row == 0, jnp.where(col % 2 == 0, 1.0, -1.0), jnp.sin(ang_f))
    ic = jnp.where(col == 0, 1.0 / n, 2.0 / n * jnp.cos(ang_i))
    isn = jnp.where(col == 0, jnp.where(row % 2 == 0, 1.0, -1.0) / n, 2.0 / n * jnp.sin(ang_i))
    return (fc.astype(BF16), fs.astype(BF16), ic.astype(BF16), isn.astype(BF16),
            _flip_matrix(half))


def _hyena_feats(seq_len):
    pos = np.arange(seq_len, dtype=np.float64)
    t = pos / max(seq_len - 1, 1)
    bands = np.linspace(1e-4, HY_BANDS - 1, HY_BANDS).astype(np.float32).astype(np.float64)
    ang = 2 * math.pi * pos[:, None] * bands[None, :] / seq_len
    feats = np.concatenate([t[:, None], np.cos(ang), -np.sin(ang)], axis=-1)
    out = np.zeros((seq_len, LANE), np.float32)
    out[:, :HY_POS_DIM] = feats
    return jnp.asarray(out)


def _split2(a):
    hi = a.astype(BF16)
    return hi, (a - hi.astype(F32)).astype(BF16)


def _dot3(a, b):
    a_hi, a_lo = _split2(a)
    b_hi, b_lo = _split2(b)
    return _bdot(a_hi, b_hi) + _bdot(a_hi, b_lo) + _bdot(a_lo, b_hi)


def _filter_kernel(feat_ref, w1_ref, b1_ref, w2_ref, b2_ref, fr_ref, w3f_ref, w3b_ref,
                   dcf_ref, dcb_ref, fc_ref, fs_ref, o_ref, h_scr):
    @pl.when((pl.program_id(1) == 0) & (pl.program_id(2) == 0))
    def _():
        hp = lax.Precision.HIGHEST
        h1 = jnp.sin(fr_ref[0, 0:1, :] * (jnp.dot(feat_ref[...], w1_ref[0], precision=hp,
                                                  preferred_element_type=F32) + b1_ref[0]))
        h_scr[...] = jnp.sin(fr_ref[0, 1:2, :] * (jnp.dot(h1, w2_ref[0], precision=hp,
                                                          preferred_element_type=F32) + b2_ref[0]))

    h = h_scr[...]
    t = feat_ref[:, 0:1]
    hf = _dot3(h, w3f_ref[0]) * jnp.exp(-t * dcf_ref[0])
    hb = _dot3(h, w3b_ref[0]) * jnp.exp(-t * dcb_ref[0])
    row = lax.broadcasted_iota(jnp.int32, hf.shape, 0)
    hb = jnp.where(row == 0, 0.0, hb)
    ss = jnp.sum(hf * hf + hb * hb, axis=0, keepdims=True)
    scale = lax.rsqrt(ss + NORM_EPS)
    even = hf * scale + hb * scale
    odd = hf * scale - hb * scale
    kc = _bdot(fc_ref[...], even.astype(BF16))
    ks = _bdot(fs_ref[...], odd.astype(BF16))
    sign = jnp.where(row % 2 == 0, 1.0, -1.0)
    nyq = jnp.sum(even * sign, axis=0, keepdims=True)
    o_ref[0, 0] = kc
    o_ref[0, 1] = jnp.where(row == 0, nyq, ks)


def _hyena_filters(seq_len, w1p, b1, w2, b2, w3, freq, decay, fc, fs, *, tn=256):
    depth = w1p.shape[0]
    hid = HY_FILTER_HIDDEN
    nb = HY_W // tn
    feats = _hyena_feats(seq_len)
    col = lambda d: (lambda l, o, c: (l, 0, (2 * o + d) * nb + c))
    lyr = lambda l, o, c: (l, 0, 0)
    return pl.pallas_call(
        _filter_kernel,
        grid=(depth, 2, nb),
        in_specs=[
            pl.BlockSpec((seq_len, LANE), lambda l, o, c: (0, 0)),
            pl.BlockSpec((1, LANE, hid), lyr),
            pl.BlockSpec((1, 1, hid), lyr),
            pl.BlockSpec((1, hid, hid), lyr),
            pl.BlockSpec((1, 1, hid), lyr),
            pl.BlockSpec((1, 2, hid), lyr),
            pl.BlockSpec((1, hid, tn), col(0)),
            pl.BlockSpec((1, hid, tn), col(1)),
            pl.BlockSpec((1, 1, tn), col(0)),
            pl.BlockSpec((1, 1, tn), col(1)),
            _resident((seq_len, seq_len), lambda l, o, c: (0, 0)),
            _resident((seq_len, seq_len), lambda l, o, c: (0, 0)),
        ],
        out_specs=pl.BlockSpec((1, 2, seq_len, tn), lambda l, o, c: (l, 0, 0, o * nb + c)),
        out_shape=jax.ShapeDtypeStruct((depth, 2, seq_len, 2 * HY_W), F32),
        scratch_shapes=[pltpu.VMEM((seq_len, hid), F32)],
        compiler_params=_cparams(3),
        name="hyena_filters",
    )(feats, w1p, b1.reshape(depth, 1, hid), w2, b2.reshape(depth, 1, hid), freq,
      w3, w3, decay.reshape(depth, 1, 4 * HY_W), decay.reshape(depth, 1, 4 * HY_W), fc, fs)


def _conv3(z, w_ref, b_ref):
    n = z.shape[0]
    row = lax.broadcasted_iota(jnp.int32, z.shape, 0)
    prev = jnp.where(row == 0, 0.0, pltpu.roll(z, 1, 0))
    nxt = jnp.where(row == n - 1, 0.0, pltpu.roll(z, n - 1, 0))
    return prev * w_ref[0:1, :] + z * w_ref[1:2, :] + nxt * w_ref[2:3, :] + b_ref[...]


def _longconv_kernel(a_ref, g_ref, aw_ref, ab_ref, gw_ref, gb_ref, kf_ref, skip_ref,
                     fc_ref, fs_ref, ic_ref, is_ref, flip_ref, o_ref, *, conv_a):
    a = a_ref[...].astype(F32)
    if conv_a:
        a = _conv3(a, aw_ref, ab_ref)
    n = a.shape[0]
    h = n // 2
    lo = slice(0, h)
    hi = slice(h, n)
    k = lax.broadcasted_iota(jnp.int32, (h, a.shape[1]), 0)
    alt = jnp.where(k % 2 == 0, 1.0, -1.0)
    first = k == 0
    a_lo = a[lo]
    a_hi = a[hi]
    a_mid = a_hi[0:1, :]
    rev = _bdot(flip_ref[...], a_hi.astype(BF16))
    even_in = (a_lo + rev).astype(BF16)
    odd_in = (a_lo - rev).astype(BF16)
    nyq = jnp.sum(alt * (a_lo + a_hi), axis=0, keepdims=True)
    ac_e = _bdot(fc_ref[lo, lo], even_in) + alt * a_mid
    as_e = jnp.where(first, nyq, _bdot(fs_ref[lo, lo], odd_in))
    ac_o = _bdot(fc_ref[hi, lo], odd_in)
    as_o = _bdot(fs_ref[hi, lo], even_in) + alt * a_mid
    kc_e = kf_ref[0, 0, lo, :]
    ks_e = kf_ref[0, 1, lo, :]
    kc_o = kf_ref[0, 0, hi, :]
    ks_o = kf_ref[0, 1, hi, :]
    cross = as_e * ks_e
    yc_e = jnp.where(first, ac_e * kc_e, ac_e * kc_e - cross)
    ys_e = jnp.where(first, 0.0, ac_e * ks_e + as_e * kc_e)
    y_nyq = cross[0:1, :]
    yc_o = ac_o * kc_o - as_o * ks_o
    ys_o = ac_o * ks_o + as_o * kc_o
    sym = (_bdot(ic_ref[lo, lo], yc_e.astype(BF16)) + _bdot(is_ref[lo, hi], ys_o.astype(BF16))
           + alt * (y_nyq * (0.5 / n)))
    anti = _bdot(ic_ref[lo, hi], yc_o.astype(BF16)) + _bdot(is_ref[lo, lo], ys_e.astype(BF16))
    y_mid = (jnp.sum(alt * (yc_e + ys_o), axis=0, keepdims=True) * (1.0 / n)
             - yc_e[0:1, :] * (0.5 / n) + y_nyq * (0.5 / n))
    y_hi = jnp.where(first, y_mid, _bdot(flip_ref[...], (sym - anti).astype(BF16)))
    conv = jnp.concatenate([sym + anti, y_hi], axis=0)
    gate = _conv3(g_ref[...].astype(F32), gw_ref, gb_ref)
    o_ref[...] = (gate * (conv + a * skip_ref[0])).astype(o_ref.dtype)


def _hyena_longconv(a_src, a_part, zm, gate_part, conv_w, conv_b, kf, layer, order, skip,
                    dft, *, seq_len, n_seq, row0, out_rows, prev_out=None, tn=256):
    fc, fs, ic, isn, flip = dft
    assert seq_len % 4 == 0
    nb = HY_W // tn
    rb0 = row0 // seq_len
    const = lambda c, b: (0, 0)
    conv_a = a_part is not None
    aw_blk = a_part * nb if conv_a else 0
    a_blk = ZM_HY // tn + aw_blk if conv_a else 0
    gw_blk = gate_part * nb
    g_blk = ZM_HY // tn + gw_blk
    kernel = functools.partial(_longconv_kernel, conv_a=conv_a)
    in_specs = [
        pl.BlockSpec((seq_len, tn), lambda c, b: (rb0 + b, a_blk + c)),
        pl.BlockSpec((seq_len, tn), lambda c, b: (rb0 + b, g_blk + c)),
        pl.BlockSpec((3, tn), lambda c, b: (0, aw_blk + c)),
        pl.BlockSpec((1, tn), lambda c, b: (0, aw_blk + c)),
        pl.BlockSpec((3, tn), lambda c, b: (0, gw_blk + c)),
        pl.BlockSpec((1, tn), lambda c, b: (0, gw_blk + c)),
        _resident((1, 2, seq_len, tn), lambda c, b: (layer, 0, 0, order * nb + c)),
        pl.BlockSpec((1, 1, tn), lambda c, b: (order, 0, c)),
        _resident((seq_len, seq_len), const),
        _resident((seq_len, seq_len), const),
        _resident((seq_len, seq_len), const),
        _resident((seq_len, seq_len), const),
        _resident((seq_len // 2, seq_len // 2), const),
    ]
    args = [a_src, zm, conv_w, conv_b.reshape(1, -1), conv_w, conv_b.reshape(1, -1), kf,
            skip.reshape(2, 1, HY_W), fc, fs, ic, isn, flip]
    aliases = {}
    if prev_out is not None:
        in_specs.append(pl.BlockSpec(memory_space=pl.ANY))
        args.append(prev_out)
        aliases = {len(args) - 1: 0}
        kernel = functools.partial(_drop_last_input, kernel, 13)
    return pl.pallas_call(
        kernel,
        grid=(nb, n_seq),
        in_specs=in_specs,
        out_specs=pl.BlockSpec((seq_len, tn), lambda c, b: (rb0 + b, c)),
        out_shape=jax.ShapeDtypeStruct((out_rows, HY_W), BF16),
        input_output_aliases=aliases,
        compiler_params=_cparams(2),
        name="hyena_longconv",
    )(*args)


def _drop_last_input(kernel, n_in, *refs):
    return kernel(*refs[:n_in], *refs[n_in + 1:])


def _flip_matrix(half):
    r = lax.broadcasted_iota(jnp.int32, (half, half), 0)
    c = lax.broadcasted_iota(jnp.int32, (half, half), 1)
    return jnp.where((r >= 1) & (c == half - r), 1.0, 0.0).astype(BF16)


def _fnet_scale(seq_len):
    return 1.0 / math.sqrt(seq_len * FN_GROUP_W)


def _fnet_dft(seq_len):
    half = seq_len // 2
    ang_l = _dft_angles(seq_len, seq_len)[:half]
    ang_w = _dft_angles(FN_GROUP_W, FN_GROUP_W)
    scale = _fnet_scale(seq_len)
    return ((jnp.cos(ang_l) * scale).astype(BF16), (jnp.sin(ang_l) * scale).astype(BF16),
            jnp.cos(ang_w).astype(BF16), jnp.sin(ang_w).astype(BF16), _flip_matrix(half))


def _fnet_kernel(z_ref, cl_ref, sl_ref, cw_ref, sw_ref, flip_ref, o_ref, *, scale):
    z = z_ref[...]
    gc, gs = [], []
    for j in range(z.shape[1] // FN_GROUP_W):
        zj = z[:, j * FN_GROUP_W:(j + 1) * FN_GROUP_W]
        gc.append(_bdot(zj, cw_ref[...]).astype(BF16))
        gs.append(_bdot(zj, sw_ref[...]).astype(BF16))
    gc = jnp.concatenate(gc, axis=-1)
    gs = jnp.concatenate(gs, axis=-1)
    p = _bdot(cl_ref[...], gc)
    q = _bdot(sl_ref[...], gs)
    pos = lax.broadcasted_iota(jnp.int32, gc.shape, 0)
    y_mid = scale * jnp.sum(jnp.where(pos % 2 == 0, 1.0, -1.0) * gc.astype(F32), axis=0,
                            keepdims=True)
    first = lax.broadcasted_iota(jnp.int32, p.shape, 0) == 0
    y_hi = jnp.where(first, y_mid, _bdot(flip_ref[...], (p + q).astype(BF16)))
    o_ref[...] = jnp.concatenate([p - q, y_hi], axis=0).astype(o_ref.dtype)


def _fnet(zm, dft, *, seq_len, n_seq, row0, out_rows, prev_out=None, tn=512):
    cl, sl, cw, sw, flip = dft
    nb = FN_W // tn
    rb0 = row0 // seq_len
    c0 = ZM_FN // tn
    half = seq_len // 2
    const = lambda c, b: (0, 0)
    in_specs = [
        pl.BlockSpec((seq_len, tn), lambda c, b: (rb0 + b, c0 + c)),
        _resident((half, seq_len), const),
        _resident((half, seq_len), const),
        pl.BlockSpec((FN_GROUP_W, FN_GROUP_W), const),
        pl.BlockSpec((FN_GROUP_W, FN_GROUP_W), const),
        _resident((half, half), const),
    ]
    args = [zm, cl, sl, cw, sw, flip]
    kernel = functools.partial(_fnet_kernel, scale=_fnet_scale(seq_len))
    aliases = {}
    if prev_out is not None:
        in_specs.append(pl.BlockSpec(memory_space=pl.ANY))
        args.append(prev_out)
        aliases = {len(args) - 1: 0}
        kernel = functools.partial(_drop_last_input, kernel, 6)
    return pl.pallas_call(
        kernel,
        grid=(nb, n_seq),
        in_specs=in_specs,
        out_specs=pl.BlockSpec((seq_len, tn), lambda c, b: (rb0 + b, c)),
        out_shape=jax.ShapeDtypeStruct((out_rows, FN_W), BF16),
        input_output_aliases=aliases,
        compiler_params=_cparams(2),
        name="fnet",
    )(*args)


def _hgrn_tables(reverse):
    c = HG_CHUNK
    tri = np.zeros((c, c), np.float32)
    for t in range(c):
        if reverse:
            tri[t, t:] = 1.0
        else:
            tri[t, :t + 1] = 1.0
    mats = [tri]
    masks = []
    for lvl in range(HG_LEVELS):
        half = (c // 2) >> lvl
        sel = np.zeros((c, c), np.float32)
        mask = np.zeros((c, c), np.float32)
        for t in range(c):
            start = (t // (2 * half)) * 2 * half
            mid = start + half
            sel[t, mid if reverse else mid - 1] = 1.0
            for s in range(start, start + 2 * half):
                if reverse and t < mid <= s:
                    mask[t, s] = 1.0
                if (not reverse) and s < mid <= t:
                    mask[t, s] = 1.0
        mats.append(tri - sel @ tri)
        masks.append(mask)
    half = c // 2
    sel = np.zeros((c, c), np.float32)
    mask = np.zeros((c, c), np.float32)
    for t in range(c):
        start = (t // half) * half
        sel[t, start + half // 2] = 1.0
        for s in range(start, start + half):
            if (s >= t) if reverse else (s <= t):
                mask[t, s] = 1.0
    mats.append(tri - sel @ tri)
    masks.append(mask)
    mats = [mats[0], mats[1], mats[-1]] + mats[2:-1]
    return np.concatenate(mats, axis=0), np.stack(masks, axis=0)


HG_BOUNDED_BLOCKS = 3
HG_MID_BLOCK = 2


def _hgrn_level_block(lvl):
    return 1 if lvl == 0 else HG_BOUNDED_BLOCKS + lvl - 1


def _hgrn_decays(zf_ref, row0, lb_ref, tab_ref, k_scr, p_scr, dec_scr):
    lb = lb_ref[...]
    f = lb + (1.0 - lb) * jax.nn.sigmoid(zf_ref[row0:row0 + HG_CHUNK, :])
    lf = jnp.log2(f)
    k_scr[...] = (1.0 - f).astype(BF16)
    p1 = lf.astype(BF16)
    r1 = lf - p1.astype(F32)
    p2 = r1.astype(BF16)
    p3 = (r1 - p2.astype(F32)).astype(BF16)
    p_scr[...] = jnp.concatenate([p1, p2, p3], axis=0)
    n = HG_BOUNDED_BLOCKS * HG_CHUNK
    dec_scr[0:n, :] = _bdot(tab_ref[0:n, :], p_scr[...])


def _hgrn_more_decays(tab_ref, p_scr, dec_scr):
    n = HG_BOUNDED_BLOCKS * HG_CHUNK
    dec_scr[n:, :] = _bdot(tab_ref[n:, :], p_scr[...])


def _hgrn_direction(q_ref, v_ref, row0, k_scr, dec_scr, mask_ref, st_scr, o_ref, end_row,
                    bounded):
    c = HG_CHUNK
    heads = lambda ref, r0: jnp.stack(
        [ref[r0:r0 + c, h * HG_DK:(h + 1) * HG_DK] for h in range(HG_HEADS)], axis=0)
    bmm = lambda eq, a, b: jnp.einsum(eq, a, b, preferred_element_type=F32)
    q = heads(q_ref, row0)
    k = heads(k_scr, 0)
    v = heads(v_ref, row0)
    b = heads(dec_scr, 0)
    st = st_scr[...]
    o = bmm('htk,hkv->htv', q * jnp.exp2(b).astype(BF16), st.astype(BF16))
    level = lambda l: jnp.exp2(-jnp.abs(heads(dec_scr, _hgrn_level_block(l) * c))).astype(BF16)
    if bounded:
        e = level(0)
        att = jnp.where(mask_ref[0] > 0.5, bmm('htk,hsk->hts', q * e, k * e), 0.0)
        d = heads(dec_scr, HG_MID_BLOCK * c)
        inner = bmm('htk,hsk->hts', q * jnp.exp2(d).astype(BF16), k * jnp.exp2(-d).astype(BF16))
        att += jnp.where(mask_ref[HG_LEVELS] > 0.5, inner, 0.0)
    else:
        diag = jnp.sum(q.astype(F32) * k.astype(F32), axis=-1, keepdims=True)
        o += diag * v.astype(F32)
        att = jnp.zeros((HG_HEADS, c, c), F32)
        for lvl in range(HG_LEVELS):
            e = level(lvl)
            att += bmm('htk,hsk->hts', q * e, k * e) * mask_ref[lvl]
    o += bmm('hts,hsv->htv', att.astype(BF16), v)
    for h in range(HG_HEADS):
        o_ref[row0:row0 + c, h * HG_DK:(h + 1) * HG_DK] = o[h]
    b_end = b[:, end_row:end_row + 1, :]
    upd = bmm('htk,htv->hkv', k * jnp.exp2(b_end - b).astype(BF16), v)
    decay = jnp.swapaxes(jnp.broadcast_to(jnp.exp2(b_end), (HG_HEADS, HG_DK, HG_DK)), 1, 2)
    st_scr[...] = decay * st + upd


HG_MAX_LOG2_SPAN = 100.0
HG_MAX_ABS_Q = 1e6


def _hgrn_kernel(qf_ref, vf_ref, zff_ref, qb_ref, vb_ref, zfb_ref, lbf_ref, lbb_ref,
                 tabf_ref, maskf_ref, tabb_ref, maskb_ref, s0f_ref, s0b_ref,
                 of_ref, ob_ref, sf_ref, sb_ref, stf_scr, stb_scr, kf_scr, kb_scr,
                 pf_scr, pb_scr, decf_scr, decb_scr):
    ci = pl.program_id(1)
    c = HG_CHUNK

    @pl.when(ci == 0)
    def _():
        stf_scr[...] = s0f_ref[0]
        stb_scr[...] = s0b_ref[0]

    q_abs = jnp.maximum(jnp.max(jnp.abs(qf_ref[...].astype(F32))),
                        jnp.max(jnp.abs(qb_ref[...].astype(F32))))
    mid = slice(HG_MID_BLOCK * c, (HG_MID_BLOCK + 1) * c)
    n_sub = qf_ref.shape[0] // c
    for sub in range(n_sub):
        rf = sub * c
        rb = (n_sub - 1 - sub) * c
        _hgrn_decays(zff_ref, rf, lbf_ref, tabf_ref, kf_scr, pf_scr, decf_scr)
        _hgrn_decays(zfb_ref, rb, lbb_ref, tabb_ref, kb_scr, pb_scr, decb_scr)
        span = jnp.maximum(jnp.max(jnp.abs(decf_scr[mid, :])), jnp.max(jnp.abs(decb_scr[mid, :])))
        bounded = (span < HG_MAX_LOG2_SPAN) & (q_abs < HG_MAX_ABS_Q)

        def run(flag, rf=rf, rb=rb):
            _hgrn_direction(qf_ref, vf_ref, rf, kf_scr, decf_scr, maskf_ref, stf_scr, of_ref,
                            c - 1, flag)
            _hgrn_direction(qb_ref, vb_ref, rb, kb_scr, decb_scr, maskb_ref, stb_scr, ob_ref,
                            0, flag)

        @pl.when(bounded)
        def _():
            run(True)

        @pl.when(jnp.logical_not(bounded))
        def _():
            _hgrn_more_decays(tabf_ref, pf_scr, decf_scr)
            _hgrn_more_decays(tabb_ref, pb_scr, decb_scr)
            run(False)

    @pl.when(ci == pl.num_programs(1) - 1)
    def _():
        sf_ref[0] = stf_scr[...]
        sb_ref[0] = stb_scr[...]


def _hgrn(zm, zf, lb_f, lb_b, s0_f, s0_b, *, seq_len, n_seq, row0, out_rows,
          prev_of=None, prev_ob=None):
    c = HG_CHUNK
    blk = c * HG_STEP_CHUNKS
    nc = seq_len // blk
    rb0 = row0 // blk
    w = HG_W
    tab_f, mask_f = _hgrn_tables(False)
    tab_b, mask_b = _hgrn_tables(True)
    fwd = lambda col: (lambda b, i: (rb0 + b * nc + i, col))
    bwd = lambda col: (lambda b, i: (rb0 + b * nc + nc - 1 - i, col))
    const2 = lambda b, i: (0, 0)
    const3 = lambda b, i: (0, 0, 0)
    state = lambda b, i: (b, 0, 0, 0)
    nt = (2 + HG_LEVELS) * c
    in_specs = [
        pl.BlockSpec((blk, w), fwd(ZM_Q // w)),
        pl.BlockSpec((blk, w), fwd(ZM_I // w)),
        pl.BlockSpec((blk, w), fwd(0)),
        pl.BlockSpec((blk, w), bwd(ZM_Q // w)),
        pl.BlockSpec((blk, w), bwd(ZM_I // w)),
        pl.BlockSpec((blk, w), bwd(1)),
        pl.BlockSpec((1, w), const2),
        pl.BlockSpec((1, w), const2),
        pl.BlockSpec((nt, 3 * c), const2),
        pl.BlockSpec((HG_LEVELS + 1, c, c), const3),
        pl.BlockSpec((nt, 3 * c), const2),
        pl.BlockSpec((HG_LEVELS + 1, c, c), const3),
        pl.BlockSpec((1, HG_HEADS, HG_DK, HG_DK), state),
        pl.BlockSpec((1, HG_HEADS, HG_DK, HG_DK), state),
    ]
    args = [zm, zm, zf, zm, zm, zf, lb_f.reshape(1, w), lb_b.reshape(1, w),
            jnp.asarray(np.tile(tab_f, (1, 3)), BF16), jnp.asarray(mask_f),
            jnp.asarray(np.tile(tab_b, (1, 3)), BF16), jnp.asarray(mask_b), s0_f, s0_b]
    kernel = _hgrn_kernel
    aliases = {}
    if prev_of is not None:
        in_specs += [pl.BlockSpec(memory_space=pl.ANY), pl.BlockSpec(memory_space=pl.ANY)]
        args += [prev_of, prev_ob]
        aliases = {14: 0, 15: 1}
        kernel = functools.partial(_hgrn_alias_kernel, 14)
    st_shape = jax.ShapeDtypeStruct((n_seq, HG_HEADS, HG_DK, HG_DK), F32)
    return pl.pallas_call(
        kernel,
        grid=(n_seq, nc),
        in_specs=in_specs,
        out_specs=[
            pl.BlockSpec((blk, w), fwd(0)),
            pl.BlockSpec((blk, w), bwd(0)),
            pl.BlockSpec((1, HG_HEADS, HG_DK, HG_DK), state),
            pl.BlockSpec((1, HG_HEADS, HG_DK, HG_DK), state),
        ],
        out_shape=[jax.ShapeDtypeStruct((out_rows, w), F32),
                   jax.ShapeDtypeStruct((out_rows, w), F32), st_shape, st_shape],
        scratch_shapes=[pltpu.VMEM((HG_HEADS, HG_DK, HG_DK), F32),
                        pltpu.VMEM((HG_HEADS, HG_DK, HG_DK), F32),
                        pltpu.VMEM((c, w), BF16), pltpu.VMEM((c, w), BF16),
                        pltpu.VMEM((3 * c, w), BF16), pltpu.VMEM((3 * c, w), BF16),
                        pltpu.VMEM((nt, w), F32), pltpu.VMEM((nt, w), F32)],
        input_output_aliases=aliases,
        compiler_params=_cparams(2),
        name="hgrn2",
    )(*args)


def _hgrn_alias_kernel(n_in, *refs):
    return _hgrn_kernel(*refs[:n_in], *refs[n_in + 2:])


def kernel(x, c, ctx, c_ctx, w_mod, b_mod, norm1_w, norm2_w, w_in, hy_conv_w, hy_conv_b,
           hy_f_w1, hy_f_b1, hy_f_w2, hy_f_b2, hy_f_w3, hy_f_freq, hy_decay, hy_skip,
           hg_lower, hg_norm_w, w_br_hy, w_br_fn, w_br_hg, w_out,
           ffn_w1, ffn_w3, ffn_w2, moe_router, moe_w1, moe_w3, moe_w2, final_norm_w):
    batch, seq, d = x.shape
    l_ctx = ctx.shape[1]
    depth = w_in.shape[0]
    n_lat = batch * seq
    n_all = n_lat + batch * l_ctx

    lb_all = jnp.cumsum(jax.nn.softmax(hg_lower.astype(F32), axis=1), axis=1)
    lb_all = lb_all - lb_all[:, :1]

    cc = jnp.zeros((8, d), F32).at[:batch].set(c).at[batch].set(c_ctx)
    mods = _modulation(cc, w_mod, b_mod).reshape(depth, 8, 6, d)

    w1p = jnp.zeros((depth, LANE, HY_FILTER_HIDDEN), F32).at[:, :HY_POS_DIM].set(hy_f_w1)
    dft_lat = _hyena_dft(seq)
    dft_ctx = _hyena_dft(l_ctx)
    kf_lat = _hyena_filters(seq, w1p, hy_f_b1, hy_f_w2, hy_f_b2, hy_f_w3, hy_f_freq, hy_decay,
                            dft_lat[0], dft_lat[1])
    kf_ctx = _hyena_filters(l_ctx, w1p, hy_f_b1, hy_f_w2, hy_f_b2, hy_f_w3, hy_f_freq, hy_decay,
                            dft_ctx[0], dft_ctx[1])
    fn_lat = _fnet_dft(seq)
    fn_ctx = _fnet_dft(l_ctx)

    rows_x = jnp.concatenate([x.reshape(n_lat, d), ctx.reshape(batch * l_ctx, d)], axis=0)
    zero_state = jnp.zeros((batch, HG_HEADS, HG_DK, HG_DK), F32)
    tile_cap = math.gcd(seq, batch * l_ctx)
    tok = dict(seq=seq, batch=batch)
    tile = lambda t: dict(tm=min(t, tile_cap))
    lat = dict(seq_len=seq, n_seq=batch, row0=0, out_rows=n_all)
    cx = dict(seq_len=l_ctx, n_seq=batch, row0=n_lat, out_rows=n_all)

    for l in range(depth):
        last = l == depth - 1
        rows = n_lat if last else n_all
        m = mods[l]

        h = _norm_rows(rows_x, norm1_w[l], m, **tok, **tile(512))
        zm, zf = _in_proj(h, w_in, l, **tile(1024))

        of, ob, s_f, s_b = _hgrn(zm, zf, lb_all[0, l], lb_all[1, l], zero_state, zero_state, **cx)
        of, ob, _, _ = _hgrn(zm, zf, lb_all[0, l], lb_all[1, l], s_f, s_b, prev_of=of,
                             prev_ob=ob, **lat)

        hy_args = (hy_conv_w[l], hy_conv_b[l])
        mix = dict(lat, out_rows=rows)
        u = _hyena_longconv(zm, 0, zm, 1, *hy_args, kf_lat, l, 0, hy_skip[l], dft_lat, **mix)
        y_hy = _hyena_longconv(u, None, zm, 2, *hy_args, kf_lat, l, 1, hy_skip[l], dft_lat, **mix)
        y_fn = _fnet(zm, fn_lat, **mix)
        if not last:
            u = _hyena_longconv(zm, 0, zm, 1, *hy_args, kf_ctx, l, 0, hy_skip[l], dft_ctx,
                                prev_out=u, **cx)
            y_hy = _hyena_longconv(u, None, zm, 2, *hy_args, kf_ctx, l, 1, hy_skip[l], dft_ctx,
                                   prev_out=y_hy, **cx)
            y_fn = _fnet(zm, fn_ctx, prev_out=y_fn, **cx)

        rows_x = _merge_out(y_hy, y_fn, of, ob, zm, rows_x, hg_norm_w[l], m,
                            w_br_hy[l].astype(BF16), w_br_fn[l].astype(BF16),
                            w_br_hg[l].astype(BF16), w_out[l].astype(BF16), rows=rows,
                            **tok, **tile(256))

        if l % 2 == 0:
            i = l // 2
            rows_x = _ffn_dense(rows_x, norm2_w[l], m, ffn_w1, ffn_w3, ffn_w2, i, rows=rows,
                                **tok, **tile(1024))
        else:
            i = l // 2
            rows_x = _moe_top2(rows_x, norm2_w[l], m, moe_router[i], moe_w1, moe_w3, moe_w2, i,
                               rows=rows, **tok, **tile(512))

    out = _final_norm(rows_x, final_norm_w, rows=n_lat, **tile(512))
    return out.reshape(batch, seq, d)
```

```python
import functools
import math

import numpy as np
import jax
import jax.numpy as jnp
from jax import lax
from jax.experimental import pallas as pl
from jax.experimental.pallas import tpu as pltpu

F32 = jnp.float32
BF16 = jnp.bfloat16
NORM_EPS = 1e-6

D_MODEL = 2048
DEPTH = 4
HY_W = 1024
HY_BANDS = 16
HY_POS_DIM = 2 * HY_BANDS + 1
HY_FILTER_HIDDEN = 64
FN_W = 1024
FN_GROUP_W = 128
HG_W = 1024
HG_HEADS = 8
HG_DK = 128
HG_CHUNK = 128
HG_LEVELS = 7
HG_STEP_CHUNKS = 1
N_EXPERTS = 8
LANE = 128

HG_FF = 3 * HY_W + FN_W + 2 * HG_W
HG_I = HG_FF + 2 * HG_W
ZM_HY = 0
ZM_FN = ZM_HY + 3 * HY_W
ZM_Q = ZM_FN + FN_W
ZM_G = ZM_Q + HG_W
ZM_I = HG_I
ZM_GATE = HG_I + HG_W
ZM_COLS = ZM_GATE + 3 * D_MODEL

VMEM_LIMIT = 56 * 1024 * 1024


def _cparams(n_axes, vmem=VMEM_LIMIT):
    return pltpu.CompilerParams(dimension_semantics=("arbitrary",) * n_axes, vmem_limit_bytes=vmem)


def _resident(shape, index_map):
    return pl.BlockSpec(shape, index_map, pipeline_mode=pl.Buffered(1))


def _silu(v):
    return v * jax.nn.sigmoid(v)


def _bdot(a, b):
    return jnp.dot(a, b, preferred_element_type=F32)


def _mod_kernel(c_ref, w_ref, b_ref, o_ref):
    c = _silu(c_ref[...])
    o_ref[0] = jnp.dot(c, w_ref[0], preferred_element_type=F32,
                       precision=lax.Precision.HIGHEST) + b_ref[0]


def _modulation(cc, w_mod, b_mod):
    depth, d, n6 = w_mod.shape
    tn = 2048
    return pl.pallas_call(
        _mod_kernel,
        grid=(depth, n6 // tn),
        in_specs=[
            pl.BlockSpec((8, d), lambda l, n: (0, 0)),
            pl.BlockSpec((1, d, tn), lambda l, n: (l, 0, n)),
            pl.BlockSpec((1, 1, tn), lambda l, n: (l, 0, n)),
        ],
        out_specs=pl.BlockSpec((1, 8, tn), lambda l, n: (l, 0, n)),
        out_shape=jax.ShapeDtypeStruct((depth, 8, n6), F32),
        compiler_params=_cparams(2),
        name="modulation",
    )(cc, w_mod, b_mod.reshape(depth, 1, n6))


def _norm_mod(x, nw, shift, scale):
    ms = jnp.mean(x * x, axis=-1, keepdims=True)
    return (x * lax.rsqrt(ms + NORM_EPS) * nw) * (1 + scale) + shift


def _group_of(row_block, tm, seq, batch):
    return jnp.minimum((row_block * tm) // seq, batch)


def _norm_rows_kernel(x_ref, nw_ref, mod_ref, o_ref):
    h = _norm_mod(x_ref[...], nw_ref[...], mod_ref[0, 0:1, :], mod_ref[0, 1:2, :])
    o_ref[...] = h.astype(o_ref.dtype)


def _norm_rows(x, nw, mods, *, seq, batch, tm=512):
    rows, d = x.shape
    return pl.pallas_call(
        _norm_rows_kernel,
        grid=(rows // tm,),
        in_specs=[
            pl.BlockSpec((tm, d), lambda m: (m, 0)),
            pl.BlockSpec((1, d), lambda m: (0, 0)),
            pl.BlockSpec((1, 6, d), lambda m: (_group_of(m, tm, seq, batch), 0, 0)),
        ],
        out_specs=pl.BlockSpec((tm, d), lambda m: (m, 0)),
        out_shape=jax.ShapeDtypeStruct((rows, d), BF16),
        compiler_params=_cparams(1),
        name="norm_rows",
    )(x, nw.reshape(1, d), mods)


def _inproj_kernel(h_ref, w_ref, z_ref, zf_ref, w_scr, *, n_main):
    @pl.when(pl.program_id(1) == 0)
    def _():
        w_scr[...] = w_ref[0].astype(BF16)

    r = _bdot(h_ref[...], w_scr[...])

    @pl.when(pl.program_id(0) < n_main)
    def _():
        z_ref[...] = r.astype(z_ref.dtype)

    @pl.when(pl.program_id(0) >= n_main)
    def _():
        zf_ref[...] = r


def _in_proj(h, w_in, layer, *, tm=1024, tn=1024):
    rows, d = h.shape
    n_cols = w_in.shape[2]
    nb = n_cols // tn
    f0 = HG_FF // tn
    nf = 2 * HG_W // tn
    n_main = nb - nf
    nm = rows // tm
    col = lambda j: jnp.where(j < f0, j, jnp.where(j < n_main, j + nf, j - n_main + f0))
    z_idx = lambda j, m: (jnp.where(j < n_main, m, nm - 1), col(jnp.minimum(j, n_main - 1)))
    zf_idx = lambda j, m: (jnp.where(j < n_main, 0, m), jnp.maximum(j - n_main, 0))
    return pl.pallas_call(
        functools.partial(_inproj_kernel, n_main=n_main),
        grid=(nb, nm),
        in_specs=[
            pl.BlockSpec((tm, d), lambda j, m: (m, 0)),
            pl.BlockSpec((1, d, tn), lambda j, m: (layer, 0, col(j))),
        ],
        out_specs=[pl.BlockSpec((tm, tn), z_idx), pl.BlockSpec((tm, tn), zf_idx)],
        out_shape=[jax.ShapeDtypeStruct((rows, n_cols), BF16),
                   jax.ShapeDtypeStruct((rows, nf * tn), F32)],
        scratch_shapes=[pltpu.VMEM((d, tn), BF16)],
        compiler_params=_cparams(2),
        name="in_proj",
    )(h, w_in)


def _merge_kernel(yhy_ref, yfn_ref, of_ref, ob_ref, g_ref, ga0_ref, ga1_ref, gb0_ref, gb1_ref,
                  gc0_ref, gc1_ref, x_ref, hnw_ref, mod_ref,
                  wbhy_ref, wbfn_ref, wbhg_ref, wo_ref, o_ref):
    d = x_ref.shape[1]
    ohg = of_ref[...] + ob_ref[...]
    nw = hnw_ref[...]
    heads = []
    for h in range(ohg.shape[1] // HG_DK):
        oh = ohg[:, h * HG_DK:(h + 1) * HG_DK]
        ms = jnp.mean(oh * oh, axis=-1, keepdims=True)
        heads.append(oh * lax.rsqrt(ms + NORM_EPS) * nw)
    y_hg = jnp.concatenate(heads, axis=-1) * _silu(g_ref[...].astype(F32))
    gate = lambda lo, hi: jax.nn.sigmoid(
        jnp.concatenate([lo[...], hi[...]], axis=-1).astype(F32))
    merged = gate(ga0_ref, ga1_ref) * _bdot(yhy_ref[...], wbhy_ref[...])
    merged += gate(gb0_ref, gb1_ref) * _bdot(yfn_ref[...], wbfn_ref[...])
    merged += gate(gc0_ref, gc1_ref) * _bdot(y_hg.astype(BF16), wbhg_ref[...])
    out = _bdot(merged.astype(BF16), wo_ref[...])
    o_ref[...] = x_ref[...] + mod_ref[0, 2:3, :] * out


def _merge_out(y_hy, y_fn, o_f, o_b, zm, x, hg_nw, mods, wb_hy, wb_fn, wb_hg, w_o,
               *, rows, seq, batch, tm=256):
    d = x.shape[1]
    w = y_hy.shape[1]
    tm = min(tm, rows)
    row = lambda m: (m, 0)
    const = lambda m: (0, 0)
    gate0 = ZM_GATE // (d // 2)
    return pl.pallas_call(
        _merge_kernel,
        grid=(rows // tm,),
        in_specs=[
            pl.BlockSpec((tm, w), row),
            pl.BlockSpec((tm, w), row),
            pl.BlockSpec((tm, w), row),
            pl.BlockSpec((tm, w), row),
            pl.BlockSpec((tm, w), lambda m: (m, ZM_G // w)),
            *[pl.BlockSpec((tm, d // 2), functools.partial(lambda i, m: (m, i), gate0 + i))
              for i in range(6)],
            pl.BlockSpec((tm, d), row),
            pl.BlockSpec((1, HG_DK), const),
            pl.BlockSpec((1, 6, d), lambda m: (_group_of(m, tm, seq, batch), 0, 0)),
            _resident((w, d), const),
            _resident((w, d), const),
            _resident((w, d), const),
            _resident((d, d), const),
        ],
        out_specs=pl.BlockSpec((tm, d), row),
        out_shape=jax.ShapeDtypeStruct((rows, d), F32),
        compiler_params=_cparams(1),
        name="merge_out",
    )(y_hy, y_fn, o_f, o_b, zm, *([zm] * 6), x, hg_nw.reshape(1, HG_DK), mods,
      wb_hy, wb_fn, wb_hg, w_o)


def _ffn_kernel(x_ref, nw_ref, mod_ref, w1_ref, w3_ref, w2_ref, o_ref, h_scr):
    f = pl.program_id(1)

    @pl.when(f == 0)
    def _():
        h = _norm_mod(x_ref[...], nw_ref[...], mod_ref[0, 3:4, :], mod_ref[0, 4:5, :])
        h_scr[...] = h.astype(BF16)
        o_ref[...] = jnp.zeros_like(o_ref)

    h = h_scr[...]
    act = _silu(_bdot(h, w1_ref[0].astype(BF16))) * _bdot(h, w3_ref[0].astype(BF16))
    o_ref[...] += _bdot(act.astype(BF16), w2_ref[0].astype(BF16))

    @pl.when(f == pl.num_programs(1) - 1)
    def _():
        o_ref[...] = x_ref[...] + mod_ref[0, 5:6, :] * o_ref[...]


def _ffn_dense(x, nw, mods, w1, w3, w2, layer, *, rows, seq, batch, tm=1024, tf=256):
    d = x.shape[1]
    ff = w1.shape[2]
    tm = min(tm, rows)
    return pl.pallas_call(
        _ffn_kernel,
        grid=(rows // tm, ff // tf),
        in_specs=[
            _resident((tm, d), lambda m, f: (m, 0)),
            pl.BlockSpec((1, d), lambda m, f: (0, 0)),
            pl.BlockSpec((1, 6, d), lambda m, f: (_group_of(m, tm, seq, batch), 0, 0)),
            pl.BlockSpec((1, d, tf), lambda m, f: (layer, 0, f)),
            pl.BlockSpec((1, d, tf), lambda m, f: (layer, 0, f)),
            pl.BlockSpec((1, tf, d), lambda m, f: (layer, f, 0)),
        ],
        out_specs=pl.BlockSpec((tm, d), lambda m, f: (m, 0)),
        out_shape=jax.ShapeDtypeStruct((rows, d), F32),
        scratch_shapes=[pltpu.VMEM((tm, d), BF16)],
        compiler_params=_cparams(2),
        name="ffn_dense",
    )(x, nw.reshape(1, d), mods, w1, w3, w2)


def _top2(logits, n_experts):
    lane = lax.broadcasted_iota(jnp.int32, logits.shape, 1).astype(F32)
    neg = jnp.float32(-jnp.inf)
    lg = jnp.where(lane < n_experts, logits, neg)
    m1 = jnp.max(lg, axis=-1, keepdims=True)
    i1 = jnp.min(jnp.where(lg == m1, lane, float(LANE)), axis=-1, keepdims=True)
    lg2 = jnp.where(lane == i1, neg, lg)
    m2 = jnp.max(lg2, axis=-1, keepdims=True)
    i2 = jnp.min(jnp.where(lg2 == m2, lane, float(LANE)), axis=-1, keepdims=True)
    e2 = jnp.exp(m2 - m1)
    return i1, i2, 1.0 / (1.0 + e2), e2 / (1.0 + e2)


def _route_kernel(x_ref, nw_ref, mod_ref, r_ref, h_ref, route_ref):
    h = _norm_mod(x_ref[...], nw_ref[...], mod_ref[0, 3:4, :], mod_ref[0, 4:5, :])
    logits = jnp.dot(h, r_ref[...], preferred_element_type=F32, precision=lax.Precision.HIGHEST)
    i1, i2, p1, p2 = _top2(logits, N_EXPERTS)
    lane = lax.broadcasted_iota(jnp.int32, logits.shape, 1)
    route = jnp.where(lane == 0, i1, jnp.where(lane == 1, i2,
                      jnp.where(lane == 2, p1, jnp.where(lane == 3, p2, 0.0))))
    h_ref[...] = h
    route_ref[...] = route


def _moe_route(x, nw, mods, router, *, rows, seq, batch, tm=512):
    d = x.shape[1]
    rpad = jnp.zeros((d, LANE), F32).at[:, :router.shape[1]].set(router)
    return pl.pallas_call(
        _route_kernel,
        grid=(rows // tm,),
        in_specs=[
            pl.BlockSpec((tm, d), lambda m: (m, 0)),
            pl.BlockSpec((1, d), lambda m: (0, 0)),
            pl.BlockSpec((1, 6, d), lambda m: (_group_of(m, tm, seq, batch), 0, 0)),
            pl.BlockSpec((d, LANE), lambda m: (0, 0)),
        ],
        out_specs=[pl.BlockSpec((tm, d), lambda m: (m, 0)),
                   pl.BlockSpec((tm, LANE), lambda m: (m, 0))],
        out_shape=[jax.ShapeDtypeStruct((rows, d), F32), jax.ShapeDtypeStruct((rows, LANE), F32)],
        compiler_params=_cparams(1),
        name="moe_route",
    )(x, nw.reshape(1, d), mods, rpad)


def _moe_plan(route, n_experts, tile):
    rows = route.shape[0]
    e = route[:, :2].astype(jnp.int32).reshape(-1)
    onehot = (e[:, None] == jnp.arange(n_experts, dtype=jnp.int32)[None, :]).astype(jnp.int32)
    before = jnp.cumsum(onehot, axis=0) - onehot
    rank = jnp.sum(before * onehot, axis=1)
    counts = jnp.sum(onehot, axis=0)
    padded = ((counts + tile - 1) // tile) * tile
    ends = jnp.cumsum(padded)
    starts = ends - padded
    pos = jnp.sum(starts[None, :] * onehot, axis=1) + rank
    n_tiles = -(-2 * rows // tile) + n_experts
    src = jnp.zeros((n_tiles * tile,), jnp.int32).at[pos].set(
        jnp.arange(2 * rows, dtype=jnp.int32) // 2)
    tile_start = jnp.arange(n_tiles, dtype=jnp.int32) * tile
    tile_expert = jnp.sum((tile_start[:, None] >= ends[None, :]).astype(jnp.int32), axis=1)
    tile_expert = jnp.minimum(tile_expert, n_experts - 1)
    n_used = (ends[-1] // tile).reshape(1)
    return pos, src, tile_expert, n_used


def _row_copy(src_hbm, src_row, dst, dst_row, sem):
    return pltpu.make_async_copy(src_hbm.at[pl.ds(src_row, 1)], dst.at[pl.ds(dst_row, 1)], sem)


def _gather_rows(idx_ref, base, stride, src_hbm, dst, sem, *, wait):
    if wait:
        pltpu.make_async_copy(src_hbm.at[pl.ds(0, dst.shape[0])], dst, sem).wait()
        return

    def body(r, carry):
        _row_copy(src_hbm, idx_ref[base + stride * r], dst, r, sem).start()
        return carry

    lax.fori_loop(0, dst.shape[0], body, 0, unroll=8)


def _moe_ffn_kernel(te_ref, nu_ref, src_ref, h_hbm, w1_ref, w3_ref, w2_ref, o_ref,
                    rows_scr, h_scr, sems):
    t = pl.program_id(0)
    f = pl.program_id(1)
    last = pl.num_programs(1) - 1
    tile = o_ref.shape[0]
    n_used = nu_ref[0]
    used = t < n_used
    slot = t % 2

    def gather(tile_idx, buf, wait):
        _gather_rows(src_ref, tile_idx * tile, 1, h_hbm, rows_scr.at[buf], sems.at[buf], wait=wait)

    @pl.when(used & (f == 0) & (t == 0))
    def _():
        gather(t, slot, False)

    @pl.when(used & (f == 0))
    def _():
        gather(t, slot, True)
        h_scr[...] = rows_scr[slot].astype(BF16)
        o_ref[...] = jnp.zeros_like(o_ref)

    @pl.when((f == 0) & (t + 1 < n_used))
    def _():
        gather(t + 1, 1 - slot, False)

    @pl.when(used)
    def _():
        h = h_scr[...]
        act = _silu(_bdot(h, w1_ref[0, 0].astype(BF16))) * _bdot(h, w3_ref[0, 0].astype(BF16))
        o_ref[...] += _bdot(act.astype(BF16), w2_ref[0, 0].astype(BF16))

    @pl.when(jnp.logical_not(used) & (f == last))
    def _():
        o_ref[...] = jnp.zeros_like(o_ref)


def _moe_ffn(h, src, tile_expert, n_used, w1, w3, w2, layer, *, tile, tf=256):
    d = h.shape[1]
    n = src.shape[0]
    ff = w1.shape[3]
    nf = ff // tf
    fsel = lambda t, f, nu: jnp.where(t < nu[0], f, nf - 1)
    up = lambda t, f, te, nu, sr: (layer, te[t], 0, fsel(t, f, nu))
    down = lambda t, f, te, nu, sr: (layer, te[t], fsel(t, f, nu), 0)
    return pl.pallas_call(
        _moe_ffn_kernel,
        grid_spec=pltpu.PrefetchScalarGridSpec(
            num_scalar_prefetch=3,
            grid=(n // tile, nf),
            in_specs=[
                pl.BlockSpec(memory_space=pl.ANY),
                pl.BlockSpec((1, 1, d, tf), up),
                pl.BlockSpec((1, 1, d, tf), up),
                pl.BlockSpec((1, 1, tf, d), down),
            ],
            out_specs=pl.BlockSpec((tile, d), lambda t, f, te, nu, sr: (t, 0)),
            scratch_shapes=[pltpu.VMEM((2, tile, d), F32), pltpu.VMEM((tile, d), BF16),
                            pltpu.SemaphoreType.DMA((2,))],
        ),
        out_shape=jax.ShapeDtypeStruct((n, d), F32),
        compiler_params=_cparams(2),
        name="moe_ffn",
    )(tile_expert, n_used, src, h, w1, w3, w2)


def _combine_kernel(pos_ref, ys_hbm, x_ref, route_ref, mod_ref, o_ref, y1_scr, y2_scr, sem):
    base = 2 * pl.program_id(0) * x_ref.shape[0]
    _gather_rows(pos_ref, base, 2, ys_hbm, y1_scr, sem, wait=False)
    _gather_rows(pos_ref, base + 1, 2, ys_hbm, y2_scr, sem, wait=False)
    _gather_rows(pos_ref, base, 2, ys_hbm, y1_scr, sem, wait=True)
    _gather_rows(pos_ref, base + 1, 2, ys_hbm, y2_scr, sem, wait=True)
    mix = route_ref[:, 2:3] * y1_scr[...] + route_ref[:, 3:4] * y2_scr[...]
    o_ref[...] = x_ref[...] + mod_ref[0, 5:6, :] * mix


def _moe_combine(x, mods, route, ys, pos, *, rows, seq, batch, tm=256):
    d = x.shape[1]
    return pl.pallas_call(
        _combine_kernel,
        grid_spec=pltpu.PrefetchScalarGridSpec(
            num_scalar_prefetch=1,
            grid=(rows // tm,),
            in_specs=[
                pl.BlockSpec(memory_space=pl.ANY),
                pl.BlockSpec((tm, d), lambda m, pos: (m, 0)),
                pl.BlockSpec((tm, LANE), lambda m, pos: (m, 0)),
                pl.BlockSpec((1, 6, d), lambda m, pos: (_group_of(m, tm, seq, batch), 0, 0)),
            ],
            out_specs=pl.BlockSpec((tm, d), lambda m, pos: (m, 0)),
            scratch_shapes=[pltpu.VMEM((tm, d), F32), pltpu.VMEM((tm, d), F32),
                            pltpu.SemaphoreType.DMA(())],
        ),
        out_shape=jax.ShapeDtypeStruct((rows, d), F32),
        compiler_params=_cparams(1),
        name="moe_combine",
    )(pos, ys, x, route, mods)


def _moe_top2(x, nw, mods, router, w1, w3, w2, layer, *, rows, seq, batch, tm, tile=768):
    tok = dict(rows=rows, seq=seq, batch=batch)
    h, route = _moe_route(x, nw, mods, router, tm=tm, **tok)
    pos, src, tile_expert, n_used = _moe_plan(route, w1.shape[1], tile)
    ys = _moe_ffn(h, src, tile_expert, n_used, w1, w3, w2, layer, tile=tile)
    return _moe_combine(x, mods, route, ys, pos, tm=min(tm, 256), **tok)


def _final_norm_kernel(x_ref, w_ref, o_ref):
    x = x_ref[...]
    ms = jnp.mean(x * x, axis=-1, keepdims=True)
    o_ref[...] = x * lax.rsqrt(ms + NORM_EPS) * w_ref[...]


def _final_norm(x, w, *, rows, tm=512):
    d = x.shape[1]
    return pl.pallas_call(
        _final_norm_kernel,
        grid=(rows // tm,),
        in_specs=[pl.BlockSpec((tm, d), lambda m: (m, 0)), pl.BlockSpec((1, d), lambda m: (0, 0))],
        out_specs=pl.BlockSpec((tm, d), lambda m: (m, 0)),
        out_shape=jax.ShapeDtypeStruct((rows, d), F32),
        compiler_params=_cparams(1),
        name="final_norm",
    )(x, w.reshape(1, d))


def _dft_angles(n, period):
    k = lax.broadcasted_iota(jnp.int32, (n, n), 0)
    s = lax.broadcasted_iota(jnp.int32, (n, n), 1)
    return ((k * s) % period).astype(F32) * (2.0 * math.pi / period)


def _hyena_dft(seq_len):
    n = 2 * seq_len
    half = seq_len // 2
    row = lax.broadcasted_iota(jnp.int32, (seq_len, seq_len), 0)
    col = lax.broadcasted_iota(jnp.int32, (seq_len, seq_len), 1)
    freq = lambda i: jnp.where(i < half, 2 * i, 2 * (i - half) + 1)
    to_angle = lambda prod: (prod % n).astype(F32) * (2.0 * math.pi / n)
    ang_f = to_angle(freq(row) * col)
    ang_i = to_angle(row * freq(col))
    fc = jnp.cos(ang_f)
    fs = jnp.where(row == 0, jnp.where(col % 2 == 0, 1.0, -1.0), jnp.sin(ang_f))
    ic = jnp.where(col == 0, 1.0 / n, 2.0 / n * jnp.cos(ang_i))
    isn = jnp.where(col == 0, jnp.where(row % 2 == 0, 1.0, -1.0) / n, 2.0 / n * jnp.sin(ang_i))
    return (fc.astype(BF16), fs.astype(BF16), ic.astype(BF16), isn.astype(BF16),
            _flip_matrix(half))


def _hyena_feats(seq_len):
    pos = np.arange(seq_len, dtype=np.float64)
    t = pos / max(seq_len - 1, 1)
    bands = np.linspace(1e-4, HY_BANDS - 1, HY_BANDS).astype(np.float32).astype(np.float64)
    ang = 2 * math.pi * pos[:, None] * bands[None, :] / seq_len
    feats = np.concatenate([t[:, None], np.cos(ang), -np.sin(ang)], axis=-1)
    out = np.zeros((seq_len, LANE), np.float32)
    out[:, :HY_POS_DIM] = feats
    return jnp.asarray(out)


def _split2(a):
    hi = a.astype(BF16)
    return hi, (a - hi.astype(F32)).astype(BF16)


def _dot3(a, b):
    a_hi, a_lo = _split2(a)
    b_hi, b_lo = _split2(b)
    return _bdot(a_hi, b_hi) + _bdot(a_hi, b_lo) + _bdot(a_lo, b_hi)


def _filter_kernel(feat_ref, w1_ref, b1_ref, w2_ref, b2_ref, fr_ref, w3f_ref, w3b_ref,
                   dcf_ref, dcb_ref, fc_ref, fs_ref, o_ref, h_scr):
    @pl.when((pl.program_id(1) == 0) & (pl.program_id(2) == 0))
    def _():
        hp = lax.Precision.HIGHEST
        h1 = jnp.sin(fr_ref[0, 0:1, :] * (jnp.dot(feat_ref[...], w1_ref[0], precision=hp,
                                                  preferred_element_type=F32) + b1_ref[0]))
        h_scr[...] = jnp.sin(fr_ref[0, 1:2, :] * (jnp.dot(h1, w2_ref[0], precision=hp,
                                                          preferred_element_type=F32) + b2_ref[0]))

    h = h_scr[...]
    t = feat_ref[:, 0:1]
    hf = _dot3(h, w3f_ref[0]) * jnp.exp(-t * dcf_ref[0])
    hb = _dot3(h, w3b_ref[0]) * jnp.exp(-t * dcb_ref[0])
    row = lax.broadcasted_iota(jnp.int32, hf.shape, 0)
    hb = jnp.where(row == 0, 0.0, hb)
    ss = jnp.sum(hf * hf + hb * hb, axis=0, keepdims=True)
    scale = lax.rsqrt(ss + NORM_EPS)
    even = hf * scale + hb * scale
    odd = hf * scale - hb * scale
    kc = _bdot(fc_ref[...], even.astype(BF16))
    ks = _bdot(fs_ref[...], odd.astype(BF16))
    sign = jnp.where(row % 2 == 0, 1.0, -1.0)
    nyq = jnp.sum(even * sign, axis=0, keepdims=True)
    o_ref[0, 0] = kc
    o_ref[0, 1] = jnp.where(row == 0, nyq, ks)


def _hyena_filters(seq_len, w1p, b1, w2, b2, w3, freq, decay, fc, fs, *, tn=256):
    depth = w1p.shape[0]
    hid = HY_FILTER_HIDDEN
    nb = HY_W // tn
    feats = _hyena_feats(seq_len)
    col = lambda d: (lambda l, o, c: (l, 0, (2 * o + d) * nb + c))
    lyr = lambda l, o, c: (l, 0, 0)
    return pl.pallas_call(
        _filter_kernel,
        grid=(depth, 2, nb),
        in_specs=[
            pl.BlockSpec((seq_len, LANE), lambda l, o, c: (0, 0)),
            pl.BlockSpec((1, LANE, hid), lyr),
            pl.BlockSpec((1, 1, hid), lyr),
            pl.BlockSpec((1, hid, hid), lyr),
            pl.BlockSpec((1, 1, hid), lyr),
            pl.BlockSpec((1, 2, hid), lyr),
            pl.BlockSpec((1, hid, tn), col(0)),
            pl.BlockSpec((1, hid, tn), col(1)),
            pl.BlockSpec((1, 1, tn), col(0)),
            pl.BlockSpec((1, 1, tn), col(1)),
            _resident((seq_len, seq_len), lambda l, o, c: (0, 0)),
            _resident((seq_len, seq_len), lambda l, o, c: (0, 0)),
        ],
        out_specs=pl.BlockSpec((1, 2, seq_len, tn), lambda l, o, c: (l, 0, 0, o * nb + c)),
        out_shape=jax.ShapeDtypeStruct((depth, 2, seq_len, 2 * HY_W), F32),
        scratch_shapes=[pltpu.VMEM((seq_len, hid), F32)],
        compiler_params=_cparams(3),
        name="hyena_filters",
    )(feats, w1p, b1.reshape(depth, 1, hid), w2, b2.reshape(depth, 1, hid), freq,
      w3, w3, decay.reshape(depth, 1, 4 * HY_W), decay.reshape(depth, 1, 4 * HY_W), fc, fs)


def _conv3(z, w_ref, b_ref):
    n = z.shape[0]
    row = lax.broadcasted_iota(jnp.int32, z.shape, 0)
    prev = jnp.where(row == 0, 0.0, pltpu.roll(z, 1, 0))
    nxt = jnp.where(row == n - 1, 0.0, pltpu.roll(z, n - 1, 0))
    return prev * w_ref[0:1, :] + z * w_ref[1:2, :] + nxt * w_ref[2:3, :] + b_ref[...]


def _longconv_kernel(a_ref, g_ref, aw_ref, ab_ref, gw_ref, gb_ref, kf_ref, skip_ref,
                     fc_ref, fs_ref, ic_ref, is_ref, flip_ref, o_ref, *, conv_a):
    a = a_ref[...].astype(F32)
    if conv_a:
        a = _conv3(a, aw_ref, ab_ref)
    n = a.shape[0]
    h = n // 2
    lo = slice(0, h)
    hi = slice(h, n)
    k = lax.broadcasted_iota(jnp.int32, (h, a.shape[1]), 0)
    alt = jnp.where(k % 2 == 0, 1.0, -1.0)
    first = k == 0
    a_lo = a[lo]
    a_hi = a[hi]
    a_mid = a_hi[0:1, :]
    rev = _bdot(flip_ref[...], a_hi.astype(BF16))
    even_in = (a_lo + rev).astype(BF16)
    odd_in = (a_lo - rev).astype(BF16)
    nyq = jnp.sum(alt * (a_lo + a_hi), axis=0, keepdims=True)
    ac_e = _bdot(fc_ref[lo, lo], even_in) + alt * a_mid
    as_e = jnp.where(first, nyq, _bdot(fs_ref[lo, lo], odd_in))
    ac_o = _bdot(fc_ref[hi, lo], odd_in)
    as_o = _bdot(fs_ref[hi, lo], even_in) + alt * a_mid
    kc_e = kf_ref[0, 0, lo, :]
    ks_e = kf_ref[0, 1, lo, :]
    kc_o = kf_ref[0, 0, hi, :]
    ks_o = kf_ref[0, 1, hi, :]
    cross = as_e * ks_e
    yc_e = jnp.where(first, ac_e * kc_e, ac_e * kc_e - cross)
    ys_e = jnp.where(first, 0.0, ac_e * ks_e + as_e * kc_e)
    y_nyq = cross[0:1, :]
    yc_o = ac_o * kc_o - as_o * ks_o
    ys_o = ac_o * ks_o + as_o * kc_o
    sym = (_bdot(ic_ref[lo, lo], yc_e.astype(BF16)) + _bdot(is_ref[lo, hi], ys_o.astype(BF16))
           + alt * (y_nyq * (0.5 / n)))
    anti = _bdot(ic_ref[lo, hi], yc_o.astype(BF16)) + _bdot(is_ref[lo, lo], ys_e.astype(BF16))
    y_mid = (jnp.sum(alt * (yc_e + ys_o), axis=0, keepdims=True) * (1.0 / n)
             - yc_e[0:1, :] * (0.5 / n) + y_nyq * (0.5 / n))
    y_hi = jnp.where(first, y_mid, _bdot(flip_ref[...], (sym - anti).astype(BF16)))
    conv = jnp.concatenate([sym + anti, y_hi], axis=0)
    gate = _conv3(g_ref[...].astype(F32), gw_ref, gb_ref)
    o_ref[...] = (gate * (conv + a * skip_ref[0])).astype(o_ref.dtype)


def _hyena_longconv(a_src, a_part, zm, gate_part, conv_w, conv_b, kf, layer, order, skip,
                    dft, *, seq_len, n_seq, row0, out_rows, prev_out=None, tn=256):
    fc, fs, ic, isn, flip = dft
    assert seq_len % 4 == 0
    nb = HY_W // tn
    rb0 = row0 // seq_len
    const = lambda c, b: (0, 0)
    conv_a = a_part is not None
    aw_blk = a_part * nb if conv_a else 0
    a_blk = ZM_HY // tn + aw_blk if conv_a else 0
    gw_blk = gate_part * nb
    g_blk = ZM_HY // tn + gw_blk
    kernel = functools.partial(_longconv_kernel, conv_a=conv_a)
    in_specs = [
        pl.BlockSpec((seq_len, tn), lambda c, b: (rb0 + b, a_blk + c)),
        pl.BlockSpec((seq_len, tn), lambda c, b: (rb0 + b, g_blk + c)),
        pl.BlockSpec((3, tn), lambda c, b: (0, aw_blk + c)),
        pl.BlockSpec((1, tn), lambda c, b: (0, aw_blk + c)),
        pl.BlockSpec((3, tn), lambda c, b: (0, gw_blk + c)),
        pl.BlockSpec((1, tn), lambda c, b: (0, gw_blk + c)),
        _resident((1, 2, seq_len, tn), lambda c, b: (layer, 0, 0, order * nb + c)),
        pl.BlockSpec((1, 1, tn), lambda c, b: (order, 0, c)),
        _resident((seq_len, seq_len), const),
        _resident((seq_len, seq_len), const),
        _resident((seq_len, seq_len), const),
        _resident((seq_len, seq_len), const),
        _resident((seq_len // 2, seq_len // 2), const),
    ]
    args = [a_src, zm, conv_w, conv_b.reshape(1, -1), conv_w, conv_b.reshape(1, -1), kf,
            skip.reshape(2, 1, HY_W), fc, fs, ic, isn, flip]
    aliases = {}
    if prev_out is not None:
        in_specs.append(pl.BlockSpec(memory_space=pl.ANY))
        args.append(prev_out)
        aliases = {len(args) - 1: 0}
        kernel = functools.partial(_drop_last_input, kernel, 13)
    return pl.pallas_call(
        kernel,
        grid=(nb, n_seq),
        in_specs=in_specs,
        out_specs=pl.BlockSpec((seq_len, tn), lambda c, b: (rb0 + b, c)),
        out_shape=jax.ShapeDtypeStruct((out_rows, HY_W), BF16),
        input_output_aliases=aliases,
        compiler_params=_cparams(2),
        name="hyena_longconv",
    )(*args)


def _drop_last_input(kernel, n_in, *refs):
    return kernel(*refs[:n_in], *refs[n_in + 1:])


def _flip_matrix(half):
    r = lax.broadcasted_iota(jnp.int32, (half, half), 0)
    c = lax.broadcasted_iota(jnp.int32, (half, half), 1)
    return jnp.where((r >= 1) & (c == half - r), 1.0, 0.0).astype(BF16)


def _fnet_scale(seq_len):
    return 1.0 / math.sqrt(seq_len * FN_GROUP_W)


def _fnet_dft(seq_len):
    half = seq_len // 2
    ang_l = _dft_angles(seq_len, seq_len)[:half]
    ang_w = _dft_angles(FN_GROUP_W, FN_GROUP_W)
    scale = _fnet_scale(seq_len)
    return ((jnp.cos(ang_l) * scale).astype(BF16), (jnp.sin(ang_l) * scale).astype(BF16),
            jnp.cos(ang_w).astype(BF16), jnp.sin(ang_w).astype(BF16), _flip_matrix(half))


def _fnet_kernel(z_ref, cl_ref, sl_ref, cw_ref, sw_ref, flip_ref, o_ref, *, scale):
    z = z_ref[...]
    gc, gs = [], []
    for j in range(z.shape[1] // FN_GROUP_W):
        zj = z[:, j * FN_GROUP_W:(j + 1) * FN_GROUP_W]
        gc.append(_bdot(zj, cw_ref[...]).astype(BF16))
        gs.append(_bdot(zj, sw_ref[...]).astype(BF16))
    gc = jnp.concatenate(gc, axis=-1)
    gs = jnp.concatenate(gs, axis=-1)
    p = _bdot(cl_ref[...], gc)
    q = _bdot(sl_ref[...], gs)
    pos = lax.broadcasted_iota(jnp.int32, gc.shape, 0)
    y_mid = scale * jnp.sum(jnp.where(pos % 2 == 0, 1.0, -1.0) * gc.astype(F32), axis=0,
                            keepdims=True)
    first = lax.broadcasted_iota(jnp.int32, p.shape, 0) == 0
    y_hi = jnp.where(first, y_mid, _bdot(flip_ref[...], (p + q).astype(BF16)))
    o_ref[...] = jnp.concatenate([p - q, y_hi], axis=0).astype(o_ref.dtype)


def _fnet(zm, dft, *, seq_len, n_seq, row0, out_rows, prev_out=None, tn=512):
    cl, sl, cw, sw, flip = dft
    nb = FN_W // tn
    rb0 = row0 // seq_len
    c0 = ZM_FN // tn
    half = seq_len // 2
    const = lambda c, b: (0, 0)
    in_specs = [
        pl.BlockSpec((seq_len, tn), lambda c, b: (rb0 + b, c0 + c)),
        _resident((half, seq_len), const),
        _resident((half, seq_len), const),
        pl.BlockSpec((FN_GROUP_W, FN_GROUP_W), const),
        pl.BlockSpec((FN_GROUP_W, FN_GROUP_W), const),
        _resident((half, half), const),
    ]
    args = [zm, cl, sl, cw, sw, flip]
    kernel = functools.partial(_fnet_kernel, scale=_fnet_scale(seq_len))
    aliases = {}
    if prev_out is not None:
        in_specs.append(pl.BlockSpec(memory_space=pl.ANY))
        args.append(prev_out)
        aliases = {len(args) - 1: 0}
        kernel = functools.partial(_drop_last_input, kernel, 6)
    return pl.pallas_call(
        kernel,
        grid=(nb, n_seq),
        in_specs=in_specs,
        out_specs=pl.BlockSpec((seq_len, tn), lambda c, b: (rb0 + b, c)),
        out_shape=jax.ShapeDtypeStruct((out_rows, FN_W), BF16),
        input_output_aliases=aliases,
        compiler_params=_cparams(2),
        name="fnet",
    )(*args)


def _hgrn_tables(reverse):
    c = HG_CHUNK
    tri = np.zeros((c, c), np.float32)
    for t in range(c):
        if reverse:
            tri[t, t:] = 1.0
        else:
            tri[t, :t + 1] = 1.0
    mats = [tri]
    masks = []
    for lvl in range(HG_LEVELS):
        half = (c // 2) >> lvl
        sel = np.zeros((c, c), np.float32)
        mask = np.zeros((c, c), np.float32)
        for t in range(c):
            start = (t // (2 * half)) * 2 * half
            mid = start + half
            sel[t, mid if reverse else mid - 1] = 1.0
            for s in range(start, start + 2 * half):
                if reverse and t < mid <= s:
                    mask[t, s] = 1.0
                if (not reverse) and s < mid <= t:
                    mask[t, s] = 1.0
        mats.append(tri - sel @ tri)
        masks.append(mask)
    half = c // 2
    sel = np.zeros((c, c), np.float32)
    mask = np.zeros((c, c), np.float32)
    for t in range(c):
        start = (t // half) * half
        sel[t, start + half // 2] = 1.0
        for s in range(start, start + half):
            if (s >= t) if reverse else (s <= t):
                mask[t, s] = 1.0
    mats.append(tri - sel @ tri)
    masks.append(mask)
    mats = [mats[0], mats[1], mats[-1]] + mats[2:-1]
    return np.concatenate(mats, axis=0), np.stack(masks, axis=0)


HG_BOUNDED_BLOCKS = 3
HG_MID_BLOCK = 2


def _hgrn_level_block(lvl):
    return 1 if lvl == 0 else HG_BOUNDED_BLOCKS + lvl - 1


def _hgrn_decays(zf_ref, row0, lb_ref, tab_ref, k_scr, p_scr, dec_scr):
    lb = lb_ref[...]
    f = lb + (1.0 - lb) * jax.nn.sigmoid(zf_ref[row0:row0 + HG_CHUNK, :])
    lf = jnp.log2(f)
    k_scr[...] = (1.0 - f).astype(BF16)
    p1 = lf.astype(BF16)
    r1 = lf - p1.astype(F32)
    p2 = r1.astype(BF16)
    p3 = (r1 - p2.astype(F32)).astype(BF16)
    p_scr[...] = jnp.concatenate([p1, p2, p3], axis=0)
    n = HG_BOUNDED_BLOCKS * HG_CHUNK
    dec_scr[0:n, :] = _bdot(tab_ref[0:n, :], p_scr[...])


def _hgrn_more_decays(tab_ref, p_scr, dec_scr):
    n = HG_BOUNDED_BLOCKS * HG_CHUNK
    dec_scr[n:, :] = _bdot(tab_ref[n:, :], p_scr[...])


def _hgrn_direction(q_ref, v_ref, row0, k_scr, dec_scr, mask_ref, st_scr, o_ref, end_row,
                    bounded):
    c = HG_CHUNK
    heads = lambda ref, r0: jnp.stack(
        [ref[r0:r0 + c, h * HG_DK:(h + 1) * HG_DK] for h in range(HG_HEADS)], axis=0)
    bmm = lambda eq, a, b: jnp.einsum(eq, a, b, preferred_element_type=F32)
    q = heads(q_ref, row0)
    k = heads(k_scr, 0)
    v = heads(v_ref, row0)
    b = heads(dec_scr, 0)
    st = st_scr[...]
    o = bmm('htk,hkv->htv', q * jnp.exp2(b).astype(BF16), st.astype(BF16))
    level = lambda l: jnp.exp2(-jnp.abs(heads(dec_scr, _hgrn_level_block(l) * c))).astype(BF16)
    if bounded:
        e = level(0)
        att = jnp.where(mask_ref[0] > 0.5, bmm('htk,hsk->hts', q * e, k * e), 0.0)
        d = heads(dec_scr, HG_MID_BLOCK * c)
        inner = bmm('htk,hsk->hts', q * jnp.exp2(d).astype(BF16), k * jnp.exp2(-d).astype(BF16))
        att += jnp.where(mask_ref[HG_LEVELS] > 0.5, inner, 0.0)
    else:
        diag = jnp.sum(q.astype(F32) * k.astype(F32), axis=-1, keepdims=True)
        o += diag * v.astype(F32)
        att = jnp.zeros((HG_HEADS, c, c), F32)
        for lvl in range(HG_LEVELS):
            e = level(lvl)
            att += bmm('htk,hsk->hts', q * e, k * e) * mask_ref[lvl]
    o += bmm('hts,hsv->htv', att.astype(BF16), v)
    for h in range(HG_HEADS):
        o_ref[row0:row0 + c, h * HG_DK:(h + 1) * HG_DK] = o[h]
    b_end = b[:, end_row:end_row + 1, :]
    upd = bmm('htk,htv->hkv', k * jnp.exp2(b_end - b).astype(BF16), v)
    decay = jnp.swapaxes(jnp.broadcast_to(jnp.exp2(b_end), (HG_HEADS, HG_DK, HG_DK)), 1, 2)
    st_scr[...] = decay * st + upd


HG_MAX_LOG2_SPAN = 100.0
HG_MAX_ABS_Q = 1e6


def _hgrn_kernel(qf_ref, vf_ref, zff_ref, qb_ref, vb_ref, zfb_ref, lbf_ref, lbb_ref,
                 tabf_ref, maskf_ref, tabb_ref, maskb_ref, s0f_ref, s0b_ref,
                 of_ref, ob_ref, sf_ref, sb_ref, stf_scr, stb_scr, kf_scr, kb_scr,
                 pf_scr, pb_scr, decf_scr, decb_scr):
    ci = pl.program_id(1)
    c = HG_CHUNK

    @pl.when(ci == 0)
    def _():
        stf_scr[...] = s0f_ref[0]
        stb_scr[...] = s0b_ref[0]

    q_abs = jnp.maximum(jnp.max(jnp.abs(qf_ref[...].astype(F32))),
                        jnp.max(jnp.abs(qb_ref[...].astype(F32))))
    mid = slice(HG_MID_BLOCK * c, (HG_MID_BLOCK + 1) * c)
    n_sub = qf_ref.shape[0] // c
    for sub in range(n_sub):
        rf = sub * c
        rb = (n_sub - 1 - sub) * c
        _hgrn_decays(zff_ref, rf, lbf_ref, tabf_ref, kf_scr, pf_scr, decf_scr)
        _hgrn_decays(zfb_ref, rb, lbb_ref, tabb_ref, kb_scr, pb_scr, decb_scr)
        span = jnp.maximum(jnp.max(jnp.abs(decf_scr[mid, :])), jnp.max(jnp.abs(decb_scr[mid, :])))
        bounded = (span < HG_MAX_LOG2_SPAN) & (q_abs < HG_MAX_ABS_Q)

        def run(flag, rf=rf, rb=rb):
            _hgrn_direction(qf_ref, vf_ref, rf, kf_scr, decf_scr, maskf_ref, stf_scr, of_ref,
                            c - 1, flag)
            _hgrn_direction(qb_ref, vb_ref, rb, kb_scr, decb_scr, maskb_ref, stb_scr, ob_ref,
                            0, flag)

        @pl.when(bounded)
        def _():
            run(True)

        @pl.when(jnp.logical_not(bounded))
        def _():
            _hgrn_more_decays(tabf_ref, pf_scr, decf_scr)
            _hgrn_more_decays(tabb_ref, pb_scr, decb_scr)
            run(False)

    @pl.when(ci == pl.num_programs(1) - 1)
    def _():
        sf_ref[0] = stf_scr[...]
        sb_ref[0] = stb_scr[...]


def _hgrn(zm, zf, lb_f, lb_b, s0_f, s0_b, *, seq_len, n_seq, row0, out_rows,
          prev_of=None, prev_ob=None):
    c = HG_CHUNK
    blk = c * HG_STEP_CHUNKS
    nc = seq_len // blk
    rb0 = row0 // blk
    w = HG_W
    tab_f, mask_f = _hgrn_tables(False)
    tab_b, mask_b = _hgrn_tables(True)
    fwd = lambda col: (lambda b, i: (rb0 + b * nc + i, col))
    bwd = lambda col: (lambda b, i: (rb0 + b * nc + nc - 1 - i, col))
    const2 = lambda b, i: (0, 0)
    const3 = lambda b, i: (0, 0, 0)
    state = lambda b, i: (b, 0, 0, 0)
    nt = (2 + HG_LEVELS) * c
    in_specs = [
        pl.BlockSpec((blk, w), fwd(ZM_Q // w)),
        pl.BlockSpec((blk, w), fwd(ZM_I // w)),
        pl.BlockSpec((blk, w), fwd(0)),
        pl.BlockSpec((blk, w), bwd(ZM_Q // w)),
        pl.BlockSpec((blk, w), bwd(ZM_I // w)),
        pl.BlockSpec((blk, w), bwd(1)),
        pl.BlockSpec((1, w), const2),
        pl.BlockSpec((1, w), const2),
        pl.BlockSpec((nt, 3 * c), const2),
        pl.BlockSpec((HG_LEVELS + 1, c, c), const3),
        pl.BlockSpec((nt, 3 * c), const2),
        pl.BlockSpec((HG_LEVELS + 1, c, c), const3),
        pl.BlockSpec((1, HG_HEADS, HG_DK, HG_DK), state),
        pl.BlockSpec((1, HG_HEADS, HG_DK, HG_DK), state),
    ]
    args = [zm, zm, zf, zm, zm, zf, lb_f.reshape(1, w), lb_b.reshape(1, w),
            jnp.asarray(np.tile(tab_f, (1, 3)), BF16), jnp.asarray(mask_f),
            jnp.asarray(np.tile(tab_b, (1, 3)), BF16), jnp.asarray(mask_b), s0_f, s0_b]
    kernel = _hgrn_kernel
    aliases = {}
    if prev_of is not None:
        in_specs += [pl.BlockSpec(memory_space=pl.ANY), pl.BlockSpec(memory_space=pl.ANY)]
        args += [prev_of, prev_ob]
        aliases = {14: 0, 15: 1}
        kernel = functools.partial(_hgrn_alias_kernel, 14)
    st_shape = jax.ShapeDtypeStruct((n_seq, HG_HEADS, HG_DK, HG_DK), F32)
    return pl.pallas_call(
        kernel,
        grid=(n_seq, nc),
        in_specs=in_specs,
        out_specs=[
            pl.BlockSpec((blk, w), fwd(0)),
            pl.BlockSpec((blk, w), bwd(0)),
            pl.BlockSpec((1, HG_HEADS, HG_DK, HG_DK), state),
            pl.BlockSpec((1, HG_HEADS, HG_DK, HG_DK), state),
        ],
        out_shape=[jax.ShapeDtypeStruct((out_rows, w), F32),
                   jax.ShapeDtypeStruct((out_rows, w), F32), st_shape, st_shape],
        scratch_shapes=[pltpu.VMEM((HG_HEADS, HG_DK, HG_DK), F32),
                        pltpu.VMEM((HG_HEADS, HG_DK, HG_DK), F32),
                        pltpu.VMEM((c, w), BF16), pltpu.VMEM((c, w), BF16),
                        pltpu.VMEM((3 * c, w), BF16), pltpu.VMEM((3 * c, w), BF16),
                        pltpu.VMEM((nt, w), F32), pltpu.VMEM((nt, w), F32)],
        input_output_aliases=aliases,
        compiler_params=_cparams(2),
        name="hgrn2",
    )(*args)


def _hgrn_alias_kernel(n_in, *refs):
    return _hgrn_kernel(*refs[:n_in], *refs[n_in + 2:])


def kernel(x, c, ctx, c_ctx, w_mod, b_mod, norm1_w, norm2_w, w_in, hy_conv_w, hy_conv_b,
           hy_f_w1, hy_f_b1, hy_f_w2, hy_f_b2, hy_f_w3, hy_f_freq, hy_decay, hy_skip,
           hg_lower, hg_norm_w, w_br_hy, w_br_fn, w_br_hg, w_out,
           ffn_w1, ffn_w3, ffn_w2, moe_router, moe_w1, moe_w3, moe_w2, final_norm_w):
    batch, seq, d = x.shape
    l_ctx = ctx.shape[1]
    depth = w_in.shape[0]
    n_lat = batch * seq
    n_all = n_lat + batch * l_ctx

    lb_all = jnp.cumsum(jax.nn.softmax(hg_lower.astype(F32), axis=1), axis=1)
    lb_all = lb_all - lb_all[:, :1]

    cc = jnp.zeros((8, d), F32).at[:batch].set(c).at[batch].set(c_ctx)
    mods = _modulation(cc, w_mod, b_mod).reshape(depth, 8, 6, d)

    w1p = jnp.zeros((depth, LANE, HY_FILTER_HIDDEN), F32).at[:, :HY_POS_DIM].set(hy_f_w1)
    dft_lat = _hyena_dft(seq)
    dft_ctx = _hyena_dft(l_ctx)
    kf_lat = _hyena_filters(seq, w1p, hy_f_b1, hy_f_w2, hy_f_b2, hy_f_w3, hy_f_freq, hy_decay,
                            dft_lat[0], dft_lat[1])
    kf_ctx = _hyena_filters(l_ctx, w1p, hy_f_b1, hy_f_w2, hy_f_b2, hy_f_w3, hy_f_freq, hy_decay,
                            dft_ctx[0], dft_ctx[1])
    fn_lat = _fnet_dft(seq)
    fn_ctx = _fnet_dft(l_ctx)

    rows_x = jnp.concatenate([x.reshape(n_lat, d), ctx.reshape(batch * l_ctx, d)], axis=0)
    zero_state = jnp.zeros((batch, HG_HEADS, HG_DK, HG_DK), F32)
    tile_cap = math.gcd(seq, batch * l_ctx)
    tok = dict(seq=seq, batch=batch)
    tile = lambda t: dict(tm=min(t, tile_cap))
    lat = dict(seq_len=seq, n_seq=batch, row0=0, out_rows=n_all)
    cx = dict(seq_len=l_ctx, n_seq=batch, row0=n_lat, out_rows=n_all)

    for l in range(depth):
        last = l == depth - 1
        rows = n_lat if last else n_all
        m = mods[l]

        h = _norm_rows(rows_x, norm1_w[l], m, **tok, **tile(512))
        zm, zf = _in_proj(h, w_in, l, **tile(1024))

        of, ob, s_f, s_b = _hgrn(zm, zf, lb_all[0, l], lb_all[1, l], zero_state, zero_state, **cx)
        of, ob, _, _ = _hgrn(zm, zf, lb_all[0, l], lb_all[1, l], s_f, s_b, prev_of=of,
                             prev_ob=ob, **lat)

        hy_args = (hy_conv_w[l], hy_conv_b[l])
        mix = dict(lat, out_rows=rows)
        u = _hyena_longconv(zm, 0, zm, 1, *hy_args, kf_lat, l, 0, hy_skip[l], dft_lat, **mix)
        y_hy = _hyena_longconv(u, None, zm, 2, *hy_args, kf_lat, l, 1, hy_skip[l], dft_lat, **mix)
        y_fn = _fnet(zm, fn_lat, **mix)
        if not last:
            u = _hyena_longconv(zm, 0, zm, 1, *hy_args, kf_ctx, l, 0, hy_skip[l], dft_ctx,
                                prev_out=u, **cx)
            y_hy = _hyena_longconv(u, None, zm, 2, *hy_args, kf_ctx, l, 1, hy_skip[l], dft_ctx,
                                   prev_out=y_hy, **cx)
            y_fn = _fnet(zm, fn_ctx, prev_out=y_fn, **cx)

        rows_x = _merge_out(y_hy, y_fn, of, ob, zm, rows_x, hg_norm_w[l], m,
                            w_br_hy[l].astype(BF16), w_br_fn[l].astype(BF16),
                            w_br_hg[l].astype(BF16), w_out[l].astype(BF16), rows=rows,
                            **tok, **tile(256))

        if l % 2 == 0:
            i = l // 2
            rows_x = _ffn_dense(rows_x, norm2_w[l], m, ffn_w1, ffn_w3, ffn_w2, i, rows=rows,
                                **tok, **tile(1024))
        else:
            i = l // 2
            rows_x = _moe_top2(rows_x, norm2_w[l], m, moe_router[i], moe_w1, moe_w3, moe_w2, i,
                               rows=rows, **tok, **tile(512))

    out = _final_norm(rows_x, final_norm_w, rows=n_lat, **tile(512))
    return out.reshape(batch, seq, d)
```

```python
import functools
import math

import numpy as np
import jax
import jax.numpy as jnp
from jax import lax
from jax.experimental import pallas as pl
from jax.experimental.pallas import tpu as pltpu

F32 = jnp.float32
BF16 = jnp.bfloat16
NORM_EPS = 1e-6

D_MODEL = 2048
DEPTH = 4
HY_W = 1024
HY_BANDS = 16
HY_POS_DIM = 2 * HY_BANDS + 1
HY_FILTER_HIDDEN = 64
FN_W = 1024
FN_GROUP_W = 128
HG_W = 1024
HG_HEADS = 8
HG_DK = 128
HG_CHUNK = 128
HG_LEVELS = 7
HG_STEP_CHUNKS = 1
N_EXPERTS = 8
LANE = 128

HG_FF = 3 * HY_W + FN_W + 2 * HG_W
HG_I = HG_FF + 2 * HG_W
ZM_HY = 0
ZM_FN = ZM_HY + 3 * HY_W
ZM_Q = ZM_FN + FN_W
ZM_G = ZM_Q + HG_W
ZM_I = HG_I
ZM_GATE = HG_I + HG_W
ZM_COLS = ZM_GATE + 3 * D_MODEL

VMEM_LIMIT = 56 * 1024 * 1024


def _cparams(n_axes, vmem=VMEM_LIMIT):
    return pltpu.CompilerParams(dimension_semantics=("arbitrary",) * n_axes, vmem_limit_bytes=vmem)


def _resident(shape, index_map):
    return pl.BlockSpec(shape, index_map, pipeline_mode=pl.Buffered(1))


def _silu(v):
    return v * jax.nn.sigmoid(v)


def _bdot(a, b):
    return jnp.dot(a, b, preferred_element_type=F32)


def _mod_kernel(c_ref, w_ref, b_ref, o_ref):
    c = _silu(c_ref[...])
    o_ref[0] = jnp.dot(c, w_ref[0], preferred_element_type=F32,
                       precision=lax.Precision.HIGHEST) + b_ref[0]


def _modulation(cc, w_mod, b_mod):
    depth, d, n6 = w_mod.shape
    tn = 2048
    return pl.pallas_call(
        _mod_kernel,
        grid=(depth, n6 // tn),
        in_specs=[
            pl.BlockSpec((8, d), lambda l, n: (0, 0)),
            pl.BlockSpec((1, d, tn), lambda l, n: (l, 0, n)),
            pl.BlockSpec((1, 1, tn), lambda l, n: (l, 0, n)),
        ],
        out_specs=pl.BlockSpec((1, 8, tn), lambda l, n: (l, 0, n)),
        out_shape=jax.ShapeDtypeStruct((depth, 8, n6), F32),
        compiler_params=_cparams(2),
        name="modulation",
    )(cc, w_mod, b_mod.reshape(depth, 1, n6))


def _norm_mod(x, nw, shift, scale):
    ms = jnp.mean(x * x, axis=-1, keepdims=True)
    return (x * lax.rsqrt(ms + NORM_EPS) * nw) * (1 + scale) + shift


def _group_of(row_block, tm, seq, batch):
    return jnp.minimum((row_block * tm) // seq, batch)


def _norm_rows_kernel(x_ref, nw_ref, mod_ref, o_ref):
    h = _norm_mod(x_ref[...], nw_ref[...], mod_ref[0, 0:1, :], mod_ref[0, 1:2, :])
    o_ref[...] = h.astype(o_ref.dtype)


def _norm_rows(x, nw, mods, *, seq, batch, tm=512):
    rows, d = x.shape
    return pl.pallas_call(
        _norm_rows_kernel,
        grid=(rows // tm,),
        in_specs=[
            pl.BlockSpec((tm, d), lambda m: (m, 0)),
            pl.BlockSpec((1, d), lambda m: (0, 0)),
            pl.BlockSpec((1, 6, d), lambda m: (_group_of(m, tm, seq, batch), 0, 0)),
        ],
        out_specs=pl.BlockSpec((tm, d), lambda m: (m, 0)),
        out_shape=jax.ShapeDtypeStruct((rows, d), BF16),
        compiler_params=_cparams(1),
        name="norm_rows",
    )(x, nw.reshape(1, d), mods)


def _inproj_kernel(h_ref, w_ref, z_ref, zf_ref, w_scr, *, n_main):
    @pl.when(pl.program_id(1) == 0)
    def _():
        w_scr[...] = w_ref[0].astype(BF16)

    r = _bdot(h_ref[...], w_scr[...])

    @pl.when(pl.program_id(0) < n_main)
    def _():
        z_ref[...] = r.astype(z_ref.dtype)

    @pl.when(pl.program_id(0) >= n_main)
    def _():
        zf_ref[...] = r


def _in_proj(h, w_in, layer, *, tm=1024, tn=1024):
    rows, d = h.shape
    n_cols = w_in.shape[2]
    nb = n_cols // tn
    f0 = HG_FF // tn
    nf = 2 * HG_W // tn
    n_main = nb - nf
    nm = rows // tm
    col = lambda j: jnp.where(j < f0, j, jnp.where(j < n_main, j + nf, j - n_main + f0))
    z_idx = lambda j, m: (jnp.where(j < n_main, m, nm - 1), col(jnp.minimum(j, n_main - 1)))
    zf_idx = lambda j, m: (jnp.where(j < n_main, 0, m), jnp.maximum(j - n_main, 0))
    return pl.pallas_call(
        functools.partial(_inproj_kernel, n_main=n_main),
        grid=(nb, nm),
        in_specs=[
            pl.BlockSpec((tm, d), lambda j, m: (m, 0)),
            pl.BlockSpec((1, d, tn), lambda j, m: (layer, 0, col(j))),
        ],
        out_specs=[pl.BlockSpec((tm, tn), z_idx), pl.BlockSpec((tm, tn), zf_idx)],
        out_shape=[jax.ShapeDtypeStruct((rows, n_cols), BF16),
                   jax.ShapeDtypeStruct((rows, nf * tn), F32)],
        scratch_shapes=[pltpu.VMEM((d, tn), BF16)],
        compiler_params=_cparams(2),
        name="in_proj",
    )(h, w_in)


def _merge_kernel(yhy_ref, yfn_ref, of_ref, ob_ref, g_ref, ga0_ref, ga1_ref, gb0_ref, gb1_ref,
                  gc0_ref, gc1_ref, x_ref, hnw_ref, mod_ref,
                  wbhy_ref, wbfn_ref, wbhg_ref, wo_ref, o_ref):
    d = x_ref.shape[1]
    ohg = of_ref[...] + ob_ref[...]
    nw = hnw_ref[...]
    heads = []
    for h in range(ohg.shape[1] // HG_DK):
        oh = ohg[:, h * HG_DK:(h + 1) * HG_DK]
        ms = jnp.mean(oh * oh, axis=-1, keepdims=True)
        heads.append(oh * lax.rsqrt(ms + NORM_EPS) * nw)
    y_hg = jnp.concatenate(heads, axis=-1) * _silu(g_ref[...].astype(F32))
    gate = lambda lo, hi: jax.nn.sigmoid(
        jnp.concatenate([lo[...], hi[...]], axis=-1).astype(F32))
    merged = gate(ga0_ref, ga1_ref) * _bdot(yhy_ref[...], wbhy_ref[...])
    merged += gate(gb0_ref, gb1_ref) * _bdot(yfn_ref[...], wbfn_ref[...])
    merged += gate(gc0_ref, gc1_ref) * _bdot(y_hg.astype(BF16), wbhg_ref[...])
    out = _bdot(merged.astype(BF16), wo_ref[...])
    o_ref[...] = x_ref[...] + mod_ref[0, 2:3, :] * out


def _merge_out(y_hy, y_fn, o_f, o_b, zm, x, hg_nw, mods, wb_hy, wb_fn, wb_hg, w_o,
               *, rows, seq, batch, tm=256):
    d = x.shape[1]
    w = y_hy.shape[1]
    tm = min(tm, rows)
    row = lambda m: (m, 0)
    const = lambda m: (0, 0)
    gate0 = ZM_GATE // (d // 2)
    return pl.pallas_call(
        _merge_kernel,
        grid=(rows // tm,),
        in_specs=[
            pl.BlockSpec((tm, w), row),
            pl.BlockSpec((tm, w), row),
            pl.BlockSpec((tm, w), row),
            pl.BlockSpec((tm, w), row),
            pl.BlockSpec((tm, w), lambda m: (m, ZM_G // w)),
            *[pl.BlockSpec((tm, d // 2), functools.partial(lambda i, m: (m, i), gate0 + i))
              for i in range(6)],
            pl.BlockSpec((tm, d), row),
            pl.BlockSpec((1, HG_DK), const),
            pl.BlockSpec((1, 6, d), lambda m: (_group_of(m, tm, seq, batch), 0, 0)),
            _resident((w, d), const),
            _resident((w, d), const),
            _resident((w, d), const),
            _resident((d, d), const),
        ],
        out_specs=pl.BlockSpec((tm, d), row),
        out_shape=jax.ShapeDtypeStruct((rows, d), F32),
        compiler_params=_cparams(1),
        name="merge_out",
    )(y_hy, y_fn, o_f, o_b, zm, *([zm] * 6), x, hg_nw.reshape(1, HG_DK), mods,
      wb_hy, wb_fn, wb_hg, w_o)


def _ffn_kernel(x_ref, nw_ref, mod_ref, w1_ref, w3_ref, w2_ref, o_ref, h_scr):
    f = pl.program_id(1)

    @pl.when(f == 0)
    def _():
        h = _norm_mod(x_ref[...], nw_ref[...], mod_ref[0, 3:4, :], mod_ref[0, 4:5, :])
        h_scr[...] = h.astype(BF16)
        o_ref[...] = jnp.zeros_like(o_ref)

    h = h_scr[...]
    act = _silu(_bdot(h, w1_ref[0].astype(BF16))) * _bdot(h, w3_ref[0].astype(BF16))
    o_ref[...] += _bdot(act.astype(BF16), w2_ref[0].astype(BF16))

    @pl.when(f == pl.num_programs(1) - 1)
    def _():
        o_ref[...] = x_ref[...] + mod_ref[0, 5:6, :] * o_ref[...]


def _ffn_dense(x, nw, mods, w1, w3, w2, layer, *, rows, seq, batch, tm=1024, tf=256):
    d = x.shape[1]
    ff = w1.shape[2]
    tm = min(tm, rows)
    return pl.pallas_call(
        _ffn_kernel,
        grid=(rows // tm, ff // tf),
        in_specs=[
            _resident((tm, d), lambda m, f: (m, 0)),
            pl.BlockSpec((1, d), lambda m, f: (0, 0)),
            pl.BlockSpec((1, 6, d), lambda m, f: (_group_of(m, tm, seq, batch), 0, 0)),
            pl.BlockSpec((1, d, tf), lambda m, f: (layer, 0, f)),
            pl.BlockSpec((1, d, tf), lambda m, f: (layer, 0, f)),
            pl.BlockSpec((1, tf, d), lambda m, f: (layer, f, 0)),
        ],
        out_specs=pl.BlockSpec((tm, d), lambda m, f: (m, 0)),
        out_shape=jax.ShapeDtypeStruct((rows, d), F32),
        scratch_shapes=[pltpu.VMEM((tm, d), BF16)],
        compiler_params=_cparams(2),
        name="ffn_dense",
    )(x, nw.reshape(1, d), mods, w1, w3, w2)


def _top2(logits, n_experts):
    lane = lax.broadcasted_iota(jnp.int32, logits.shape, 1).astype(F32)
    neg = jnp.float32(-jnp.inf)
    lg = jnp.where(lane < n_experts, logits, neg)
    m1 = jnp.max(lg, axis=-1, keepdims=True)
    i1 = jnp.min(jnp.where(lg == m1, lane, float(LANE)), axis=-1, keepdims=True)
    lg2 = jnp.where(lane == i1, neg, lg)
    m2 = jnp.max(lg2, axis=-1, keepdims=True)
    i2 = jnp.min(jnp.where(lg2 == m2, lane, float(LANE)), axis=-1, keepdims=True)
    e2 = jnp.exp(m2 - m1)
    return i1, i2, 1.0 / (1.0 + e2), e2 / (1.0 + e2)


def _route_kernel(x_ref, nw_ref, mod_ref, r_ref, h_ref, route_ref):
    h = _norm_mod(x_ref[...], nw_ref[...], mod_ref[0, 3:4, :], mod_ref[0, 4:5, :])
    logits = jnp.dot(h, r_ref[...], preferred_element_type=F32, precision=lax.Precision.HIGHEST)
    i1, i2, p1, p2 = _top2(logits, N_EXPERTS)
    lane = lax.broadcasted_iota(jnp.int32, logits.shape, 1)
    route = jnp.where(lane == 0, i1, jnp.where(lane == 1, i2,
                      jnp.where(lane == 2, p1, jnp.where(lane == 3, p2, 0.0))))
    h_ref[...] = h
    route_ref[...] = route


def _moe_route(x, nw, mods, router, *, rows, seq, batch, tm=512):
    d = x.shape[1]
    rpad = jnp.zeros((d, LANE), F32).at[:, :router.shape[1]].set(router)
    return pl.pallas_call(
        _route_kernel,
        grid=(rows // tm,),
        in_specs=[
            pl.BlockSpec((tm, d), lambda m: (m, 0)),
            pl.BlockSpec((1, d), lambda m: (0, 0)),
            pl.BlockSpec((1, 6, d), lambda m: (_group_of(m, tm, seq, batch), 0, 0)),
            pl.BlockSpec((d, LANE), lambda m: (0, 0)),
        ],
        out_specs=[pl.BlockSpec((tm, d), lambda m: (m, 0)),
                   pl.BlockSpec((tm, LANE), lambda m: (m, 0))],
        out_shape=[jax.ShapeDtypeStruct((rows, d), F32), jax.ShapeDtypeStruct((rows, LANE), F32)],
        compiler_params=_cparams(1),
        name="moe_route",
    )(x, nw.reshape(1, d), mods, rpad)


def _moe_plan(route, n_experts, tile):
    rows = route.shape[0]
    e = route[:, :2].astype(jnp.int32).reshape(-1)
    onehot = (e[:, None] == jnp.arange(n_experts, dtype=jnp.int32)[None, :]).astype(jnp.int32)
    before = jnp.cumsum(onehot, axis=0) - onehot
    rank = jnp.sum(before * onehot, axis=1)
    counts = jnp.sum(onehot, axis=0)
    padded = ((counts + tile - 1) // tile) * tile
    ends = jnp.cumsum(padded)
    starts = ends - padded
    pos = jnp.sum(starts[None, :] * onehot, axis=1) + rank
    n_tiles = -(-2 * rows // tile) + n_experts
    src = jnp.zeros((n_tiles * tile,), jnp.int32).at[pos].set(
        jnp.arange(2 * rows, dtype=jnp.int32) // 2)
    tile_start = jnp.arange(n_tiles, dtype=jnp.int32) * tile
    tile_expert = jnp.sum((tile_start[:, None] >= ends[None, :]).astype(jnp.int32), axis=1)
    tile_expert = jnp.minimum(tile_expert, n_experts - 1)
    n_used = (ends[-1] // tile).reshape(1)
    return pos, src, tile_expert, n_used


def _row_copy(src_hbm, src_row, dst, dst_row, sem):
    return pltpu.make_async_copy(src_hbm.at[pl.ds(src_row, 1)], dst.at[pl.ds(dst_row, 1)], sem)


def _gather_rows(idx_ref, base, stride, src_hbm, dst, sem, *, wait):
    if wait:
        pltpu.make_async_copy(src_hbm.at[pl.ds(0, dst.shape[0])], dst, sem).wait()
        return

    def body(r, carry):
        _row_copy(src_hbm, idx_ref[base + stride * r], dst, r, sem).start()
        return carry

    lax.fori_loop(0, dst.shape[0], body, 0, unroll=8)


def _moe_ffn_kernel(te_ref, nu_ref, src_ref, h_hbm, w1_ref, w3_ref, w2_ref, o_ref,
                    rows_scr, h_scr, sems):
    t = pl.program_id(0)
    f = pl.program_id(1)
    last = pl.num_programs(1) - 1
    tile = o_ref.shape[0]
    n_used = nu_ref[0]
    used = t < n_used
    slot = t % 2

    def gather(tile_idx, buf, wait):
        _gather_rows(src_ref, tile_idx * tile, 1, h_hbm, rows_scr.at[buf], sems.at[buf], wait=wait)

    @pl.when(used & (f == 0) & (t == 0))
    def _():
        gather(t, slot, False)

    @pl.when(used & (f == 0))
    def _():
        gather(t, slot, True)
        h_scr[...] = rows_scr[slot].astype(BF16)
        o_ref[...] = jnp.zeros_like(o_ref)

    @pl.when((f == 0) & (t + 1 < n_used))
    def _():
        gather(t + 1, 1 - slot, False)

    @pl.when(used)
    def _():
        h = h_scr[...]
        act = _silu(_bdot(h, w1_ref[0, 0].astype(BF16))) * _bdot(h, w3_ref[0, 0].astype(BF16))
        o_ref[...] += _bdot(act.astype(BF16), w2_ref[0, 0].astype(BF16))

    @pl.when(jnp.logical_not(used) & (f == last))
    def _():
        o_ref[...] = jnp.zeros_like(o_ref)


def _moe_ffn(h, src, tile_expert, n_used, w1, w3, w2, layer, *, tile, tf=256):
    d = h.shape[1]
    n = src.shape[0]
    ff = w1.shape[3]
    nf = ff // tf
    fsel = lambda t, f, nu: jnp.where(t < nu[0], f, nf - 1)
    up = lambda t, f, te, nu, sr: (layer, te[t], 0, fsel(t, f, nu))
    down = lambda t, f, te, nu, sr: (layer, te[t], fsel(t, f, nu), 0)
    return pl.pallas_call(
        _moe_ffn_kernel,
        grid_spec=pltpu.PrefetchScalarGridSpec(
            num_scalar_prefetch=3,
            grid=(n // tile, nf),
            in_specs=[
                pl.BlockSpec(memory_space=pl.ANY),
                pl.BlockSpec((1, 1, d, tf), up),
                pl.BlockSpec((1, 1, d, tf), up),
                pl.BlockSpec((1, 1, tf, d), down),
            ],
            out_specs=pl.BlockSpec((tile, d), lambda t, f, te, nu, sr: (t, 0)),
            scratch_shapes=[pltpu.VMEM((2, tile, d), F32), pltpu.VMEM((tile, d), BF16),
                            pltpu.SemaphoreType.DMA((2,))],
        ),
        out_shape=jax.ShapeDtypeStruct((n, d), F32),
        compiler_params=_cparams(2),
        name="moe_ffn",
    )(tile_expert, n_used, src, h, w1, w3, w2)


def _combine_kernel(pos_ref, ys_hbm, x_ref, route_ref, mod_ref, o_ref, y1_scr, y2_scr, sem):
    base = 2 * pl.program_id(0) * x_ref.shape[0]
    _gather_rows(pos_ref, base, 2, ys_hbm, y1_scr, sem, wait=False)
    _gather_rows(pos_ref, base + 1, 2, ys_hbm, y2_scr, sem, wait=False)
    _gather_rows(pos_ref, base, 2, ys_hbm, y1_scr, sem, wait=True)
    _gather_rows(pos_ref, base + 1, 2, ys_hbm, y2_scr, sem, wait=True)
    mix = route_ref[:, 2:3] * y1_scr[...] + route_ref[:, 3:4] * y2_scr[...]
    o_ref[...] = x_ref[...] + mod_ref[0, 5:6, :] * mix


def _moe_combine(x, mods, route, ys, pos, *, rows, seq, batch, tm=256):
    d = x.shape[1]
    return pl.pallas_call(
        _combine_kernel,
        grid_spec=pltpu.PrefetchScalarGridSpec(
            num_scalar_prefetch=1,
            grid=(rows // tm,),
            in_specs=[
                pl.BlockSpec(memory_space=pl.ANY),
                pl.BlockSpec((tm, d), lambda m, pos: (m, 0)),
                pl.BlockSpec((tm, LANE), lambda m, pos: (m, 0)),
                pl.BlockSpec((1, 6, d), lambda m, pos: (_group_of(m, tm, seq, batch), 0, 0)),
            ],
            out_specs=pl.BlockSpec((tm, d), lambda m, pos: (m, 0)),
            scratch_shapes=[pltpu.VMEM((tm, d), F32), pltpu.VMEM((tm, d), F32),
                            pltpu.SemaphoreType.DMA(())],
        ),
        out_shape=jax.ShapeDtypeStruct((rows, d), F32),
        compiler_params=_cparams(1),
        name="moe_combine",
    )(pos, ys, x, route, mods)


def _moe_top2(x, nw, mods, router, w1, w3, w2, layer, *, rows, seq, batch, tm, tile=768):
    tok = dict(rows=rows, seq=seq, batch=batch)
    h, route = _moe_route(x, nw, mods, router, tm=tm, **tok)
    pos, src, tile_expert, n_used = _moe_plan(route, w1.shape[1], tile)
    ys = _moe_ffn(h, src, tile_expert, n_used, w1, w3, w2, layer, tile=tile)
    return _moe_combine(x, mods, route, ys, pos, tm=min(tm, 256), **tok)


def _final_norm_kernel(x_ref, w_ref, o_ref):
    x = x_ref[...]
    ms = jnp.mean(x * x, axis=-1, keepdims=True)
    o_ref[...] = x * lax.rsqrt(ms + NORM_EPS) * w_ref[...]


def _final_norm(x, w, *, rows, tm=512):
    d = x.shape[1]
    return pl.pallas_call(
        _final_norm_kernel,
        grid=(rows // tm,),
        in_specs=[pl.BlockSpec((tm, d), lambda m: (m, 0)), pl.BlockSpec((1, d), lambda m: (0, 0))],
        out_specs=pl.BlockSpec((tm, d), lambda m: (m, 0)),
        out_shape=jax.ShapeDtypeStruct((rows, d), F32),
        compiler_params=_cparams(1),
        name="final_norm",
    )(x, w.reshape(1, d))


def _dft_angles(n, period):
    k = lax.broadcasted_iota(jnp.int32, (n, n), 0)
    s = lax.broadcasted_iota(jnp.int32, (n, n), 1)
    return ((k * s) % period).astype(F32) * (2.0 * math.pi / period)


def _hyena_dft(seq_len):
    n = 2 * seq_len
    half = seq_len // 2
    row = lax.broadcasted_iota(jnp.int32, (seq_len, seq_len), 0)
    col = lax.broadcasted_iota(jnp.int32, (seq_len, seq_len), 1)
    freq = lambda i: jnp.where(i < half, 2 * i, 2 * (i - half) + 1)
    to_angle = lambda prod: (prod % n).astype(F32) * (2.0 * math.pi / n)
    ang = to_angle(freq(row) * col)
    fc = jnp.cos(ang)
    sin = jnp.sin(ang)
    fs = jnp.where(row == 0, jnp.where(col % 2 == 0, 1.0, -1.0), sin)
    ic = jnp.where(col == 0, 1.0 / n, 2.0 / n * fc.T)
    isn = jnp.where(col == 0, jnp.where(row % 2 == 0, 1.0, -1.0) / n, 2.0 / n * sin.T)
    return (fc.astype(BF16), fs.astype(BF16), ic.astype(BF16), isn.astype(BF16),
            _flip_matrix(half))


def _hyena_feats(seq_len):
    pos = np.arange(seq_len, dtype=np.float64)
    t = pos / max(seq_len - 1, 1)
    bands = np.linspace(1e-4, HY_BANDS - 1, HY_BANDS).astype(np.float32).astype(np.float64)
    ang = 2 * math.pi * pos[:, None] * bands[None, :] / seq_len
    feats = np.concatenate([t[:, None], np.cos(ang), -np.sin(ang)], axis=-1)
    out = np.zeros((seq_len, LANE), np.float32)
    out[:, :HY_POS_DIM] = feats
    return jnp.asarray(out)


def _split2(a):
    hi = a.astype(BF16)
    return hi, (a - hi.astype(F32)).astype(BF16)


def _dot3(a, b):
    a_hi, a_lo = _split2(a)
    b_hi, b_lo = _split2(b)
    return _bdot(a_hi, b_hi) + _bdot(a_hi, b_lo) + _bdot(a_lo, b_hi)


def _filter_kernel(feat_ref, w1_ref, b1_ref, w2_ref, b2_ref, fr_ref, w3f_ref, w3b_ref,
                   dcf_ref, dcb_ref, fc_ref, fs_ref, o_ref, h_scr):
    @pl.when((pl.program_id(1) == 0) & (pl.program_id(2) == 0))
    def _():
        hp = lax.Precision.HIGHEST
        h1 = jnp.sin(fr_ref[0, 0:1, :] * (jnp.dot(feat_ref[...], w1_ref[0], precision=hp,
                                                  preferred_element_type=F32) + b1_ref[0]))
        h_scr[...] = jnp.sin(fr_ref[0, 1:2, :] * (jnp.dot(h1, w2_ref[0], precision=hp,
                                                          preferred_element_type=F32) + b2_ref[0]))

    h = h_scr[...]
    t = feat_ref[:, 0:1]
    hf = _dot3(h, w3f_ref[0]) * jnp.exp(-t * dcf_ref[0])
    hb = _dot3(h, w3b_ref[0]) * jnp.exp(-t * dcb_ref[0])
    row = lax.broadcasted_iota(jnp.int32, hf.shape, 0)
    hb = jnp.where(row == 0, 0.0, hb)
    ss = jnp.sum(hf * hf + hb * hb, axis=0, keepdims=True)
    scale = lax.rsqrt(ss + NORM_EPS)
    even = hf * scale + hb * scale
    odd = hf * scale - hb * scale
    kc = _bdot(fc_ref[...], even.astype(BF16))
    ks = _bdot(fs_ref[...], odd.astype(BF16))
    sign = jnp.where(row % 2 == 0, 1.0, -1.0)
    nyq = jnp.sum(even * sign, axis=0, keepdims=True)
    o_ref[0, 0] = kc
    o_ref[0, 1] = jnp.where(row == 0, nyq, ks)


def _hyena_filters(seq_len, w1p, b1, w2, b2, w3, freq, decay, fc, fs, *, tn=256):
    depth = w1p.shape[0]
    hid = HY_FILTER_HIDDEN
    nb = HY_W // tn
    feats = _hyena_feats(seq_len)
    col = lambda d: (lambda l, o, c: (l, 0, (2 * o + d) * nb + c))
    lyr = lambda l, o, c: (l, 0, 0)
    return pl.pallas_call(
        _filter_kernel,
        grid=(depth, 2, nb),
        in_specs=[
            pl.BlockSpec((seq_len, LANE), lambda l, o, c: (0, 0)),
            pl.BlockSpec((1, LANE, hid), lyr),
            pl.BlockSpec((1, 1, hid), lyr),
            pl.BlockSpec((1, hid, hid), lyr),
            pl.BlockSpec((1, 1, hid), lyr),
            pl.BlockSpec((1, 2, hid), lyr),
            pl.BlockSpec((1, hid, tn), col(0)),
            pl.BlockSpec((1, hid, tn), col(1)),
            pl.BlockSpec((1, 1, tn), col(0)),
            pl.BlockSpec((1, 1, tn), col(1)),
            _resident((seq_len, seq_len), lambda l, o, c: (0, 0)),
            _resident((seq_len, seq_len), lambda l, o, c: (0, 0)),
        ],
        out_specs=pl.BlockSpec((1, 2, seq_len, tn), lambda l, o, c: (l, 0, 0, o * nb + c)),
        out_shape=jax.ShapeDtypeStruct((depth, 2, seq_len, 2 * HY_W), F32),
        scratch_shapes=[pltpu.VMEM((seq_len, hid), F32)],
        compiler_params=_cparams(3),
        name="hyena_filters",
    )(feats, w1p, b1.reshape(depth, 1, hid), w2, b2.reshape(depth, 1, hid), freq,
      w3, w3, decay.reshape(depth, 1, 4 * HY_W), decay.reshape(depth, 1, 4 * HY_W), fc, fs)


def _conv3(z, w_ref, b_ref):
    n = z.shape[0]
    row = lax.broadcasted_iota(jnp.int32, z.shape, 0)
    prev = jnp.where(row == 0, 0.0, pltpu.roll(z, 1, 0))
    nxt = jnp.where(row == n - 1, 0.0, pltpu.roll(z, n - 1, 0))
    return prev * w_ref[0:1, :] + z * w_ref[1:2, :] + nxt * w_ref[2:3, :] + b_ref[...]


def _longconv_kernel(a_ref, g_ref, aw_ref, ab_ref, gw_ref, gb_ref, kf_ref, skip_ref,
                     fc_ref, fs_ref, ic_ref, is_ref, flip_ref, o_ref, *, conv_a):
    a = a_ref[...].astype(F32)
    if conv_a:
        a = _conv3(a, aw_ref, ab_ref)
    n = a.shape[0]
    h = n // 2
    lo = slice(0, h)
    hi = slice(h, n)
    k = lax.broadcasted_iota(jnp.int32, (h, a.shape[1]), 0)
    alt = jnp.where(k % 2 == 0, 1.0, -1.0)
    first = k == 0
    a_lo = a[lo]
    a_hi = a[hi]
    a_mid = a_hi[0:1, :]
    rev = _bdot(flip_ref[...], a_hi.astype(BF16))
    even_in = (a_lo + rev).astype(BF16)
    odd_in = (a_lo - rev).astype(BF16)
    nyq = jnp.sum(alt * (a_lo + a_hi), axis=0, keepdims=True)
    ac_e = _bdot(fc_ref[lo, lo], even_in) + alt * a_mid
    as_e = jnp.where(first, nyq, _bdot(fs_ref[lo, lo], odd_in))
    ac_o = _bdot(fc_ref[hi, lo], odd_in)
    as_o = _bdot(fs_ref[hi, lo], even_in) + alt * a_mid
    kc_e = kf_ref[0, 0, lo, :]
    ks_e = kf_ref[0, 1, lo, :]
    kc_o = kf_ref[0, 0, hi, :]
    ks_o = kf_ref[0, 1, hi, :]
    cross = as_e * ks_e
    yc_e = jnp.where(first, ac_e * kc_e, ac_e * kc_e - cross)
    ys_e = jnp.where(first, 0.0, ac_e * ks_e + as_e * kc_e)
    y_nyq = cross[0:1, :]
    yc_o = ac_o * kc_o - as_o * ks_o
    ys_o = ac_o * ks_o + as_o * kc_o
    sym = (_bdot(ic_ref[lo, lo], yc_e.astype(BF16)) + _bdot(is_ref[lo, hi], ys_o.astype(BF16))
           + alt * (y_nyq * (0.5 / n)))
    anti = _bdot(ic_ref[lo, hi], yc_o.astype(BF16)) + _bdot(is_ref[lo, lo], ys_e.astype(BF16))
    y_mid = (jnp.sum(alt * (yc_e + ys_o), axis=0, keepdims=True) * (1.0 / n)
             - yc_e[0:1, :] * (0.5 / n) + y_nyq * (0.5 / n))
    y_hi = jnp.where(first, y_mid, _bdot(flip_ref[...], (sym - anti).astype(BF16)))
    conv = jnp.concatenate([sym + anti, y_hi], axis=0)
    gate = _conv3(g_ref[...].astype(F32), gw_ref, gb_ref)
    o_ref[...] = (gate * (conv + a * skip_ref[0])).astype(o_ref.dtype)


def _hyena_longconv(a_src, a_part, zm, gate_part, conv_w, conv_b, kf, layer, order, skip,
                    dft, *, seq_len, n_seq, row0, out_rows, prev_out=None, tn=256):
    fc, fs, ic, isn, flip = dft
    assert seq_len % 4 == 0
    nb = HY_W // tn
    rb0 = row0 // seq_len
    const = lambda c, b: (0, 0)
    conv_a = a_part is not None
    aw_blk = a_part * nb if conv_a else 0
    a_blk = ZM_HY // tn + aw_blk if conv_a else 0
    gw_blk = gate_part * nb
    g_blk = ZM_HY // tn + gw_blk
    kernel = functools.partial(_longconv_kernel, conv_a=conv_a)
    in_specs = [
        pl.BlockSpec((seq_len, tn), lambda c, b: (rb0 + b, a_blk + c)),
        pl.BlockSpec((seq_len, tn), lambda c, b: (rb0 + b, g_blk + c)),
        pl.BlockSpec((3, tn), lambda c, b: (0, aw_blk + c)),
        pl.BlockSpec((1, tn), lambda c, b: (0, aw_blk + c)),
        pl.BlockSpec((3, tn), lambda c, b: (0, gw_blk + c)),
        pl.BlockSpec((1, tn), lambda c, b: (0, gw_blk + c)),
        _resident((1, 2, seq_len, tn), lambda c, b: (layer, 0, 0, order * nb + c)),
        pl.BlockSpec((1, 1, tn), lambda c, b: (order, 0, c)),
        _resident((seq_len, seq_len), const),
        _resident((seq_len, seq_len), const),
        _resident((seq_len, seq_len), const),
        _resident((seq_len, seq_len), const),
        _resident((seq_len // 2, seq_len // 2), const),
    ]
    args = [a_src, zm, conv_w, conv_b.reshape(1, -1), conv_w, conv_b.reshape(1, -1), kf,
            skip.reshape(2, 1, HY_W), fc, fs, ic, isn, flip]
    aliases = {}
    if prev_out is not None:
        in_specs.append(pl.BlockSpec(memory_space=pl.ANY))
        args.append(prev_out)
        aliases = {len(args) - 1: 0}
        kernel = functools.partial(_drop_last_input, kernel, 13)
    return pl.pallas_call(
        kernel,
        grid=(nb, n_seq),
        in_specs=in_specs,
        out_specs=pl.BlockSpec((seq_len, tn), lambda c, b: (rb0 + b, c)),
        out_shape=jax.ShapeDtypeStruct((out_rows, HY_W), BF16),
        input_output_aliases=aliases,
        compiler_params=_cparams(2),
        name="hyena_longconv",
    )(*args)


def _drop_last_input(kernel, n_in, *refs):
    return kernel(*refs[:n_in], *refs[n_in + 1:])


def _flip_matrix(half):
    r = lax.broadcasted_iota(jnp.int32, (half, half), 0)
    c = lax.broadcasted_iota(jnp.int32, (half, half), 1)
    return jnp.where((r >= 1) & (c == half - r), 1.0, 0.0).astype(BF16)


def _fnet_scale(seq_len):
    return 1.0 / math.sqrt(seq_len * FN_GROUP_W)


def _fnet_dft(seq_len):
    half = seq_len // 2
    ang_l = _dft_angles(seq_len, seq_len)[:half]
    ang_w = _dft_angles(FN_GROUP_W, FN_GROUP_W)
    scale = _fnet_scale(seq_len)
    return ((jnp.cos(ang_l) * scale).astype(BF16), (jnp.sin(ang_l) * scale).astype(BF16),
            jnp.cos(ang_w).astype(BF16), jnp.sin(ang_w).astype(BF16), _flip_matrix(half))


def _fnet_kernel(z_ref, cl_ref, sl_ref, cw_ref, sw_ref, flip_ref, o_ref, *, scale):
    z = z_ref[...]
    gc, gs = [], []
    for j in range(z.shape[1] // FN_GROUP_W):
        zj = z[:, j * FN_GROUP_W:(j + 1) * FN_GROUP_W]
        gc.append(_bdot(zj, cw_ref[...]).astype(BF16))
        gs.append(_bdot(zj, sw_ref[...]).astype(BF16))
    gc = jnp.concatenate(gc, axis=-1)
    gs = jnp.concatenate(gs, axis=-1)
    p = _bdot(cl_ref[...], gc)
    q = _bdot(sl_ref[...], gs)
    pos = lax.broadcasted_iota(jnp.int32, gc.shape, 0)
    y_mid = scale * jnp.sum(jnp.where(pos % 2 == 0, 1.0, -1.0) * gc.astype(F32), axis=0,
                            keepdims=True)
    first = lax.broadcasted_iota(jnp.int32, p.shape, 0) == 0
    y_hi = jnp.where(first, y_mid, _bdot(flip_ref[...], (p + q).astype(BF16)))
    o_ref[...] = jnp.concatenate([p - q, y_hi], axis=0).astype(o_ref.dtype)


def _fnet(zm, dft, *, seq_len, n_seq, row0, out_rows, prev_out=None, tn=512):
    cl, sl, cw, sw, flip = dft
    nb = FN_W // tn
    rb0 = row0 // seq_len
    c0 = ZM_FN // tn
    half = seq_len // 2
    const = lambda c, b: (0, 0)
    in_specs = [
        pl.BlockSpec((seq_len, tn), lambda c, b: (rb0 + b, c0 + c)),
        _resident((half, seq_len), const),
        _resident((half, seq_len), const),
        pl.BlockSpec((FN_GROUP_W, FN_GROUP_W), const),
        pl.BlockSpec((FN_GROUP_W, FN_GROUP_W), const),
        _resident((half, half), const),
    ]
    args = [zm, cl, sl, cw, sw, flip]
    kernel = functools.partial(_fnet_kernel, scale=_fnet_scale(seq_len))
    aliases = {}
    if prev_out is not None:
        in_specs.append(pl.BlockSpec(memory_space=pl.ANY))
        args.append(prev_out)
        aliases = {len(args) - 1: 0}
        kernel = functools.partial(_drop_last_input, kernel, 6)
    return pl.pallas_call(
        kernel,
        grid=(nb, n_seq),
        in_specs=in_specs,
        out_specs=pl.BlockSpec((seq_len, tn), lambda c, b: (rb0 + b, c)),
        out_shape=jax.ShapeDtypeStruct((out_rows, FN_W), BF16),
        input_output_aliases=aliases,
        compiler_params=_cparams(2),
        name="fnet",
    )(*args)


def _hgrn_tables(reverse):
    c = HG_CHUNK
    tri = np.zeros((c, c), np.float32)
    for t in range(c):
        if reverse:
            tri[t, t:] = 1.0
        else:
            tri[t, :t + 1] = 1.0
    mats = [tri]
    masks = []
    for lvl in range(HG_LEVELS):
        half = (c // 2) >> lvl
        sel = np.zeros((c, c), np.float32)
        mask = np.zeros((c, c), np.float32)
        for t in range(c):
            start = (t // (2 * half)) * 2 * half
            mid = start + half
            sel[t, mid if reverse else mid - 1] = 1.0
            for s in range(start, start + 2 * half):
                if reverse and t < mid <= s:
                    mask[t, s] = 1.0
                if (not reverse) and s < mid <= t:
                    mask[t, s] = 1.0
        mats.append(tri - sel @ tri)
        masks.append(mask)
    half = c // 2
    sel = np.zeros((c, c), np.float32)
    mask = np.zeros((c, c), np.float32)
    for t in range(c):
        start = (t // half) * half
        sel[t, start + half // 2] = 1.0
        for s in range(start, start + half):
            if (s >= t) if reverse else (s <= t):
                mask[t, s] = 1.0
    mats.append(tri - sel @ tri)
    masks.append(mask)
    mats = [mats[0], mats[1], mats[-1]] + mats[2:-1]
    return np.concatenate(mats, axis=0), np.stack(masks, axis=0)


HG_BOUNDED_BLOCKS = 3
HG_MID_BLOCK = 2


def _hgrn_level_block(lvl):
    return 1 if lvl == 0 else HG_BOUNDED_BLOCKS + lvl - 1


def _hgrn_decays(zf_ref, row0, lb_ref, tab_ref, k_scr, p_scr, dec_scr):
    lb = lb_ref[...]
    f = lb + (1.0 - lb) * jax.nn.sigmoid(zf_ref[row0:row0 + HG_CHUNK, :])
    lf = jnp.log2(f)
    k_scr[...] = (1.0 - f).astype(BF16)
    p1 = lf.astype(BF16)
    r1 = lf - p1.astype(F32)
    p2 = r1.astype(BF16)
    p3 = (r1 - p2.astype(F32)).astype(BF16)
    p_scr[...] = jnp.concatenate([p1, p2, p3], axis=0)
    n = HG_BOUNDED_BLOCKS * HG_CHUNK
    dec_scr[0:n, :] = _bdot(tab_ref[0:n, :], p_scr[...])


def _hgrn_more_decays(tab_ref, p_scr, dec_scr):
    n = HG_BOUNDED_BLOCKS * HG_CHUNK
    dec_scr[n:, :] = _bdot(tab_ref[n:, :], p_scr[...])


def _hgrn_direction(q_ref, v_ref, row0, k_scr, dec_scr, mask_ref, st_scr, o_ref, end_row,
                    bounded):
    c = HG_CHUNK
    heads = lambda ref, r0: jnp.stack(
        [ref[r0:r0 + c, h * HG_DK:(h + 1) * HG_DK] for h in range(HG_HEADS)], axis=0)
    bmm = lambda eq, a, b: jnp.einsum(eq, a, b, preferred_element_type=F32)
    q = heads(q_ref, row0)
    k = heads(k_scr, 0)
    v = heads(v_ref, row0)
    b = heads(dec_scr, 0)
    st = st_scr[...]
    o = bmm('htk,hkv->htv', q * jnp.exp2(b).astype(BF16), st.astype(BF16))
    level = lambda l: jnp.exp2(-jnp.abs(heads(dec_scr, _hgrn_level_block(l) * c))).astype(BF16)
    if bounded:
        e = level(0)
        att = jnp.where(mask_ref[0] > 0.5, bmm('htk,hsk->hts', q * e, k * e), 0.0)
        d = heads(dec_scr, HG_MID_BLOCK * c)
        inner = bmm('htk,hsk->hts', q * jnp.exp2(d).astype(BF16), k * jnp.exp2(-d).astype(BF16))
        att += jnp.where(mask_ref[HG_LEVELS] > 0.5, inner, 0.0)
    else:
        diag = jnp.sum(q.astype(F32) * k.astype(F32), axis=-1, keepdims=True)
        o += diag * v.astype(F32)
        att = jnp.zeros((HG_HEADS, c, c), F32)
        for lvl in range(HG_LEVELS):
            e = level(lvl)
            att += bmm('htk,hsk->hts', q * e, k * e) * mask_ref[lvl]
    o += bmm('hts,hsv->htv', att.astype(BF16), v)
    for h in range(HG_HEADS):
        o_ref[row0:row0 + c, h * HG_DK:(h + 1) * HG_DK] = o[h]
    b_end = b[:, end_row:end_row + 1, :]
    upd = bmm('htk,htv->hkv', k * jnp.exp2(b_end - b).astype(BF16), v)
    decay = jnp.swapaxes(jnp.broadcast_to(jnp.exp2(b_end), (HG_HEADS, HG_DK, HG_DK)), 1, 2)
    st_scr[...] = decay * st + upd


HG_MAX_LOG2_SPAN = 100.0
HG_MAX_ABS_Q = 1e6


def _hgrn_kernel(qf_ref, vf_ref, zff_ref, qb_ref, vb_ref, zfb_ref, lbf_ref, lbb_ref,
                 tabf_ref, maskf_ref, tabb_ref, maskb_ref, s0f_ref, s0b_ref,
                 of_ref, ob_ref, sf_ref, sb_ref, stf_scr, stb_scr, kf_scr, kb_scr,
                 pf_scr, pb_scr, decf_scr, decb_scr):
    ci = pl.program_id(1)
    c = HG_CHUNK

    @pl.when(ci == 0)
    def _():
        stf_scr[...] = s0f_ref[0]
        stb_scr[...] = s0b_ref[0]

    q_abs = jnp.maximum(jnp.max(jnp.abs(qf_ref[...].astype(F32))),
                        jnp.max(jnp.abs(qb_ref[...].astype(F32))))
    mid = slice(HG_MID_BLOCK * c, (HG_MID_BLOCK + 1) * c)
    n_sub = qf_ref.shape[0] // c
    for sub in range(n_sub):
        rf = sub * c
        rb = (n_sub - 1 - sub) * c
        _hgrn_decays(zff_ref, rf, lbf_ref, tabf_ref, kf_scr, pf_scr, decf_scr)
        _hgrn_decays(zfb_ref, rb, lbb_ref, tabb_ref, kb_scr, pb_scr, decb_scr)
        span = jnp.maximum(jnp.max(jnp.abs(decf_scr[mid, :])), jnp.max(jnp.abs(decb_scr[mid, :])))
        bounded = (span < HG_MAX_LOG2_SPAN) & (q_abs < HG_MAX_ABS_Q)

        def run(flag, rf=rf, rb=rb):
            _hgrn_direction(qf_ref, vf_ref, rf, kf_scr, decf_scr, maskf_ref, stf_scr, of_ref,
                            c - 1, flag)
            _hgrn_direction(qb_ref, vb_ref, rb, kb_scr, decb_scr, maskb_ref, stb_scr, ob_ref,
                            0, flag)

        @pl.when(bounded)
        def _():
            run(True)

        @pl.when(jnp.logical_not(bounded))
        def _():
            _hgrn_more_decays(tabf_ref, pf_scr, decf_scr)
            _hgrn_more_decays(tabb_ref, pb_scr, decb_scr)
            run(False)

    @pl.when(ci == pl.num_programs(1) - 1)
    def _():
        sf_ref[0] = stf_scr[...]
        sb_ref[0] = stb_scr[...]


def _hgrn(zm, zf, lb_f, lb_b, s0_f, s0_b, *, seq_len, n_seq, row0, out_rows,
          prev_of=None, prev_ob=None):
    c = HG_CHUNK
    blk = c * HG_STEP_CHUNKS
    nc = seq_len // blk
    rb0 = row0 // blk
    w = HG_W
    tab_f, mask_f = _hgrn_tables(False)
    tab_b, mask_b = _hgrn_tables(True)
    fwd = lambda col: (lambda b, i: (rb0 + b * nc + i, col))
    bwd = lambda col: (lambda b, i: (rb0 + b * nc + nc - 1 - i, col))
    const2 = lambda b, i: (0, 0)
    const3 = lambda b, i: (0, 0, 0)
    state = lambda b, i: (b, 0, 0, 0)
    nt = (2 + HG_LEVELS) * c
    in_specs = [
        pl.BlockSpec((blk, w), fwd(ZM_Q // w)),
        pl.BlockSpec((blk, w), fwd(ZM_I // w)),
        pl.BlockSpec((blk, w), fwd(0)),
        pl.BlockSpec((blk, w), bwd(ZM_Q // w)),
        pl.BlockSpec((blk, w), bwd(ZM_I // w)),
        pl.BlockSpec((blk, w), bwd(1)),
        pl.BlockSpec((1, w), const2),
        pl.BlockSpec((1, w), const2),
        pl.BlockSpec((nt, 3 * c), const2),
        pl.BlockSpec((HG_LEVELS + 1, c, c), const3),
        pl.BlockSpec((nt, 3 * c), const2),
        pl.BlockSpec((HG_LEVELS + 1, c, c), const3),
        pl.BlockSpec((1, HG_HEADS, HG_DK, HG_DK), state),
        pl.BlockSpec((1, HG_HEADS, HG_DK, HG_DK), state),
    ]
    args = [zm, zm, zf, zm, zm, zf, lb_f.reshape(1, w), lb_b.reshape(1, w),
            jnp.asarray(np.tile(tab_f, (1, 3)), BF16), jnp.asarray(mask_f),
            jnp.asarray(np.tile(tab_b, (1, 3)), BF16), jnp.asarray(mask_b), s0_f, s0_b]
    kernel = _hgrn_kernel
    aliases = {}
    if prev_of is not None:
        in_specs += [pl.BlockSpec(memory_space=pl.ANY), pl.BlockSpec(memory_space=pl.ANY)]
        args += [prev_of, prev_ob]
        aliases = {14: 0, 15: 1}
        kernel = functools.partial(_hgrn_alias_kernel, 14)
    st_shape = jax.ShapeDtypeStruct((n_seq, HG_HEADS, HG_DK, HG_DK), F32)
    return pl.pallas_call(
        kernel,
        grid=(n_seq, nc),
        in_specs=in_specs,
        out_specs=[
            pl.BlockSpec((blk, w), fwd(0)),
            pl.BlockSpec((blk, w), bwd(0)),
            pl.BlockSpec((1, HG_HEADS, HG_DK, HG_DK), state),
            pl.BlockSpec((1, HG_HEADS, HG_DK, HG_DK), state),
        ],
        out_shape=[jax.ShapeDtypeStruct((out_rows, w), F32),
                   jax.ShapeDtypeStruct((out_rows, w), F32), st_shape, st_shape],
        scratch_shapes=[pltpu.VMEM((HG_HEADS, HG_DK, HG_DK), F32),
                        pltpu.VMEM((HG_HEADS, HG_DK, HG_DK), F32),
                        pltpu.VMEM((c, w), BF16), pltpu.VMEM((c, w), BF16),
                        pltpu.VMEM((3 * c, w), BF16), pltpu.VMEM((3 * c, w), BF16),
                        pltpu.VMEM((nt, w), F32), pltpu.VMEM((nt, w), F32)],
        input_output_aliases=aliases,
        compiler_params=_cparams(2),
        name="hgrn2",
    )(*args)


def _hgrn_alias_kernel(n_in, *refs):
    return _hgrn_kernel(*refs[:n_in], *refs[n_in + 2:])


def kernel(x, c, ctx, c_ctx, w_mod, b_mod, norm1_w, norm2_w, w_in, hy_conv_w, hy_conv_b,
           hy_f_w1, hy_f_b1, hy_f_w2, hy_f_b2, hy_f_w3, hy_f_freq, hy_decay, hy_skip,
           hg_lower, hg_norm_w, w_br_hy, w_br_fn, w_br_hg, w_out,
           ffn_w1, ffn_w3, ffn_w2, moe_router, moe_w1, moe_w3, moe_w2, final_norm_w):
    batch, seq, d = x.shape
    l_ctx = ctx.shape[1]
    depth = w_in.shape[0]
    n_lat = batch * seq
    n_all = n_lat + batch * l_ctx

    lb_all = jnp.cumsum(jax.nn.softmax(hg_lower.astype(F32), axis=1), axis=1)
    lb_all = lb_all - lb_all[:, :1]

    cc = jnp.zeros((8, d), F32).at[:batch].set(c).at[batch].set(c_ctx)
    mods = _modulation(cc, w_mod, b_mod).reshape(depth, 8, 6, d)

    w1p = jnp.zeros((depth, LANE, HY_FILTER_HIDDEN), F32).at[:, :HY_POS_DIM].set(hy_f_w1)
    dft_lat = _hyena_dft(seq)
    dft_ctx = _hyena_dft(l_ctx)
    kf_lat = _hyena_filters(seq, w1p, hy_f_b1, hy_f_w2, hy_f_b2, hy_f_w3, hy_f_freq, hy_decay,
                            dft_lat[0], dft_lat[1])
    kf_ctx = _hyena_filters(l_ctx, w1p, hy_f_b1, hy_f_w2, hy_f_b2, hy_f_w3, hy_f_freq, hy_decay,
                            dft_ctx[0], dft_ctx[1])
    fn_lat = _fnet_dft(seq)
    fn_ctx = _fnet_dft(l_ctx)

    rows_x = jnp.concatenate([x.reshape(n_lat, d), ctx.reshape(batch * l_ctx, d)], axis=0)
    zero_state = jnp.zeros((batch, HG_HEADS, HG_DK, HG_DK), F32)
    tile_cap = math.gcd(seq, batch * l_ctx)
    tok = dict(seq=seq, batch=batch)
    tile = lambda t: dict(tm=min(t, tile_cap))
    lat = dict(seq_len=seq, n_seq=batch, row0=0, out_rows=n_all)
    cx = dict(seq_len=l_ctx, n_seq=batch, row0=n_lat, out_rows=n_all)

    for l in range(depth):
        last = l == depth - 1
        rows = n_lat if last else n_all
        m = mods[l]

        h = _norm_rows(rows_x, norm1_w[l], m, **tok, **tile(512))
        zm, zf = _in_proj(h, w_in, l, **tile(1024))

        of, ob, s_f, s_b = _hgrn(zm, zf, lb_all[0, l], lb_all[1, l], zero_state, zero_state, **cx)
        of, ob, _, _ = _hgrn(zm, zf, lb_all[0, l], lb_all[1, l], s_f, s_b, prev_of=of,
                             prev_ob=ob, **lat)

        hy_args = (hy_conv_w[l], hy_conv_b[l])
        mix = dict(lat, out_rows=rows)
        u = _hyena_longconv(zm, 0, zm, 1, *hy_args, kf_lat, l, 0, hy_skip[l], dft_lat, **mix)
        y_hy = _hyena_longconv(u, None, zm, 2, *hy_args, kf_lat, l, 1, hy_skip[l], dft_lat, **mix)
        y_fn = _fnet(zm, fn_lat, **mix)
        if not last:
            u = _hyena_longconv(zm, 0, zm, 1, *hy_args, kf_ctx, l, 0, hy_skip[l], dft_ctx,
                                prev_out=u, **cx)
            y_hy = _hyena_longconv(u, None, zm, 2, *hy_args, kf_ctx, l, 1, hy_skip[l], dft_ctx,
                                   prev_out=y_hy, **cx)
            y_fn = _fnet(zm, fn_ctx, prev_out=y_fn, **cx)

        rows_x = _merge_out(y_hy, y_fn, of, ob, zm, rows_x, hg_norm_w[l], m,
                            w_br_hy[l].astype(BF16), w_br_fn[l].astype(BF16),
                            w_br_hg[l].astype(BF16), w_out[l].astype(BF16), rows=rows,
                            **tok, **tile(256))

        if l % 2 == 0:
            i = l // 2
            rows_x = _ffn_dense(rows_x, norm2_w[l], m, ffn_w1, ffn_w3, ffn_w2, i, rows=rows,
                                **tok, **tile(1024))
        else:
            i = l // 2
            rows_x = _moe_top2(rows_x, norm2_w[l], m, moe_router[i], moe_w1, moe_w3, moe_w2, i,
                               rows=rows, **tok, **tile(512))

    out = _final_norm(rows_x, final_norm_w, rows=n_lat, **tile(512))
    return out.reshape(batch, seq, d)
```

```python
import functools
import math

import numpy as np
import jax
import jax.numpy as jnp
from jax import lax
from jax.experimental import pallas as pl
from jax.experimental.pallas import tpu as pltpu

F32 = jnp.float32
BF16 = jnp.bfloat16
NORM_EPS = 1e-6

D_MODEL = 2048
DEPTH = 4
HY_W = 1024
HY_BANDS = 16
HY_POS_DIM = 2 * HY_BANDS + 1
HY_FILTER_HIDDEN = 64
FN_W = 1024
FN_GROUP_W = 128
HG_W = 1024
HG_HEADS = 8
HG_DK = 128
HG_CHUNK = 128
HG_LEVELS = 7
HG_STEP_CHUNKS = 1
N_EXPERTS = 8
LANE = 128

HG_FF = 3 * HY_W + FN_W + 2 * HG_W
HG_I = HG_FF + 2 * HG_W
ZM_HY = 0
ZM_FN = ZM_HY + 3 * HY_W
ZM_Q = ZM_FN + FN_W
ZM_G = ZM_Q + HG_W
ZM_I = HG_I
ZM_GATE = HG_I + HG_W
ZM_COLS = ZM_GATE + 3 * D_MODEL

VMEM_LIMIT = 56 * 1024 * 1024


def _cparams(n_axes, vmem=VMEM_LIMIT):
    return pltpu.CompilerParams(dimension_semantics=("arbitrary",) * n_axes, vmem_limit_bytes=vmem)


def _resident(shape, index_map):
    return pl.BlockSpec(shape, index_map, pipeline_mode=pl.Buffered(1))


def _silu(v):
    return v * jax.nn.sigmoid(v)


def _bdot(a, b):
    return jnp.dot(a, b, preferred_element_type=F32)


def _mod_kernel(c_ref, w_ref, b_ref, o_ref):
    c = _silu(c_ref[...])
    o_ref[0] = jnp.dot(c, w_ref[0], preferred_element_type=F32,
                       precision=lax.Precision.HIGHEST) + b_ref[0]


def _modulation(cc, w_mod, b_mod):
    depth, d, n6 = w_mod.shape
    tn = 2048
    return pl.pallas_call(
        _mod_kernel,
        grid=(depth, n6 // tn),
        in_specs=[
            pl.BlockSpec((8, d), lambda l, n: (0, 0)),
            pl.BlockSpec((1, d, tn), lambda l, n: (l, 0, n)),
            pl.BlockSpec((1, 1, tn), lambda l, n: (l, 0, n)),
        ],
        out_specs=pl.BlockSpec((1, 8, tn), lambda l, n: (l, 0, n)),
        out_shape=jax.ShapeDtypeStruct((depth, 8, n6), F32),
        compiler_params=_cparams(2),
        name="modulation",
    )(cc, w_mod, b_mod.reshape(depth, 1, n6))


def _norm_mod(x, nw, shift, scale):
    ms = jnp.mean(x * x, axis=-1, keepdims=True)
    return (x * lax.rsqrt(ms + NORM_EPS) * nw) * (1 + scale) + shift


def _group_of(row_block, tm, seq, batch):
    return jnp.minimum((row_block * tm) // seq, batch)


def _norm_rows_kernel(x_ref, nw_ref, mod_ref, o_ref):
    h = _norm_mod(x_ref[...], nw_ref[...], mod_ref[0, 0:1, :], mod_ref[0, 1:2, :])
    o_ref[...] = h.astype(o_ref.dtype)


def _norm_rows(x, nw, mods, *, seq, batch, tm=512):
    rows, d = x.shape
    return pl.pallas_call(
        _norm_rows_kernel,
        grid=(rows // tm,),
        in_specs=[
            pl.BlockSpec((tm, d), lambda m: (m, 0)),
            pl.BlockSpec((1, d), lambda m: (0, 0)),
            pl.BlockSpec((1, 6, d), lambda m: (_group_of(m, tm, seq, batch), 0, 0)),
        ],
        out_specs=pl.BlockSpec((tm, d), lambda m: (m, 0)),
        out_shape=jax.ShapeDtypeStruct((rows, d), BF16),
        compiler_params=_cparams(1),
        name="norm_rows",
    )(x, nw.reshape(1, d), mods)


def _inproj_kernel(h_ref, w_ref, z_ref, zf_ref, w_scr, *, n_main):
    @pl.when(pl.program_id(1) == 0)
    def _():
        w_scr[...] = w_ref[0].astype(BF16)

    r = _bdot(h_ref[...], w_scr[...])

    @pl.when(pl.program_id(0) < n_main)
    def _():
        z_ref[...] = r.astype(z_ref.dtype)

    @pl.when(pl.program_id(0) >= n_main)
    def _():
        zf_ref[...] = r


def _in_proj(h, w_in, layer, *, tm=1024, tn=1024):
    rows, d = h.shape
    n_cols = w_in.shape[2]
    nb = n_cols // tn
    f0 = HG_FF // tn
    nf = 2 * HG_W // tn
    n_main = nb - nf
    nm = rows // tm
    col = lambda j: jnp.where(j < f0, j, jnp.where(j < n_main, j + nf, j - n_main + f0))
    z_idx = lambda j, m: (jnp.where(j < n_main, m, nm - 1), col(jnp.minimum(j, n_main - 1)))
    zf_idx = lambda j, m: (jnp.where(j < n_main, 0, m), jnp.maximum(j - n_main, 0))
    return pl.pallas_call(
        functools.partial(_inproj_kernel, n_main=n_main),
        grid=(nb, nm),
        in_specs=[
            pl.BlockSpec((tm, d), lambda j, m: (m, 0)),
            pl.BlockSpec((1, d, tn), lambda j, m: (layer, 0, col(j))),
        ],
        out_specs=[pl.BlockSpec((tm, tn), z_idx), pl.BlockSpec((tm, tn), zf_idx)],
        out_shape=[jax.ShapeDtypeStruct((rows, n_cols), BF16),
                   jax.ShapeDtypeStruct((rows, nf * tn), F32)],
        scratch_shapes=[pltpu.VMEM((d, tn), BF16)],
        compiler_params=_cparams(2),
        name="in_proj",
    )(h, w_in)


def _merge_kernel(yhy_ref, yfn_ref, of_ref, ob_ref, g_ref, ga0_ref, ga1_ref, gb0_ref, gb1_ref,
                  gc0_ref, gc1_ref, x_ref, hnw_ref, mod_ref,
                  wbhy_ref, wbfn_ref, wbhg_ref, wo_ref, o_ref):
    d = x_ref.shape[1]
    ohg = of_ref[...] + ob_ref[...]
    nw = hnw_ref[...]
    heads = []
    for h in range(ohg.shape[1] // HG_DK):
        oh = ohg[:, h * HG_DK:(h + 1) * HG_DK]
        ms = jnp.mean(oh * oh, axis=-1, keepdims=True)
        heads.append(oh * lax.rsqrt(ms + NORM_EPS) * nw)
    y_hg = jnp.concatenate(heads, axis=-1) * _silu(g_ref[...].astype(F32))
    gate = lambda lo, hi: jax.nn.sigmoid(
        jnp.concatenate([lo[...], hi[...]], axis=-1).astype(F32))
    merged = gate(ga0_ref, ga1_ref) * _bdot(yhy_ref[...], wbhy_ref[...])
    merged += gate(gb0_ref, gb1_ref) * _bdot(yfn_ref[...], wbfn_ref[...])
    merged += gate(gc0_ref, gc1_ref) * _bdot(y_hg.astype(BF16), wbhg_ref[...])
    out = _bdot(merged.astype(BF16), wo_ref[...])
    o_ref[...] = x_ref[...] + mod_ref[0, 2:3, :] * out


def _merge_out(y_hy, y_fn, o_f, o_b, zm, x, hg_nw, mods, wb_hy, wb_fn, wb_hg, w_o,
               *, rows, seq, batch, tm=256):
    d = x.shape[1]
    w = y_hy.shape[1]
    tm = min(tm, rows)
    row = lambda m: (m, 0)
    const = lambda m: (0, 0)
    gate0 = ZM_GATE // (d // 2)
    return pl.pallas_call(
        _merge_kernel,
        grid=(rows // tm,),
        in_specs=[
            pl.BlockSpec((tm, w), row),
            pl.BlockSpec((tm, w), row),
            pl.BlockSpec((tm, w), row),
            pl.BlockSpec((tm, w), row),
            pl.BlockSpec((tm, w), lambda m: (m, ZM_G // w)),
            *[pl.BlockSpec((tm, d // 2), functools.partial(lambda i, m: (m, i), gate0 + i))
              for i in range(6)],
            pl.BlockSpec((tm, d), row),
            pl.BlockSpec((1, HG_DK), const),
            pl.BlockSpec((1, 6, d), lambda m: (_group_of(m, tm, seq, batch), 0, 0)),
            _resident((w, d), const),
            _resident((w, d), const),
            _resident((w, d), const),
            _resident((d, d), const),
        ],
        out_specs=pl.BlockSpec((tm, d), row),
        out_shape=jax.ShapeDtypeStruct((rows, d), F32),
        compiler_params=_cparams(1),
        name="merge_out",
    )(y_hy, y_fn, o_f, o_b, zm, *([zm] * 6), x, hg_nw.reshape(1, HG_DK), mods,
      wb_hy, wb_fn, wb_hg, w_o)


def _ffn_kernel(x_ref, nw_ref, mod_ref, w1_ref, w3_ref, w2_ref, o_ref, h_scr):
    f = pl.program_id(1)

    @pl.when(f == 0)
    def _():
        h = _norm_mod(x_ref[...], nw_ref[...], mod_ref[0, 3:4, :], mod_ref[0, 4:5, :])
        h_scr[...] = h.astype(BF16)
        o_ref[...] = jnp.zeros_like(o_ref)

    h = h_scr[...]
    act = _silu(_bdot(h, w1_ref[0].astype(BF16))) * _bdot(h, w3_ref[0].astype(BF16))
    o_ref[...] += _bdot(act.astype(BF16), w2_ref[0].astype(BF16))

    @pl.when(f == pl.num_programs(1) - 1)
    def _():
        o_ref[...] = x_ref[...] + mod_ref[0, 5:6, :] * o_ref[...]


def _ffn_dense(x, nw, mods, w1, w3, w2, layer, *, rows, seq, batch, tm=1024, tf=256):
    d = x.shape[1]
    ff = w1.shape[2]
    tm = min(tm, rows)
    return pl.pallas_call(
        _ffn_kernel,
        grid=(rows // tm, ff // tf),
        in_specs=[
            _resident((tm, d), lambda m, f: (m, 0)),
            pl.BlockSpec((1, d), lambda m, f: (0, 0)),
            pl.BlockSpec((1, 6, d), lambda m, f: (_group_of(m, tm, seq, batch), 0, 0)),
            pl.BlockSpec((1, d, tf), lambda m, f: (layer, 0, f)),
            pl.BlockSpec((1, d, tf), lambda m, f: (layer, 0, f)),
            pl.BlockSpec((1, tf, d), lambda m, f: (layer, f, 0)),
        ],
        out_specs=pl.BlockSpec((tm, d), lambda m, f: (m, 0)),
        out_shape=jax.ShapeDtypeStruct((rows, d), F32),
        scratch_shapes=[pltpu.VMEM((tm, d), BF16)],
        compiler_params=_cparams(2),
        name="ffn_dense",
    )(x, nw.reshape(1, d), mods, w1, w3, w2)


def _top2(logits, n_experts):
    lane = lax.broadcasted_iota(jnp.int32, logits.shape, 1).astype(F32)
    neg = jnp.float32(-jnp.inf)
    lg = jnp.where(lane < n_experts, logits, neg)
    m1 = jnp.max(lg, axis=-1, keepdims=True)
    i1 = jnp.min(jnp.where(lg == m1, lane, float(LANE)), axis=-1, keepdims=True)
    lg2 = jnp.where(lane == i1, neg, lg)
    m2 = jnp.max(lg2, axis=-1, keepdims=True)
    i2 = jnp.min(jnp.where(lg2 == m2, lane, float(LANE)), axis=-1, keepdims=True)
    e2 = jnp.exp(m2 - m1)
    return i1, i2, 1.0 / (1.0 + e2), e2 / (1.0 + e2)


def _route_kernel(x_ref, nw_ref, mod_ref, r_ref, h_ref, route_ref):
    h = _norm_mod(x_ref[...], nw_ref[...], mod_ref[0, 3:4, :], mod_ref[0, 4:5, :])
    logits = jnp.dot(h, r_ref[...], preferred_element_type=F32, precision=lax.Precision.HIGHEST)
    i1, i2, p1, p2 = _top2(logits, N_EXPERTS)
    lane = lax.broadcasted_iota(jnp.int32, logits.shape, 1)
    route = jnp.where(lane == 0, i1, jnp.where(lane == 1, i2,
                      jnp.where(lane == 2, p1, jnp.where(lane == 3, p2, 0.0))))
    h_ref[...] = h
    route_ref[...] = route


def _moe_route(x, nw, mods, router, *, rows, seq, batch, tm=512):
    d = x.shape[1]
    rpad = jnp.zeros((d, LANE), F32).at[:, :router.shape[1]].set(router)
    return pl.pallas_call(
        _route_kernel,
        grid=(rows // tm,),
        in_specs=[
            pl.BlockSpec((tm, d), lambda m: (m, 0)),
            pl.BlockSpec((1, d), lambda m: (0, 0)),
            pl.BlockSpec((1, 6, d), lambda m: (_group_of(m, tm, seq, batch), 0, 0)),
            pl.BlockSpec((d, LANE), lambda m: (0, 0)),
        ],
        out_specs=[pl.BlockSpec((tm, d), lambda m: (m, 0)),
                   pl.BlockSpec((tm, LANE), lambda m: (m, 0))],
        out_shape=[jax.ShapeDtypeStruct((rows, d), F32), jax.ShapeDtypeStruct((rows, LANE), F32)],
        compiler_params=_cparams(1),
        name="moe_route",
    )(x, nw.reshape(1, d), mods, rpad)


def _moe_plan(route, n_experts, tile):
    rows = route.shape[0]
    e = route[:, :2].astype(jnp.int32).reshape(-1)
    onehot = (e[:, None] == jnp.arange(n_experts, dtype=jnp.int32)[None, :]).astype(jnp.int32)
    before = jnp.cumsum(onehot, axis=0) - onehot
    rank = jnp.sum(before * onehot, axis=1)
    counts = jnp.sum(onehot, axis=0)
    padded = ((counts + tile - 1) // tile) * tile
    ends = jnp.cumsum(padded)
    starts = ends - padded
    pos = jnp.sum(starts[None, :] * onehot, axis=1) + rank
    n_tiles = -(-2 * rows // tile) + n_experts
    src = jnp.zeros((n_tiles * tile,), jnp.int32).at[pos].set(
        jnp.arange(2 * rows, dtype=jnp.int32) // 2)
    tile_start = jnp.arange(n_tiles, dtype=jnp.int32) * tile
    tile_expert = jnp.sum((tile_start[:, None] >= ends[None, :]).astype(jnp.int32), axis=1)
    tile_expert = jnp.minimum(tile_expert, n_experts - 1)
    n_used = (ends[-1] // tile).reshape(1)
    return pos, src, tile_expert, n_used


def _row_copy(src_hbm, src_row, dst, dst_row, sem):
    return pltpu.make_async_copy(src_hbm.at[pl.ds(src_row, 1)], dst.at[pl.ds(dst_row, 1)], sem)


def _gather_rows(idx_ref, base, stride, src_hbm, dst, sem, *, wait):
    if wait:
        pltpu.make_async_copy(src_hbm.at[pl.ds(0, dst.shape[0])], dst, sem).wait()
        return

    def body(r, carry):
        _row_copy(src_hbm, idx_ref[base + stride * r], dst, r, sem).start()
        return carry

    lax.fori_loop(0, dst.shape[0], body, 0, unroll=8)


def _moe_ffn_kernel(te_ref, nu_ref, src_ref, h_hbm, w1_ref, w3_ref, w2_ref, o_ref,
                    rows_scr, h_scr, sems):
    t = pl.program_id(0)
    f = pl.program_id(1)
    last = pl.num_programs(1) - 1
    tile = o_ref.shape[0]
    n_used = nu_ref[0]
    used = t < n_used
    slot = t % 2

    def gather(tile_idx, buf, wait):
        _gather_rows(src_ref, tile_idx * tile, 1, h_hbm, rows_scr.at[buf], sems.at[buf], wait=wait)

    @pl.when(used & (f == 0) & (t == 0))
    def _():
        gather(t, slot, False)

    @pl.when(used & (f == 0))
    def _():
        gather(t, slot, True)
        h_scr[...] = rows_scr[slot].astype(BF16)
        o_ref[...] = jnp.zeros_like(o_ref)

    @pl.when((f == 0) & (t + 1 < n_used))
    def _():
        gather(t + 1, 1 - slot, False)

    @pl.when(used)
    def _():
        h = h_scr[...]
        act = _silu(_bdot(h, w1_ref[0, 0].astype(BF16))) * _bdot(h, w3_ref[0, 0].astype(BF16))
        o_ref[...] += _bdot(act.astype(BF16), w2_ref[0, 0].astype(BF16))

    @pl.when(jnp.logical_not(used) & (f == last))
    def _():
        o_ref[...] = jnp.zeros_like(o_ref)


def _moe_ffn(h, src, tile_expert, n_used, w1, w3, w2, layer, *, tile, tf=256):
    d = h.shape[1]
    n = src.shape[0]
    ff = w1.shape[3]
    nf = ff // tf
    fsel = lambda t, f, nu: jnp.where(t < nu[0], f, nf - 1)
    up = lambda t, f, te, nu, sr: (layer, te[t], 0, fsel(t, f, nu))
    down = lambda t, f, te, nu, sr: (layer, te[t], fsel(t, f, nu), 0)
    return pl.pallas_call(
        _moe_ffn_kernel,
        grid_spec=pltpu.PrefetchScalarGridSpec(
            num_scalar_prefetch=3,
            grid=(n // tile, nf),
            in_specs=[
                pl.BlockSpec(memory_space=pl.ANY),
                pl.BlockSpec((1, 1, d, tf), up),
                pl.BlockSpec((1, 1, d, tf), up),
                pl.BlockSpec((1, 1, tf, d), down),
            ],
            out_specs=pl.BlockSpec((tile, d), lambda t, f, te, nu, sr: (t, 0)),
            scratch_shapes=[pltpu.VMEM((2, tile, d), F32), pltpu.VMEM((tile, d), BF16),
                            pltpu.SemaphoreType.DMA((2,))],
        ),
        out_shape=jax.ShapeDtypeStruct((n, d), F32),
        compiler_params=_cparams(2),
        name="moe_ffn",
    )(tile_expert, n_used, src, h, w1, w3, w2)


def _combine_kernel(pos_ref, ys_hbm, x_ref, route_ref, mod_ref, o_ref, y1_scr, y2_scr, sem):
    base = 2 * pl.program_id(0) * x_ref.shape[0]
    _gather_rows(pos_ref, base, 2, ys_hbm, y1_scr, sem, wait=False)
    _gather_rows(pos_ref, base + 1, 2, ys_hbm, y2_scr, sem, wait=False)
    _gather_rows(pos_ref, base, 2, ys_hbm, y1_scr, sem, wait=True)
    _gather_rows(pos_ref, base + 1, 2, ys_hbm, y2_scr, sem, wait=True)
    mix = route_ref[:, 2:3] * y1_scr[...] + route_ref[:, 3:4] * y2_scr[...]
    o_ref[...] = x_ref[...] + mod_ref[0, 5:6, :] * mix


def _moe_combine(x, mods, route, ys, pos, *, rows, seq, batch, tm=256):
    d = x.shape[1]
    return pl.pallas_call(
        _combine_kernel,
        grid_spec=pltpu.PrefetchScalarGridSpec(
            num_scalar_prefetch=1,
            grid=(rows // tm,),
            in_specs=[
                pl.BlockSpec(memory_space=pl.ANY),
                pl.BlockSpec((tm, d), lambda m, pos: (m, 0)),
                pl.BlockSpec((tm, LANE), lambda m, pos: (m, 0)),
                pl.BlockSpec((1, 6, d), lambda m, pos: (_group_of(m, tm, seq, batch), 0, 0)),
            ],
            out_specs=pl.BlockSpec((tm, d), lambda m, pos: (m, 0)),
            scratch_shapes=[pltpu.VMEM((tm, d), F32), pltpu.VMEM((tm, d), F32),
                            pltpu.SemaphoreType.DMA(())],
        ),
        out_shape=jax.ShapeDtypeStruct((rows, d), F32),
        compiler_params=_cparams(1),
        name="moe_combine",
    )(pos, ys, x, route, mods)


def _moe_top2(x, nw, mods, router, w1, w3, w2, layer, *, rows, seq, batch, tm, tile=768):
    tok = dict(rows=rows, seq=seq, batch=batch)
    h, route = _moe_route(x, nw, mods, router, tm=tm, **tok)
    pos, src, tile_expert, n_used = _moe_plan(route, w1.shape[1], tile)
    ys = _moe_ffn(h, src, tile_expert, n_used, w1, w3, w2, layer, tile=tile)
    return _moe_combine(x, mods, route, ys, pos, tm=min(tm, 256), **tok)


def _final_norm_kernel(x_ref, w_ref, o_ref):
    x = x_ref[...]
    ms = jnp.mean(x * x, axis=-1, keepdims=True)
    o_ref[...] = x * lax.rsqrt(ms + NORM_EPS) * w_ref[...]


def _final_norm(x, w, *, rows, tm=512):
    d = x.shape[1]
    return pl.pallas_call(
        _final_norm_kernel,
        grid=(rows // tm,),
        in_specs=[pl.BlockSpec((tm, d), lambda m: (m, 0)), pl.BlockSpec((1, d), lambda m: (0, 0))],
        out_specs=pl.BlockSpec((tm, d), lambda m: (m, 0)),
        out_shape=jax.ShapeDtypeStruct((rows, d), F32),
        compiler_params=_cparams(1),
        name="final_norm",
    )(x, w.reshape(1, d))


def _dft_angles(rows, cols, period):
    k = np.asarray(rows, np.int64)[:, None]
    s = np.asarray(cols, np.int64)[None, :]
    return ((k * s) % period).astype(np.float64) * (2.0 * math.pi / period)


def _bf16_const(a):
    return jnp.asarray(np.asarray(a, np.float32).astype(jnp.bfloat16))


def _hyena_dft(seq_len):
    n = 2 * seq_len
    half = seq_len // 2
    pos = np.arange(seq_len)
    freq = np.concatenate([2 * np.arange(half), 2 * np.arange(half) + 1])
    ang = _dft_angles(freq, pos, n)
    alt = np.where(pos % 2 == 0, 1.0, -1.0)
    fc = np.cos(ang)
    fs = np.sin(ang)
    ic = 2.0 / n * fc.T
    isn = 2.0 / n * fs.T
    fs[0, :] = alt
    ic[:, 0] = 1.0 / n
    isn[:, 0] = alt / n
    return (_bf16_const(fc), _bf16_const(fs), _bf16_const(ic), _bf16_const(isn),
            _flip_matrix(half))


def _hyena_feats(seq_len):
    pos = np.arange(seq_len, dtype=np.float64)
    t = pos / max(seq_len - 1, 1)
    bands = np.linspace(1e-4, HY_BANDS - 1, HY_BANDS).astype(np.float32).astype(np.float64)
    ang = 2 * math.pi * pos[:, None] * bands[None, :] / seq_len
    feats = np.concatenate([t[:, None], np.cos(ang), -np.sin(ang)], axis=-1)
    out = np.zeros((seq_len, LANE), np.float32)
    out[:, :HY_POS_DIM] = feats
    return jnp.asarray(out)


def _split2(a):
    hi = a.astype(BF16)
    return hi, (a - hi.astype(F32)).astype(BF16)


def _dot3(a, b):
    a_hi, a_lo = _split2(a)
    b_hi, b_lo = _split2(b)
    return _bdot(a_hi, b_hi) + _bdot(a_hi, b_lo) + _bdot(a_lo, b_hi)


def _filter_kernel(feat_ref, w1_ref, b1_ref, w2_ref, b2_ref, fr_ref, w3f_ref, w3b_ref,
                   dcf_ref, dcb_ref, fc_ref, fs_ref, o_ref, h_scr):
    @pl.when((pl.program_id(1) == 0) & (pl.program_id(2) == 0))
    def _():
        hp = lax.Precision.HIGHEST
        h1 = jnp.sin(fr_ref[0, 0:1, :] * (jnp.dot(feat_ref[...], w1_ref[0], precision=hp,
                                                  preferred_element_type=F32) + b1_ref[0]))
        h_scr[...] = jnp.sin(fr_ref[0, 1:2, :] * (jnp.dot(h1, w2_ref[0], precision=hp,
                                                          preferred_element_type=F32) + b2_ref[0]))

    h = h_scr[...]
    t = feat_ref[:, 0:1]
    hf = _dot3(h, w3f_ref[0]) * jnp.exp(-t * dcf_ref[0])
    hb = _dot3(h, w3b_ref[0]) * jnp.exp(-t * dcb_ref[0])
    row = lax.broadcasted_iota(jnp.int32, hf.shape, 0)
    hb = jnp.where(row == 0, 0.0, hb)
    ss = jnp.sum(hf * hf + hb * hb, axis=0, keepdims=True)
    scale = lax.rsqrt(ss + NORM_EPS)
    even = hf * scale + hb * scale
    odd = hf * scale - hb * scale
    kc = _bdot(fc_ref[...], even.astype(BF16))
    ks = _bdot(fs_ref[...], odd.astype(BF16))
    sign = jnp.where(row % 2 == 0, 1.0, -1.0)
    nyq = jnp.sum(even * sign, axis=0, keepdims=True)
    o_ref[0, 0] = kc
    o_ref[0, 1] = jnp.where(row == 0, nyq, ks)


def _hyena_filters(seq_len, w1p, b1, w2, b2, w3, freq, decay, fc, fs, *, tn=256):
    depth = w1p.shape[0]
    hid = HY_FILTER_HIDDEN
    nb = HY_W // tn
    feats = _hyena_feats(seq_len)
    col = lambda d: (lambda l, o, c: (l, 0, (2 * o + d) * nb + c))
    lyr = lambda l, o, c: (l, 0, 0)
    return pl.pallas_call(
        _filter_kernel,
        grid=(depth, 2, nb),
        in_specs=[
            pl.BlockSpec((seq_len, LANE), lambda l, o, c: (0, 0)),
            pl.BlockSpec((1, LANE, hid), lyr),
            pl.BlockSpec((1, 1, hid), lyr),
            pl.BlockSpec((1, hid, hid), lyr),
            pl.BlockSpec((1, 1, hid), lyr),
            pl.BlockSpec((1, 2, hid), lyr),
            pl.BlockSpec((1, hid, tn), col(0)),
            pl.BlockSpec((1, hid, tn), col(1)),
            pl.BlockSpec((1, 1, tn), col(0)),
            pl.BlockSpec((1, 1, tn), col(1)),
            _resident((seq_len, seq_len), lambda l, o, c: (0, 0)),
            _resident((seq_len, seq_len), lambda l, o, c: (0, 0)),
        ],
        out_specs=pl.BlockSpec((1, 2, seq_len, tn), lambda l, o, c: (l, 0, 0, o * nb + c)),
        out_shape=jax.ShapeDtypeStruct((depth, 2, seq_len, 2 * HY_W), F32),
        scratch_shapes=[pltpu.VMEM((seq_len, hid), F32)],
        compiler_params=_cparams(3),
        name="hyena_filters",
    )(feats, w1p, b1.reshape(depth, 1, hid), w2, b2.reshape(depth, 1, hid), freq,
      w3, w3, decay.reshape(depth, 1, 4 * HY_W), decay.reshape(depth, 1, 4 * HY_W), fc, fs)


def _conv3(z, w_ref, b_ref):
    n = z.shape[0]
    row = lax.broadcasted_iota(jnp.int32, z.shape, 0)
    prev = jnp.where(row == 0, 0.0, pltpu.roll(z, 1, 0))
    nxt = jnp.where(row == n - 1, 0.0, pltpu.roll(z, n - 1, 0))
    return prev * w_ref[0:1, :] + z * w_ref[1:2, :] + nxt * w_ref[2:3, :] + b_ref[...]


def _longconv_kernel(a_ref, g_ref, aw_ref, ab_ref, gw_ref, gb_ref, kf_ref, skip_ref,
                     fc_ref, fs_ref, ic_ref, is_ref, flip_ref, o_ref, *, conv_a):
    a = a_ref[...].astype(F32)
    if conv_a:
        a = _conv3(a, aw_ref, ab_ref)
    n = a.shape[0]
    h = n // 2
    lo = slice(0, h)
    hi = slice(h, n)
    k = lax.broadcasted_iota(jnp.int32, (h, a.shape[1]), 0)
    alt = jnp.where(k % 2 == 0, 1.0, -1.0)
    first = k == 0
    a_lo = a[lo]
    a_hi = a[hi]
    a_mid = a_hi[0:1, :]
    rev = _bdot(flip_ref[...], a_hi.astype(BF16))
    even_in = (a_lo + rev).astype(BF16)
    odd_in = (a_lo - rev).astype(BF16)
    nyq = jnp.sum(alt * (a_lo + a_hi), axis=0, keepdims=True)
    ac_e = _bdot(fc_ref[lo, lo], even_in) + alt * a_mid
    as_e = jnp.where(first, nyq, _bdot(fs_ref[lo, lo], odd_in))
    ac_o = _bdot(fc_ref[hi, lo], odd_in)
    as_o = _bdot(fs_ref[hi, lo], even_in) + alt * a_mid
    kc_e = kf_ref[0, 0, lo, :]
    ks_e = kf_ref[0, 1, lo, :]
    kc_o = kf_ref[0, 0, hi, :]
    ks_o = kf_ref[0, 1, hi, :]
    cross = as_e * ks_e
    yc_e = jnp.where(first, ac_e * kc_e, ac_e * kc_e - cross)
    ys_e = jnp.where(first, 0.0, ac_e * ks_e + as_e * kc_e)
    y_nyq = cross[0:1, :]
    yc_o = ac_o * kc_o - as_o * ks_o
    ys_o = ac_o * ks_o + as_o * kc_o
    sym = (_bdot(ic_ref[lo, lo], yc_e.astype(BF16)) + _bdot(is_ref[lo, hi], ys_o.astype(BF16))
           + alt * (y_nyq * (0.5 / n)))
    anti = _bdot(ic_ref[lo, hi], yc_o.astype(BF16)) + _bdot(is_ref[lo, lo], ys_e.astype(BF16))
    y_mid = (jnp.sum(alt * (yc_e + ys_o), axis=0, keepdims=True) * (1.0 / n)
             - yc_e[0:1, :] * (0.5 / n) + y_nyq * (0.5 / n))
    y_hi = jnp.where(first, y_mid, _bdot(flip_ref[...], (sym - anti).astype(BF16)))
    conv = jnp.concatenate([sym + anti, y_hi], axis=0)
    gate = _conv3(g_ref[...].astype(F32), gw_ref, gb_ref)
    o_ref[...] = (gate * (conv + a * skip_ref[0])).astype(o_ref.dtype)


def _hyena_longconv(a_src, a_part, zm, gate_part, conv_w, conv_b, kf, layer, order, skip,
                    dft, *, seq_len, n_seq, row0, out_rows, prev_out=None, tn=256):
    fc, fs, ic, isn, flip = dft
    assert seq_len % 4 == 0
    nb = HY_W // tn
    rb0 = row0 // seq_len
    const = lambda c, b: (0, 0)
    conv_a = a_part is not None
    aw_blk = a_part * nb if conv_a else 0
    a_blk = ZM_HY // tn + aw_blk if conv_a else 0
    gw_blk = gate_part * nb
    g_blk = ZM_HY // tn + gw_blk
    kernel = functools.partial(_longconv_kernel, conv_a=conv_a)
    in_specs = [
        pl.BlockSpec((seq_len, tn), lambda c, b: (rb0 + b, a_blk + c)),
        pl.BlockSpec((seq_len, tn), lambda c, b: (rb0 + b, g_blk + c)),
        pl.BlockSpec((3, tn), lambda c, b: (0, aw_blk + c)),
        pl.BlockSpec((1, tn), lambda c, b: (0, aw_blk + c)),
        pl.BlockSpec((3, tn), lambda c, b: (0, gw_blk + c)),
        pl.BlockSpec((1, tn), lambda c, b: (0, gw_blk + c)),
        _resident((1, 2, seq_len, tn), lambda c, b: (layer, 0, 0, order * nb + c)),
        pl.BlockSpec((1, 1, tn), lambda c, b: (order, 0, c)),
        _resident((seq_len, seq_len), const),
        _resident((seq_len, seq_len), const),
        _resident((seq_len, seq_len), const),
        _resident((seq_len, seq_len), const),
        _resident((seq_len // 2, seq_len // 2), const),
    ]
    args = [a_src, zm, conv_w, conv_b.reshape(1, -1), conv_w, conv_b.reshape(1, -1), kf,
            skip.reshape(2, 1, HY_W), fc, fs, ic, isn, flip]
    aliases = {}
    if prev_out is not None:
        in_specs.append(pl.BlockSpec(memory_space=pl.ANY))
        args.append(prev_out)
        aliases = {len(args) - 1: 0}
        kernel = functools.partial(_drop_last_input, kernel, 13)
    return pl.pallas_call(
        kernel,
        grid=(nb, n_seq),
        in_specs=in_specs,
        out_specs=pl.BlockSpec((seq_len, tn), lambda c, b: (rb0 + b, c)),
        out_shape=jax.ShapeDtypeStruct((out_rows, HY_W), BF16),
        input_output_aliases=aliases,
        compiler_params=_cparams(2),
        name="hyena_longconv",
    )(*args)


def _drop_last_input(kernel, n_in, *refs):
    return kernel(*refs[:n_in], *refs[n_in + 1:])


def _flip_matrix(half):
    r = np.arange(half)[:, None]
    c = np.arange(half)[None, :]
    return _bf16_const(np.where((r >= 1) & (c == half - r), 1.0, 0.0))


def _fnet_scale(seq_len):
    return 1.0 / math.sqrt(seq_len * FN_GROUP_W)


def _fnet_dft(seq_len):
    half = seq_len // 2
    ang_l = _dft_angles(np.arange(half), np.arange(seq_len), seq_len)
    ang_w = _dft_angles(np.arange(FN_GROUP_W), np.arange(FN_GROUP_W), FN_GROUP_W)
    scale = _fnet_scale(seq_len)
    return (_bf16_const(np.cos(ang_l) * scale), _bf16_const(np.sin(ang_l) * scale),
            _bf16_const(np.cos(ang_w)), _bf16_const(np.sin(ang_w)), _flip_matrix(half))


def _fnet_kernel(z_ref, cl_ref, sl_ref, cw_ref, sw_ref, flip_ref, o_ref, *, scale):
    z = z_ref[...]
    gc, gs = [], []
    for j in range(z.shape[1] // FN_GROUP_W):
        zj = z[:, j * FN_GROUP_W:(j + 1) * FN_GROUP_W]
        gc.append(_bdot(zj, cw_ref[...]).astype(BF16))
        gs.append(_bdot(zj, sw_ref[...]).astype(BF16))
    gc = jnp.concatenate(gc, axis=-1)
    gs = jnp.concatenate(gs, axis=-1)
    p = _bdot(cl_ref[...], gc)
    q = _bdot(sl_ref[...], gs)
    pos = lax.broadcasted_iota(jnp.int32, gc.shape, 0)
    y_mid = scale * jnp.sum(jnp.where(pos % 2 == 0, 1.0, -1.0) * gc.astype(F32), axis=0,
                            keepdims=True)
    first = lax.broadcasted_iota(jnp.int32, p.shape, 0) == 0
    y_hi = jnp.where(first, y_mid, _bdot(flip_ref[...], (p + q).astype(BF16)))
    o_ref[...] = jnp.concatenate([p - q, y_hi], axis=0).astype(o_ref.dtype)


def _fnet(zm, dft, *, seq_len, n_seq, row0, out_rows, prev_out=None, tn=512):
    cl, sl, cw, sw, flip = dft
    nb = FN_W // tn
    rb0 = row0 // seq_len
    c0 = ZM_FN // tn
    half = seq_len // 2
    const = lambda c, b: (0, 0)
    in_specs = [
        pl.BlockSpec((seq_len, tn), lambda c, b: (rb0 + b, c0 + c)),
        _resident((half, seq_len), const),
        _resident((half, seq_len), const),
        pl.BlockSpec((FN_GROUP_W, FN_GROUP_W), const),
        pl.BlockSpec((FN_GROUP_W, FN_GROUP_W), const),
        _resident((half, half), const),
    ]
    args = [zm, cl, sl, cw, sw, flip]
    kernel = functools.partial(_fnet_kernel, scale=_fnet_scale(seq_len))
    aliases = {}
    if prev_out is not None:
        in_specs.append(pl.BlockSpec(memory_space=pl.ANY))
        args.append(prev_out)
        aliases = {len(args) - 1: 0}
        kernel = functools.partial(_drop_last_input, kernel, 6)
    return pl.pallas_call(
        kernel,
        grid=(nb, n_seq),
        in_specs=in_specs,
        out_specs=pl.BlockSpec((seq_len, tn), lambda c, b: (rb0 + b, c)),
        out_shape=jax.ShapeDtypeStruct((out_rows, FN_W), BF16),
        input_output_aliases=aliases,
        compiler_params=_cparams(2),
        name="fnet",
    )(*args)


def _hgrn_tables(reverse):
    c = HG_CHUNK
    tri = np.zeros((c, c), np.float32)
    for t in range(c):
        if reverse:
            tri[t, t:] = 1.0
        else:
            tri[t, :t + 1] = 1.0
    mats = [tri]
    masks = []
    for lvl in range(HG_LEVELS):
        half = (c // 2) >> lvl
        sel = np.zeros((c, c), np.float32)
        mask = np.zeros((c, c), np.float32)
        for t in range(c):
            start = (t // (2 * half)) * 2 * half
            mid = start + half
            sel[t, mid if reverse else mid - 1] = 1.0
            for s in range(start, start + 2 * half):
                if reverse and t < mid <= s:
                    mask[t, s] = 1.0
                if (not reverse) and s < mid <= t:
                    mask[t, s] = 1.0
        mats.append(tri - sel @ tri)
        masks.append(mask)
    half = c // 2
    sel = np.zeros((c, c), np.float32)
    mask = np.zeros((c, c), np.float32)
    for t in range(c):
        start = (t // half) * half
        sel[t, start + half // 2] = 1.0
        for s in range(start, start + half):
            if (s >= t) if reverse else (s <= t):
                mask[t, s] = 1.0
    mats.append(tri - sel @ tri)
    masks.append(mask)
    mats = [mats[0], mats[1], mats[-1]] + mats[2:-1]
    return np.concatenate(mats, axis=0), np.stack(masks, axis=0)


HG_BOUNDED_BLOCKS = 3
HG_MID_BLOCK = 2


def _hgrn_level_block(lvl):
    return 1 if lvl == 0 else HG_BOUNDED_BLOCKS + lvl - 1


def _hgrn_decays(zf_ref, row0, lb_ref, tab_ref, k_scr, p_scr, dec_scr):
    lb = lb_ref[...]
    f = lb + (1.0 - lb) * jax.nn.sigmoid(zf_ref[row0:row0 + HG_CHUNK, :])
    lf = jnp.log2(f)
    k_scr[...] = (1.0 - f).astype(BF16)
    p1 = lf.astype(BF16)
    r1 = lf - p1.astype(F32)
    p2 = r1.astype(BF16)
    p3 = (r1 - p2.astype(F32)).astype(BF16)
    p_scr[...] = jnp.concatenate([p1, p2, p3], axis=0)
    n = HG_BOUNDED_BLOCKS * HG_CHUNK
    dec_scr[0:n, :] = _bdot(tab_ref[0:n, :], p_scr[...])


def _hgrn_more_decays(tab_ref, p_scr, dec_scr):
    n = HG_BOUNDED_BLOCKS * HG_CHUNK
    dec_scr[n:, :] = _bdot(tab_ref[n:, :], p_scr[...])


def _hgrn_direction(q_ref, v_ref, row0, k_scr, dec_scr, mask_ref, st_scr, o_ref, end_row,
                    bounded):
    c = HG_CHUNK
    heads = lambda ref, r0: jnp.stack(
        [ref[r0:r0 + c, h * HG_DK:(h + 1) * HG_DK] for h in range(HG_HEADS)], axis=0)
    bmm = lambda eq, a, b: jnp.einsum(eq, a, b, preferred_element_type=F32)
    q = heads(q_ref, row0)
    k = heads(k_scr, 0)
    v = heads(v_ref, row0)
    b = heads(dec_scr, 0)
    st = st_scr[...]
    o = bmm('htk,hkv->htv', q * jnp.exp2(b).astype(BF16), st.astype(BF16))
    level = lambda l: jnp.exp2(-jnp.abs(heads(dec_scr, _hgrn_level_block(l) * c))).astype(BF16)
    if bounded:
        e = level(0)
        att = jnp.where(mask_ref[0] > 0.5, bmm('htk,hsk->hts', q * e, k * e), 0.0)
        d = heads(dec_scr, HG_MID_BLOCK * c)
        inner = bmm('htk,hsk->hts', q * jnp.exp2(d).astype(BF16), k * jnp.exp2(-d).astype(BF16))
        att += jnp.where(mask_ref[HG_LEVELS] > 0.5, inner, 0.0)
    else:
        diag = jnp.sum(q.astype(F32) * k.astype(F32), axis=-1, keepdims=True)
        o += diag * v.astype(F32)
        att = jnp.zeros((HG_HEADS, c, c), F32)
        for lvl in range(HG_LEVELS):
            e = level(lvl)
            att += bmm('htk,hsk->hts', q * e, k * e) * mask_ref[lvl]
    o += bmm('hts,hsv->htv', att.astype(BF16), v)
    for h in range(HG_HEADS):
        o_ref[row0:row0 + c, h * HG_DK:(h + 1) * HG_DK] = o[h]
    b_end = b[:, end_row:end_row + 1, :]
    upd = bmm('htk,htv->hkv', k * jnp.exp2(b_end - b).astype(BF16), v)
    decay = jnp.swapaxes(jnp.broadcast_to(jnp.exp2(b_end), (HG_HEADS, HG_DK, HG_DK)), 1, 2)
    st_scr[...] = decay * st + upd


HG_MAX_LOG2_SPAN = 100.0
HG_MAX_ABS_Q = 1e6


def _hgrn_kernel(qf_ref, vf_ref, zff_ref, qb_ref, vb_ref, zfb_ref, lbf_ref, lbb_ref,
                 tabf_ref, maskf_ref, tabb_ref, maskb_ref, s0f_ref, s0b_ref,
                 of_ref, ob_ref, sf_ref, sb_ref, stf_scr, stb_scr, kf_scr, kb_scr,
                 pf_scr, pb_scr, decf_scr, decb_scr):
    ci = pl.program_id(1)
    c = HG_CHUNK

    @pl.when(ci == 0)
    def _():
        stf_scr[...] = s0f_ref[0]
        stb_scr[...] = s0b_ref[0]

    q_abs = jnp.maximum(jnp.max(jnp.abs(qf_ref[...].astype(F32))),
                        jnp.max(jnp.abs(qb_ref[...].astype(F32))))
    mid = slice(HG_MID_BLOCK * c, (HG_MID_BLOCK + 1) * c)
    n_sub = qf_ref.shape[0] // c
    for sub in range(n_sub):
        rf = sub * c
        rb = (n_sub - 1 - sub) * c
        _hgrn_decays(zff_ref, rf, lbf_ref, tabf_ref, kf_scr, pf_scr, decf_scr)
        _hgrn_decays(zfb_ref, rb, lbb_ref, tabb_ref, kb_scr, pb_scr, decb_scr)
        span = jnp.maximum(jnp.max(jnp.abs(decf_scr[mid, :])), jnp.max(jnp.abs(decb_scr[mid, :])))
        bounded = (span < HG_MAX_LOG2_SPAN) & (q_abs < HG_MAX_ABS_Q)

        def run(flag, rf=rf, rb=rb):
            _hgrn_direction(qf_ref, vf_ref, rf, kf_scr, decf_scr, maskf_ref, stf_scr, of_ref,
                            c - 1, flag)
            _hgrn_direction(qb_ref, vb_ref, rb, kb_scr, decb_scr, maskb_ref, stb_scr, ob_ref,
                            0, flag)

        @pl.when(bounded)
        def _():
            run(True)

        @pl.when(jnp.logical_not(bounded))
        def _():
            _hgrn_more_decays(tabf_ref, pf_scr, decf_scr)
            _hgrn_more_decays(tabb_ref, pb_scr, decb_scr)
            run(False)

    @pl.when(ci == pl.num_programs(1) - 1)
    def _():
        sf_ref[0] = stf_scr[...]
        sb_ref[0] = stb_scr[...]


def _hgrn(zm, zf, lb_f, lb_b, s0_f, s0_b, *, seq_len, n_seq, row0, out_rows,
          prev_of=None, prev_ob=None):
    c = HG_CHUNK
    blk = c * HG_STEP_CHUNKS
    nc = seq_len // blk
    rb0 = row0 // blk
    w = HG_W
    tab_f, mask_f = _hgrn_tables(False)
    tab_b, mask_b = _hgrn_tables(True)
    fwd = lambda col: (lambda b, i: (rb0 + b * nc + i, col))
    bwd = lambda col: (lambda b, i: (rb0 + b * nc + nc - 1 - i, col))
    const2 = lambda b, i: (0, 0)
    const3 = lambda b, i: (0, 0, 0)
    state = lambda b, i: (b, 0, 0, 0)
    nt = (2 + HG_LEVELS) * c
    in_specs = [
        pl.BlockSpec((blk, w), fwd(ZM_Q // w)),
        pl.BlockSpec((blk, w), fwd(ZM_I // w)),
        pl.BlockSpec((blk, w), fwd(0)),
        pl.BlockSpec((blk, w), bwd(ZM_Q // w)),
        pl.BlockSpec((blk, w), bwd(ZM_I // w)),
        pl.BlockSpec((blk, w), bwd(1)),
        pl.BlockSpec((1, w), const2),
        pl.BlockSpec((1, w), const2),
        pl.BlockSpec((nt, 3 * c), const2),
        pl.BlockSpec((HG_LEVELS + 1, c, c), const3),
        pl.BlockSpec((nt, 3 * c), const2),
        pl.BlockSpec((HG_LEVELS + 1, c, c), const3),
        pl.BlockSpec((1, HG_HEADS, HG_DK, HG_DK), state),
        pl.BlockSpec((1, HG_HEADS, HG_DK, HG_DK), state),
    ]
    args = [zm, zm, zf, zm, zm, zf, lb_f.reshape(1, w), lb_b.reshape(1, w),
            jnp.asarray(np.tile(tab_f, (1, 3)), BF16), jnp.asarray(mask_f),
            jnp.asarray(np.tile(tab_b, (1, 3)), BF16), jnp.asarray(mask_b), s0_f, s0_b]
    kernel = _hgrn_kernel
    aliases = {}
    if prev_of is not None:
        in_specs += [pl.BlockSpec(memory_space=pl.ANY), pl.BlockSpec(memory_space=pl.ANY)]
        args += [prev_of, prev_ob]
        aliases = {14: 0, 15: 1}
        kernel = functools.partial(_hgrn_alias_kernel, 14)
    st_shape = jax.ShapeDtypeStruct((n_seq, HG_HEADS, HG_DK, HG_DK), F32)
    return pl.pallas_call(
        kernel,
        grid=(n_seq, nc),
        in_specs=in_specs,
        out_specs=[
            pl.BlockSpec((blk, w), fwd(0)),
            pl.BlockSpec((blk, w), bwd(0)),
            pl.BlockSpec((1, HG_HEADS, HG_DK, HG_DK), state),
            pl.BlockSpec((1, HG_HEADS, HG_DK, HG_DK), state),
        ],
        out_shape=[jax.ShapeDtypeStruct((out_rows, w), F32),
                   jax.ShapeDtypeStruct((out_rows, w), F32), st_shape, st_shape],
        scratch_shapes=[pltpu.VMEM((HG_HEADS, HG_DK, HG_DK), F32),
                        pltpu.VMEM((HG_HEADS, HG_DK, HG_DK), F32),
                        pltpu.VMEM((c, w), BF16), pltpu.VMEM((c, w), BF16),
                        pltpu.VMEM((3 * c, w), BF16), pltpu.VMEM((3 * c, w), BF16),
                        pltpu.VMEM((nt, w), F32), pltpu.VMEM((nt, w), F32)],
        input_output_aliases=aliases,
        compiler_params=_cparams(2),
        name="hgrn2",
    )(*args)


def _hgrn_alias_kernel(n_in, *refs):
    return _hgrn_kernel(*refs[:n_in], *refs[n_in + 2:])


def kernel(x, c, ctx, c_ctx, w_mod, b_mod, norm1_w, norm2_w, w_in, hy_conv_w, hy_conv_b,
           hy_f_w1, hy_f_b1, hy_f_w2, hy_f_b2, hy_f_w3, hy_f_freq, hy_decay, hy_skip,
           hg_lower, hg_norm_w, w_br_hy, w_br_fn, w_br_hg, w_out,
           ffn_w1, ffn_w3, ffn_w2, moe_router, moe_w1, moe_w3, moe_w2, final_norm_w):
    batch, seq, d = x.shape
    l_ctx = ctx.shape[1]
    depth = w_in.shape[0]
    n_lat = batch * seq
    n_all = n_lat + batch * l_ctx

    lb_all = jnp.cumsum(jax.nn.softmax(hg_lower.astype(F32), axis=1), axis=1)
    lb_all = lb_all - lb_all[:, :1]

    cc = jnp.zeros((8, d), F32).at[:batch].set(c).at[batch].set(c_ctx)
    mods = _modulation(cc, w_mod, b_mod).reshape(depth, 8, 6, d)

    w1p = jnp.zeros((depth, LANE, HY_FILTER_HIDDEN), F32).at[:, :HY_POS_DIM].set(hy_f_w1)
    dft_lat = _hyena_dft(seq)
    dft_ctx = _hyena_dft(l_ctx)
    kf_lat = _hyena_filters(seq, w1p, hy_f_b1, hy_f_w2, hy_f_b2, hy_f_w3, hy_f_freq, hy_decay,
                            dft_lat[0], dft_lat[1])
    kf_ctx = _hyena_filters(l_ctx, w1p, hy_f_b1, hy_f_w2, hy_f_b2, hy_f_w3, hy_f_freq, hy_decay,
                            dft_ctx[0], dft_ctx[1])
    fn_lat = _fnet_dft(seq)
    fn_ctx = _fnet_dft(l_ctx)

    rows_x = jnp.concatenate([x.reshape(n_lat, d), ctx.reshape(batch * l_ctx, d)], axis=0)
    zero_state = jnp.zeros((batch, HG_HEADS, HG_DK, HG_DK), F32)
    tile_cap = math.gcd(seq, batch * l_ctx)
    tok = dict(seq=seq, batch=batch)
    tile = lambda t: dict(tm=min(t, tile_cap))
    lat = dict(seq_len=seq, n_seq=batch, row0=0, out_rows=n_all)
    cx = dict(seq_len=l_ctx, n_seq=batch, row0=n_lat, out_rows=n_all)

    for l in range(depth):
        last = l == depth - 1
        rows = n_lat if last else n_all
        m = mods[l]

        h = _norm_rows(rows_x, norm1_w[l], m, **tok, **tile(512))
        zm, zf = _in_proj(h, w_in, l, **tile(1024))

        of, ob, s_f, s_b = _hgrn(zm, zf, lb_all[0, l], lb_all[1, l], zero_state, zero_state, **cx)
        of, ob, _, _ = _hgrn(zm, zf, lb_all[0, l], lb_all[1, l], s_f, s_b, prev_of=of,
                             prev_ob=ob, **lat)

        hy_args = (hy_conv_w[l], hy_conv_b[l])
        mix = dict(lat, out_rows=rows)
        u = _hyena_longconv(zm, 0, zm, 1, *hy_args, kf_lat, l, 0, hy_skip[l], dft_lat, **mix)
        y_hy = _hyena_longconv(u, None, zm, 2, *hy_args, kf_lat, l, 1, hy_skip[l], dft_lat, **mix)
        y_fn = _fnet(zm, fn_lat, **mix)
        if not last:
            u = _hyena_longconv(zm, 0, zm, 1, *hy_args, kf_ctx, l, 0, hy_skip[l], dft_ctx,
                                prev_out=u, **cx)
            y_hy = _hyena_longconv(u, None, zm, 2, *hy_args, kf_ctx, l, 1, hy_skip[l], dft_ctx,
                                   prev_out=y_hy, **cx)
            y_fn = _fnet(zm, fn_ctx, prev_out=y_fn, **cx)

        rows_x = _merge_out(y_hy, y_fn, of, ob, zm, rows_x, hg_norm_w[l], m,
                            w_br_hy[l].astype(BF16), w_br_fn[l].astype(BF16),
                            w_br_hg[l].astype(BF16), w_out[l].astype(BF16), rows=rows,
                            **tok, **tile(256))

        if l % 2 == 0:
            i = l // 2
            rows_x = _ffn_dense(rows_x, norm2_w[l], m, ffn_w1, ffn_w3, ffn_w2, i, rows=rows,
                                **tok, **tile(1024))
        else:
            i = l // 2
            rows_x = _moe_top2(rows_x, norm2_w[l], m, moe_router[i], moe_w1, moe_w3, moe_w2, i,
                               rows=rows, **tok, **tile(512))

    out = _final_norm(rows_x, final_norm_w, rows=n_lat, **tile(512))
    return out.reshape(batch, seq, d)
```

```python
import functools
import math

import numpy as np
import jax
import jax.numpy as jnp
from jax import lax
from jax.experimental import pallas as pl
from jax.experimental.pallas import tpu as pltpu

F32 = jnp.float32
BF16 = jnp.bfloat16
NORM_EPS = 1e-6

D_MODEL = 2048
DEPTH = 4
HY_W = 1024
HY_BANDS = 16
HY_POS_DIM = 2 * HY_BANDS + 1
HY_FILTER_HIDDEN = 64
FN_W = 1024
FN_GROUP_W = 128
HG_W = 1024
HG_HEADS = 8
HG_DK = 128
HG_CHUNK = 128
HG_LEVELS = 7
HG_STEP_CHUNKS = 1
N_EXPERTS = 8
LANE = 128

HG_FF = 3 * HY_W + FN_W + 2 * HG_W
HG_I = HG_FF + 2 * HG_W
ZM_HY = 0
ZM_FN = ZM_HY + 3 * HY_W
ZM_Q = ZM_FN + FN_W
ZM_G = ZM_Q + HG_W
ZM_I = HG_I
ZM_GATE = HG_I + HG_W
ZM_COLS = ZM_GATE + 3 * D_MODEL

VMEM_LIMIT = 56 * 1024 * 1024


def _cparams(n_axes, vmem=VMEM_LIMIT):
    return pltpu.CompilerParams(dimension_semantics=("arbitrary",) * n_axes, vmem_limit_bytes=vmem)


def _resident(shape, index_map):
    return pl.BlockSpec(shape, index_map, pipeline_mode=pl.Buffered(1))


def _silu(v):
    return v * jax.nn.sigmoid(v)


def _bdot(a, b):
    return jnp.dot(a, b, preferred_element_type=F32)


def _mod_kernel(c_ref, w_ref, b_ref, o_ref):
    c = _silu(c_ref[...])
    o_ref[0] = jnp.dot(c, w_ref[0], preferred_element_type=F32,
                       precision=lax.Precision.HIGHEST) + b_ref[0]


def _modulation(cc, w_mod, b_mod):
    depth, d, n6 = w_mod.shape
    tn = 2048
    return pl.pallas_call(
        _mod_kernel,
        grid=(depth, n6 // tn),
        in_specs=[
            pl.BlockSpec((8, d), lambda l, n: (0, 0)),
            pl.BlockSpec((1, d, tn), lambda l, n: (l, 0, n)),
            pl.BlockSpec((1, 1, tn), lambda l, n: (l, 0, n)),
        ],
        out_specs=pl.BlockSpec((1, 8, tn), lambda l, n: (l, 0, n)),
        out_shape=jax.ShapeDtypeStruct((depth, 8, n6), F32),
        compiler_params=_cparams(2),
        name="modulation",
    )(cc, w_mod, b_mod.reshape(depth, 1, n6))


def _norm_mod(x, nw, shift, scale):
    ms = jnp.mean(x * x, axis=-1, keepdims=True)
    return (x * lax.rsqrt(ms + NORM_EPS) * nw) * (1 + scale) + shift


def _group_of(row_block, tm, seq, batch):
    return jnp.minimum((row_block * tm) // seq, batch)


def _norm_rows_kernel(x_ref, nw_ref, mod_ref, o_ref):
    h = _norm_mod(x_ref[...], nw_ref[...], mod_ref[0, 0:1, :], mod_ref[0, 1:2, :])
    o_ref[...] = h.astype(o_ref.dtype)


def _norm_rows(x, nw, mods, *, seq, batch, tm=512):
    rows, d = x.shape
    return pl.pallas_call(
        _norm_rows_kernel,
        grid=(rows // tm,),
        in_specs=[
            pl.BlockSpec((tm, d), lambda m: (m, 0)),
            pl.BlockSpec((1, d), lambda m: (0, 0)),
            pl.BlockSpec((1, 6, d), lambda m: (_group_of(m, tm, seq, batch), 0, 0)),
        ],
        out_specs=pl.BlockSpec((tm, d), lambda m: (m, 0)),
        out_shape=jax.ShapeDtypeStruct((rows, d), BF16),
        compiler_params=_cparams(1),
        name="norm_rows",
    )(x, nw.reshape(1, d), mods)


def _inproj_kernel(h_ref, w_ref, z_ref, zf_ref, w_scr, *, n_main):
    @pl.when(pl.program_id(1) == 0)
    def _():
        w_scr[...] = w_ref[0].astype(BF16)

    r = _bdot(h_ref[...], w_scr[...])

    @pl.when(pl.program_id(0) < n_main)
    def _():
        z_ref[...] = r.astype(z_ref.dtype)

    @pl.when(pl.program_id(0) >= n_main)
    def _():
        zf_ref[...] = r


def _in_proj(h, w_in, layer, *, tm=1024, tn=1024):
    rows, d = h.shape
    n_cols = w_in.shape[2]
    nb = n_cols // tn
    f0 = HG_FF // tn
    nf = 2 * HG_W // tn
    n_main = nb - nf
    nm = rows // tm
    col = lambda j: jnp.where(j < f0, j, jnp.where(j < n_main, j + nf, j - n_main + f0))
    z_idx = lambda j, m: (jnp.where(j < n_main, m, nm - 1), col(jnp.minimum(j, n_main - 1)))
    zf_idx = lambda j, m: (jnp.where(j < n_main, 0, m), jnp.maximum(j - n_main, 0))
    return pl.pallas_call(
        functools.partial(_inproj_kernel, n_main=n_main),
        grid=(nb, nm),
        in_specs=[
            pl.BlockSpec((tm, d), lambda j, m: (m, 0)),
            pl.BlockSpec((1, d, tn), lambda j, m: (layer, 0, col(j))),
        ],
        out_specs=[pl.BlockSpec((tm, tn), z_idx), pl.BlockSpec((tm, tn), zf_idx)],
        out_shape=[jax.ShapeDtypeStruct((rows, n_cols), BF16),
                   jax.ShapeDtypeStruct((rows, nf * tn), F32)],
        scratch_shapes=[pltpu.VMEM((d, tn), BF16)],
        compiler_params=_cparams(2),
        name="in_proj",
    )(h, w_in)


def _merge_kernel(yhy_ref, yfn_ref, of_ref, ob_ref, g_ref, ga0_ref, ga1_ref, gb0_ref, gb1_ref,
                  gc0_ref, gc1_ref, x_ref, hnw_ref, mod_ref,
                  wbhy_ref, wbfn_ref, wbhg_ref, wo_ref, o_ref):
    d = x_ref.shape[1]
    ohg = of_ref[...] + ob_ref[...]
    nw = hnw_ref[...]
    heads = []
    for h in range(ohg.shape[1] // HG_DK):
        oh = ohg[:, h * HG_DK:(h + 1) * HG_DK]
        ms = jnp.mean(oh * oh, axis=-1, keepdims=True)
        heads.append(oh * lax.rsqrt(ms + NORM_EPS) * nw)
    y_hg = jnp.concatenate(heads, axis=-1) * _silu(g_ref[...].astype(F32))
    gate = lambda lo, hi: jax.nn.sigmoid(
        jnp.concatenate([lo[...], hi[...]], axis=-1).astype(F32))
    merged = gate(ga0_ref, ga1_ref) * _bdot(yhy_ref[...], wbhy_ref[...])
    merged += gate(gb0_ref, gb1_ref) * _bdot(yfn_ref[...], wbfn_ref[...])
    merged += gate(gc0_ref, gc1_ref) * _bdot(y_hg.astype(BF16), wbhg_ref[...])
    out = _bdot(merged.astype(BF16), wo_ref[...])
    o_ref[...] = x_ref[...] + mod_ref[0, 2:3, :] * out


def _merge_out(y_hy, y_fn, o_f, o_b, zm, x, hg_nw, mods, wb_hy, wb_fn, wb_hg, w_o,
               *, rows, seq, batch, tm=256):
    d = x.shape[1]
    w = y_hy.shape[1]
    tm = min(tm, rows)
    row = lambda m: (m, 0)
    const = lambda m: (0, 0)
    gate0 = ZM_GATE // (d // 2)
    return pl.pallas_call(
        _merge_kernel,
        grid=(rows // tm,),
        in_specs=[
            pl.BlockSpec((tm, w), row),
            pl.BlockSpec((tm, w), row),
            pl.BlockSpec((tm, w), row),
            pl.BlockSpec((tm, w), row),
            pl.BlockSpec((tm, w), lambda m: (m, ZM_G // w)),
            *[pl.BlockSpec((tm, d // 2), functools.partial(lambda i, m: (m, i), gate0 + i))
              for i in range(6)],
            pl.BlockSpec((tm, d), row),
            pl.BlockSpec((1, HG_DK), const),
            pl.BlockSpec((1, 6, d), lambda m: (_group_of(m, tm, seq, batch), 0, 0)),
            _resident((w, d), const),
            _resident((w, d), const),
            _resident((w, d), const),
            _resident((d, d), const),
        ],
        out_specs=pl.BlockSpec((tm, d), row),
        out_shape=jax.ShapeDtypeStruct((rows, d), F32),
        compiler_params=_cparams(1),
        name="merge_out",
    )(y_hy, y_fn, o_f, o_b, zm, *([zm] * 6), x, hg_nw.reshape(1, HG_DK), mods,
      wb_hy, wb_fn, wb_hg, w_o)


def _ffn_kernel(x_ref, nw_ref, mod_ref, w1_ref, w3_ref, w2_ref, o_ref, h_scr):
    f = pl.program_id(1)

    @pl.when(f == 0)
    def _():
        h = _norm_mod(x_ref[...], nw_ref[...], mod_ref[0, 3:4, :], mod_ref[0, 4:5, :])
        h_scr[...] = h.astype(BF16)
        o_ref[...] = jnp.zeros_like(o_ref)

    h = h_scr[...]
    act = _silu(_bdot(h, w1_ref[0].astype(BF16))) * _bdot(h, w3_ref[0].astype(BF16))
    o_ref[...] += _bdot(act.astype(BF16), w2_ref[0].astype(BF16))

    @pl.when(f == pl.num_programs(1) - 1)
    def _():
        o_ref[...] = x_ref[...] + mod_ref[0, 5:6, :] * o_ref[...]


def _ffn_dense(x, nw, mods, w1, w3, w2, layer, *, rows, seq, batch, tm=1024, tf=256):
    d = x.shape[1]
    ff = w1.shape[2]
    tm = min(tm, rows)
    return pl.pallas_call(
        _ffn_kernel,
        grid=(rows // tm, ff // tf),
        in_specs=[
            _resident((tm, d), lambda m, f: (m, 0)),
            pl.BlockSpec((1, d), lambda m, f: (0, 0)),
            pl.BlockSpec((1, 6, d), lambda m, f: (_group_of(m, tm, seq, batch), 0, 0)),
            pl.BlockSpec((1, d, tf), lambda m, f: (layer, 0, f)),
            pl.BlockSpec((1, d, tf), lambda m, f: (layer, 0, f)),
            pl.BlockSpec((1, tf, d), lambda m, f: (layer, f, 0)),
        ],
        out_specs=pl.BlockSpec((tm, d), lambda m, f: (m, 0)),
        out_shape=jax.ShapeDtypeStruct((rows, d), F32),
        scratch_shapes=[pltpu.VMEM((tm, d), BF16)],
        compiler_params=_cparams(2),
        name="ffn_dense",
    )(x, nw.reshape(1, d), mods, w1, w3, w2)


def _top2(logits, n_experts):
    lane = lax.broadcasted_iota(jnp.int32, logits.shape, 1).astype(F32)
    neg = jnp.float32(-jnp.inf)
    lg = jnp.where(lane < n_experts, logits, neg)
    m1 = jnp.max(lg, axis=-1, keepdims=True)
    i1 = jnp.min(jnp.where(lg == m1, lane, float(LANE)), axis=-1, keepdims=True)
    lg2 = jnp.where(lane == i1, neg, lg)
    m2 = jnp.max(lg2, axis=-1, keepdims=True)
    i2 = jnp.min(jnp.where(lg2 == m2, lane, float(LANE)), axis=-1, keepdims=True)
    e2 = jnp.exp(m2 - m1)
    return i1, i2, 1.0 / (1.0 + e2), e2 / (1.0 + e2)


def _route_kernel(x_ref, nw_ref, mod_ref, r_ref, h_ref, route_ref):
    h = _norm_mod(x_ref[...], nw_ref[...], mod_ref[0, 3:4, :], mod_ref[0, 4:5, :])
    logits = jnp.dot(h, r_ref[...], preferred_element_type=F32, precision=lax.Precision.HIGHEST)
    i1, i2, p1, p2 = _top2(logits, N_EXPERTS)
    lane = lax.broadcasted_iota(jnp.int32, logits.shape, 1)
    route = jnp.where(lane == 0, i1, jnp.where(lane == 1, i2,
                      jnp.where(lane == 2, p1, jnp.where(lane == 3, p2, 0.0))))
    h_ref[...] = h
    route_ref[...] = route


def _moe_route(x, nw, mods, router, *, rows, seq, batch, tm=512):
    d = x.shape[1]
    rpad = jnp.zeros((d, LANE), F32).at[:, :router.shape[1]].set(router)
    return pl.pallas_call(
        _route_kernel,
        grid=(rows // tm,),
        in_specs=[
            pl.BlockSpec((tm, d), lambda m: (m, 0)),
            pl.BlockSpec((1, d), lambda m: (0, 0)),
            pl.BlockSpec((1, 6, d), lambda m: (_group_of(m, tm, seq, batch), 0, 0)),
            pl.BlockSpec((d, LANE), lambda m: (0, 0)),
        ],
        out_specs=[pl.BlockSpec((tm, d), lambda m: (m, 0)),
                   pl.BlockSpec((tm, LANE), lambda m: (m, 0))],
        out_shape=[jax.ShapeDtypeStruct((rows, d), F32), jax.ShapeDtypeStruct((rows, LANE), F32)],
        compiler_params=_cparams(1),
        name="moe_route",
    )(x, nw.reshape(1, d), mods, rpad)


def _moe_plan(route, n_experts, tile):
    rows = route.shape[0]
    e = route[:, :2].astype(jnp.int32).reshape(-1)
    onehot = (e[:, None] == jnp.arange(n_experts, dtype=jnp.int32)[None, :]).astype(jnp.int32)
    before = jnp.cumsum(onehot, axis=0) - onehot
    rank = jnp.sum(before * onehot, axis=1)
    counts = jnp.sum(onehot, axis=0)
    padded = ((counts + tile - 1) // tile) * tile
    ends = jnp.cumsum(padded)
    starts = ends - padded
    pos = jnp.sum(starts[None, :] * onehot, axis=1) + rank
    n_tiles = -(-2 * rows // tile) + n_experts
    src = jnp.zeros((n_tiles * tile,), jnp.int32).at[pos].set(
        jnp.arange(2 * rows, dtype=jnp.int32) // 2)
    tile_start = jnp.arange(n_tiles, dtype=jnp.int32) * tile
    tile_expert = jnp.sum((tile_start[:, None] >= ends[None, :]).astype(jnp.int32), axis=1)
    tile_expert = jnp.minimum(tile_expert, n_experts - 1)
    n_used = (ends[-1] // tile).reshape(1)
    return pos, src, tile_expert, n_used


def _row_copy(src_hbm, src_row, dst, dst_row, sem):
    return pltpu.make_async_copy(src_hbm.at[pl.ds(src_row, 1)], dst.at[pl.ds(dst_row, 1)], sem)


def _gather_rows(idx_ref, base, stride, src_hbm, dst, sem, *, wait, split=False):
    if wait:
        pltpu.make_async_copy(src_hbm.at[pl.ds(0, dst.shape[0])], dst, sem).wait()
        return

    def body(i, carry):
        for j in range(2):
            r = 2 * i + j
            _row_copy(src_hbm, idx_ref[base + stride * r], dst, r, sem).start(
                priority=j if split else 0)
        return carry

    lax.fori_loop(0, dst.shape[0] // 2, body, 0, unroll=4)


def _moe_ffn_kernel(te_ref, nu_ref, src_ref, h_hbm, w1_ref, w3_ref, w2_ref, o_ref,
                    rows_scr, h_scr, sems):
    t = pl.program_id(0)
    f = pl.program_id(1)
    last = pl.num_programs(1) - 1
    tile = o_ref.shape[0]
    n_used = nu_ref[0]
    used = t < n_used
    slot = t % 2

    def gather(tile_idx, buf, wait):
        _gather_rows(src_ref, tile_idx * tile, 1, h_hbm, rows_scr.at[buf], sems.at[buf], wait=wait)

    @pl.when(used & (f == 0) & (t == 0))
    def _():
        gather(t, slot, False)

    @pl.when(used & (f == 0))
    def _():
        gather(t, slot, True)
        h_scr[...] = rows_scr[slot].astype(BF16)
        o_ref[...] = jnp.zeros_like(o_ref)

    @pl.when((f == 0) & (t + 1 < n_used))
    def _():
        gather(t + 1, 1 - slot, False)

    @pl.when(used)
    def _():
        h = h_scr[...]
        act = _silu(_bdot(h, w1_ref[0, 0].astype(BF16))) * _bdot(h, w3_ref[0, 0].astype(BF16))
        o_ref[...] += _bdot(act.astype(BF16), w2_ref[0, 0].astype(BF16))

    @pl.when(jnp.logical_not(used) & (f == last))
    def _():
        o_ref[...] = jnp.zeros_like(o_ref)


def _moe_ffn(h, src, tile_expert, n_used, w1, w3, w2, layer, *, tile, tf=256):
    d = h.shape[1]
    n = src.shape[0]
    ff = w1.shape[3]
    nf = ff // tf
    fsel = lambda t, f, nu: jnp.where(t < nu[0], f, nf - 1)
    up = lambda t, f, te, nu, sr: (layer, te[t], 0, fsel(t, f, nu))
    down = lambda t, f, te, nu, sr: (layer, te[t], fsel(t, f, nu), 0)
    return pl.pallas_call(
        _moe_ffn_kernel,
        grid_spec=pltpu.PrefetchScalarGridSpec(
            num_scalar_prefetch=3,
            grid=(n // tile, nf),
            in_specs=[
                pl.BlockSpec(memory_space=pl.ANY),
                pl.BlockSpec((1, 1, d, tf), up),
                pl.BlockSpec((1, 1, d, tf), up),
                pl.BlockSpec((1, 1, tf, d), down),
            ],
            out_specs=pl.BlockSpec((tile, d), lambda t, f, te, nu, sr: (t, 0)),
            scratch_shapes=[pltpu.VMEM((2, tile, d), F32), pltpu.VMEM((tile, d), BF16),
                            pltpu.SemaphoreType.DMA((2,))],
        ),
        out_shape=jax.ShapeDtypeStruct((n, d), F32),
        compiler_params=_cparams(2),
        name="moe_ffn",
    )(tile_expert, n_used, src, h, w1, w3, w2)


def _combine_kernel(pos_ref, ys_hbm, x_ref, route_ref, mod_ref, o_ref, y1_scr, y2_scr, sem):
    base = 2 * pl.program_id(0) * x_ref.shape[0]
    _gather_rows(pos_ref, base, 2, ys_hbm, y1_scr, sem, wait=False, split=True)
    _gather_rows(pos_ref, base + 1, 2, ys_hbm, y2_scr, sem, wait=False, split=True)
    _gather_rows(pos_ref, base, 2, ys_hbm, y1_scr, sem, wait=True)
    _gather_rows(pos_ref, base + 1, 2, ys_hbm, y2_scr, sem, wait=True)
    mix = route_ref[:, 2:3] * y1_scr[...] + route_ref[:, 3:4] * y2_scr[...]
    o_ref[...] = x_ref[...] + mod_ref[0, 5:6, :] * mix


def _moe_combine(x, mods, route, ys, pos, *, rows, seq, batch, tm=256):
    d = x.shape[1]
    return pl.pallas_call(
        _combine_kernel,
        grid_spec=pltpu.PrefetchScalarGridSpec(
            num_scalar_prefetch=1,
            grid=(rows // tm,),
            in_specs=[
                pl.BlockSpec(memory_space=pl.ANY),
                pl.BlockSpec((tm, d), lambda m, pos: (m, 0)),
                pl.BlockSpec((tm, LANE), lambda m, pos: (m, 0)),
                pl.BlockSpec((1, 6, d), lambda m, pos: (_group_of(m, tm, seq, batch), 0, 0)),
            ],
            out_specs=pl.BlockSpec((tm, d), lambda m, pos: (m, 0)),
            scratch_shapes=[pltpu.VMEM((tm, d), F32), pltpu.VMEM((tm, d), F32),
                            pltpu.SemaphoreType.DMA(())],
        ),
        out_shape=jax.ShapeDtypeStruct((rows, d), F32),
        compiler_params=_cparams(1),
        name="moe_combine",
    )(pos, ys, x, route, mods)


def _moe_top2(x, nw, mods, router, w1, w3, w2, layer, *, rows, seq, batch, tm, tile=768):
    tok = dict(rows=rows, seq=seq, batch=batch)
    h, route = _moe_route(x, nw, mods, router, tm=tm, **tok)
    pos, src, tile_expert, n_used = _moe_plan(route, w1.shape[1], tile)
    ys = _moe_ffn(h, src, tile_expert, n_used, w1, w3, w2, layer, tile=tile)
    return _moe_combine(x, mods, route, ys, pos, tm=min(tm, 256), **tok)


def _final_norm_kernel(x_ref, w_ref, o_ref):
    x = x_ref[...]
    ms = jnp.mean(x * x, axis=-1, keepdims=True)
    o_ref[...] = x * lax.rsqrt(ms + NORM_EPS) * w_ref[...]


def _final_norm(x, w, *, rows, tm=512):
    d = x.shape[1]
    return pl.pallas_call(
        _final_norm_kernel,
        grid=(rows // tm,),
        in_specs=[pl.BlockSpec((tm, d), lambda m: (m, 0)), pl.BlockSpec((1, d), lambda m: (0, 0))],
        out_specs=pl.BlockSpec((tm, d), lambda m: (m, 0)),
        out_shape=jax.ShapeDtypeStruct((rows, d), F32),
        compiler_params=_cparams(1),
        name="final_norm",
    )(x, w.reshape(1, d))


def _dft_angles(rows, cols, period):
    k = np.asarray(rows, np.int64)[:, None]
    s = np.asarray(cols, np.int64)[None, :]
    return ((k * s) % period).astype(np.float64) * (2.0 * math.pi / period)


def _bf16_const(a):
    return jnp.asarray(np.asarray(a, np.float32).astype(jnp.bfloat16))


def _hyena_dft(seq_len):
    n = 2 * seq_len
    half = seq_len // 2
    pos = np.arange(seq_len)
    freq = np.concatenate([2 * np.arange(half), 2 * np.arange(half) + 1])
    ang = _dft_angles(freq, pos, n)
    alt = np.where(pos % 2 == 0, 1.0, -1.0)
    fc = np.cos(ang)
    fs = np.sin(ang)
    ic = 2.0 / n * fc.T
    isn = 2.0 / n * fs.T
    fs[0, :] = alt
    ic[:, 0] = 1.0 / n
    isn[:, 0] = alt / n
    return (_bf16_const(fc), _bf16_const(fs), _bf16_const(ic), _bf16_const(isn),
            _flip_matrix(half))


def _hyena_feats(seq_len):
    pos = np.arange(seq_len, dtype=np.float64)
    t = pos / max(seq_len - 1, 1)
    bands = np.linspace(1e-4, HY_BANDS - 1, HY_BANDS).astype(np.float32).astype(np.float64)
    ang = 2 * math.pi * pos[:, None] * bands[None, :] / seq_len
    feats = np.concatenate([t[:, None], np.cos(ang), -np.sin(ang)], axis=-1)
    out = np.zeros((seq_len, LANE), np.float32)
    out[:, :HY_POS_DIM] = feats
    return jnp.asarray(out)


def _split2(a):
    hi = a.astype(BF16)
    return hi, (a - hi.astype(F32)).astype(BF16)


def _dot3(a, b):
    a_hi, a_lo = _split2(a)
    b_hi, b_lo = _split2(b)
    return _bdot(a_hi, b_hi) + _bdot(a_hi, b_lo) + _bdot(a_lo, b_hi)


def _filter_kernel(feat_ref, w1_ref, b1_ref, w2_ref, b2_ref, fr_ref, w3f_ref, w3b_ref,
                   dcf_ref, dcb_ref, fc_ref, fs_ref, o_ref, h_scr):
    @pl.when((pl.program_id(1) == 0) & (pl.program_id(2) == 0))
    def _():
        hp = lax.Precision.HIGHEST
        h1 = jnp.sin(fr_ref[0, 0:1, :] * (jnp.dot(feat_ref[...], w1_ref[0], precision=hp,
                                                  preferred_element_type=F32) + b1_ref[0]))
        h_scr[...] = jnp.sin(fr_ref[0, 1:2, :] * (jnp.dot(h1, w2_ref[0], precision=hp,
                                                          preferred_element_type=F32) + b2_ref[0]))

    h = h_scr[...]
    t = feat_ref[:, 0:1]
    hf = _dot3(h, w3f_ref[0]) * jnp.exp(-t * dcf_ref[0])
    hb = _dot3(h, w3b_ref[0]) * jnp.exp(-t * dcb_ref[0])
    row = lax.broadcasted_iota(jnp.int32, hf.shape, 0)
    hb = jnp.where(row == 0, 0.0, hb)
    ss = jnp.sum(hf * hf + hb * hb, axis=0, keepdims=True)
    scale = lax.rsqrt(ss + NORM_EPS)
    even = hf * scale + hb * scale
    odd = hf * scale - hb * scale
    kc = _bdot(fc_ref[...], even.astype(BF16))
    ks = _bdot(fs_ref[...], odd.astype(BF16))
    sign = jnp.where(row % 2 == 0, 1.0, -1.0)
    nyq = jnp.sum(even * sign, axis=0, keepdims=True)
    o_ref[0, 0] = kc
    o_ref[0, 1] = jnp.where(row == 0, nyq, ks)


def _hyena_filters(seq_len, w1p, b1, w2, b2, w3, freq, decay, fc, fs, *, tn=256):
    depth = w1p.shape[0]
    hid = HY_FILTER_HIDDEN
    nb = HY_W // tn
    feats = _hyena_feats(seq_len)
    col = lambda d: (lambda l, o, c: (l, 0, (2 * o + d) * nb + c))
    lyr = lambda l, o, c: (l, 0, 0)
    return pl.pallas_call(
        _filter_kernel,
        grid=(depth, 2, nb),
        in_specs=[
            pl.BlockSpec((seq_len, LANE), lambda l, o, c: (0, 0)),
            pl.BlockSpec((1, LANE, hid), lyr),
            pl.BlockSpec((1, 1, hid), lyr),
            pl.BlockSpec((1, hid, hid), lyr),
            pl.BlockSpec((1, 1, hid), lyr),
            pl.BlockSpec((1, 2, hid), lyr),
            pl.BlockSpec((1, hid, tn), col(0)),
            pl.BlockSpec((1, hid, tn), col(1)),
            pl.BlockSpec((1, 1, tn), col(0)),
            pl.BlockSpec((1, 1, tn), col(1)),
            _resident((seq_len, seq_len), lambda l, o, c: (0, 0)),
            _resident((seq_len, seq_len), lambda l, o, c: (0, 0)),
        ],
        out_specs=pl.BlockSpec((1, 2, seq_len, tn), lambda l, o, c: (l, 0, 0, o * nb + c)),
        out_shape=jax.ShapeDtypeStruct((depth, 2, seq_len, 2 * HY_W), F32),
        scratch_shapes=[pltpu.VMEM((seq_len, hid), F32)],
        compiler_params=_cparams(3),
        name="hyena_filters",
    )(feats, w1p, b1.reshape(depth, 1, hid), w2, b2.reshape(depth, 1, hid), freq,
      w3, w3, decay.reshape(depth, 1, 4 * HY_W), decay.reshape(depth, 1, 4 * HY_W), fc, fs)


def _conv3(z, w_ref, b_ref):
    n = z.shape[0]
    row = lax.broadcasted_iota(jnp.int32, z.shape, 0)
    prev = jnp.where(row == 0, 0.0, pltpu.roll(z, 1, 0))
    nxt = jnp.where(row == n - 1, 0.0, pltpu.roll(z, n - 1, 0))
    return prev * w_ref[0:1, :] + z * w_ref[1:2, :] + nxt * w_ref[2:3, :] + b_ref[...]


def _longconv_kernel(a_ref, g_ref, aw_ref, ab_ref, gw_ref, gb_ref, kf_ref, skip_ref,
                     fc_ref, fs_ref, ic_ref, is_ref, flip_ref, o_ref, *, conv_a):
    a = a_ref[...].astype(F32)
    if conv_a:
        a = _conv3(a, aw_ref, ab_ref)
    n = a.shape[0]
    h = n // 2
    lo = slice(0, h)
    hi = slice(h, n)
    k = lax.broadcasted_iota(jnp.int32, (h, a.shape[1]), 0)
    alt = jnp.where(k % 2 == 0, 1.0, -1.0)
    first = k == 0
    a_lo = a[lo]
    a_hi = a[hi]
    a_mid = a_hi[0:1, :]
    rev = _bdot(flip_ref[...], a_hi.astype(BF16))
    even_in = (a_lo + rev).astype(BF16)
    odd_in = (a_lo - rev).astype(BF16)
    nyq = jnp.sum(alt * (a_lo + a_hi), axis=0, keepdims=True)
    ac_e = _bdot(fc_ref[lo, lo], even_in) + alt * a_mid
    as_e = jnp.where(first, nyq, _bdot(fs_ref[lo, lo], odd_in))
    ac_o = _bdot(fc_ref[hi, lo], odd_in)
    as_o = _bdot(fs_ref[hi, lo], even_in) + alt * a_mid
    kc_e = kf_ref[0, 0, lo, :]
    ks_e = kf_ref[0, 1, lo, :]
    kc_o = kf_ref[0, 0, hi, :]
    ks_o = kf_ref[0, 1, hi, :]
    cross = as_e * ks_e
    yc_e = jnp.where(first, ac_e * kc_e, ac_e * kc_e - cross)
    ys_e = jnp.where(first, 0.0, ac_e * ks_e + as_e * kc_e)
    y_nyq = cross[0:1, :]
    yc_o = ac_o * kc_o - as_o * ks_o
    ys_o = ac_o * ks_o + as_o * kc_o
    sym = (_bdot(ic_ref[lo, lo], yc_e.astype(BF16)) + _bdot(is_ref[lo, hi], ys_o.astype(BF16))
           + alt * (y_nyq * (0.5 / n)))
    anti = _bdot(ic_ref[lo, hi], yc_o.astype(BF16)) + _bdot(is_ref[lo, lo], ys_e.astype(BF16))
    y_mid = (jnp.sum(alt * (yc_e + ys_o), axis=0, keepdims=True) * (1.0 / n)
             - yc_e[0:1, :] * (0.5 / n) + y_nyq * (0.5 / n))
    y_hi = jnp.where(first, y_mid, _bdot(flip_ref[...], (sym - anti).astype(BF16)))
    conv = jnp.concatenate([sym + anti, y_hi], axis=0)
    gate = _conv3(g_ref[...].astype(F32), gw_ref, gb_ref)
    o_ref[...] = (gate * (conv + a * skip_ref[0])).astype(o_ref.dtype)


def _hyena_longconv(a_src, a_part, zm, gate_part, conv_w, conv_b, kf, layer, order, skip,
                    dft, *, seq_len, n_seq, row0, out_rows, prev_out=None, tn=256):
    fc, fs, ic, isn, flip = dft
    assert seq_len % 4 == 0
    nb = HY_W // tn
    rb0 = row0 // seq_len
    const = lambda c, b: (0, 0)
    conv_a = a_part is not None
    aw_blk = a_part * nb if conv_a else 0
    a_blk = ZM_HY // tn + aw_blk if conv_a else 0
    gw_blk = gate_part * nb
    g_blk = ZM_HY // tn + gw_blk
    kernel = functools.partial(_longconv_kernel, conv_a=conv_a)
    in_specs = [
        pl.BlockSpec((seq_len, tn), lambda c, b: (rb0 + b, a_blk + c)),
        pl.BlockSpec((seq_len, tn), lambda c, b: (rb0 + b, g_blk + c)),
        pl.BlockSpec((3, tn), lambda c, b: (0, aw_blk + c)),
        pl.BlockSpec((1, tn), lambda c, b: (0, aw_blk + c)),
        pl.BlockSpec((3, tn), lambda c, b: (0, gw_blk + c)),
        pl.BlockSpec((1, tn), lambda c, b: (0, gw_blk + c)),
        _resident((1, 2, seq_len, tn), lambda c, b: (layer, 0, 0, order * nb + c)),
        pl.BlockSpec((1, 1, tn), lambda c, b: (order, 0, c)),
        _resident((seq_len, seq_len), const),
        _resident((seq_len, seq_len), const),
        _resident((seq_len, seq_len), const),
        _resident((seq_len, seq_len), const),
        _resident((seq_len // 2, seq_len // 2), const),
    ]
    args = [a_src, zm, conv_w, conv_b.reshape(1, -1), conv_w, conv_b.reshape(1, -1), kf,
            skip.reshape(2, 1, HY_W), fc, fs, ic, isn, flip]
    aliases = {}
    if prev_out is not None:
        in_specs.append(pl.BlockSpec(memory_space=pl.ANY))
        args.append(prev_out)
        aliases = {len(args) - 1: 0}
        kernel = functools.partial(_drop_last_input, kernel, 13)
    return pl.pallas_call(
        kernel,
        grid=(nb, n_seq),
        in_specs=in_specs,
        out_specs=pl.BlockSpec((seq_len, tn), lambda c, b: (rb0 + b, c)),
        out_shape=jax.ShapeDtypeStruct((out_rows, HY_W), BF16),
        input_output_aliases=aliases,
        compiler_params=_cparams(2),
        name="hyena_longconv",
    )(*args)


def _drop_last_input(kernel, n_in, *refs):
    return kernel(*refs[:n_in], *refs[n_in + 1:])


def _flip_matrix(half):
    r = np.arange(half)[:, None]
    c = np.arange(half)[None, :]
    return _bf16_const(np.where((r >= 1) & (c == half - r), 1.0, 0.0))


def _fnet_scale(seq_len):
    return 1.0 / math.sqrt(seq_len * FN_GROUP_W)


def _fnet_dft(seq_len):
    half = seq_len // 2
    ang_l = _dft_angles(np.arange(half), np.arange(seq_len), seq_len)
    ang_w = _dft_angles(np.arange(FN_GROUP_W), np.arange(FN_GROUP_W), FN_GROUP_W)
    scale = _fnet_scale(seq_len)
    return (_bf16_const(np.cos(ang_l) * scale), _bf16_const(np.sin(ang_l) * scale),
            _bf16_const(np.cos(ang_w)), _bf16_const(np.sin(ang_w)), _flip_matrix(half))


def _fnet_kernel(z_ref, cl_ref, sl_ref, cw_ref, sw_ref, flip_ref, o_ref, *, scale):
    z = z_ref[...]
    gc, gs = [], []
    for j in range(z.shape[1] // FN_GROUP_W):
        zj = z[:, j * FN_GROUP_W:(j + 1) * FN_GROUP_W]
        gc.append(_bdot(zj, cw_ref[...]).astype(BF16))
        gs.append(_bdot(zj, sw_ref[...]).astype(BF16))
    gc = jnp.concatenate(gc, axis=-1)
    gs = jnp.concatenate(gs, axis=-1)
    p = _bdot(cl_ref[...], gc)
    q = _bdot(sl_ref[...], gs)
    pos = lax.broadcasted_iota(jnp.int32, gc.shape, 0)
    y_mid = scale * jnp.sum(jnp.where(pos % 2 == 0, 1.0, -1.0) * gc.astype(F32), axis=0,
                            keepdims=True)
    first = lax.broadcasted_iota(jnp.int32, p.shape, 0) == 0
    y_hi = jnp.where(first, y_mid, _bdot(flip_ref[...], (p + q).astype(BF16)))
    o_ref[...] = jnp.concatenate([p - q, y_hi], axis=0).astype(o_ref.dtype)


def _fnet(zm, dft, *, seq_len, n_seq, row0, out_rows, prev_out=None, tn=512):
    cl, sl, cw, sw, flip = dft
    nb = FN_W // tn
    rb0 = row0 // seq_len
    c0 = ZM_FN // tn
    half = seq_len // 2
    const = lambda c, b: (0, 0)
    in_specs = [
        pl.BlockSpec((seq_len, tn), lambda c, b: (rb0 + b, c0 + c)),
        _resident((half, seq_len), const),
        _resident((half, seq_len), const),
        pl.BlockSpec((FN_GROUP_W, FN_GROUP_W), const),
        pl.BlockSpec((FN_GROUP_W, FN_GROUP_W), const),
        _resident((half, half), const),
    ]
    args = [zm, cl, sl, cw, sw, flip]
    kernel = functools.partial(_fnet_kernel, scale=_fnet_scale(seq_len))
    aliases = {}
    if prev_out is not None:
        in_specs.append(pl.BlockSpec(memory_space=pl.ANY))
        args.append(prev_out)
        aliases = {len(args) - 1: 0}
        kernel = functools.partial(_drop_last_input, kernel, 6)
    return pl.pallas_call(
        kernel,
        grid=(nb, n_seq),
        in_specs=in_specs,
        out_specs=pl.BlockSpec((seq_len, tn), lambda c, b: (rb0 + b, c)),
        out_shape=jax.ShapeDtypeStruct((out_rows, FN_W), BF16),
        input_output_aliases=aliases,
        compiler_params=_cparams(2),
        name="fnet",
    )(*args)


def _hgrn_tables(reverse):
    c = HG_CHUNK
    tri = np.zeros((c, c), np.float32)
    for t in range(c):
        if reverse:
            tri[t, t:] = 1.0
        else:
            tri[t, :t + 1] = 1.0
    mats = [tri]
    masks = []
    for lvl in range(HG_LEVELS):
        half = (c // 2) >> lvl
        sel = np.zeros((c, c), np.float32)
        mask = np.zeros((c, c), np.float32)
        for t in range(c):
            start = (t // (2 * half)) * 2 * half
            mid = start + half
            sel[t, mid if reverse else mid - 1] = 1.0
            for s in range(start, start + 2 * half):
                if reverse and t < mid <= s:
                    mask[t, s] = 1.0
                if (not reverse) and s < mid <= t:
                    mask[t, s] = 1.0
        mats.append(tri - sel @ tri)
        masks.append(mask)
    half = c // 2
    sel = np.zeros((c, c), np.float32)
    mask = np.zeros((c, c), np.float32)
    for t in range(c):
        start = (t // half) * half
        sel[t, start + half // 2] = 1.0
        for s in range(start, start + half):
            if (s >= t) if reverse else (s <= t):
                mask[t, s] = 1.0
    mats.append(tri - sel @ tri)
    masks.append(mask)
    mats = [mats[0], mats[1], mats[-1]] + mats[2:-1]
    return np.concatenate(mats, axis=0), np.stack(masks, axis=0)


HG_BOUNDED_BLOCKS = 3
HG_MID_BLOCK = 2


def _hgrn_level_block(lvl):
    return 1 if lvl == 0 else HG_BOUNDED_BLOCKS + lvl - 1


def _hgrn_decays(zf_ref, row0, lb_ref, tab_ref, k_scr, p_scr, dec_scr):
    lb = lb_ref[...]
    f = lb + (1.0 - lb) * jax.nn.sigmoid(zf_ref[row0:row0 + HG_CHUNK, :])
    lf = jnp.log2(f)
    k_scr[...] = (1.0 - f).astype(BF16)
    p1 = lf.astype(BF16)
    r1 = lf - p1.astype(F32)
    p2 = r1.astype(BF16)
    p3 = (r1 - p2.astype(F32)).astype(BF16)
    p_scr[...] = jnp.concatenate([p1, p2, p3], axis=0)
    n = HG_BOUNDED_BLOCKS * HG_CHUNK
    dec_scr[0:n, :] = _bdot(tab_ref[0:n, :], p_scr[...])


def _hgrn_more_decays(tab_ref, p_scr, dec_scr):
    n = HG_BOUNDED_BLOCKS * HG_CHUNK
    dec_scr[n:, :] = _bdot(tab_ref[n:, :], p_scr[...])


def _hgrn_direction(q_ref, v_ref, row0, k_scr, dec_scr, mask_ref, st_scr, o_ref, end_row,
                    bounded):
    c = HG_CHUNK
    heads = lambda ref, r0: jnp.stack(
        [ref[r0:r0 + c, h * HG_DK:(h + 1) * HG_DK] for h in range(HG_HEADS)], axis=0)
    bmm = lambda eq, a, b: jnp.einsum(eq, a, b, preferred_element_type=F32)
    q = heads(q_ref, row0)
    k = heads(k_scr, 0)
    v = heads(v_ref, row0)
    b = heads(dec_scr, 0)
    st = st_scr[...]
    o = bmm('htk,hkv->htv', q * jnp.exp2(b).astype(BF16), st.astype(BF16))
    level = lambda l: jnp.exp2(-jnp.abs(heads(dec_scr, _hgrn_level_block(l) * c))).astype(BF16)
    if bounded:
        e = level(0)
        att = jnp.where(mask_ref[0] > 0.5, bmm('htk,hsk->hts', q * e, k * e), 0.0)
        d = heads(dec_scr, HG_MID_BLOCK * c)
        inner = bmm('htk,hsk->hts', q * jnp.exp2(d).astype(BF16), k * jnp.exp2(-d).astype(BF16))
        att += jnp.where(mask_ref[HG_LEVELS] > 0.5, inner, 0.0)
    else:
        diag = jnp.sum(q.astype(F32) * k.astype(F32), axis=-1, keepdims=True)
        o += diag * v.astype(F32)
        att = jnp.zeros((HG_HEADS, c, c), F32)
        for lvl in range(HG_LEVELS):
            e = level(lvl)
            att += bmm('htk,hsk->hts', q * e, k * e) * mask_ref[lvl]
    o += bmm('hts,hsv->htv', att.astype(BF16), v)
    for h in range(HG_HEADS):
        o_ref[row0:row0 + c, h * HG_DK:(h + 1) * HG_DK] = o[h]
    b_end = b[:, end_row:end_row + 1, :]
    upd = bmm('htk,htv->hkv', k * jnp.exp2(b_end - b).astype(BF16), v)
    decay = jnp.swapaxes(jnp.broadcast_to(jnp.exp2(b_end), (HG_HEADS, HG_DK, HG_DK)), 1, 2)
    st_scr[...] = decay * st + upd


HG_MAX_LOG2_SPAN = 100.0
HG_MAX_ABS_Q = 1e6


def _hgrn_kernel(qf_ref, vf_ref, zff_ref, qb_ref, vb_ref, zfb_ref, lbf_ref, lbb_ref,
                 tabf_ref, maskf_ref, tabb_ref, maskb_ref, s0f_ref, s0b_ref,
                 of_ref, ob_ref, sf_ref, sb_ref, stf_scr, stb_scr, kf_scr, kb_scr,
                 pf_scr, pb_scr, decf_scr, decb_scr):
    ci = pl.program_id(1)
    c = HG_CHUNK

    @pl.when(ci == 0)
    def _():
        stf_scr[...] = s0f_ref[0]
        stb_scr[...] = s0b_ref[0]

    q_abs = jnp.maximum(jnp.max(jnp.abs(qf_ref[...].astype(F32))),
                        jnp.max(jnp.abs(qb_ref[...].astype(F32))))
    mid = slice(HG_MID_BLOCK * c, (HG_MID_BLOCK + 1) * c)
    n_sub = qf_ref.shape[0] // c
    for sub in range(n_sub):
        rf = sub * c
        rb = (n_sub - 1 - sub) * c
        _hgrn_decays(zff_ref, rf, lbf_ref, tabf_ref, kf_scr, pf_scr, decf_scr)
        _hgrn_decays(zfb_ref, rb, lbb_ref, tabb_ref, kb_scr, pb_scr, decb_scr)
        span = jnp.maximum(jnp.max(jnp.abs(decf_scr[mid, :])), jnp.max(jnp.abs(decb_scr[mid, :])))
        bounded = (span < HG_MAX_LOG2_SPAN) & (q_abs < HG_MAX_ABS_Q)

        def run(flag, rf=rf, rb=rb):
            _hgrn_direction(qf_ref, vf_ref, rf, kf_scr, decf_scr, maskf_ref, stf_scr, of_ref,
                            c - 1, flag)
            _hgrn_direction(qb_ref, vb_ref, rb, kb_scr, decb_scr, maskb_ref, stb_scr, ob_ref,
                            0, flag)

        @pl.when(bounded)
        def _():
            run(True)

        @pl.when(jnp.logical_not(bounded))
        def _():
            _hgrn_more_decays(tabf_ref, pf_scr, decf_scr)
            _hgrn_more_decays(tabb_ref, pb_scr, decb_scr)
            run(False)

    @pl.when(ci == pl.num_programs(1) - 1)
    def _():
        sf_ref[0] = stf_scr[...]
        sb_ref[0] = stb_scr[...]


def _hgrn(zm, zf, lb_f, lb_b, s0_f, s0_b, *, seq_len, n_seq, row0, out_rows,
          prev_of=None, prev_ob=None):
    c = HG_CHUNK
    blk = c * HG_STEP_CHUNKS
    nc = seq_len // blk
    rb0 = row0 // blk
    w = HG_W
    tab_f, mask_f = _hgrn_tables(False)
    tab_b, mask_b = _hgrn_tables(True)
    fwd = lambda col: (lambda b, i: (rb0 + b * nc + i, col))
    bwd = lambda col: (lambda b, i: (rb0 + b * nc + nc - 1 - i, col))
    const2 = lambda b, i: (0, 0)
    const3 = lambda b, i: (0, 0, 0)
    state = lambda b, i: (b, 0, 0, 0)
    nt = (2 + HG_LEVELS) * c
    in_specs = [
        pl.BlockSpec((blk, w), fwd(ZM_Q // w)),
        pl.BlockSpec((blk, w), fwd(ZM_I // w)),
        pl.BlockSpec((blk, w), fwd(0)),
        pl.BlockSpec((blk, w), bwd(ZM_Q // w)),
        pl.BlockSpec((blk, w), bwd(ZM_I // w)),
        pl.BlockSpec((blk, w), bwd(1)),
        pl.BlockSpec((1, w), const2),
        pl.BlockSpec((1, w), const2),
        pl.BlockSpec((nt, 3 * c), const2),
        pl.BlockSpec((HG_LEVELS + 1, c, c), const3),
        pl.BlockSpec((nt, 3 * c), const2),
        pl.BlockSpec((HG_LEVELS + 1, c, c), const3),
        pl.BlockSpec((1, HG_HEADS, HG_DK, HG_DK), state),
        pl.BlockSpec((1, HG_HEADS, HG_DK, HG_DK), state),
    ]
    args = [zm, zm, zf, zm, zm, zf, lb_f.reshape(1, w), lb_b.reshape(1, w),
            jnp.asarray(np.tile(tab_f, (1, 3)), BF16), jnp.asarray(mask_f),
            jnp.asarray(np.tile(tab_b, (1, 3)), BF16), jnp.asarray(mask_b), s0_f, s0_b]
    kernel = _hgrn_kernel
    aliases = {}
    if prev_of is not None:
        in_specs += [pl.BlockSpec(memory_space=pl.ANY), pl.BlockSpec(memory_space=pl.ANY)]
        args += [prev_of, prev_ob]
        aliases = {14: 0, 15: 1}
        kernel = functools.partial(_hgrn_alias_kernel, 14)
    st_shape = jax.ShapeDtypeStruct((n_seq, HG_HEADS, HG_DK, HG_DK), F32)
    return pl.pallas_call(
        kernel,
        grid=(n_seq, nc),
        in_specs=in_specs,
        out_specs=[
            pl.BlockSpec((blk, w), fwd(0)),
            pl.BlockSpec((blk, w), bwd(0)),
            pl.BlockSpec((1, HG_HEADS, HG_DK, HG_DK), state),
            pl.BlockSpec((1, HG_HEADS, HG_DK, HG_DK), state),
        ],
        out_shape=[jax.ShapeDtypeStruct((out_rows, w), F32),
                   jax.ShapeDtypeStruct((out_rows, w), F32), st_shape, st_shape],
        scratch_shapes=[pltpu.VMEM((HG_HEADS, HG_DK, HG_DK), F32),
                        pltpu.VMEM((HG_HEADS, HG_DK, HG_DK), F32),
                        pltpu.VMEM((c, w), BF16), pltpu.VMEM((c, w), BF16),
                        pltpu.VMEM((3 * c, w), BF16), pltpu.VMEM((3 * c, w), BF16),
                        pltpu.VMEM((nt, w), F32), pltpu.VMEM((nt, w), F32)],
        input_output_aliases=aliases,
        compiler_params=_cparams(2),
        name="hgrn2",
    )(*args)


def _hgrn_alias_kernel(n_in, *refs):
    return _hgrn_kernel(*refs[:n_in], *refs[n_in + 2:])


def kernel(x, c, ctx, c_ctx, w_mod, b_mod, norm1_w, norm2_w, w_in, hy_conv_w, hy_conv_b,
           hy_f_w1, hy_f_b1, hy_f_w2, hy_f_b2, hy_f_w3, hy_f_freq, hy_decay, hy_skip,
           hg_lower, hg_norm_w, w_br_hy, w_br_fn, w_br_hg, w_out,
           ffn_w1, ffn_w3, ffn_w2, moe_router, moe_w1, moe_w3, moe_w2, final_norm_w):
    batch, seq, d = x.shape
    l_ctx = ctx.shape[1]
    depth = w_in.shape[0]
    n_lat = batch * seq
    n_all = n_lat + batch * l_ctx

    lb_all = jnp.cumsum(jax.nn.softmax(hg_lower.astype(F32), axis=1), axis=1)
    lb_all = lb_all - lb_all[:, :1]

    cc = jnp.zeros((8, d), F32).at[:batch].set(c).at[batch].set(c_ctx)
    mods = _modulation(cc, w_mod, b_mod).reshape(depth, 8, 6, d)

    w1p = jnp.zeros((depth, LANE, HY_FILTER_HIDDEN), F32).at[:, :HY_POS_DIM].set(hy_f_w1)
    dft_lat = _hyena_dft(seq)
    dft_ctx = _hyena_dft(l_ctx)
    kf_lat = _hyena_filters(seq, w1p, hy_f_b1, hy_f_w2, hy_f_b2, hy_f_w3, hy_f_freq, hy_decay,
                            dft_lat[0], dft_lat[1])
    kf_ctx = _hyena_filters(l_ctx, w1p, hy_f_b1, hy_f_w2, hy_f_b2, hy_f_w3, hy_f_freq, hy_decay,
                            dft_ctx[0], dft_ctx[1])
    fn_lat = _fnet_dft(seq)
    fn_ctx = _fnet_dft(l_ctx)

    rows_x = jnp.concatenate([x.reshape(n_lat, d), ctx.reshape(batch * l_ctx, d)], axis=0)
    zero_state = jnp.zeros((batch, HG_HEADS, HG_DK, HG_DK), F32)
    tile_cap = math.gcd(seq, batch * l_ctx)
    tok = dict(seq=seq, batch=batch)
    tile = lambda t: dict(tm=min(t, tile_cap))
    lat = dict(seq_len=seq, n_seq=batch, row0=0, out_rows=n_all)
    cx = dict(seq_len=l_ctx, n_seq=batch, row0=n_lat, out_rows=n_all)

    for l in range(depth):
        last = l == depth - 1
        rows = n_lat if last else n_all
        m = mods[l]

        h = _norm_rows(rows_x, norm1_w[l], m, **tok, **tile(512))
        zm, zf = _in_proj(h, w_in, l, **tile(1024))

        of, ob, s_f, s_b = _hgrn(zm, zf, lb_all[0, l], lb_all[1, l], zero_state, zero_state, **cx)
        of, ob, _, _ = _hgrn(zm, zf, lb_all[0, l], lb_all[1, l], s_f, s_b, prev_of=of,
                             prev_ob=ob, **lat)

        hy_args = (hy_conv_w[l], hy_conv_b[l])
        mix = dict(lat, out_rows=rows)
        u = _hyena_longconv(zm, 0, zm, 1, *hy_args, kf_lat, l, 0, hy_skip[l], dft_lat, **mix)
        y_hy = _hyena_longconv(u, None, zm, 2, *hy_args, kf_lat, l, 1, hy_skip[l], dft_lat, **mix)
        y_fn = _fnet(zm, fn_lat, **mix)
        if not last:
            u = _hyena_longconv(zm, 0, zm, 1, *hy_args, kf_ctx, l, 0, hy_skip[l], dft_ctx,
                                prev_out=u, **cx)
            y_hy = _hyena_longconv(u, None, zm, 2, *hy_args, kf_ctx, l, 1, hy_skip[l], dft_ctx,
                                   prev_out=y_hy, **cx)
            y_fn = _fnet(zm, fn_ctx, prev_out=y_fn, **cx)

        rows_x = _merge_out(y_hy, y_fn, of, ob, zm, rows_x, hg_norm_w[l], m,
                            w_br_hy[l].astype(BF16), w_br_fn[l].astype(BF16),
                            w_br_hg[l].astype(BF16), w_out[l].astype(BF16), rows=rows,
                            **tok, **tile(256))

        if l % 2 == 0:
            i = l // 2
            rows_x = _ffn_dense(rows_x, norm2_w[l], m, ffn_w1, ffn_w3, ffn_w2, i, rows=rows,
                                **tok, **tile(1024))
        else:
            i = l // 2
            rows_x = _moe_top2(rows_x, norm2_w[l], m, moe_router[i], moe_w1, moe_w3, moe_w2, i,
                               rows=rows, **tok, **tile(512))

    out = _final_norm(rows_x, final_norm_w, rows=n_lat, **tile(512))
    return out.reshape(batch, seq, d)
```
